```python
import functools
import jax, jax.numpy as jnp
from jax import lax
import numpy as np

D_MODEL = 2048
BATCH = 4
SEQ = 2048
DEPTH = 1
DEC_BATCH = 32
DEC_SEQ = 1
PAST_LEN = 16384
PAGE_SIZE = 128

MIX_WIDTH = D_MODEL
ATTN_WIDTH = MIX_WIDTH // 2
GMLP_WIDTH = MIX_WIDTH - ATTN_WIDTH
HEAD_DIM = 64
N_HEADS = ATTN_WIDTH // HEAD_DIM
N_KV_HEADS = 4
GQA_GROUP = N_HEADS // N_KV_HEADS
KV_WIDTH = N_KV_HEADS * HEAD_DIM
WINDOW = 128
ROPE_THETA = 500000.0
ROPE_DIM = HEAD_DIM // 4
CHUNK = 128
GMLP_GROUP_DIM = 128
N_GMLP_GROUPS = GMLP_WIDTH // GMLP_GROUP_DIM
D_FF = ((8 * D_MODEL // 3 + 127) // 128) * 128
N_SUB = 3
ALPHA = (2.0 * DEPTH) ** 0.25
BETA = (8.0 * DEPTH) ** -0.25
LN_EPS = 1e-5
Q_END = ATTN_WIDTH
K_END = Q_END + KV_WIDTH
V_END = K_END + KV_WIDTH
U_END = V_END + GMLP_WIDTH
IN_WIDTH = U_END + GMLP_WIDTH

kernel_name = "hymba_swa_sink_gmlp_macaron_deepnorm_adaln_step"


def layer_norm(x, g, b):
    xf = x.astype(jnp.float32)
    mu = xf.mean(-1, keepdims=True)
    var = jnp.square(xf - mu).mean(-1, keepdims=True)
    return ((xf - mu) * lax.rsqrt(var + LN_EPS) * g + b).astype(x.dtype)


def rms_norm(x, g):
    xf = x.astype(jnp.float32)
    return (xf * lax.rsqrt(jnp.square(xf).mean(-1, keepdims=True) + LN_EPS) * g).astype(x.dtype)


def partial_rope(x, pos):
    half = ROPE_DIM // 2
    inv = ROPE_THETA ** (-(jnp.arange(half, dtype=jnp.float32) * 2.0) / ROPE_DIM)
    ang = pos.astype(jnp.float32)[:, None] * inv[None, :]
    cos = jnp.cos(ang)[None, :, None, :]
    sin = jnp.sin(ang)[None, :, None, :]
    xr = x[..., :ROPE_DIM].astype(jnp.float32)
    x1, x2 = xr[..., :half], xr[..., half:]
    rot = jnp.concatenate([x1 * cos - x2 * sin, x2 * cos + x1 * sin], -1).astype(x.dtype)
    return jnp.concatenate([rot, x[..., ROPE_DIM:]], -1)


def sink_attention(q, k, v, mask, sinks):
    s = jnp.einsum('...qkgd,...ckd->...kgqc', q.astype(jnp.float32), k.astype(jnp.float32)) * (HEAD_DIM ** -0.5)
    s = jnp.where(mask, s, -jnp.inf)
    sink = sinks.astype(jnp.float32).reshape(N_KV_HEADS, GQA_GROUP, 1, 1)
    m = jnp.maximum(s.max(-1, keepdims=True), sink)
    p = jnp.exp(s - m)
    w = p / (p.sum(-1, keepdims=True) + jnp.exp(sink - m))
    return jnp.einsum('...kgqc,...ckd->...qkgd', w.astype(v.dtype), v)


def spatial_mix(vn, w_s, b_s):
    t = vn.shape[-3]
    w = jnp.where(jnp.tril(jnp.ones((t, t), dtype=bool)), w_s[:, :t, :t], 0.0).astype(vn.dtype)
    mix = jnp.einsum('gij,...jgd->...igd', w, vn)
    return mix + b_s[:, :t].T[:, :, None].astype(vn.dtype)


def gmlp_branch(pu, pv, gn_g, gn_b, w_s, b_s):
    u = jax.nn.gelu(pu)
    v = jax.nn.gelu(pv)
    grp = v.shape[:-1] + (N_GMLP_GROUPS, GMLP_GROUP_DIM)
    vn = layer_norm(v.reshape(grp), gn_g.reshape(N_GMLP_GROUPS, GMLP_GROUP_DIM), gn_b.reshape(N_GMLP_GROUPS, GMLP_GROUP_DIM))
    gated = u.reshape(grp) * spatial_mix(vn, w_s, b_s)
    return gated.reshape(u.shape), vn.reshape(v.shape)


def split_projection(p):
    return p[..., :Q_END], p[..., Q_END:K_END], p[..., K_END:V_END], p[..., V_END:U_END], p[..., U_END:IN_WIDTH]


def mix_output(attn, gm, out_g, w_o):
    merged = jnp.concatenate([rms_norm(attn, out_g[:ATTN_WIDTH]), rms_norm(gm, out_g[ATTN_WIDTH:])], -1)
    return merged @ w_o


def mixer_prompt(h, w_in, sinks, gn_g, gn_b, w_s, b_s, out_g, w_o):
    b, s, _ = h.shape
    q, k, v, pu, pv = split_projection(h @ w_in)
    pos = jnp.arange(s)
    q = partial_rope(q.reshape(b, s, N_HEADS, HEAD_DIM), pos)
    k = partial_rope(k.reshape(b, s, N_KV_HEADS, HEAD_DIM), pos)
    v = v.reshape(b, s, N_KV_HEADS, HEAD_DIM)
    nb = s // WINDOW
    qb = q.reshape(b, nb, WINDOW, N_KV_HEADS, GQA_GROUP, HEAD_DIM)

    def band_keys(t):
        prev = jnp.concatenate([jnp.zeros_like(t[:, :WINDOW]), t[:, :-WINDOW]], 1)
        return jnp.concatenate([prev.reshape(b, nb, WINDOW, N_KV_HEADS, HEAD_DIM),
                                t.reshape(b, nb, WINDOW, N_KV_HEADS, HEAD_DIM)], 2)

    kb, vb = band_keys(k), band_keys(v)
    qi = jnp.arange(WINDOW)[:, None]
    ci = jnp.arange(2 * WINDOW)[None, :]
    rel = WINDOW + qi - ci
    band = (rel >= 0) & (rel < WINDOW)
    valid = band[None] & ((jnp.arange(nb)[:, None, None] > 0) | (ci >= WINDOW)[None])
    attn = sink_attention(qb, kb, vb, valid[:, None, None], sinks).reshape(b, s, ATTN_WIDTH)
    nc = s // CHUNK
    gm, _ = gmlp_branch(pu.reshape(b, nc, CHUNK, GMLP_WIDTH), pv.reshape(b, nc, CHUNK, GMLP_WIDTH), gn_g, gn_b, w_s, b_s)
    gm = gm.reshape(b, s, GMLP_WIDTH)
    buf = min(WINDOW, s)
    return mix_output(attn, gm, out_g, w_o), (k[:, s - buf:], v[:, s - buf:])


def mixer_sample(h, cache_k, cache_v, w_in, sinks, gn_g, gn_b, w_s, b_s, out_g, w_o):
    b, t, _ = h.shape
    buf = cache_k.shape[1]
    q, k, v, pu, pv = split_projection(h @ w_in)
    pos = PAST_LEN + jnp.arange(t)
    q = partial_rope(q.reshape(b, t, N_HEADS, HEAD_DIM), pos)
    k = partial_rope(k.reshape(b, t, N_KV_HEADS, HEAD_DIM), pos)
    v = v.reshape(b, t, N_KV_HEADS, HEAD_DIM)
    k_all = jnp.concatenate([cache_k.astype(k.dtype), k], 1)
    v_all = jnp.concatenate([cache_v.astype(v.dtype), v], 1)
    kpos = PAST_LEN - buf + jnp.arange(buf + t)
    rel = pos[:, None] - kpos[None, :]
    mask = (rel >= 0) & (rel < WINDOW)
    attn = sink_attention(q.reshape(b, t, N_KV_HEADS, GQA_GROUP, HEAD_DIM), k_all, v_all, mask[None, None], sinks)
    attn = attn.reshape(b, t, ATTN_WIDTH)
    gm, vn = gmlp_branch(pu, pv, gn_g, gn_b, w_s, b_s)
    return mix_output(attn, gm, out_g, w_o), (k_all[:, t:], v_all[:, t:], vn)


def swiglu(h, w_up, w_down):
    up = h @ w_up
    return (jax.nn.silu(up[..., D_FF:]) * up[..., :D_FF]) @ w_down


def run_layer(x, c, mixer, w_ada, b_ada, ln_g, ln_b, w_up, w_down):
    mod = (jax.nn.silu(c) @ w_ada + b_ada).reshape(c.shape[0], N_SUB, 3, D_MODEL)
    shift, scale, gate = mod[:, :, 0, None], mod[:, :, 1, None], mod[:, :, 2, None]

    def modulate(t, i):
        return t * (1.0 + scale[:, i]) + shift[:, i]

    x = layer_norm(ALPHA * x + 0.5 * gate[:, 0] * swiglu(modulate(x, 0), w_up[0], w_down[0]), ln_g[0], ln_b[0])
    mixed, states = mixer(modulate(x, 1))
    x = layer_norm(ALPHA * x + gate[:, 1] * mixed, ln_g[1], ln_b[1])
    x = layer_norm(ALPHA * x + 0.5 * gate[:, 2] * swiglu(modulate(x, 2), w_up[1], w_down[1]), ln_g[2], ln_b[2])
    return x, states


def setup_inputs(seed: int = 0) -> dict:
    key = jax.random.key(seed)
    ks = jax.random.split(key, 22)
    buf = min(WINDOW, PAST_LEN)

    def nrm(k, shape, scale=1.0):
        return scale * jax.random.normal(k, shape, jnp.float32)

    return {
        "x_prompt": nrm(ks[0], (BATCH, SEQ, D_MODEL)),
        "x_sample": nrm(ks[1], (DEC_BATCH, DEC_SEQ, D_MODEL)),
        "cache_k_win": nrm(ks[2], (DEPTH, DEC_BATCH, buf, N_KV_HEADS, HEAD_DIM)),
        "cache_v_win": nrm(ks[3], (DEPTH, DEC_BATCH, buf, N_KV_HEADS, HEAD_DIM)),
        "c_prompt": nrm(ks[4], (BATCH, D_MODEL)),
        "c_sample": nrm(ks[5], (DEC_BATCH, D_MODEL)),
        "w_ada": nrm(ks[6], (DEPTH, D_MODEL, N_SUB * 3 * D_MODEL), 0.5 * D_MODEL ** -0.5),
        "b_ada": nrm(ks[7], (DEPTH, N_SUB * 3 * D_MODEL), 0.02),
        "ln_g": 1.0 + nrm(ks[8], (DEPTH, N_SUB, D_MODEL), 0.02),
        "ln_b": nrm(ks[9], (DEPTH, N_SUB, D_MODEL), 0.02),
        "w_ffn_up": nrm(ks[10], (DEPTH, 2, D_MODEL, 2 * D_FF), D_MODEL ** -0.5),
        "w_ffn_down": nrm(ks[11], (DEPTH, 2, D_FF, D_MODEL), BETA * D_FF ** -0.5),
        "w_in": nrm(ks[12], (DEPTH, D_MODEL, IN_WIDTH), D_MODEL ** -0.5),
        "attn_sinks": nrm(ks[13], (DEPTH, N_HEADS), 0.5),
        "gmlp_norm_g": 1.0 + nrm(ks[14], (DEPTH, GMLP_WIDTH), 0.02),
        "gmlp_norm_b": nrm(ks[15], (DEPTH, GMLP_WIDTH), 0.02),
        "w_spatial": nrm(ks[16], (DEPTH, N_GMLP_GROUPS, CHUNK, CHUNK), CHUNK ** -0.5),
        "b_spatial": 1.0 + nrm(ks[17], (DEPTH, N_GMLP_GROUPS, CHUNK), 0.02),
        "out_norm_g": 1.0 + nrm(ks[18], (DEPTH, MIX_WIDTH), 0.02),
        "w_o": nrm(ks[19], (DEPTH, MIX_WIDTH, D_MODEL), BETA * MIX_WIDTH ** -0.5),
    }


def reference(x_prompt, x_sample, cache_k_win, cache_v_win, c_prompt, c_sample, w_ada, b_ada, ln_g, ln_b,
              w_ffn_up, w_ffn_down, w_in, attn_sinks, gmlp_norm_g, gmlp_norm_b, w_spatial, b_spatial,
              out_norm_g, w_o):
    yp, ys = x_prompt, x_sample
    kp, vp, kss, vss, gvs = [], [], [], [], []
    for l in range(DEPTH):
        mix_w = (w_in[l], attn_sinks[l], gmlp_norm_g[l], gmlp_norm_b[l], w_spatial[l], b_spatial[l], out_norm_g[l], w_o[l])
        common = (w_ada[l], b_ada[l], ln_g[l], ln_b[l], w_ffn_up[l], w_ffn_down[l])
        yp, (k_new, v_new) = run_layer(yp, c_prompt, functools.partial(mixer_prompt, w_in=mix_w[0], sinks=mix_w[1], gn_g=mix_w[2], gn_b=mix_w[3], w_s=mix_w[4], b_s=mix_w[5], out_g=mix_w[6], w_o=mix_w[7]), *common)
        ys, (k_s, v_s, gv_s) = run_layer(ys, c_sample, functools.partial(mixer_sample, cache_k=cache_k_win[l], cache_v=cache_v_win[l], w_in=mix_w[0], sinks=mix_w[1], gn_g=mix_w[2], gn_b=mix_w[3], w_s=mix_w[4], b_s=mix_w[5], out_g=mix_w[6], w_o=mix_w[7]), *common)
        kp.append(k_new)
        vp.append(v_new)
        kss.append(k_s)
        vss.append(v_s)
        gvs.append(gv_s)
    return (yp, ys, jnp.stack(kp), jnp.stack(vp), jnp.stack(kss), jnp.stack(vss), jnp.stack(gvs))
```

```python
import functools

import jax
import jax.numpy as jnp
import numpy as np
from jax import lax
from jax.experimental import pallas as pl
from jax.experimental.pallas import tpu as pltpu

D_MODEL = 2048
SEQ = 2048
PAST_LEN = 16384
ATTN_WIDTH = 1024
GMLP_WIDTH = 1024
HEAD_DIM = 64
N_HEADS = 16
N_KV_HEADS = 4
GQA_GROUP = 4
KV_WIDTH = 256
WINDOW = 128
ROPE_THETA = 500000.0
ROPE_DIM = 16
CHUNK = 128
N_GMLP_GROUPS = 8
D_FF = 5504
N_SUB = 3
DEPTH = 1
ALPHA = (2.0 * DEPTH) ** 0.25
LN_EPS = 1e-5
Q_END = ATTN_WIDTH
K_END = Q_END + KV_WIDTH
V_END = K_END + KV_WIDTH
U_END = V_END + GMLP_WIDTH
IN_WIDTH = U_END + GMLP_WIDTH

LANES = 128
FF_TILE = 512
D_FF_PAD = ((D_FF + FF_TILE - 1) // FF_TILE) * FF_TILE
VMEM_LIMIT = 56 * 1024 * 1024

BF16 = jnp.bfloat16
F32 = jnp.float32


def _dot(a, b):
    return jnp.dot(a, b, preferred_element_type=F32)


def _dot_nt(a, b):
    return lax.dot_general(a, b, (((1,), (1,)), ((), ())), preferred_element_type=F32)


def _layer_norm(y, g, b):
    mu = jnp.mean(y, axis=-1, keepdims=True)
    d = y - mu
    var = jnp.mean(d * d, axis=-1, keepdims=True)
    return d * lax.rsqrt(var + LN_EPS) * g + b


def _rms_norm(y, g):
    return y * lax.rsqrt(jnp.mean(y * y, axis=-1, keepdims=True) + LN_EPS) * g


def _gelu_tanh(x):
    c = np.float32(np.sqrt(2.0 / np.pi))
    return 0.5 * x * (1.0 + jnp.tanh(c * (x + 0.044715 * (x * x * x))))


def _rope_lanes(t, cos, sin_lo, sin_hi):
    pieces = []
    for c in range(t.shape[-1] // LANES):
        x = t[:, c * LANES:(c + 1) * LANES]
        pieces.append(x * cos + pltpu.roll(x, 8, 1) * sin_hi + pltpu.roll(x, LANES - 8, 1) * sin_lo)
    return pieces[0] if len(pieces) == 1 else jnp.concatenate(pieces, axis=-1)


def _ada_kernel(c_ref, w_ref, b_ref, o_ref):
    c = c_ref[...]
    h = (c * jax.nn.sigmoid(c)).astype(BF16)
    o_ref[...] = _dot(h, w_ref[...].astype(BF16)) + b_ref[...]


def _ada(c, w_ada, b_ada):
    rows = c.shape[0]
    n = w_ada.shape[1]
    tn = 1024
    return pl.pallas_call(
        _ada_kernel,
        grid=(n // tn,),
        in_specs=[
            pl.BlockSpec((rows, D_MODEL), lambda j: (0, 0)),
            pl.BlockSpec((D_MODEL, tn), lambda j: (0, j)),
            pl.BlockSpec((1, tn), lambda j: (0, j)),
        ],
        out_specs=pl.BlockSpec((rows, tn), lambda j: (0, j)),
        out_shape=jax.ShapeDtypeStruct((rows, n), F32),
        compiler_params=pltpu.CompilerParams(dimension_semantics=("arbitrary",), vmem_limit_bytes=VMEM_LIMIT),
        name="ada_mod",
    )(c, w_ada, b_ada.reshape(1, n))


def _ffn_kernel(x_ref, shift_ref, scale_ref, gate_ref, wv_ref, wg_ref, wd_ref, g_ref, b_ref, o_ref, xm_ref, acc_ref):
    f = pl.program_id(1)

    @pl.when(f == 0)
    def _():
        xm_ref[...] = (x_ref[...] * (1.0 + scale_ref[0]) + shift_ref[0]).astype(BF16)

    xm = xm_ref[...]
    hv = _dot(xm, wv_ref[...])
    hg = _dot(xm, wg_ref[...])
    a = (hg * jax.nn.sigmoid(hg) * hv).astype(BF16)
    contrib = _dot(a, wd_ref[...])

    @pl.when(f == 0)
    def _():
        acc_ref[...] = contrib

    @pl.when(f > 0)
    def _():
        acc_ref[...] += contrib

    @pl.when(f == pl.num_programs(1) - 1)
    def _():
        y = ALPHA * x_ref[...] + 0.5 * gate_ref[0] * acc_ref[...]
        o_ref[...] = _layer_norm(y, g_ref[...], b_ref[...])


def _ffn(x, shift, scale, gate, wv, wg, wd, ln_g, ln_b, *, tm, rows_per_mod):
    m = x.shape[0]
    r = shift.shape[1]
    tiles_per_mod = rows_per_mod // tm
    mod_spec = pl.BlockSpec((1, r, D_MODEL), lambda i, f: (i // tiles_per_mod, 0, 0))
    return pl.pallas_call(
        _ffn_kernel,
        grid=(m // tm, D_FF_PAD // FF_TILE),
        in_specs=[
            pl.BlockSpec((tm, D_MODEL), lambda i, f: (i, 0)),
            mod_spec, mod_spec, mod_spec,
            pl.BlockSpec((D_MODEL, FF_TILE), lambda i, f: (0, f)),
            pl.BlockSpec((D_MODEL, FF_TILE), lambda i, f: (0, f)),
            pl.BlockSpec((FF_TILE, D_MODEL), lambda i, f: (f, 0)),
            pl.BlockSpec((1, D_MODEL), lambda i, f: (0, 0)),
            pl.BlockSpec((1, D_MODEL), lambda i, f: (0, 0)),
        ],
        out_specs=pl.BlockSpec((tm, D_MODEL), lambda i, f: (i, 0)),
        out_shape=jax.ShapeDtypeStruct((m, D_MODEL), F32),
        scratch_shapes=[pltpu.VMEM((tm, D_MODEL), BF16), pltpu.VMEM((tm, D_MODEL), F32)],
        compiler_params=pltpu.CompilerParams(dimension_semantics=("arbitrary", "arbitrary"), vmem_limit_bytes=VMEM_LIMIT),
        name="swiglu_ln",
    )(x, shift, scale, gate, wv, wg, wd, ln_g.reshape(1, D_MODEL), ln_b.reshape(1, D_MODEL))


def _inproj_kernel(x_ref, shift_ref, scale_ref, w_ref, o_ref, xm_ref):
    @pl.when(pl.program_id(1) == 0)
    def _():
        xm_ref[...] = (x_ref[...] * (1.0 + scale_ref[0]) + shift_ref[0]).astype(BF16)

    o_ref[...] = _dot(xm_ref[...], w_ref[...])


def _inproj(x, shift, scale, w_in, *, tm, rows_per_mod):
    m = x.shape[0]
    r = shift.shape[1]
    tn = 512
    tiles_per_mod = rows_per_mod // tm
    mod_spec = pl.BlockSpec((1, r, D_MODEL), lambda i, j: (i // tiles_per_mod, 0, 0))
    return pl.pallas_call(
        _inproj_kernel,
        grid=(m // tm, IN_WIDTH // tn),
        in_specs=[
            pl.BlockSpec((tm, D_MODEL), lambda i, j: (i, 0)),
            mod_spec, mod_spec,
            pl.BlockSpec((D_MODEL, tn), lambda i, j: (0, j)),
        ],
        out_specs=pl.BlockSpec((tm, tn), lambda i, j: (i, j)),
        out_shape=jax.ShapeDtypeStruct((m, IN_WIDTH), F32),
        scratch_shapes=[pltpu.VMEM((tm, D_MODEL), BF16)],
        compiler_params=pltpu.CompilerParams(dimension_semantics=("arbitrary", "arbitrary"), vmem_limit_bytes=VMEM_LIMIT),
        name="mixer_inproj",
    )(x, shift, scale, w_in)


def _outproj_kernel(m_ref, x_ref, gate_ref, w_ref, g_ref, b_ref, o_ref):
    mixed = _dot(m_ref[...], w_ref[...])
    y = ALPHA * x_ref[...] + gate_ref[0] * mixed
    o_ref[...] = _layer_norm(y, g_ref[...], b_ref[...])


def _outproj(merged, x, gate, w_o, ln_g, ln_b, *, tm, rows_per_mod):
    m = x.shape[0]
    r = gate.shape[1]
    tiles_per_mod = rows_per_mod // tm
    return pl.pallas_call(
        _outproj_kernel,
        grid=(m // tm,),
        in_specs=[
            pl.BlockSpec((tm, D_MODEL), lambda i: (i, 0)),
            pl.BlockSpec((tm, D_MODEL), lambda i: (i, 0)),
            pl.BlockSpec((1, r, D_MODEL), lambda i: (i // tiles_per_mod, 0, 0)),
            pl.BlockSpec((D_MODEL, D_MODEL), lambda i: (0, 0)),
            pl.BlockSpec((1, D_MODEL), lambda i: (0, 0)),
            pl.BlockSpec((1, D_MODEL), lambda i: (0, 0)),
        ],
        out_specs=pl.BlockSpec((tm, D_MODEL), lambda i: (i, 0)),
        out_shape=jax.ShapeDtypeStruct((m, D_MODEL), F32),
        compiler_params=pltpu.CompilerParams(dimension_semantics=("arbitrary",), vmem_limit_bytes=VMEM_LIMIT),
        name="mixer_outproj_ln",
    )(merged, x, gate, w_o, ln_g.reshape(1, D_MODEL), ln_b.reshape(1, D_MODEL))


def _gmlp_norm(pv, gng, gnb):
    v = _gelu_tanh(pv)
    outs = []
    for g in range(N_GMLP_GROUPS):
        sl = slice(g * LANES, (g + 1) * LANES)
        outs.append(_layer_norm(v[:, sl], gng[:, sl], gnb[:, sl]))
    return outs


def _pmix_kernel(sinks_ref, p_ref, cos_ref, slo_ref, shi_ref, gng_ref, gnb_ref, ws_ref, bs_ref, og_ref,
                 m_ref, k_ref, v_ref, q_s, kk_s, vv_s, attn_s, wm_s):
    blk = pl.program_id(1)

    @pl.when(jnp.logical_and(pl.program_id(0) == 0, blk == 0))
    def _():
        row = lax.broadcasted_iota(jnp.int32, (CHUNK, CHUNK), 0)
        col = lax.broadcasted_iota(jnp.int32, (CHUNK, CHUNK), 1)
        for g in range(N_GMLP_GROUPS):
            wm_s[g] = jnp.where(col <= row, ws_ref[g], 0.0).astype(BF16)

    @pl.when(blk == 0)
    def _():
        kk_s[0:WINDOW, :] = jnp.zeros((WINDOW, KV_WIDTH), BF16)
        vv_s[0:WINDOW, :] = jnp.zeros((WINDOW, KV_WIDTH), BF16)

    cos, slo, shi = cos_ref[...], slo_ref[...], shi_ref[...]
    q_s[...] = (_rope_lanes(p_ref[:, 0:Q_END], cos, slo, shi) * (HEAD_DIM ** -0.5)).astype(BF16)
    k = _rope_lanes(p_ref[:, Q_END:K_END], cos, slo, shi)
    v = p_ref[:, K_END:V_END]
    k_ref[0] = k
    v_ref[0] = v
    kk_s[WINDOW:2 * WINDOW, :] = k.astype(BF16)
    vv_s[WINDOW:2 * WINDOW, :] = v.astype(BF16)

    rows = GQA_GROUP * WINDOW
    qi = lax.broadcasted_iota(jnp.int32, (rows, 2 * WINDOW), 0) & (WINDOW - 1)
    ci = lax.broadcasted_iota(jnp.int32, (rows, 2 * WINDOW), 1)
    lo = jnp.where(blk > 0, qi, WINDOW - 1)
    valid = jnp.logical_and(ci > lo, ci <= qi + WINDOW)
    for g in range(N_KV_HEADS):
        heads = [GQA_GROUP * g + mm for mm in range(GQA_GROUP)]
        qg = jnp.concatenate([q_s[:, h * HEAD_DIM:(h + 1) * HEAD_DIM] for h in heads], axis=0)
        sink = jnp.concatenate([jnp.full((WINDOW, 1), sinks_ref[h], F32) for h in heads], axis=0)
        kg = kk_s[:, g * HEAD_DIM:(g + 1) * HEAD_DIM]
        vg = vv_s[:, g * HEAD_DIM:(g + 1) * HEAD_DIM]
        s = jnp.where(valid, _dot_nt(qg, kg), -jnp.inf)
        mx = jnp.maximum(jnp.max(s, axis=-1, keepdims=True), sink)
        p = jnp.exp(s - mx)
        denom = jnp.sum(p, axis=-1, keepdims=True) + jnp.exp(sink - mx)
        o = _dot(p.astype(BF16), vg) / denom
        for mm, h in enumerate(heads):
            attn_s[:, h * HEAD_DIM:(h + 1) * HEAD_DIM] = o[mm * WINDOW:(mm + 1) * WINDOW]

    kk_s[0:WINDOW, :] = kk_s[WINDOW:2 * WINDOW, :]
    vv_s[0:WINDOW, :] = vv_s[WINDOW:2 * WINDOW, :]

    vn = _gmlp_norm(p_ref[:, U_END:IN_WIDTH], gng_ref[...], gnb_ref[...])
    gated = []
    for g in range(N_GMLP_GROUPS):
        u = _gelu_tanh(p_ref[:, V_END + g * LANES:V_END + (g + 1) * LANES])
        mix = _dot(wm_s[g], vn[g].astype(BF16)) + bs_ref[:, g:g + 1]
        gated.append(u * mix)
    gm = jnp.concatenate(gated, axis=-1)
    og = og_ref[...]
    m_ref[:, 0:ATTN_WIDTH] = _rms_norm(attn_s[...], og[:, 0:ATTN_WIDTH]).astype(BF16)
    m_ref[:, ATTN_WIDTH:] = _rms_norm(gm, og[:, ATTN_WIDTH:]).astype(BF16)


def _rope_tables(pos):
    half = ROPE_DIM // 2
    inv = ROPE_THETA ** (-(jnp.arange(half, dtype=F32) * 2.0) / ROPE_DIM)
    ang = pos.astype(F32)[:, None] * inv[None, :]
    cos, sin = jnp.cos(ang), jnp.sin(ang)
    n = pos.shape[0]
    one = jnp.ones((n, HEAD_DIM - ROPE_DIM), F32)
    zero = jnp.zeros((n, HEAD_DIM - ROPE_DIM), F32)
    z8 = jnp.zeros((n, half), F32)
    cos_t = jnp.concatenate([cos, cos, one], axis=-1)
    sin_lo = jnp.concatenate([-sin, z8, zero], axis=-1)
    sin_hi = jnp.concatenate([z8, sin, zero], axis=-1)
    rep = LANES // HEAD_DIM
    return jnp.tile(cos_t, (1, rep)), jnp.tile(sin_lo, (1, rep)), jnp.tile(sin_hi, (1, rep))


def _pmix(p, sinks, tables, gn_g, gn_b, w_s, b_s, out_g, *, batch):
    nb = SEQ // WINDOW
    cos_t, sin_lo, sin_hi = tables
    tab_spec = pl.BlockSpec((WINDOW, LANES), lambda b, j: (j, 0))
    full2 = lambda b, j: (0, 0)
    return pl.pallas_call(
        _pmix_kernel,
        grid=(batch, nb),
        in_specs=[
            pl.BlockSpec(memory_space=pltpu.SMEM),
            pl.BlockSpec((WINDOW, IN_WIDTH), lambda b, j: (b * nb + j, 0)),
            tab_spec, tab_spec, tab_spec,
            pl.BlockSpec((1, GMLP_WIDTH), full2),
            pl.BlockSpec((1, GMLP_WIDTH), full2),
            pl.BlockSpec((N_GMLP_GROUPS, CHUNK, CHUNK), lambda b, j: (0, 0, 0)),
            pl.BlockSpec((CHUNK, N_GMLP_GROUPS), full2),
            pl.BlockSpec((1, D_MODEL), full2),
        ],
        out_specs=[
            pl.BlockSpec((WINDOW, D_MODEL), lambda b, j: (b * nb + j, 0)),
            pl.BlockSpec((1, WINDOW, KV_WIDTH), lambda b, j: (b, 0, 0)),
            pl.BlockSpec((1, WINDOW, KV_WIDTH), lambda b, j: (b, 0, 0)),
        ],
        out_shape=[
            jax.ShapeDtypeStruct((batch * SEQ, D_MODEL), BF16),
            jax.ShapeDtypeStruct((batch, WINDOW, KV_WIDTH), F32),
            jax.ShapeDtypeStruct((batch, WINDOW, KV_WIDTH), F32),
        ],
        scratch_shapes=[
            pltpu.VMEM((WINDOW, ATTN_WIDTH), BF16),
            pltpu.VMEM((2 * WINDOW, KV_WIDTH), BF16),
            pltpu.VMEM((2 * WINDOW, KV_WIDTH), BF16),
            pltpu.VMEM((WINDOW, ATTN_WIDTH), F32),
            pltpu.VMEM((N_GMLP_GROUPS, CHUNK, CHUNK), BF16),
        ],
        compiler_params=pltpu.CompilerParams(dimension_semantics=("arbitrary", "arbitrary"), vmem_limit_bytes=VMEM_LIMIT),
        name="prompt_mixer_core",
    )(sinks, p, cos_t, sin_lo, sin_hi, gn_g.reshape(1, GMLP_WIDTH), gn_b.reshape(1, GMLP_WIDTH), w_s,
      b_s.T, out_g.reshape(1, D_MODEL))


def _sattn_kernel(qe_ref, p_ref, ck_ref, cv_ref, sink_ref, cos_ref, slo_ref, shi_ref, o_ref, ko_ref, vo_ref):
    cos, slo, shi = cos_ref[...], slo_ref[...], shi_ref[...]
    qe = _rope_lanes(qe_ref[0], cos, slo, shi) * (HEAD_DIM ** -0.5)
    kn = _rope_lanes(p_ref[0, :, Q_END:K_END], cos, slo, shi)
    vn = p_ref[0, :, K_END:V_END]
    ck = ck_ref[0]
    cv = cv_ref[0]
    sink = sink_ref[...]
    s_c = _dot_nt(qe.astype(BF16), ck.astype(BF16))
    key = lax.broadcasted_iota(jnp.int32, s_c.shape, 1)
    s_c = jnp.where(key >= 1, s_c, -jnp.inf)
    s_n = jnp.sum(qe * kn, axis=-1, keepdims=True)
    mx = jnp.maximum(jnp.maximum(jnp.max(s_c, axis=-1, keepdims=True), s_n), sink)
    p_c = jnp.exp(s_c - mx)
    p_n = jnp.exp(s_n - mx)
    denom = jnp.sum(p_c, axis=-1, keepdims=True) + p_n + jnp.exp(sink - mx)
    o = (_dot(p_c.astype(BF16), cv.astype(BF16)) + p_n * vn) / denom
    head = lax.broadcasted_iota(jnp.int32, o.shape, 0)
    lane = lax.broadcasted_iota(jnp.int32, o.shape, 1)
    o_ref[0] = jnp.where(head // GQA_GROUP == lane // HEAD_DIM, o, 0.0)
    row = lax.broadcasted_iota(jnp.int32, ck.shape, 0)
    last = row == WINDOW - 1
    ko_ref[0] = jnp.where(last, kn, pltpu.roll(ck, WINDOW - 1, 0))
    vo_ref[0] = jnp.where(last, vn, pltpu.roll(cv, WINDOW - 1, 0))


def _sattn(qe, p3, ck, cv, sinks, tables):
    b = qe.shape[0]
    cos_t, sin_lo, sin_hi = tables
    tab_spec = pl.BlockSpec((1, LANES), lambda i: (0, 0))
    blk3 = lambda i: (i, 0, 0)
    return pl.pallas_call(
        _sattn_kernel,
        grid=(b,),
        in_specs=[
            pl.BlockSpec((1, N_HEADS, KV_WIDTH), blk3),
            pl.BlockSpec((1, 1, IN_WIDTH), blk3),
            pl.BlockSpec((1, WINDOW, KV_WIDTH), blk3),
            pl.BlockSpec((1, WINDOW, KV_WIDTH), blk3),
            pl.BlockSpec((N_HEADS, 1), lambda i: (0, 0)),
            tab_spec, tab_spec, tab_spec,
        ],
        out_specs=[
            pl.BlockSpec((1, N_HEADS, KV_WIDTH), blk3),
            pl.BlockSpec((1, WINDOW, KV_WIDTH), blk3),
            pl.BlockSpec((1, WINDOW, KV_WIDTH), blk3),
        ],
        out_shape=[
            jax.ShapeDtypeStruct((b, N_HEADS, KV_WIDTH), F32),
            jax.ShapeDtypeStruct((b, WINDOW, KV_WIDTH), F32),
            jax.ShapeDtypeStruct((b, WINDOW, KV_WIDTH), F32),
        ],
        compiler_params=pltpu.CompilerParams(dimension_semantics=("arbitrary",)),
        name="sample_attn",
    )(qe, p3, ck, cv, sinks.reshape(N_HEADS, 1), cos_t, sin_lo, sin_hi)


def _smerge_kernel(attn_ref, p_ref, gng_ref, gnb_ref, w0_ref, b0_ref, og_ref, m_ref, vn_ref):
    og = og_ref[...]
    vn = jnp.concatenate(_gmlp_norm(p_ref[:, U_END:IN_WIDTH], gng_ref[...], gnb_ref[...]), axis=-1)
    vn_ref[...] = vn
    u = _gelu_tanh(p_ref[:, V_END:U_END])
    gm = u * (w0_ref[...] * vn + b0_ref[...])
    m_ref[:, 0:ATTN_WIDTH] = _rms_norm(attn_ref[...], og[:, 0:ATTN_WIDTH]).astype(BF16)
    m_ref[:, ATTN_WIDTH:] = _rms_norm(gm, og[:, ATTN_WIDTH:]).astype(BF16)


def _smerge(attn, p, gn_g, gn_b, w_s, b_s, out_g):
    b = attn.shape[0]
    w0 = jnp.repeat(w_s[:, 0, 0], LANES).reshape(1, GMLP_WIDTH)
    b0 = jnp.repeat(b_s[:, 0], LANES).reshape(1, GMLP_WIDTH)
    return pl.pallas_call(
        _smerge_kernel,
        out_shape=[
            jax.ShapeDtypeStruct((b, D_MODEL), BF16),
            jax.ShapeDtypeStruct((b, GMLP_WIDTH), F32),
        ],
        name="sample_gmlp_merge",
    )(attn, p, gn_g.reshape(1, GMLP_WIDTH), gn_b.reshape(1, GMLP_WIDTH), w0, b0, out_g.reshape(1, D_MODEL))


def kernel(x_prompt, x_sample, cache_k_win, cache_v_win, c_prompt, c_sample, w_ada, b_ada, ln_g, ln_b,
           w_ffn_up, w_ffn_down, w_in, attn_sinks, gmlp_norm_g, gmlp_norm_b, w_spatial, b_spatial,
           out_norm_g, w_o):
    batch, seq, _ = x_prompt.shape
    dec_batch = x_sample.shape[0]
    buf = cache_k_win.shape[2]
    assert seq == SEQ and buf == WINDOW and x_sample.shape[1] == 1 and w_ada.shape[0] == DEPTH == 1

    xp = x_prompt.reshape(batch * seq, D_MODEL)
    xs = x_sample.reshape(dec_batch, D_MODEL)

    n_c = batch + dec_batch
    n_c_pad = ((n_c + 7) // 8) * 8
    c_all = jnp.concatenate([c_prompt, c_sample, jnp.zeros((n_c_pad - n_c, D_MODEL), F32)], axis=0)
    mod = _ada(c_all, w_ada[0], b_ada[0]).reshape(n_c_pad, N_SUB, 3, D_MODEL)
    mod_p = mod[:batch]
    mod_s = mod[batch:n_c]

    def pmod(i, j):
        return mod_p[:, i, j].reshape(batch, 1, D_MODEL)

    def smod(i, j):
        return mod_s[:, i, j].reshape(1, dec_batch, D_MODEL)

    pad_f = D_FF_PAD - D_FF
    w_in_b = w_in[0].astype(BF16)
    w_o_b = w_o[0].astype(BF16)

    def ffn_weights(i):
        up = w_ffn_up[0, i]
        wv = jnp.pad(up[:, :D_FF], ((0, 0), (0, pad_f))).astype(BF16)
        wg = jnp.pad(up[:, D_FF:], ((0, 0), (0, pad_f))).astype(BF16)
        wd = jnp.pad(w_ffn_down[0, i], ((0, pad_f), (0, 0))).astype(BF16)
        return wv, wg, wd

    tm_p = 512
    ffn_p = functools.partial(_ffn, tm=tm_p, rows_per_mod=seq)
    ffn_s = functools.partial(_ffn, tm=dec_batch, rows_per_mod=dec_batch)

    wts0 = ffn_weights(0)
    xp = ffn_p(xp, pmod(0, 0), pmod(0, 1), pmod(0, 2), *wts0, ln_g[0, 0], ln_b[0, 0])
    xs = ffn_s(xs, smod(0, 0), smod(0, 1), smod(0, 2), *wts0, ln_g[0, 0], ln_b[0, 0])

    pp = _inproj(xp, pmod(1, 0), pmod(1, 1), w_in_b, tm=1024, rows_per_mod=seq)
    merged_p, k_p, v_p = _pmix(pp, attn_sinks[0], _rope_tables(jnp.arange(seq)), gmlp_norm_g[0], gmlp_norm_b[0],
                               w_spatial[0], b_spatial[0], out_norm_g[0], batch=batch)
    xp = _outproj(merged_p, xp, pmod(1, 2), w_o_b, ln_g[0, 1], ln_b[0, 1], tm=512, rows_per_mod=seq)

    ps = _inproj(xs, smod(1, 0), smod(1, 1), w_in_b, tm=dec_batch, rows_per_mod=dec_batch)
    eye = jnp.eye(N_KV_HEADS, dtype=F32)
    q5 = ps[:, :Q_END].reshape(dec_batch, N_KV_HEADS, GQA_GROUP, 1, HEAD_DIM)
    qe = (q5 * eye[None, :, None, :, None]).reshape(dec_batch, N_HEADS, KV_WIDTH)
    ck = cache_k_win[0].reshape(dec_batch, buf, KV_WIDTH)
    cv = cache_v_win[0].reshape(dec_batch, buf, KV_WIDTH)
    oe, k_s, v_s = _sattn(qe, ps.reshape(dec_batch, 1, IN_WIDTH), ck, cv, attn_sinks[0],
                          _rope_tables(jnp.full((1,), PAST_LEN)))
    attn_s = oe.reshape(dec_batch, N_KV_HEADS, GQA_GROUP, N_KV_HEADS, HEAD_DIM).sum(axis=3).reshape(dec_batch, ATTN_WIDTH)
    merged_s, vn_s = _smerge(attn_s, ps, gmlp_norm_g[0], gmlp_norm_b[0], w_spatial[0], b_spatial[0], out_norm_g[0])
    xs = _outproj(merged_s, xs, smod(1, 2), w_o_b, ln_g[0, 1], ln_b[0, 1], tm=dec_batch, rows_per_mod=dec_batch)

    wts1 = ffn_weights(1)
    xp = ffn_p(xp, pmod(2, 0), pmod(2, 1), pmod(2, 2), *wts1, ln_g[0, 2], ln_b[0, 2])
    xs = ffn_s(xs, smod(2, 0), smod(2, 1), smod(2, 2), *wts1, ln_g[0, 2], ln_b[0, 2])

    return (
        xp.reshape(batch, seq, D_MODEL),
        xs.reshape(dec_batch, 1, D_MODEL),
        k_p.reshape(1, batch, WINDOW, N_KV_HEADS, HEAD_DIM),
        v_p.reshape(1, batch, WINDOW, N_KV_HEADS, HEAD_DIM),
        k_s.reshape(1, dec_batch, buf, N_KV_HEADS, HEAD_DIM),
        v_s.reshape(1, dec_batch, buf, N_KV_HEADS, HEAD_DIM),
        vn_s.reshape(1, dec_batch, 1, GMLP_WIDTH),
    )
```

```python
import functools

import jax
import jax.numpy as jnp
import numpy as np
from jax import lax
from jax.experimental import pallas as pl
from jax.experimental.pallas import tpu as pltpu

D_MODEL = 2048
SEQ = 2048
PAST_LEN = 16384
ATTN_WIDTH = 1024
GMLP_WIDTH = 1024
HEAD_DIM = 64
N_HEADS = 16
N_KV_HEADS = 4
GQA_GROUP = 4
KV_WIDTH = 256
WINDOW = 128
ROPE_THETA = 500000.0
ROPE_DIM = 16
CHUNK = 128
N_GMLP_GROUPS = 8
D_FF = 5504
N_SUB = 3
DEPTH = 1
ALPHA = (2.0 * DEPTH) ** 0.25
LN_EPS = 1e-5
Q_END = ATTN_WIDTH
K_END = Q_END + KV_WIDTH
V_END = K_END + KV_WIDTH
U_END = V_END + GMLP_WIDTH
IN_WIDTH = U_END + GMLP_WIDTH

LANES = 128
MXU_COLS = 256
FF_TILE = 512
N_FF_STEPS = -(-D_FF // FF_TILE)
FF_LAST_OFF = D_FF - FF_TILE
FF_OVERLAP = N_FF_STEPS * FF_TILE - D_FF
OUT_TILE = 512
N_OUT_STEPS = D_MODEL // OUT_TILE
VMEM_LIMIT = 56 * 1024 * 1024

BF16 = jnp.bfloat16
F32 = jnp.float32


def _dot(a, b):
    return jnp.dot(a, b, preferred_element_type=F32)


def _dot_nt(a, b):
    return lax.dot_general(a, b, (((1,), (1,)), ((), ())), preferred_element_type=F32)


def _layer_norm(y, g, b):
    mu = jnp.mean(y, axis=-1, keepdims=True)
    d = y - mu
    var = jnp.mean(d * d, axis=-1, keepdims=True)
    return d * lax.rsqrt(var + LN_EPS) * g + b


def _rms_norm(y, g):
    return y * lax.rsqrt(jnp.mean(y * y, axis=-1, keepdims=True) + LN_EPS) * g


def _gelu_tanh(x):
    c = np.float32(np.sqrt(2.0 / np.pi))
    return 0.5 * x * (1.0 + jnp.tanh(c * (x + 0.044715 * (x * x * x))))


def _rope_lanes(t, cos, sin_lo, sin_hi):
    pieces = []
    for c in range(t.shape[-1] // LANES):
        x = t[:, c * LANES:(c + 1) * LANES]
        pieces.append(x * cos + pltpu.roll(x, 8, 1) * sin_hi + pltpu.roll(x, LANES - 8, 1) * sin_lo)
    return pieces[0] if len(pieces) == 1 else jnp.concatenate(pieces, axis=-1)


def _ada_kernel(c_ref, w_ref, b_ref, o_ref):
    c = c_ref[...]
    h = (c * jax.nn.sigmoid(c)).astype(BF16)
    o_ref[...] = _dot(h, w_ref[...].astype(BF16)) + b_ref[...]


def _ada(c, w_ada, b_ada):
    rows = c.shape[0]
    n = w_ada.shape[1]
    tn = 1024
    return pl.pallas_call(
        _ada_kernel,
        grid=(n // tn,),
        in_specs=[
            pl.BlockSpec((rows, D_MODEL), lambda j: (0, 0)),
            pl.BlockSpec((D_MODEL, tn), lambda j: (0, j)),
            pl.BlockSpec((1, tn), lambda j: (0, j)),
        ],
        out_specs=pl.BlockSpec((rows, tn), lambda j: (0, j)),
        out_shape=jax.ShapeDtypeStruct((rows, n), F32),
        compiler_params=pltpu.CompilerParams(dimension_semantics=("arbitrary",), vmem_limit_bytes=VMEM_LIMIT),
        name="ada_mod",
    )(c, w_ada, b_ada.reshape(1, n))


def _ff_offset(step):
    return jnp.minimum(step * FF_TILE, FF_LAST_OFF)


def _ffn_kernel(x_ref, shift_ref, scale_ref, gate_ref, wv_ref, wg_ref, wd_ref, g_ref, b_ref, o_ref,
                xm_ref, a_ref, y_ref):
    s = pl.program_id(1)

    @pl.when(s == 0)
    def _():
        xm_ref[...] = (x_ref[...] * (1.0 + scale_ref[0]) + shift_ref[0]).astype(BF16)

    @pl.when(s < N_FF_STEPS)
    def _():
        xm = xm_ref[...]
        for c in range(FF_TILE // MXU_COLS):
            sl = slice(c * MXU_COLS, (c + 1) * MXU_COLS)
            hv = _dot(xm, wv_ref[:, sl])
            hg = _dot(xm, wg_ref[:, sl])
            a = hg * jax.nn.sigmoid(hg) * hv
            if c * MXU_COLS < FF_OVERLAP:
                col = lax.broadcasted_iota(jnp.int32, a.shape, 1) + c * MXU_COLS
                a = jnp.where(jnp.logical_and(s == N_FF_STEPS - 1, col < FF_OVERLAP), 0.0, a)
            a_ref[s, :, sl] = a.astype(BF16)

    @pl.when(s >= N_FF_STEPS)
    def _():
        acc = None
        for f in range(N_FF_STEPS):
            r0 = min(f * FF_TILE, FF_LAST_OFF)
            part = _dot(a_ref[f], wd_ref[r0:r0 + FF_TILE, :])
            acc = part if acc is None else acc + part
        y_ref[s - N_FF_STEPS] = acc

    @pl.when(s == pl.num_programs(1) - 1)
    def _():
        mixed = jnp.concatenate([y_ref[n] for n in range(N_OUT_STEPS)], axis=-1)
        y = ALPHA * x_ref[...] + 0.5 * gate_ref[0] * mixed
        o_ref[...] = _layer_norm(y, g_ref[...], b_ref[...])


def _ffn(x, shift, scale, gate, w_up, w_down, ln_g, ln_b, *, tm, rows_per_mod):
    m = x.shape[0]
    r = shift.shape[1]
    tiles_per_mod = rows_per_mod // tm
    mod_spec = pl.BlockSpec((1, r, D_MODEL), lambda i, s: (i // tiles_per_mod, 0, 0))
    return pl.pallas_call(
        _ffn_kernel,
        grid=(m // tm, N_FF_STEPS + N_OUT_STEPS),
        in_specs=[
            pl.BlockSpec((tm, D_MODEL), lambda i, s: (i, 0)),
            mod_spec, mod_spec, mod_spec,
            pl.BlockSpec((pl.Element(D_MODEL), pl.Element(FF_TILE)),
                         lambda i, s: (0, pl.multiple_of(_ff_offset(s), LANES))),
            pl.BlockSpec((pl.Element(D_MODEL), pl.Element(FF_TILE)),
                         lambda i, s: (0, pl.multiple_of(D_FF + _ff_offset(s), LANES))),
            pl.BlockSpec((D_FF, OUT_TILE), lambda i, s: (0, jnp.maximum(s - N_FF_STEPS, 0))),
            pl.BlockSpec((1, D_MODEL), lambda i, s: (0, 0)),
            pl.BlockSpec((1, D_MODEL), lambda i, s: (0, 0)),
        ],
        out_specs=pl.BlockSpec((tm, D_MODEL), lambda i, s: (i, 0)),
        out_shape=jax.ShapeDtypeStruct((m, D_MODEL), F32),
        scratch_shapes=[
            pltpu.VMEM((tm, D_MODEL), BF16),
            pltpu.VMEM((N_FF_STEPS, tm, FF_TILE), BF16),
            pltpu.VMEM((N_OUT_STEPS, tm, OUT_TILE), F32),
        ],
        compiler_params=pltpu.CompilerParams(dimension_semantics=("arbitrary", "arbitrary"), vmem_limit_bytes=VMEM_LIMIT),
        name="swiglu_ln",
    )(x, shift, scale, gate, w_up, w_up, w_down, ln_g.reshape(1, D_MODEL), ln_b.reshape(1, D_MODEL))


def _inproj_kernel(x_ref, shift_ref, scale_ref, w_ref, o_ref, xm_ref):
    @pl.when(pl.program_id(1) == 0)
    def _():
        xm_ref[...] = (x_ref[...] * (1.0 + scale_ref[0]) + shift_ref[0]).astype(BF16)

    o_ref[...] = _dot(xm_ref[...], w_ref[...])


def _inproj(x, shift, scale, w_in, *, tm, rows_per_mod):
    m = x.shape[0]
    r = shift.shape[1]
    tn = 512
    tiles_per_mod = rows_per_mod // tm
    mod_spec = pl.BlockSpec((1, r, D_MODEL), lambda i, j: (i // tiles_per_mod, 0, 0))
    return pl.pallas_call(
        _inproj_kernel,
        grid=(m // tm, IN_WIDTH // tn),
        in_specs=[
            pl.BlockSpec((tm, D_MODEL), lambda i, j: (i, 0)),
            mod_spec, mod_spec,
            pl.BlockSpec((D_MODEL, tn), lambda i, j: (0, j)),
        ],
        out_specs=pl.BlockSpec((tm, tn), lambda i, j: (i, j)),
        out_shape=jax.ShapeDtypeStruct((m, IN_WIDTH), F32),
        scratch_shapes=[pltpu.VMEM((tm, D_MODEL), BF16)],
        compiler_params=pltpu.CompilerParams(dimension_semantics=("arbitrary", "arbitrary"), vmem_limit_bytes=VMEM_LIMIT),
        name="mixer_inproj",
    )(x, shift, scale, w_in)


def _outproj_kernel(m_ref, x_ref, gate_ref, w_ref, g_ref, b_ref, o_ref):
    mixed = _dot(m_ref[...], w_ref[...])
    y = ALPHA * x_ref[...] + gate_ref[0] * mixed
    o_ref[...] = _layer_norm(y, g_ref[...], b_ref[...])


def _outproj(merged, x, gate, w_o, ln_g, ln_b, *, tm, rows_per_mod):
    m = x.shape[0]
    r = gate.shape[1]
    tiles_per_mod = rows_per_mod // tm
    return pl.pallas_call(
        _outproj_kernel,
        grid=(m // tm,),
        in_specs=[
            pl.BlockSpec((tm, D_MODEL), lambda i: (i, 0)),
            pl.BlockSpec((tm, D_MODEL), lambda i: (i, 0)),
            pl.BlockSpec((1, r, D_MODEL), lambda i: (i // tiles_per_mod, 0, 0)),
            pl.BlockSpec((D_MODEL, D_MODEL), lambda i: (0, 0)),
            pl.BlockSpec((1, D_MODEL), lambda i: (0, 0)),
            pl.BlockSpec((1, D_MODEL), lambda i: (0, 0)),
        ],
        out_specs=pl.BlockSpec((tm, D_MODEL), lambda i: (i, 0)),
        out_shape=jax.ShapeDtypeStruct((m, D_MODEL), F32),
        compiler_params=pltpu.CompilerParams(dimension_semantics=("arbitrary",), vmem_limit_bytes=VMEM_LIMIT),
        name="mixer_outproj_ln",
    )(merged, x, gate, w_o, ln_g.reshape(1, D_MODEL), ln_b.reshape(1, D_MODEL))


def _gmlp_norm(pv, gng, gnb):
    v = _gelu_tanh(pv)
    outs = []
    for g in range(N_GMLP_GROUPS):
        sl = slice(g * LANES, (g + 1) * LANES)
        outs.append(_layer_norm(v[:, sl], gng[:, sl], gnb[:, sl]))
    return outs


def _pmix_kernel(sinks_ref, p_ref, cos_ref, slo_ref, shi_ref, gng_ref, gnb_ref, ws_ref, bs_ref, og_ref,
                 m_ref, k_ref, v_ref, q_s, kk_s, vv_s, attn_s, wm_s):
    blk = pl.program_id(1)

    @pl.when(jnp.logical_and(pl.program_id(0) == 0, blk == 0))
    def _():
        row = lax.broadcasted_iota(jnp.int32, (CHUNK, CHUNK), 0)
        col = lax.broadcasted_iota(jnp.int32, (CHUNK, CHUNK), 1)
        for g in range(N_GMLP_GROUPS):
            wm_s[g] = jnp.where(col <= row, ws_ref[g], 0.0).astype(BF16)

    @pl.when(blk == 0)
    def _():
        kk_s[0:WINDOW, :] = jnp.zeros((WINDOW, KV_WIDTH), BF16)
        vv_s[0:WINDOW, :] = jnp.zeros((WINDOW, KV_WIDTH), BF16)

    cos, slo, shi = cos_ref[...], slo_ref[...], shi_ref[...]
    q_s[...] = (_rope_lanes(p_ref[:, 0:Q_END], cos, slo, shi) * (HEAD_DIM ** -0.5)).astype(BF16)
    k = _rope_lanes(p_ref[:, Q_END:K_END], cos, slo, shi)
    v = p_ref[:, K_END:V_END]
    k_ref[0] = k
    v_ref[0] = v
    kk_s[WINDOW:2 * WINDOW, :] = k.astype(BF16)
    vv_s[WINDOW:2 * WINDOW, :] = v.astype(BF16)

    rows = GQA_GROUP * WINDOW
    qi = lax.broadcasted_iota(jnp.int32, (rows, 2 * WINDOW), 0) & (WINDOW - 1)
    ci = lax.broadcasted_iota(jnp.int32, (rows, 2 * WINDOW), 1)
    lo = jnp.where(blk > 0, qi, WINDOW - 1)
    valid = jnp.logical_and(ci > lo, ci <= qi + WINDOW)
    for g in range(N_KV_HEADS):
        heads = [GQA_GROUP * g + mm for mm in range(GQA_GROUP)]
        qg = jnp.concatenate([q_s[:, h * HEAD_DIM:(h + 1) * HEAD_DIM] for h in heads], axis=0)
        sink = jnp.concatenate([jnp.full((WINDOW, 1), sinks_ref[h], F32) for h in heads], axis=0)
        kg = kk_s[:, g * HEAD_DIM:(g + 1) * HEAD_DIM]
        vg = vv_s[:, g * HEAD_DIM:(g + 1) * HEAD_DIM]
        s = jnp.where(valid, _dot_nt(qg, kg), -jnp.inf)
        mx = jnp.maximum(jnp.max(s, axis=-1, keepdims=True), sink)
        p = jnp.exp(s - mx)
        denom = jnp.sum(p, axis=-1, keepdims=True) + jnp.exp(sink - mx)
        o = _dot(p.astype(BF16), vg) / denom
        for mm, h in enumerate(heads):
            attn_s[:, h * HEAD_DIM:(h + 1) * HEAD_DIM] = o[mm * WINDOW:(mm + 1) * WINDOW]

    kk_s[0:WINDOW, :] = kk_s[WINDOW:2 * WINDOW, :]
    vv_s[0:WINDOW, :] = vv_s[WINDOW:2 * WINDOW, :]

    vn = _gmlp_norm(p_ref[:, U_END:IN_WIDTH], gng_ref[...], gnb_ref[...])
    gated = []
    for g in range(N_GMLP_GROUPS):
        u = _gelu_tanh(p_ref[:, V_END + g * LANES:V_END + (g + 1) * LANES])
        mix = _dot(wm_s[g], vn[g].astype(BF16)) + bs_ref[:, g:g + 1]
        gated.append(u * mix)
    gm = jnp.concatenate(gated, axis=-1)
    og = og_ref[...]
    m_ref[:, 0:ATTN_WIDTH] = _rms_norm(attn_s[...], og[:, 0:ATTN_WIDTH]).astype(BF16)
    m_ref[:, ATTN_WIDTH:] = _rms_norm(gm, og[:, ATTN_WIDTH:]).astype(BF16)


def _rope_tables(pos):
    half = ROPE_DIM // 2
    inv = ROPE_THETA ** (-(jnp.arange(half, dtype=F32) * 2.0) / ROPE_DIM)
    ang = pos.astype(F32)[:, None] * inv[None, :]
    cos, sin = jnp.cos(ang), jnp.sin(ang)
    n = pos.shape[0]
    one = jnp.ones((n, HEAD_DIM - ROPE_DIM), F32)
    zero = jnp.zeros((n, HEAD_DIM - ROPE_DIM), F32)
    z8 = jnp.zeros((n, half), F32)
    cos_t = jnp.concatenate([cos, cos, one], axis=-1)
    sin_lo = jnp.concatenate([-sin, z8, zero], axis=-1)
    sin_hi = jnp.concatenate([z8, sin, zero], axis=-1)
    rep = LANES // HEAD_DIM
    return jnp.tile(cos_t, (1, rep)), jnp.tile(sin_lo, (1, rep)), jnp.tile(sin_hi, (1, rep))


def _pmix(p, sinks, tables, gn_g, gn_b, w_s, b_s, out_g, *, batch):
    nb = SEQ // WINDOW
    cos_t, sin_lo, sin_hi = tables
    tab_spec = pl.BlockSpec((WINDOW, LANES), lambda b, j: (j, 0))
    full2 = lambda b, j: (0, 0)
    return pl.pallas_call(
        _pmix_kernel,
        grid=(batch, nb),
        in_specs=[
            pl.BlockSpec(memory_space=pltpu.SMEM),
            pl.BlockSpec((WINDOW, IN_WIDTH), lambda b, j: (b * nb + j, 0)),
            tab_spec, tab_spec, tab_spec,
            pl.BlockSpec((1, GMLP_WIDTH), full2),
            pl.BlockSpec((1, GMLP_WIDTH), full2),
            pl.BlockSpec((N_GMLP_GROUPS, CHUNK, CHUNK), lambda b, j: (0, 0, 0)),
            pl.BlockSpec((CHUNK, N_GMLP_GROUPS), full2),
            pl.BlockSpec((1, D_MODEL), full2),
        ],
        out_specs=[
            pl.BlockSpec((WINDOW, D_MODEL), lambda b, j: (b * nb + j, 0)),
            pl.BlockSpec((1, WINDOW, KV_WIDTH), lambda b, j: (b, 0, 0)),
            pl.BlockSpec((1, WINDOW, KV_WIDTH), lambda b, j: (b, 0, 0)),
        ],
        out_shape=[
            jax.ShapeDtypeStruct((batch * SEQ, D_MODEL), BF16),
            jax.ShapeDtypeStruct((batch, WINDOW, KV_WIDTH), F32),
            jax.ShapeDtypeStruct((batch, WINDOW, KV_WIDTH), F32),
        ],
        scratch_shapes=[
            pltpu.VMEM((WINDOW, ATTN_WIDTH), BF16),
            pltpu.VMEM((2 * WINDOW, KV_WIDTH), BF16),
            pltpu.VMEM((2 * WINDOW, KV_WIDTH), BF16),
            pltpu.VMEM((WINDOW, ATTN_WIDTH), F32),
            pltpu.VMEM((N_GMLP_GROUPS, CHUNK, CHUNK), BF16),
        ],
        compiler_params=pltpu.CompilerParams(dimension_semantics=("arbitrary", "arbitrary"), vmem_limit_bytes=VMEM_LIMIT),
        name="prompt_mixer_core",
    )(sinks, p, cos_t, sin_lo, sin_hi, gn_g.reshape(1, GMLP_WIDTH), gn_b.reshape(1, GMLP_WIDTH), w_s,
      b_s.T, out_g.reshape(1, D_MODEL))


def _sattn_kernel(qe_ref, p_ref, ck_ref, cv_ref, sink_ref, cos_ref, slo_ref, shi_ref, o_ref, ko_ref, vo_ref):
    cos, slo, shi = cos_ref[...], slo_ref[...], shi_ref[...]
    qe = _rope_lanes(qe_ref[0], cos, slo, shi) * (HEAD_DIM ** -0.5)
    kn = _rope_lanes(p_ref[0, :, Q_END:K_END], cos, slo, shi)
    vn = p_ref[0, :, K_END:V_END]
    ck = ck_ref[0]
    cv = cv_ref[0]
    sink = sink_ref[...]
    s_c = _dot_nt(qe.astype(BF16), ck.astype(BF16))
    key = lax.broadcasted_iota(jnp.int32, s_c.shape, 1)
    s_c = jnp.where(key >= 1, s_c, -jnp.inf)
    s_n = jnp.sum(qe * kn, axis=-1, keepdims=True)
    mx = jnp.maximum(jnp.maximum(jnp.max(s_c, axis=-1, keepdims=True), s_n), sink)
    p_c = jnp.exp(s_c - mx)
    p_n = jnp.exp(s_n - mx)
    denom = jnp.sum(p_c, axis=-1, keepdims=True) + p_n + jnp.exp(sink - mx)
    o = (_dot(p_c.astype(BF16), cv.astype(BF16)) + p_n * vn) / denom
    head = lax.broadcasted_iota(jnp.int32, o.shape, 0)
    lane = lax.broadcasted_iota(jnp.int32, o.shape, 1)
    o_ref[0] = jnp.where(head // GQA_GROUP == lane // HEAD_DIM, o, 0.0)
    row = lax.broadcasted_iota(jnp.int32, ck.shape, 0)
    last = row == WINDOW - 1
    ko_ref[0] = jnp.where(last, kn, pltpu.roll(ck, WINDOW - 1, 0))
    vo_ref[0] = jnp.where(last, vn, pltpu.roll(cv, WINDOW - 1, 0))


def _sattn(qe, p3, ck, cv, sinks, tables):
    b = qe.shape[0]
    cos_t, sin_lo, sin_hi = tables
    tab_spec = pl.BlockSpec((1, LANES), lambda i: (0, 0))
    blk3 = lambda i: (i, 0, 0)
    return pl.pallas_call(
        _sattn_kernel,
        grid=(b,),
        in_specs=[
            pl.BlockSpec((1, N_HEADS, KV_WIDTH), blk3),
            pl.BlockSpec((1, 1, IN_WIDTH), blk3),
            pl.BlockSpec((1, WINDOW, KV_WIDTH), blk3),
            pl.BlockSpec((1, WINDOW, KV_WIDTH), blk3),
            pl.BlockSpec((N_HEADS, 1), lambda i: (0, 0)),
            tab_spec, tab_spec, tab_spec,
        ],
        out_specs=[
            pl.BlockSpec((1, N_HEADS, KV_WIDTH), blk3),
            pl.BlockSpec((1, WINDOW, KV_WIDTH), blk3),
            pl.BlockSpec((1, WINDOW, KV_WIDTH), blk3),
        ],
        out_shape=[
            jax.ShapeDtypeStruct((b, N_HEADS, KV_WIDTH), F32),
            jax.ShapeDtypeStruct((b, WINDOW, KV_WIDTH), F32),
            jax.ShapeDtypeStruct((b, WINDOW, KV_WIDTH), F32),
        ],
        compiler_params=pltpu.CompilerParams(dimension_semantics=("arbitrary",)),
        name="sample_attn",
    )(qe, p3, ck, cv, sinks.reshape(N_HEADS, 1), cos_t, sin_lo, sin_hi)


def _smerge_kernel(attn_ref, p_ref, gng_ref, gnb_ref, w0_ref, b0_ref, og_ref, m_ref, vn_ref):
    og = og_ref[...]
    vn = jnp.concatenate(_gmlp_norm(p_ref[:, U_END:IN_WIDTH], gng_ref[...], gnb_ref[...]), axis=-1)
    vn_ref[...] = vn
    u = _gelu_tanh(p_ref[:, V_END:U_END])
    gm = u * (w0_ref[...] * vn + b0_ref[...])
    m_ref[:, 0:ATTN_WIDTH] = _rms_norm(attn_ref[...], og[:, 0:ATTN_WIDTH]).astype(BF16)
    m_ref[:, ATTN_WIDTH:] = _rms_norm(gm, og[:, ATTN_WIDTH:]).astype(BF16)


def _smerge(attn, p, gn_g, gn_b, w_s, b_s, out_g):
    b = attn.shape[0]
    w0 = jnp.repeat(w_s[:, 0, 0], LANES).reshape(1, GMLP_WIDTH)
    b0 = jnp.repeat(b_s[:, 0], LANES).reshape(1, GMLP_WIDTH)
    return pl.pallas_call(
        _smerge_kernel,
        out_shape=[
            jax.ShapeDtypeStruct((b, D_MODEL), BF16),
            jax.ShapeDtypeStruct((b, GMLP_WIDTH), F32),
        ],
        name="sample_gmlp_merge",
    )(attn, p, gn_g.reshape(1, GMLP_WIDTH), gn_b.reshape(1, GMLP_WIDTH), w0, b0, out_g.reshape(1, D_MODEL))


def kernel(x_prompt, x_sample, cache_k_win, cache_v_win, c_prompt, c_sample, w_ada, b_ada, ln_g, ln_b,
           w_ffn_up, w_ffn_down, w_in, attn_sinks, gmlp_norm_g, gmlp_norm_b, w_spatial, b_spatial,
           out_norm_g, w_o):
    batch, seq, _ = x_prompt.shape
    dec_batch = x_sample.shape[0]
    buf = cache_k_win.shape[2]
    assert seq == SEQ and buf == WINDOW and x_sample.shape[1] == 1 and w_ada.shape[0] == DEPTH == 1

    xp = x_prompt.reshape(batch * seq, D_MODEL)
    xs = x_sample.reshape(dec_batch, D_MODEL)

    n_c = batch + dec_batch
    n_c_pad = ((n_c + 7) // 8) * 8
    c_all = jnp.concatenate([c_prompt, c_sample, jnp.zeros((n_c_pad - n_c, D_MODEL), F32)], axis=0)
    mod = _ada(c_all, w_ada[0], b_ada[0]).reshape(n_c_pad, N_SUB, 3, D_MODEL)
    mod_p = mod[:batch]
    mod_s = mod[batch:n_c]

    def pmod(i, j):
        return mod_p[:, i, j].reshape(batch, 1, D_MODEL)

    def smod(i, j):
        return mod_s[:, i, j].reshape(1, dec_batch, D_MODEL)

    w_in_b = w_in[0].astype(BF16)
    w_o_b = w_o[0].astype(BF16)

    def ffn_weights(i):
        return w_ffn_up[0, i].astype(BF16), w_ffn_down[0, i].astype(BF16)

    tm_p = 512
    ffn_p = functools.partial(_ffn, tm=tm_p, rows_per_mod=seq)
    ffn_s = functools.partial(_ffn, tm=dec_batch, rows_per_mod=dec_batch)

    wts0 = ffn_weights(0)
    xp = ffn_p(xp, pmod(0, 0), pmod(0, 1), pmod(0, 2), *wts0, ln_g[0, 0], ln_b[0, 0])
    xs = ffn_s(xs, smod(0, 0), smod(0, 1), smod(0, 2), *wts0, ln_g[0, 0], ln_b[0, 0])

    pp = _inproj(xp, pmod(1, 0), pmod(1, 1), w_in_b, tm=1024, rows_per_mod=seq)
    merged_p, k_p, v_p = _pmix(pp, attn_sinks[0], _rope_tables(jnp.arange(seq)), gmlp_norm_g[0], gmlp_norm_b[0],
                               w_spatial[0], b_spatial[0], out_norm_g[0], batch=batch)
    xp = _outproj(merged_p, xp, pmod(1, 2), w_o_b, ln_g[0, 1], ln_b[0, 1], tm=512, rows_per_mod=seq)

    ps = _inproj(xs, smod(1, 0), smod(1, 1), w_in_b, tm=dec_batch, rows_per_mod=dec_batch)
    eye = jnp.eye(N_KV_HEADS, dtype=F32)
    q5 = ps[:, :Q_END].reshape(dec_batch, N_KV_HEADS, GQA_GROUP, 1, HEAD_DIM)
    qe = (q5 * eye[None, :, None, :, None]).reshape(dec_batch, N_HEADS, KV_WIDTH)
    ck = cache_k_win[0].reshape(dec_batch, buf, KV_WIDTH)
    cv = cache_v_win[0].reshape(dec_batch, buf, KV_WIDTH)
    oe, k_s, v_s = _sattn(qe, ps.reshape(dec_batch, 1, IN_WIDTH), ck, cv, attn_sinks[0],
                          _rope_tables(jnp.full((1,), PAST_LEN)))
    attn_s = oe.reshape(dec_batch, N_KV_HEADS, GQA_GROUP, N_KV_HEADS, HEAD_DIM).sum(axis=3).reshape(dec_batch, ATTN_WIDTH)
    merged_s, vn_s = _smerge(attn_s, ps, gmlp_norm_g[0], gmlp_norm_b[0], w_spatial[0], b_spatial[0], out_norm_g[0])
    xs = _outproj(merged_s, xs, smod(1, 2), w_o_b, ln_g[0, 1], ln_b[0, 1], tm=dec_batch, rows_per_mod=dec_batch)

    wts1 = ffn_weights(1)
    xp = ffn_p(xp, pmod(2, 0), pmod(2, 1), pmod(2, 2), *wts1, ln_g[0, 2], ln_b[0, 2])
    xs = ffn_s(xs, smod(2, 0), smod(2, 1), smod(2, 2), *wts1, ln_g[0, 2], ln_b[0, 2])

    return (
        xp.reshape(batch, seq, D_MODEL),
        xs.reshape(dec_batch, 1, D_MODEL),
        k_p.reshape(1, batch, WINDOW, N_KV_HEADS, HEAD_DIM),
        v_p.reshape(1, batch, WINDOW, N_KV_HEADS, HEAD_DIM),
        k_s.reshape(1, dec_batch, buf, N_KV_HEADS, HEAD_DIM),
        v_s.reshape(1, dec_batch, buf, N_KV_HEADS, HEAD_DIM),
        vn_s.reshape(1, dec_batch, 1, GMLP_WIDTH),
    )
```

```python
import functools

import jax
import jax.numpy as jnp
import numpy as np
from jax import lax
from jax.experimental import pallas as pl
from jax.experimental.pallas import tpu as pltpu

D_MODEL = 2048
SEQ = 2048
PAST_LEN = 16384
ATTN_WIDTH = 1024
GMLP_WIDTH = 1024
HEAD_DIM = 64
N_HEADS = 16
N_KV_HEADS = 4
GQA_GROUP = 4
KV_WIDTH = 256
WINDOW = 128
ROPE_THETA = 500000.0
ROPE_DIM = 16
CHUNK = 128
N_GMLP_GROUPS = 8
D_FF = 5504
N_SUB = 3
DEPTH = 1
ALPHA = (2.0 * DEPTH) ** 0.25
LN_EPS = 1e-5
Q_END = ATTN_WIDTH
K_END = Q_END + KV_WIDTH
V_END = K_END + KV_WIDTH
U_END = V_END + GMLP_WIDTH
IN_WIDTH = U_END + GMLP_WIDTH

LANES = 128
MXU_COLS = 256
FF_TILE = 512
N_FF_STEPS = -(-D_FF // FF_TILE)
FF_LAST_OFF = D_FF - FF_TILE
FF_OVERLAP = N_FF_STEPS * FF_TILE - D_FF
OUT_TILE = 512
N_OUT_STEPS = D_MODEL // OUT_TILE
MIX_TILE = 256
PROJ_TILE = 512
VMEM_LIMIT = 56 * 1024 * 1024

BF16 = jnp.bfloat16
F32 = jnp.float32


def _dot(a, b):
    return jnp.dot(a, b, preferred_element_type=F32)


def _dot_nt(a, b):
    return lax.dot_general(a, b, (((1,), (1,)), ((), ())), preferred_element_type=F32)


def _layer_norm(y, g, b):
    mu = jnp.mean(y, axis=-1, keepdims=True)
    d = y - mu
    var = jnp.mean(d * d, axis=-1, keepdims=True)
    return d * lax.rsqrt(var + LN_EPS) * g + b


def _rms_norm(y, g):
    return y * lax.rsqrt(jnp.mean(y * y, axis=-1, keepdims=True) + LN_EPS) * g


def _gelu_tanh(x):
    c = np.float32(np.sqrt(2.0 / np.pi))
    return 0.5 * x * (1.0 + jnp.tanh(c * (x + 0.044715 * (x * x * x))))


def _rope_lanes(t, cos, sin_lo, sin_hi):
    pieces = []
    for c in range(t.shape[-1] // LANES):
        x = t[:, c * LANES:(c + 1) * LANES]
        pieces.append(x * cos + pltpu.roll(x, 8, 1) * sin_hi + pltpu.roll(x, LANES - 8, 1) * sin_lo)
    return pieces[0] if len(pieces) == 1 else jnp.concatenate(pieces, axis=-1)


def _ada_kernel(c_ref, w_ref, b_ref, o_ref):
    c = c_ref[...]
    h = (c * jax.nn.sigmoid(c)).astype(BF16)
    o_ref[...] = _dot(h, w_ref[...].astype(BF16)) + b_ref[...]


def _ada(c, w_ada, b_ada):
    rows = c.shape[0]
    n = w_ada.shape[1]
    tn = 1024
    return pl.pallas_call(
        _ada_kernel,
        grid=(n // tn,),
        in_specs=[
            pl.BlockSpec((rows, D_MODEL), lambda j: (0, 0)),
            pl.BlockSpec((D_MODEL, tn), lambda j: (0, j)),
            pl.BlockSpec((1, tn), lambda j: (0, j)),
        ],
        out_specs=pl.BlockSpec((rows, tn), lambda j: (0, j)),
        out_shape=jax.ShapeDtypeStruct((rows, n), F32),
        compiler_params=pltpu.CompilerParams(dimension_semantics=("arbitrary",), vmem_limit_bytes=VMEM_LIMIT),
        name="ada_mod",
    )(c, w_ada, b_ada.reshape(1, n))


def _ff_offset(step):
    return jnp.minimum(step * FF_TILE, FF_LAST_OFF)


def _ffn_kernel(x_ref, shift_ref, scale_ref, gate_ref, wv_ref, wg_ref, wd_ref, g_ref, b_ref, o_ref,
                xm_ref, a_ref, y_ref):
    s = pl.program_id(1)

    @pl.when(s == 0)
    def _():
        xm_ref[...] = (x_ref[...] * (1.0 + scale_ref[0]) + shift_ref[0]).astype(BF16)

    @pl.when(s < N_FF_STEPS)
    def _():
        xm = xm_ref[...]
        for c in range(FF_TILE // MXU_COLS):
            sl = slice(c * MXU_COLS, (c + 1) * MXU_COLS)
            hv = _dot(xm, wv_ref[0, :, sl])
            hg = _dot(xm, wg_ref[0, :, sl])
            a = hg * jax.nn.sigmoid(hg) * hv
            if c * MXU_COLS < FF_OVERLAP:
                col = lax.broadcasted_iota(jnp.int32, a.shape, 1) + c * MXU_COLS
                a = jnp.where(jnp.logical_and(s == N_FF_STEPS - 1, col < FF_OVERLAP), 0.0, a)
            a_ref[s, :, sl] = a.astype(BF16)

    @pl.when(s >= N_FF_STEPS)
    def _():
        acc = None
        for f in range(N_FF_STEPS):
            r0 = min(f * FF_TILE, FF_LAST_OFF)
            part = _dot(a_ref[f], wd_ref[r0:r0 + FF_TILE, :])
            acc = part if acc is None else acc + part
        y_ref[s - N_FF_STEPS] = acc

    @pl.when(s == pl.num_programs(1) - 1)
    def _():
        mixed = jnp.concatenate([y_ref[n] for n in range(N_OUT_STEPS)], axis=-1)
        y = ALPHA * x_ref[...] + 0.5 * gate_ref[0] * mixed
        o_ref[...] = _layer_norm(y, g_ref[...], b_ref[...])


def _ffn(x, shift, scale, gate, w_up, w_down, ln_g, ln_b, *, which, tm, rows_per_mod):
    m = x.shape[0]
    r = shift.shape[1]
    tiles_per_mod = rows_per_mod // tm
    mod_spec = pl.BlockSpec((1, r, D_MODEL), lambda i, s: (i // tiles_per_mod, 0, 0))
    return pl.pallas_call(
        _ffn_kernel,
        grid=(m // tm, N_FF_STEPS + N_OUT_STEPS),
        in_specs=[
            pl.BlockSpec((tm, D_MODEL), lambda i, s: (i, 0)),
            mod_spec, mod_spec, mod_spec,
            pl.BlockSpec((pl.Element(1), pl.Element(D_MODEL), pl.Element(FF_TILE)),
                         lambda i, s: (which, 0, pl.multiple_of(_ff_offset(s), LANES))),
            pl.BlockSpec((pl.Element(1), pl.Element(D_MODEL), pl.Element(FF_TILE)),
                         lambda i, s: (which, 0, pl.multiple_of(D_FF + _ff_offset(s), LANES))),
            pl.BlockSpec((None, D_FF, OUT_TILE), lambda i, s: (which, 0, jnp.maximum(s - N_FF_STEPS, 0))),
            pl.BlockSpec((1, D_MODEL), lambda i, s: (0, 0)),
            pl.BlockSpec((1, D_MODEL), lambda i, s: (0, 0)),
        ],
        out_specs=pl.BlockSpec((tm, D_MODEL), lambda i, s: (i, 0)),
        out_shape=jax.ShapeDtypeStruct((m, D_MODEL), F32),
        scratch_shapes=[
            pltpu.VMEM((tm, D_MODEL), BF16),
            pltpu.VMEM((N_FF_STEPS, tm, FF_TILE), BF16),
            pltpu.VMEM((N_OUT_STEPS, tm, OUT_TILE), F32),
        ],
        compiler_params=pltpu.CompilerParams(dimension_semantics=("arbitrary", "arbitrary"), vmem_limit_bytes=VMEM_LIMIT),
        name="swiglu_ln",
    )(x, shift, scale, gate, w_up, w_up, w_down, ln_g.reshape(1, D_MODEL), ln_b.reshape(1, D_MODEL))


def _inproj_kernel(x_ref, shift_ref, scale_ref, w_ref, o_ref, xm_ref):
    @pl.when(pl.program_id(1) == 0)
    def _():
        xm_ref[...] = (x_ref[...] * (1.0 + scale_ref[0]) + shift_ref[0]).astype(BF16)

    o_ref[...] = _dot(xm_ref[...], w_ref[...])


def _inproj(x, shift, scale, w_in, *, tm, rows_per_mod):
    m = x.shape[0]
    r = shift.shape[1]
    tn = 512
    tiles_per_mod = rows_per_mod // tm
    mod_spec = pl.BlockSpec((1, r, D_MODEL), lambda i, j: (i // tiles_per_mod, 0, 0))
    return pl.pallas_call(
        _inproj_kernel,
        grid=(m // tm, IN_WIDTH // tn),
        in_specs=[
            pl.BlockSpec((tm, D_MODEL), lambda i, j: (i, 0)),
            mod_spec, mod_spec,
            pl.BlockSpec((D_MODEL, tn), lambda i, j: (0, j)),
        ],
        out_specs=pl.BlockSpec((tm, tn), lambda i, j: (i, j)),
        out_shape=jax.ShapeDtypeStruct((m, IN_WIDTH), F32),
        scratch_shapes=[pltpu.VMEM((tm, D_MODEL), BF16)],
        compiler_params=pltpu.CompilerParams(dimension_semantics=("arbitrary", "arbitrary"), vmem_limit_bytes=VMEM_LIMIT),
        name="mixer_inproj",
    )(x, shift, scale, w_in)


def _outproj_kernel(m_ref, x_ref, gate_ref, w_ref, g_ref, b_ref, o_ref):
    mixed = _dot(m_ref[...], w_ref[...])
    y = ALPHA * x_ref[...] + gate_ref[0] * mixed
    o_ref[...] = _layer_norm(y, g_ref[...], b_ref[...])


def _outproj(merged, x, gate, w_o, ln_g, ln_b, *, tm, rows_per_mod):
    m = x.shape[0]
    r = gate.shape[1]
    tiles_per_mod = rows_per_mod // tm
    return pl.pallas_call(
        _outproj_kernel,
        grid=(m // tm,),
        in_specs=[
            pl.BlockSpec((tm, D_MODEL), lambda i: (i, 0)),
            pl.BlockSpec((tm, D_MODEL), lambda i: (i, 0)),
            pl.BlockSpec((1, r, D_MODEL), lambda i: (i // tiles_per_mod, 0, 0)),
            pl.BlockSpec((D_MODEL, D_MODEL), lambda i: (0, 0)),
            pl.BlockSpec((1, D_MODEL), lambda i: (0, 0)),
            pl.BlockSpec((1, D_MODEL), lambda i: (0, 0)),
        ],
        out_specs=pl.BlockSpec((tm, D_MODEL), lambda i: (i, 0)),
        out_shape=jax.ShapeDtypeStruct((m, D_MODEL), F32),
        compiler_params=pltpu.CompilerParams(dimension_semantics=("arbitrary",), vmem_limit_bytes=VMEM_LIMIT),
        name="mixer_outproj_ln",
    )(merged, x, gate, w_o, ln_g.reshape(1, D_MODEL), ln_b.reshape(1, D_MODEL))


def _gmlp_norm(pv, gng, gnb):
    v = _gelu_tanh(pv)
    outs = []
    for g in range(N_GMLP_GROUPS):
        sl = slice(g * LANES, (g + 1) * LANES)
        outs.append(_layer_norm(v[:, sl], gng[:, sl], gnb[:, sl]))
    return outs


def _band_attention(q_s, kk_s, vv_s, attn_s, sinks_ref, wb, first_block):
    rows = GQA_GROUP * WINDOW
    r0 = wb * WINDOW
    qi = lax.broadcasted_iota(jnp.int32, (rows, WINDOW), 0) & (WINDOW - 1)
    kj = lax.broadcasted_iota(jnp.int32, (rows, WINDOW), 1)
    own = kj <= qi
    ones = jnp.ones((2 * WINDOW, LANES), BF16)
    v2 = vv_s[r0:r0 + 2 * WINDOW, :]
    for g in range(N_KV_HEADS):
        heads = [GQA_GROUP * g + mm for mm in range(GQA_GROUP)]
        qg = jnp.concatenate([q_s[r0:r0 + WINDOW, h * HEAD_DIM:(h + 1) * HEAD_DIM] for h in heads], axis=0)
        sink = jnp.concatenate([jnp.full((WINDOW, 1), sinks_ref[h], F32) for h in heads], axis=0)
        kg = kk_s[r0:r0 + 2 * WINDOW, g * HEAD_DIM:(g + 1) * HEAD_DIM]
        s = _dot_nt(qg, kg)
        s_prev, s_own = s[:, 0:WINDOW], s[:, WINDOW:2 * WINDOW]
        c = jnp.where(own, s_own, s_prev)
        if first_block is not None:
            c = jnp.where(kj <= jnp.where(first_block, qi, WINDOW), c, -jnp.inf)
        mx = jnp.maximum(jnp.max(c, axis=-1, keepdims=True), sink)
        p = jnp.exp(c - mx)
        pcat = jnp.concatenate([jnp.where(own, 0.0, p), jnp.where(own, p, 0.0)], axis=-1).astype(BF16)
        o = _dot(pcat, v2)
        den = _dot(pcat, ones) + jnp.exp(sink - mx)
        half = LANES // HEAD_DIM
        on = o[:, (g // half) * LANES:(g // half + 1) * LANES] / den
        off = (g % half) * HEAD_DIM
        for mm, h in enumerate(heads):
            attn_s[r0:r0 + WINDOW, h * HEAD_DIM:(h + 1) * HEAD_DIM] = on[mm * WINDOW:(mm + 1) * WINDOW, off:off + HEAD_DIM]


def _pmix_kernel(sinks_ref, x_ref, shift_ref, scale_ref, gate_ref, win_ref, wo_ref, cos_ref, slo_ref, shi_ref,
                 gng_ref, gnb_ref, ws_ref, bs_ref, og_ref, lng_ref, lnb_ref,
                 o_ref, k_ref, v_ref, xm_s, p_s, q_s, kk_s, vv_s, kp_s, vp_s, attn_s, m_s, wm_s):
    t = pl.program_id(1)
    tm = x_ref.shape[0]
    nblk = tm // WINDOW

    @pl.when(jnp.logical_and(pl.program_id(0) == 0, t == 0))
    def _():
        row = lax.broadcasted_iota(jnp.int32, (CHUNK, CHUNK), 0)
        col = lax.broadcasted_iota(jnp.int32, (CHUNK, CHUNK), 1)
        for g in range(N_GMLP_GROUPS):
            wm_s[g] = jnp.where(col <= row, ws_ref[g], 0.0).astype(BF16)

    @pl.when(t == 0)
    def _():
        kp_s[...] = jnp.zeros((WINDOW, KV_WIDTH), BF16)
        vp_s[...] = jnp.zeros((WINDOW, KV_WIDTH), BF16)

    xm_s[...] = (x_ref[...] * (1.0 + scale_ref[0]) + shift_ref[0]).astype(BF16)
    for c in range(IN_WIDTH // PROJ_TILE):
        sl = slice(c * PROJ_TILE, (c + 1) * PROJ_TILE)
        p_s[:, sl] = _dot(xm_s[...], win_ref[:, sl])

    cos, slo, shi = cos_ref[...], slo_ref[...], shi_ref[...]
    q_s[...] = (_rope_lanes(p_s[:, 0:Q_END], cos, slo, shi) * (HEAD_DIM ** -0.5)).astype(BF16)
    k = _rope_lanes(p_s[:, Q_END:K_END], cos, slo, shi)
    v = p_s[:, K_END:V_END]
    k_ref[0] = k[tm - WINDOW:tm]
    v_ref[0] = v[tm - WINDOW:tm]
    kb, vb = k.astype(BF16), v.astype(BF16)
    kk_s[0:WINDOW, :] = kp_s[...]
    vv_s[0:WINDOW, :] = vp_s[...]
    kk_s[WINDOW:WINDOW + tm, :] = kb
    vv_s[WINDOW:WINDOW + tm, :] = vb
    kp_s[...] = kb[tm - WINDOW:tm]
    vp_s[...] = vb[tm - WINDOW:tm]

    for wb in range(nblk):
        _band_attention(q_s, kk_s, vv_s, attn_s, sinks_ref, wb, (t == 0) if wb == 0 else None)

    vn = _gmlp_norm(p_s[:, U_END:IN_WIDTH], gng_ref[...], gnb_ref[...])
    gated = []
    for g in range(N_GMLP_GROUPS):
        u = _gelu_tanh(p_s[:, V_END + g * LANES:V_END + (g + 1) * LANES])
        vg = vn[g].astype(BF16)
        cols = jnp.concatenate([vg[wb * CHUNK:(wb + 1) * CHUNK] for wb in range(nblk)], axis=-1)
        mix = _dot(wm_s[g], cols)
        mix = jnp.concatenate([mix[:, wb * LANES:(wb + 1) * LANES] for wb in range(nblk)], axis=0)
        bias = jnp.concatenate([bs_ref[:, g:g + 1]] * nblk, axis=0)
        gated.append(u * (mix + bias))
    gm = jnp.concatenate(gated, axis=-1)
    og = og_ref[...]
    m_s[:, 0:ATTN_WIDTH] = _rms_norm(attn_s[...], og[:, 0:ATTN_WIDTH]).astype(BF16)
    m_s[:, ATTN_WIDTH:] = _rms_norm(gm, og[:, ATTN_WIDTH:]).astype(BF16)

    for c in range(D_MODEL // PROJ_TILE):
        sl = slice(c * PROJ_TILE, (c + 1) * PROJ_TILE)
        o_ref[:, sl] = ALPHA * x_ref[:, sl] + gate_ref[0, :, sl] * _dot(m_s[...], wo_ref[:, sl])
    o_ref[...] = _layer_norm(o_ref[...], lng_ref[...], lnb_ref[...])


def _rope_tables(pos):
    half = ROPE_DIM // 2
    inv = ROPE_THETA ** (-(jnp.arange(half, dtype=F32) * 2.0) / ROPE_DIM)
    ang = pos.astype(F32)[:, None] * inv[None, :]
    cos, sin = jnp.cos(ang), jnp.sin(ang)
    n = pos.shape[0]
    one = jnp.ones((n, HEAD_DIM - ROPE_DIM), F32)
    zero = jnp.zeros((n, HEAD_DIM - ROPE_DIM), F32)
    z8 = jnp.zeros((n, half), F32)
    cos_t = jnp.concatenate([cos, cos, one], axis=-1)
    sin_lo = jnp.concatenate([-sin, z8, zero], axis=-1)
    sin_hi = jnp.concatenate([z8, sin, zero], axis=-1)
    rep = LANES // HEAD_DIM
    return jnp.tile(cos_t, (1, rep)), jnp.tile(sin_lo, (1, rep)), jnp.tile(sin_hi, (1, rep))


def _pmix(x, shift, scale, gate, w_in, w_o, sinks, tables, gn_g, gn_b, w_s, b_s, out_g, ln_g, ln_b, *, batch, tm):
    nt = SEQ // tm
    cos_t, sin_lo, sin_hi = tables
    tab_spec = pl.BlockSpec((tm, LANES), lambda b, j: (j, 0))
    full2 = lambda b, j: (0, 0)
    mod_spec = pl.BlockSpec((1, 1, D_MODEL), lambda b, j: (b, 0, 0))
    resident = dict(pipeline_mode=pl.Buffered(1))
    return pl.pallas_call(
        _pmix_kernel,
        grid=(batch, nt),
        in_specs=[
            pl.BlockSpec(memory_space=pltpu.SMEM),
            pl.BlockSpec((tm, D_MODEL), lambda b, j: (b * nt + j, 0)),
            mod_spec, mod_spec, mod_spec,
            pl.BlockSpec((D_MODEL, IN_WIDTH), full2, **resident),
            pl.BlockSpec((D_MODEL, D_MODEL), full2, **resident),
            tab_spec, tab_spec, tab_spec,
            pl.BlockSpec((1, GMLP_WIDTH), full2),
            pl.BlockSpec((1, GMLP_WIDTH), full2),
            pl.BlockSpec((N_GMLP_GROUPS, CHUNK, CHUNK), lambda b, j: (0, 0, 0)),
            pl.BlockSpec((CHUNK, N_GMLP_GROUPS), full2),
            pl.BlockSpec((1, D_MODEL), full2),
            pl.BlockSpec((1, D_MODEL), full2),
            pl.BlockSpec((1, D_MODEL), full2),
        ],
        out_specs=[
            pl.BlockSpec((tm, D_MODEL), lambda b, j: (b * nt + j, 0)),
            pl.BlockSpec((1, WINDOW, KV_WIDTH), lambda b, j: (b, 0, 0)),
            pl.BlockSpec((1, WINDOW, KV_WIDTH), lambda b, j: (b, 0, 0)),
        ],
        out_shape=[
            jax.ShapeDtypeStruct((batch * SEQ, D_MODEL), F32),
            jax.ShapeDtypeStruct((batch, WINDOW, KV_WIDTH), F32),
            jax.ShapeDtypeStruct((batch, WINDOW, KV_WIDTH), F32),
        ],
        scratch_shapes=[
            pltpu.VMEM((tm, D_MODEL), BF16),
            pltpu.VMEM((tm, IN_WIDTH), F32),
            pltpu.VMEM((tm, ATTN_WIDTH), BF16),
            pltpu.VMEM((WINDOW + tm, KV_WIDTH), BF16),
            pltpu.VMEM((WINDOW + tm, KV_WIDTH), BF16),
            pltpu.VMEM((WINDOW, KV_WIDTH), BF16),
            pltpu.VMEM((WINDOW, KV_WIDTH), BF16),
            pltpu.VMEM((tm, ATTN_WIDTH), F32),
            pltpu.VMEM((tm, D_MODEL), BF16),
            pltpu.VMEM((N_GMLP_GROUPS, CHUNK, CHUNK), BF16),
        ],
        compiler_params=pltpu.CompilerParams(dimension_semantics=("arbitrary", "arbitrary"), vmem_limit_bytes=VMEM_LIMIT),
        name="prompt_mixer",
    )(sinks, x, shift, scale, gate, w_in, w_o, cos_t, sin_lo, sin_hi, gn_g.reshape(1, GMLP_WIDTH),
      gn_b.reshape(1, GMLP_WIDTH), w_s, b_s.T, out_g.reshape(1, D_MODEL), ln_g.reshape(1, D_MODEL),
      ln_b.reshape(1, D_MODEL))


def _sattn_kernel(qe_ref, p_ref, ck_ref, cv_ref, sink_ref, cos_ref, slo_ref, shi_ref, o_ref, ko_ref, vo_ref):
    cos, slo, shi = cos_ref[...], slo_ref[...], shi_ref[...]
    qe = _rope_lanes(qe_ref[0], cos, slo, shi) * (HEAD_DIM ** -0.5)
    kn = _rope_lanes(p_ref[0, :, Q_END:K_END], cos, slo, shi)
    vn = p_ref[0, :, K_END:V_END]
    ck = ck_ref[0]
    cv = cv_ref[0]
    sink = sink_ref[...]
    s_c = _dot_nt(qe.astype(BF16), ck.astype(BF16))
    key = lax.broadcasted_iota(jnp.int32, s_c.shape, 1)
    s_c = jnp.where(key >= 1, s_c, -jnp.inf)
    s_n = jnp.sum(qe * kn, axis=-1, keepdims=True)
    mx = jnp.maximum(jnp.maximum(jnp.max(s_c, axis=-1, keepdims=True), s_n), sink)
    p_c = jnp.exp(s_c - mx)
    p_n = jnp.exp(s_n - mx)
    denom = jnp.sum(p_c, axis=-1, keepdims=True) + p_n + jnp.exp(sink - mx)
    o = (_dot(p_c.astype(BF16), cv.astype(BF16)) + p_n * vn) / denom
    head = lax.broadcasted_iota(jnp.int32, o.shape, 0)
    lane = lax.broadcasted_iota(jnp.int32, o.shape, 1)
    o_ref[0] = jnp.where(head // GQA_GROUP == lane // HEAD_DIM, o, 0.0)
    row = lax.broadcasted_iota(jnp.int32, ck.shape, 0)
    last = row == WINDOW - 1
    ko_ref[0] = jnp.where(last, kn, pltpu.roll(ck, WINDOW - 1, 0))
    vo_ref[0] = jnp.where(last, vn, pltpu.roll(cv, WINDOW - 1, 0))


def _sattn(qe, p3, ck, cv, sinks, tables):
    b = qe.shape[0]
    cos_t, sin_lo, sin_hi = tables
    tab_spec = pl.BlockSpec((1, LANES), lambda i: (0, 0))
    blk3 = lambda i: (i, 0, 0)
    return pl.pallas_call(
        _sattn_kernel,
        grid=(b,),
        in_specs=[
            pl.BlockSpec((1, N_HEADS, KV_WIDTH), blk3),
            pl.BlockSpec((1, 1, IN_WIDTH), blk3),
            pl.BlockSpec((1, WINDOW, KV_WIDTH), blk3),
            pl.BlockSpec((1, WINDOW, KV_WIDTH), blk3),
            pl.BlockSpec((N_HEADS, 1), lambda i: (0, 0)),
            tab_spec, tab_spec, tab_spec,
        ],
        out_specs=[
            pl.BlockSpec((1, N_HEADS, KV_WIDTH), blk3),
            pl.BlockSpec((1, WINDOW, KV_WIDTH), blk3),
            pl.BlockSpec((1, WINDOW, KV_WIDTH), blk3),
        ],
        out_shape=[
            jax.ShapeDtypeStruct((b, N_HEADS, KV_WIDTH), F32),
            jax.ShapeDtypeStruct((b, WINDOW, KV_WIDTH), F32),
            jax.ShapeDtypeStruct((b, WINDOW, KV_WIDTH), F32),
        ],
        compiler_params=pltpu.CompilerParams(dimension_semantics=("arbitrary",)),
        name="sample_attn",
    )(qe, p3, ck, cv, sinks.reshape(N_HEADS, 1), cos_t, sin_lo, sin_hi)


def _smerge_kernel(attn_ref, p_ref, gng_ref, gnb_ref, w0_ref, b0_ref, og_ref, m_ref, vn_ref):
    og = og_ref[...]
    vn = jnp.concatenate(_gmlp_norm(p_ref[:, U_END:IN_WIDTH], gng_ref[...], gnb_ref[...]), axis=-1)
    vn_ref[...] = vn
    u = _gelu_tanh(p_ref[:, V_END:U_END])
    gm = u * (w0_ref[...] * vn + b0_ref[...])
    m_ref[:, 0:ATTN_WIDTH] = _rms_norm(attn_ref[...], og[:, 0:ATTN_WIDTH]).astype(BF16)
    m_ref[:, ATTN_WIDTH:] = _rms_norm(gm, og[:, ATTN_WIDTH:]).astype(BF16)


def _smerge(attn, p, gn_g, gn_b, w_s, b_s, out_g):
    b = attn.shape[0]
    w0 = jnp.repeat(w_s[:, 0, 0], LANES).reshape(1, GMLP_WIDTH)
    b0 = jnp.repeat(b_s[:, 0], LANES).reshape(1, GMLP_WIDTH)
    return pl.pallas_call(
        _smerge_kernel,
        out_shape=[
            jax.ShapeDtypeStruct((b, D_MODEL), BF16),
            jax.ShapeDtypeStruct((b, GMLP_WIDTH), F32),
        ],
        name="sample_gmlp_merge",
    )(attn, p, gn_g.reshape(1, GMLP_WIDTH), gn_b.reshape(1, GMLP_WIDTH), w0, b0, out_g.reshape(1, D_MODEL))


def kernel(x_prompt, x_sample, cache_k_win, cache_v_win, c_prompt, c_sample, w_ada, b_ada, ln_g, ln_b,
           w_ffn_up, w_ffn_down, w_in, attn_sinks, gmlp_norm_g, gmlp_norm_b, w_spatial, b_spatial,
           out_norm_g, w_o):
    batch, seq, _ = x_prompt.shape
    dec_batch = x_sample.shape[0]
    buf = cache_k_win.shape[2]
    assert seq == SEQ and buf == WINDOW and x_sample.shape[1] == 1 and w_ada.shape[0] == DEPTH == 1

    xp = x_prompt.reshape(batch * seq, D_MODEL)
    xs = x_sample.reshape(dec_batch, D_MODEL)

    n_c = batch + dec_batch
    n_c_pad = ((n_c + 7) // 8) * 8
    c_all = jnp.concatenate([c_prompt, c_sample, jnp.zeros((n_c_pad - n_c, D_MODEL), F32)], axis=0)
    mod = _ada(c_all, w_ada[0], b_ada[0]).reshape(n_c_pad, N_SUB, 3, D_MODEL)
    mod_p = mod[:batch]
    mod_s = mod[batch:n_c]

    def pmod(i, j):
        return mod_p[:, i, j].reshape(batch, 1, D_MODEL)

    def smod(i, j):
        return mod_s[:, i, j].reshape(1, dec_batch, D_MODEL)

    w_in_b = w_in[0].astype(BF16)
    w_o_b = w_o[0].astype(BF16)

    w_up_b = w_ffn_up[0].astype(BF16)
    w_down_b = w_ffn_down[0].astype(BF16)

    tm_p = 512
    ffn_p = functools.partial(_ffn, tm=tm_p, rows_per_mod=seq)
    ffn_s = functools.partial(_ffn, tm=dec_batch, rows_per_mod=dec_batch)

    xp = ffn_p(xp, pmod(0, 0), pmod(0, 1), pmod(0, 2), w_up_b, w_down_b, ln_g[0, 0], ln_b[0, 0], which=0)
    xs = ffn_s(xs, smod(0, 0), smod(0, 1), smod(0, 2), w_up_b, w_down_b, ln_g[0, 0], ln_b[0, 0], which=0)

    xp, k_p, v_p = _pmix(xp, pmod(1, 0), pmod(1, 1), pmod(1, 2), w_in_b, w_o_b, attn_sinks[0],
                         _rope_tables(jnp.arange(seq)), gmlp_norm_g[0], gmlp_norm_b[0], w_spatial[0], b_spatial[0],
                         out_norm_g[0], ln_g[0, 1], ln_b[0, 1], batch=batch, tm=MIX_TILE)

    ps = _inproj(xs, smod(1, 0), smod(1, 1), w_in_b, tm=dec_batch, rows_per_mod=dec_batch)
    eye = jnp.eye(N_KV_HEADS, dtype=F32)
    q5 = ps[:, :Q_END].reshape(dec_batch, N_KV_HEADS, GQA_GROUP, 1, HEAD_DIM)
    qe = (q5 * eye[None, :, None, :, None]).reshape(dec_batch, N_HEADS, KV_WIDTH)
    ck = cache_k_win[0].reshape(dec_batch, buf, KV_WIDTH)
    cv = cache_v_win[0].reshape(dec_batch, buf, KV_WIDTH)
    oe, k_s, v_s = _sattn(qe, ps.reshape(dec_batch, 1, IN_WIDTH), ck, cv, attn_sinks[0],
                          _rope_tables(jnp.full((1,), PAST_LEN)))
    attn_s = oe.reshape(dec_batch, N_KV_HEADS, GQA_GROUP, N_KV_HEADS, HEAD_DIM).sum(axis=3).reshape(dec_batch, ATTN_WIDTH)
    merged_s, vn_s = _smerge(attn_s, ps, gmlp_norm_g[0], gmlp_norm_b[0], w_spatial[0], b_spatial[0], out_norm_g[0])
    xs = _outproj(merged_s, xs, smod(1, 2), w_o_b, ln_g[0, 1], ln_b[0, 1], tm=dec_batch, rows_per_mod=dec_batch)

    xp = ffn_p(xp, pmod(2, 0), pmod(2, 1), pmod(2, 2), w_up_b, w_down_b, ln_g[0, 2], ln_b[0, 2], which=1)
    xs = ffn_s(xs, smod(2, 0), smod(2, 1), smod(2, 2), w_up_b, w_down_b, ln_g[0, 2], ln_b[0, 2], which=1)

    return (
        xp.reshape(batch, seq, D_MODEL),
        xs.reshape(dec_batch, 1, D_MODEL),
        k_p.reshape(1, batch, WINDOW, N_KV_HEADS, HEAD_DIM),
        v_p.reshape(1, batch, WINDOW, N_KV_HEADS, HEAD_DIM),
        k_s.reshape(1, dec_batch, buf, N_KV_HEADS, HEAD_DIM),
        v_s.reshape(1, dec_batch, buf, N_KV_HEADS, HEAD_DIM),
        vn_s.reshape(1, dec_batch, 1, GMLP_WIDTH),
    )
```

```python
import functools

import jax
import jax.numpy as jnp
import numpy as np
from jax import lax
from jax.experimental import pallas as pl
from jax.experimental.pallas import tpu as pltpu

D_MODEL = 2048
SEQ = 2048
PAST_LEN = 16384
ATTN_WIDTH = 1024
GMLP_WIDTH = 1024
HEAD_DIM = 64
N_HEADS = 16
N_KV_HEADS = 4
GQA_GROUP = 4
KV_WIDTH = 256
WINDOW = 128
ROPE_THETA = 500000.0
ROPE_DIM = 16
CHUNK = 128
N_GMLP_GROUPS = 8
D_FF = 5504
N_SUB = 3
DEPTH = 1
ALPHA = (2.0 * DEPTH) ** 0.25
LN_EPS = 1e-5
Q_END = ATTN_WIDTH
K_END = Q_END + KV_WIDTH
V_END = K_END + KV_WIDTH
U_END = V_END + GMLP_WIDTH
IN_WIDTH = U_END + GMLP_WIDTH

LANES = 128
MXU_COLS = 256
FF_TILE = 512
N_FF_STEPS = -(-D_FF // FF_TILE)
FF_LAST_OFF = D_FF - FF_TILE
FF_OVERLAP = N_FF_STEPS * FF_TILE - D_FF
OUT_TILE = 512
N_OUT_STEPS = D_MODEL // OUT_TILE
MIX_TILE = 256
PROJ_TILE = 512
VMEM_LIMIT = 56 * 1024 * 1024

BF16 = jnp.bfloat16
F32 = jnp.float32


def _dot(a, b):
    return jnp.dot(a, b, preferred_element_type=F32)


def _dot_nt(a, b):
    return lax.dot_general(a, b, (((1,), (1,)), ((), ())), preferred_element_type=F32)


def _layer_norm(y, g, b):
    mu = jnp.mean(y, axis=-1, keepdims=True)
    d = y - mu
    var = jnp.mean(d * d, axis=-1, keepdims=True)
    return d * lax.rsqrt(var + LN_EPS) * g + b


def _rms_norm(y, g):
    return y * lax.rsqrt(jnp.mean(y * y, axis=-1, keepdims=True) + LN_EPS) * g


def _gelu_tanh(x):
    c = np.float32(np.sqrt(2.0 / np.pi))
    return 0.5 * x * (1.0 + jnp.tanh(c * (x + 0.044715 * (x * x * x))))


def _rope_lanes(t, cos, sin_lo, sin_hi):
    pieces = []
    for c in range(t.shape[-1] // LANES):
        x = t[:, c * LANES:(c + 1) * LANES]
        pieces.append(x * cos + pltpu.roll(x, 8, 1) * sin_hi + pltpu.roll(x, LANES - 8, 1) * sin_lo)
    return pieces[0] if len(pieces) == 1 else jnp.concatenate(pieces, axis=-1)


def _ada_kernel(c_ref, w_ref, b_ref, o_ref):
    c = c_ref[...]
    h = (c * jax.nn.sigmoid(c)).astype(BF16)
    o_ref[...] = _dot(h, w_ref[...].astype(BF16)) + b_ref[...]


def _ada(c, w_ada, b_ada):
    rows = c.shape[0]
    n = w_ada.shape[1]
    tn = 1024
    return pl.pallas_call(
        _ada_kernel,
        grid=(n // tn,),
        in_specs=[
            pl.BlockSpec((rows, D_MODEL), lambda j: (0, 0)),
            pl.BlockSpec((D_MODEL, tn), lambda j: (0, j)),
            pl.BlockSpec((1, tn), lambda j: (0, j)),
        ],
        out_specs=pl.BlockSpec((rows, tn), lambda j: (0, j)),
        out_shape=jax.ShapeDtypeStruct((rows, n), F32),
        compiler_params=pltpu.CompilerParams(dimension_semantics=("arbitrary",), vmem_limit_bytes=VMEM_LIMIT),
        name="ada_mod",
    )(c, w_ada, b_ada.reshape(1, n))


def _ff_offset(step):
    return jnp.minimum(step * FF_TILE, FF_LAST_OFF)


def _ffn_kernel(n_side, *refs):
    x_ref, shift_ref, scale_ref, gate_ref, wv_ref, wg_ref, wd_ref, g_ref, b_ref = refs[:9]
    side_in = refs[9:9 + n_side]
    o_ref = refs[9 + n_side]
    side_out = refs[10 + n_side:10 + 2 * n_side]
    xm_ref, a_ref, y_ref = refs[10 + 2 * n_side:]
    s = pl.program_id(1)

    def cast_side():
        for src, dst in zip(side_in, side_out):
            dst[...] = src[...].astype(BF16)

    @pl.when(s == 0)
    def _():
        xm_ref[...] = (x_ref[...] * (1.0 + scale_ref[0]) + shift_ref[0]).astype(BF16)

    @pl.when(s < N_FF_STEPS)
    def _():
        cast_side()
        xm = xm_ref[...]
        for c in range(FF_TILE // MXU_COLS):
            sl = slice(c * MXU_COLS, (c + 1) * MXU_COLS)
            hv = _dot(xm, wv_ref[:, sl])
            hg = _dot(xm, wg_ref[:, sl])
            a = hg * jax.nn.sigmoid(hg) * hv
            if c * MXU_COLS < FF_OVERLAP:
                col = lax.broadcasted_iota(jnp.int32, a.shape, 1) + c * MXU_COLS
                a = jnp.where(jnp.logical_and(s == N_FF_STEPS - 1, col < FF_OVERLAP), 0.0, a)
            a_ref[s, :, sl] = a.astype(BF16)

    @pl.when(s >= N_FF_STEPS)
    def _():
        cast_side()
        acc = None
        for f in range(N_FF_STEPS):
            r0 = min(f * FF_TILE, FF_LAST_OFF)
            part = _dot(a_ref[f], wd_ref[r0:r0 + FF_TILE, :])
            acc = part if acc is None else acc + part
        y_ref[s - N_FF_STEPS] = acc

    @pl.when(s == pl.num_programs(1) - 1)
    def _():
        mixed = jnp.concatenate([y_ref[n] for n in range(N_OUT_STEPS)], axis=-1)
        y = ALPHA * x_ref[...] + 0.5 * gate_ref[0] * mixed
        o_ref[...] = _layer_norm(y, g_ref[...], b_ref[...])


def _ffn(x, shift, scale, gate, w_up, w_down, ln_g, ln_b, *, tm, rows_per_mod, side=()):
    m = x.shape[0]
    r = shift.shape[1]
    n_steps = N_FF_STEPS + N_OUT_STEPS
    tiles_per_mod = rows_per_mod // tm
    mod_spec = pl.BlockSpec((1, r, D_MODEL), lambda i, s: (i // tiles_per_mod, 0, 0))
    side_in_specs, side_out_specs, side_shapes = [], [], []
    for arr, lead, rows in side:
        n_rows, n_cols = arr.shape[-2:]
        n_blocks = n_rows // rows
        assert n_blocks * rows == n_rows and n_blocks <= (m // tm) * n_steps
        blk = lambda i, s, n_blocks=n_blocks: jnp.minimum(i * n_steps + s, n_blocks - 1)
        side_in_specs.append(pl.BlockSpec((None,) * len(lead) + (rows, n_cols),
                                          lambda i, s, lead=lead, blk=blk: (*lead, blk(i, s), 0)))
        side_out_specs.append(pl.BlockSpec((rows, n_cols), lambda i, s, blk=blk: (blk(i, s), 0)))
        side_shapes.append(jax.ShapeDtypeStruct((n_rows, n_cols), BF16))
    wd_block = lambda i, s: (0, jnp.where(s < N_FF_STEPS, N_OUT_STEPS - 1, s - N_FF_STEPS))
    outs = pl.pallas_call(
        functools.partial(_ffn_kernel, len(side)),
        grid=(m // tm, n_steps),
        in_specs=[
            pl.BlockSpec((tm, D_MODEL), lambda i, s: (i, 0)),
            mod_spec, mod_spec, mod_spec,
            pl.BlockSpec((pl.Element(D_MODEL), pl.Element(FF_TILE)),
                         lambda i, s: (0, pl.multiple_of(_ff_offset(s), LANES))),
            pl.BlockSpec((pl.Element(D_MODEL), pl.Element(FF_TILE)),
                         lambda i, s: (0, pl.multiple_of(D_FF + _ff_offset(s), LANES))),
            pl.BlockSpec((D_FF, OUT_TILE), wd_block),
            pl.BlockSpec((1, D_MODEL), lambda i, s: (0, 0)),
            pl.BlockSpec((1, D_MODEL), lambda i, s: (0, 0)),
            *side_in_specs,
        ],
        out_specs=[pl.BlockSpec((tm, D_MODEL), lambda i, s: (i, 0)), *side_out_specs],
        out_shape=[jax.ShapeDtypeStruct((m, D_MODEL), F32), *side_shapes],
        scratch_shapes=[
            pltpu.VMEM((tm, D_MODEL), BF16),
            pltpu.VMEM((N_FF_STEPS, tm, FF_TILE), BF16),
            pltpu.VMEM((N_OUT_STEPS, tm, OUT_TILE), F32),
        ],
        compiler_params=pltpu.CompilerParams(dimension_semantics=("arbitrary", "arbitrary"), vmem_limit_bytes=VMEM_LIMIT),
        name="swiglu_ln",
    )(x, shift, scale, gate, w_up, w_up, w_down, ln_g.reshape(1, D_MODEL), ln_b.reshape(1, D_MODEL),
      *[arr for arr, _, _ in side])
    return outs if side else outs[0]


def _inproj_kernel(x_ref, shift_ref, scale_ref, w_ref, o_ref, xm_ref):
    @pl.when(pl.program_id(1) == 0)
    def _():
        xm_ref[...] = (x_ref[...] * (1.0 + scale_ref[0]) + shift_ref[0]).astype(BF16)

    o_ref[...] = _dot(xm_ref[...], w_ref[...])


def _inproj(x, shift, scale, w_in, *, tm, rows_per_mod):
    m = x.shape[0]
    r = shift.shape[1]
    tn = 512
    tiles_per_mod = rows_per_mod // tm
    mod_spec = pl.BlockSpec((1, r, D_MODEL), lambda i, j: (i // tiles_per_mod, 0, 0))
    return pl.pallas_call(
        _inproj_kernel,
        grid=(m // tm, IN_WIDTH // tn),
        in_specs=[
            pl.BlockSpec((tm, D_MODEL), lambda i, j: (i, 0)),
            mod_spec, mod_spec,
            pl.BlockSpec((D_MODEL, tn), lambda i, j: (0, j)),
        ],
        out_specs=pl.BlockSpec((tm, tn), lambda i, j: (i, j)),
        out_shape=jax.ShapeDtypeStruct((m, IN_WIDTH), F32),
        scratch_shapes=[pltpu.VMEM((tm, D_MODEL), BF16)],
        compiler_params=pltpu.CompilerParams(dimension_semantics=("arbitrary", "arbitrary"), vmem_limit_bytes=VMEM_LIMIT),
        name="mixer_inproj",
    )(x, shift, scale, w_in)


def _outproj_kernel(m_ref, x_ref, gate_ref, w_ref, g_ref, b_ref, o_ref):
    mixed = _dot(m_ref[...], w_ref[...])
    y = ALPHA * x_ref[...] + gate_ref[0] * mixed
    o_ref[...] = _layer_norm(y, g_ref[...], b_ref[...])


def _outproj(merged, x, gate, w_o, ln_g, ln_b, *, tm, rows_per_mod):
    m = x.shape[0]
    r = gate.shape[1]
    tiles_per_mod = rows_per_mod // tm
    return pl.pallas_call(
        _outproj_kernel,
        grid=(m // tm,),
        in_specs=[
            pl.BlockSpec((tm, D_MODEL), lambda i: (i, 0)),
            pl.BlockSpec((tm, D_MODEL), lambda i: (i, 0)),
            pl.BlockSpec((1, r, D_MODEL), lambda i: (i // tiles_per_mod, 0, 0)),
            pl.BlockSpec((D_MODEL, D_MODEL), lambda i: (0, 0)),
            pl.BlockSpec((1, D_MODEL), lambda i: (0, 0)),
            pl.BlockSpec((1, D_MODEL), lambda i: (0, 0)),
        ],
        out_specs=pl.BlockSpec((tm, D_MODEL), lambda i: (i, 0)),
        out_shape=jax.ShapeDtypeStruct((m, D_MODEL), F32),
        compiler_params=pltpu.CompilerParams(dimension_semantics=("arbitrary",), vmem_limit_bytes=VMEM_LIMIT),
        name="mixer_outproj_ln",
    )(merged, x, gate, w_o, ln_g.reshape(1, D_MODEL), ln_b.reshape(1, D_MODEL))


def _gmlp_norm(pv, gng, gnb):
    v = _gelu_tanh(pv)
    outs = []
    for g in range(N_GMLP_GROUPS):
        sl = slice(g * LANES, (g + 1) * LANES)
        outs.append(_layer_norm(v[:, sl], gng[:, sl], gnb[:, sl]))
    return outs


def _band_attention(q_s, kk_s, vv_s, attn_s, sinks_ref, wb, first_block):
    rows = GQA_GROUP * WINDOW
    r0 = wb * WINDOW
    qi = lax.broadcasted_iota(jnp.int32, (rows, WINDOW), 0) & (WINDOW - 1)
    kj = lax.broadcasted_iota(jnp.int32, (rows, WINDOW), 1)
    own = kj <= qi
    ones = jnp.ones((2 * WINDOW, LANES), BF16)
    v2 = vv_s[r0:r0 + 2 * WINDOW, :]
    for g in range(N_KV_HEADS):
        heads = [GQA_GROUP * g + mm for mm in range(GQA_GROUP)]
        qg = jnp.concatenate([q_s[r0:r0 + WINDOW, h * HEAD_DIM:(h + 1) * HEAD_DIM] for h in heads], axis=0)
        sink = jnp.concatenate([jnp.full((WINDOW, 1), sinks_ref[h], F32) for h in heads], axis=0)
        kg = kk_s[r0:r0 + 2 * WINDOW, g * HEAD_DIM:(g + 1) * HEAD_DIM]
        s = _dot_nt(qg, kg)
        s_prev, s_own = s[:, 0:WINDOW], s[:, WINDOW:2 * WINDOW]
        c = jnp.where(own, s_own, s_prev)
        if first_block is not None:
            c = jnp.where(kj <= jnp.where(first_block, qi, WINDOW), c, -jnp.inf)
        mx = jnp.maximum(jnp.max(c, axis=-1, keepdims=True), sink)
        p = jnp.exp(c - mx)
        pcat = jnp.concatenate([jnp.where(own, 0.0, p), jnp.where(own, p, 0.0)], axis=-1).astype(BF16)
        o = _dot(pcat, v2)
        den = _dot(pcat, ones) + jnp.exp(sink - mx)
        half = LANES // HEAD_DIM
        on = o[:, (g // half) * LANES:(g // half + 1) * LANES] / den
        off = (g % half) * HEAD_DIM
        for mm, h in enumerate(heads):
            attn_s[r0:r0 + WINDOW, h * HEAD_DIM:(h + 1) * HEAD_DIM] = on[mm * WINDOW:(mm + 1) * WINDOW, off:off + HEAD_DIM]


def _pmix_kernel(sinks_ref, x_ref, shift_ref, scale_ref, gate_ref, win_ref, wo_ref, cos_ref, slo_ref, shi_ref,
                 gng_ref, gnb_ref, ws_ref, bs_ref, og_ref, lng_ref, lnb_ref,
                 o_ref, k_ref, v_ref, xm_s, p_s, q_s, kk_s, vv_s, kp_s, vp_s, attn_s, m_s, wm_s):
    t = pl.program_id(1)
    tm = x_ref.shape[0]
    nblk = tm // WINDOW

    @pl.when(jnp.logical_and(pl.program_id(0) == 0, t == 0))
    def _():
        row = lax.broadcasted_iota(jnp.int32, (CHUNK, CHUNK), 0)
        col = lax.broadcasted_iota(jnp.int32, (CHUNK, CHUNK), 1)
        for g in range(N_GMLP_GROUPS):
            wm_s[g] = jnp.where(col <= row, ws_ref[g], 0.0).astype(BF16)

    @pl.when(t == 0)
    def _():
        kp_s[...] = jnp.zeros((WINDOW, KV_WIDTH), BF16)
        vp_s[...] = jnp.zeros((WINDOW, KV_WIDTH), BF16)

    xm_s[...] = (x_ref[...] * (1.0 + scale_ref[0]) + shift_ref[0]).astype(BF16)
    for c in range(IN_WIDTH // PROJ_TILE):
        sl = slice(c * PROJ_TILE, (c + 1) * PROJ_TILE)
        p_s[:, sl] = _dot(xm_s[...], win_ref[:, sl])

    cos, slo, shi = cos_ref[...], slo_ref[...], shi_ref[...]
    q_s[...] = (_rope_lanes(p_s[:, 0:Q_END], cos, slo, shi) * (HEAD_DIM ** -0.5)).astype(BF16)
    k = _rope_lanes(p_s[:, Q_END:K_END], cos, slo, shi)
    v = p_s[:, K_END:V_END]
    k_ref[0] = k[tm - WINDOW:tm]
    v_ref[0] = v[tm - WINDOW:tm]
    kb, vb = k.astype(BF16), v.astype(BF16)
    kk_s[0:WINDOW, :] = kp_s[...]
    vv_s[0:WINDOW, :] = vp_s[...]
    kk_s[WINDOW:WINDOW + tm, :] = kb
    vv_s[WINDOW:WINDOW + tm, :] = vb
    kp_s[...] = kb[tm - WINDOW:tm]
    vp_s[...] = vb[tm - WINDOW:tm]

    for wb in range(nblk):
        _band_attention(q_s, kk_s, vv_s, attn_s, sinks_ref, wb, (t == 0) if wb == 0 else None)

    vn = _gmlp_norm(p_s[:, U_END:IN_WIDTH], gng_ref[...], gnb_ref[...])
    gated = []
    for g in range(N_GMLP_GROUPS):
        u = _gelu_tanh(p_s[:, V_END + g * LANES:V_END + (g + 1) * LANES])
        vg = vn[g].astype(BF16)
        cols = jnp.concatenate([vg[wb * CHUNK:(wb + 1) * CHUNK] for wb in range(nblk)], axis=-1)
        mix = _dot(wm_s[g], cols)
        mix = jnp.concatenate([mix[:, wb * LANES:(wb + 1) * LANES] for wb in range(nblk)], axis=0)
        bias = jnp.concatenate([bs_ref[:, g:g + 1]] * nblk, axis=0)
        gated.append(u * (mix + bias))
    gm = jnp.concatenate(gated, axis=-1)
    og = og_ref[...]
    m_s[:, 0:ATTN_WIDTH] = _rms_norm(attn_s[...], og[:, 0:ATTN_WIDTH]).astype(BF16)
    m_s[:, ATTN_WIDTH:] = _rms_norm(gm, og[:, ATTN_WIDTH:]).astype(BF16)

    for c in range(D_MODEL // PROJ_TILE):
        sl = slice(c * PROJ_TILE, (c + 1) * PROJ_TILE)
        o_ref[:, sl] = ALPHA * x_ref[:, sl] + gate_ref[0, :, sl] * _dot(m_s[...], wo_ref[:, sl])
    o_ref[...] = _layer_norm(o_ref[...], lng_ref[...], lnb_ref[...])


def _rope_tables(pos):
    half = ROPE_DIM // 2
    inv = ROPE_THETA ** (-(jnp.arange(half, dtype=F32) * 2.0) / ROPE_DIM)
    ang = pos.astype(F32)[:, None] * inv[None, :]
    cos, sin = jnp.cos(ang), jnp.sin(ang)
    n = pos.shape[0]
    one = jnp.ones((n, HEAD_DIM - ROPE_DIM), F32)
    zero = jnp.zeros((n, HEAD_DIM - ROPE_DIM), F32)
    z8 = jnp.zeros((n, half), F32)
    cos_t = jnp.concatenate([cos, cos, one], axis=-1)
    sin_lo = jnp.concatenate([-sin, z8, zero], axis=-1)
    sin_hi = jnp.concatenate([z8, sin, zero], axis=-1)
    rep = LANES // HEAD_DIM
    return jnp.tile(cos_t, (1, rep)), jnp.tile(sin_lo, (1, rep)), jnp.tile(sin_hi, (1, rep))


def _pmix(x, shift, scale, gate, w_in, w_o, sinks, tables, gn_g, gn_b, w_s, b_s, out_g, ln_g, ln_b, *, batch, tm):
    nt = SEQ // tm
    cos_t, sin_lo, sin_hi = tables
    tab_spec = pl.BlockSpec((tm, LANES), lambda b, j: (j, 0))
    full2 = lambda b, j: (0, 0)
    mod_spec = pl.BlockSpec((1, 1, D_MODEL), lambda b, j: (b, 0, 0))
    resident = dict(pipeline_mode=pl.Buffered(1))
    return pl.pallas_call(
        _pmix_kernel,
        grid=(batch, nt),
        in_specs=[
            pl.BlockSpec(memory_space=pltpu.SMEM),
            pl.BlockSpec((tm, D_MODEL), lambda b, j: (b * nt + j, 0)),
            mod_spec, mod_spec, mod_spec,
            pl.BlockSpec((D_MODEL, IN_WIDTH), full2, **resident),
            pl.BlockSpec((D_MODEL, D_MODEL), full2, **resident),
            tab_spec, tab_spec, tab_spec,
            pl.BlockSpec((1, GMLP_WIDTH), full2),
            pl.BlockSpec((1, GMLP_WIDTH), full2),
            pl.BlockSpec((N_GMLP_GROUPS, CHUNK, CHUNK), lambda b, j: (0, 0, 0)),
            pl.BlockSpec((CHUNK, N_GMLP_GROUPS), full2),
            pl.BlockSpec((1, D_MODEL), full2),
            pl.BlockSpec((1, D_MODEL), full2),
            pl.BlockSpec((1, D_MODEL), full2),
        ],
        out_specs=[
            pl.BlockSpec((tm, D_MODEL), lambda b, j: (b * nt + j, 0)),
            pl.BlockSpec((1, WINDOW, KV_WIDTH), lambda b, j: (b, 0, 0)),
            pl.BlockSpec((1, WINDOW, KV_WIDTH), lambda b, j: (b, 0, 0)),
        ],
        out_shape=[
            jax.ShapeDtypeStruct((batch * SEQ, D_MODEL), F32),
            jax.ShapeDtypeStruct((batch, WINDOW, KV_WIDTH), F32),
            jax.ShapeDtypeStruct((batch, WINDOW, KV_WIDTH), F32),
        ],
        scratch_shapes=[
            pltpu.VMEM((tm, D_MODEL), BF16),
            pltpu.VMEM((tm, IN_WIDTH), F32),
            pltpu.VMEM((tm, ATTN_WIDTH), BF16),
            pltpu.VMEM((WINDOW + tm, KV_WIDTH), BF16),
            pltpu.VMEM((WINDOW + tm, KV_WIDTH), BF16),
            pltpu.VMEM((WINDOW, KV_WIDTH), BF16),
            pltpu.VMEM((WINDOW, KV_WIDTH), BF16),
            pltpu.VMEM((tm, ATTN_WIDTH), F32),
            pltpu.VMEM((tm, D_MODEL), BF16),
            pltpu.VMEM((N_GMLP_GROUPS, CHUNK, CHUNK), BF16),
        ],
        compiler_params=pltpu.CompilerParams(dimension_semantics=("arbitrary", "arbitrary"), vmem_limit_bytes=VMEM_LIMIT),
        name="prompt_mixer",
    )(sinks, x, shift, scale, gate, w_in, w_o, cos_t, sin_lo, sin_hi, gn_g.reshape(1, GMLP_WIDTH),
      gn_b.reshape(1, GMLP_WIDTH), w_s, b_s.T, out_g.reshape(1, D_MODEL), ln_g.reshape(1, D_MODEL),
      ln_b.reshape(1, D_MODEL))


def _sattn_kernel(qe_ref, p_ref, ck_ref, cv_ref, sink_ref, cos_ref, slo_ref, shi_ref, o_ref, ko_ref, vo_ref):
    cos, slo, shi = cos_ref[...], slo_ref[...], shi_ref[...]
    qe = _rope_lanes(qe_ref[0], cos, slo, shi) * (HEAD_DIM ** -0.5)
    kn = _rope_lanes(p_ref[0, :, Q_END:K_END], cos, slo, shi)
    vn = p_ref[0, :, K_END:V_END]
    ck = ck_ref[0]
    cv = cv_ref[0]
    sink = sink_ref[...]
    s_c = _dot_nt(qe.astype(BF16), ck.astype(BF16))
    key = lax.broadcasted_iota(jnp.int32, s_c.shape, 1)
    s_c = jnp.where(key >= 1, s_c, -jnp.inf)
    s_n = jnp.sum(qe * kn, axis=-1, keepdims=True)
    mx = jnp.maximum(jnp.maximum(jnp.max(s_c, axis=-1, keepdims=True), s_n), sink)
    p_c = jnp.exp(s_c - mx)
    p_n = jnp.exp(s_n - mx)
    denom = jnp.sum(p_c, axis=-1, keepdims=True) + p_n + jnp.exp(sink - mx)
    o = (_dot(p_c.astype(BF16), cv.astype(BF16)) + p_n * vn) / denom
    head = lax.broadcasted_iota(jnp.int32, o.shape, 0)
    lane = lax.broadcasted_iota(jnp.int32, o.shape, 1)
    o_ref[0] = jnp.where(head // GQA_GROUP == lane // HEAD_DIM, o, 0.0)
    row = lax.broadcasted_iota(jnp.int32, ck.shape, 0)
    last = row == WINDOW - 1
    ko_ref[0] = jnp.where(last, kn, pltpu.roll(ck, WINDOW - 1, 0))
    vo_ref[0] = jnp.where(last, vn, pltpu.roll(cv, WINDOW - 1, 0))


def _sattn(qe, p3, ck, cv, sinks, tables):
    b = qe.shape[0]
    cos_t, sin_lo, sin_hi = tables
    tab_spec = pl.BlockSpec((1, LANES), lambda i: (0, 0))
    blk3 = lambda i: (i, 0, 0)
    return pl.pallas_call(
        _sattn_kernel,
        grid=(b,),
        in_specs=[
            pl.BlockSpec((1, N_HEADS, KV_WIDTH), blk3),
            pl.BlockSpec((1, 1, IN_WIDTH), blk3),
            pl.BlockSpec((1, WINDOW, KV_WIDTH), blk3),
            pl.BlockSpec((1, WINDOW, KV_WIDTH), blk3),
            pl.BlockSpec((N_HEADS, 1), lambda i: (0, 0)),
            tab_spec, tab_spec, tab_spec,
        ],
        out_specs=[
            pl.BlockSpec((1, N_HEADS, KV_WIDTH), blk3),
            pl.BlockSpec((1, WINDOW, KV_WIDTH), blk3),
            pl.BlockSpec((1, WINDOW, KV_WIDTH), blk3),
        ],
        out_shape=[
            jax.ShapeDtypeStruct((b, N_HEADS, KV_WIDTH), F32),
            jax.ShapeDtypeStruct((b, WINDOW, KV_WIDTH), F32),
            jax.ShapeDtypeStruct((b, WINDOW, KV_WIDTH), F32),
        ],
        compiler_params=pltpu.CompilerParams(dimension_semantics=("arbitrary",)),
        name="sample_attn",
    )(qe, p3, ck, cv, sinks.reshape(N_HEADS, 1), cos_t, sin_lo, sin_hi)


def _smerge_kernel(attn_ref, p_ref, gng_ref, gnb_ref, w0_ref, b0_ref, og_ref, m_ref, vn_ref):
    og = og_ref[...]
    vn = jnp.concatenate(_gmlp_norm(p_ref[:, U_END:IN_WIDTH], gng_ref[...], gnb_ref[...]), axis=-1)
    vn_ref[...] = vn
    u = _gelu_tanh(p_ref[:, V_END:U_END])
    gm = u * (w0_ref[...] * vn + b0_ref[...])
    m_ref[:, 0:ATTN_WIDTH] = _rms_norm(attn_ref[...], og[:, 0:ATTN_WIDTH]).astype(BF16)
    m_ref[:, ATTN_WIDTH:] = _rms_norm(gm, og[:, ATTN_WIDTH:]).astype(BF16)


def _smerge(attn, p, gn_g, gn_b, w_s, b_s, out_g):
    b = attn.shape[0]
    w0 = jnp.repeat(w_s[:, 0, 0], LANES).reshape(1, GMLP_WIDTH)
    b0 = jnp.repeat(b_s[:, 0], LANES).reshape(1, GMLP_WIDTH)
    return pl.pallas_call(
        _smerge_kernel,
        out_shape=[
            jax.ShapeDtypeStruct((b, D_MODEL), BF16),
            jax.ShapeDtypeStruct((b, GMLP_WIDTH), F32),
        ],
        name="sample_gmlp_merge",
    )(attn, p, gn_g.reshape(1, GMLP_WIDTH), gn_b.reshape(1, GMLP_WIDTH), w0, b0, out_g.reshape(1, D_MODEL))


def kernel(x_prompt, x_sample, cache_k_win, cache_v_win, c_prompt, c_sample, w_ada, b_ada, ln_g, ln_b,
           w_ffn_up, w_ffn_down, w_in, attn_sinks, gmlp_norm_g, gmlp_norm_b, w_spatial, b_spatial,
           out_norm_g, w_o):
    batch, seq, _ = x_prompt.shape
    dec_batch = x_sample.shape[0]
    buf = cache_k_win.shape[2]
    assert seq == SEQ and buf == WINDOW and x_sample.shape[1] == 1 and w_ada.shape[0] == DEPTH == 1

    xp = x_prompt.reshape(batch * seq, D_MODEL)
    xs = x_sample.reshape(dec_batch, D_MODEL)

    n_c = batch + dec_batch
    n_c_pad = ((n_c + 7) // 8) * 8
    c_all = jnp.concatenate([c_prompt, c_sample, jnp.zeros((n_c_pad - n_c, D_MODEL), F32)], axis=0)
    mod = _ada(c_all, w_ada[0], b_ada[0]).reshape(n_c_pad, N_SUB, 3, D_MODEL)
    mod_p = mod[:batch]
    mod_s = mod[batch:n_c]

    def pmod(i, j):
        return mod_p[:, i, j].reshape(batch, 1, D_MODEL)

    def smod(i, j):
        return mod_s[:, i, j].reshape(1, dec_batch, D_MODEL)

    tm_p = 512
    ffn_p = functools.partial(_ffn, tm=tm_p, rows_per_mod=seq)
    ffn_s = functools.partial(_ffn, tm=dec_batch, rows_per_mod=dec_batch)

    w_up0 = w_ffn_up[0, 0].astype(BF16)
    w_down0 = w_ffn_down[0, 0].astype(BF16)
    later = ((w_ffn_up, (0, 1), 16), (w_ffn_down, (0, 1), 32), (w_in, (0,), 16), (w_o, (0,), 16))
    xp, w_up1, w_down1, w_in_b, w_o_b = ffn_p(xp, pmod(0, 0), pmod(0, 1), pmod(0, 2), w_up0, w_down0,
                                              ln_g[0, 0], ln_b[0, 0], side=later)
    xs = ffn_s(xs, smod(0, 0), smod(0, 1), smod(0, 2), w_up0, w_down0, ln_g[0, 0], ln_b[0, 0])

    xp, k_p, v_p = _pmix(xp, pmod(1, 0), pmod(1, 1), pmod(1, 2), w_in_b, w_o_b, attn_sinks[0],
                         _rope_tables(jnp.arange(seq)), gmlp_norm_g[0], gmlp_norm_b[0], w_spatial[0], b_spatial[0],
                         out_norm_g[0], ln_g[0, 1], ln_b[0, 1], batch=batch, tm=MIX_TILE)

    ps = _inproj(xs, smod(1, 0), smod(1, 1), w_in_b, tm=dec_batch, rows_per_mod=dec_batch)
    eye = jnp.eye(N_KV_HEADS, dtype=F32)
    q5 = ps[:, :Q_END].reshape(dec_batch, N_KV_HEADS, GQA_GROUP, 1, HEAD_DIM)
    qe = (q5 * eye[None, :, None, :, None]).reshape(dec_batch, N_HEADS, KV_WIDTH)
    ck = cache_k_win[0].reshape(dec_batch, buf, KV_WIDTH)
    cv = cache_v_win[0].reshape(dec_batch, buf, KV_WIDTH)
    oe, k_s, v_s = _sattn(qe, ps.reshape(dec_batch, 1, IN_WIDTH), ck, cv, attn_sinks[0],
                          _rope_tables(jnp.full((1,), PAST_LEN)))
    attn_s = oe.reshape(dec_batch, N_KV_HEADS, GQA_GROUP, N_KV_HEADS, HEAD_DIM).sum(axis=3).reshape(dec_batch, ATTN_WIDTH)
    merged_s, vn_s = _smerge(attn_s, ps, gmlp_norm_g[0], gmlp_norm_b[0], w_spatial[0], b_spatial[0], out_norm_g[0])
    xs = _outproj(merged_s, xs, smod(1, 2), w_o_b, ln_g[0, 1], ln_b[0, 1], tm=dec_batch, rows_per_mod=dec_batch)

    xp = ffn_p(xp, pmod(2, 0), pmod(2, 1), pmod(2, 2), w_up1, w_down1, ln_g[0, 2], ln_b[0, 2])
    xs = ffn_s(xs, smod(2, 0), smod(2, 1), smod(2, 2), w_up1, w_down1, ln_g[0, 2], ln_b[0, 2])

    return (
        xp.reshape(batch, seq, D_MODEL),
        xs.reshape(dec_batch, 1, D_MODEL),
        k_p.reshape(1, batch, WINDOW, N_KV_HEADS, HEAD_DIM),
        v_p.reshape(1, batch, WINDOW, N_KV_HEADS, HEAD_DIM),
        k_s.reshape(1, dec_batch, buf, N_KV_HEADS, HEAD_DIM),
        v_s.reshape(1, dec_batch, buf, N_KV_HEADS, HEAD_DIM),
        vn_s.reshape(1, dec_batch, 1, GMLP_WIDTH),
    )
```

```python
import functools

import jax
import jax.numpy as jnp
import numpy as np
from jax import lax
from jax.experimental import pallas as pl
from jax.experimental.pallas import tpu as pltpu

D_MODEL = 2048
SEQ = 2048
PAST_LEN = 16384
ATTN_WIDTH = 1024
GMLP_WIDTH = 1024
HEAD_DIM = 64
N_HEADS = 16
N_KV_HEADS = 4
GQA_GROUP = 4
KV_WIDTH = 256
WINDOW = 128
ROPE_THETA = 500000.0
ROPE_DIM = 16
CHUNK = 128
N_GMLP_GROUPS = 8
D_FF = 5504
N_SUB = 3
DEPTH = 1
ALPHA = (2.0 * DEPTH) ** 0.25
LN_EPS = 1e-5
Q_END = ATTN_WIDTH
K_END = Q_END + KV_WIDTH
V_END = K_END + KV_WIDTH
U_END = V_END + GMLP_WIDTH
IN_WIDTH = U_END + GMLP_WIDTH

LANES = 128
MXU_COLS = 256
FF_TILE = 512
N_FF_STEPS = -(-D_FF // FF_TILE)
FF_LAST_OFF = D_FF - FF_TILE
FF_OVERLAP = N_FF_STEPS * FF_TILE - D_FF
OUT_TILE = 512
N_OUT_STEPS = D_MODEL // OUT_TILE
MIX_TILE = 256
PROJ_TILE = 512
VMEM_LIMIT = 56 * 1024 * 1024

BF16 = jnp.bfloat16
F32 = jnp.float32


def _dot(a, b):
    return jnp.dot(a, b, preferred_element_type=F32)


def _dot_nt(a, b):
    return lax.dot_general(a, b, (((1,), (1,)), ((), ())), preferred_element_type=F32)


def _layer_norm(y, g, b):
    mu = jnp.mean(y, axis=-1, keepdims=True)
    d = y - mu
    var = jnp.mean(d * d, axis=-1, keepdims=True)
    return d * lax.rsqrt(var + LN_EPS) * g + b


def _rms_norm(y, g):
    return y * lax.rsqrt(jnp.mean(y * y, axis=-1, keepdims=True) + LN_EPS) * g


def _gelu_tanh(x):
    c = np.float32(np.sqrt(2.0 / np.pi))
    return 0.5 * x * (1.0 + jnp.tanh(c * (x + 0.044715 * (x * x * x))))


def _rope_lanes(t, cos, sin_lo, sin_hi):
    pieces = []
    for c in range(t.shape[-1] // LANES):
        x = t[:, c * LANES:(c + 1) * LANES]
        pieces.append(x * cos + pltpu.roll(x, 8, 1) * sin_hi + pltpu.roll(x, LANES - 8, 1) * sin_lo)
    return pieces[0] if len(pieces) == 1 else jnp.concatenate(pieces, axis=-1)


def _ada_kernel(c_ref, w_ref, b_ref, o_ref):
    c = c_ref[...]
    h = (c * jax.nn.sigmoid(c)).astype(BF16)
    o_ref[...] = _dot(h, w_ref[...].astype(BF16)) + b_ref[...]


def _ada(c, w_ada, b_ada):
    rows = c.shape[0]
    n = w_ada.shape[1]
    tn = 1024
    return pl.pallas_call(
        _ada_kernel,
        grid=(n // tn,),
        in_specs=[
            pl.BlockSpec((rows, D_MODEL), lambda j: (0, 0)),
            pl.BlockSpec((D_MODEL, tn), lambda j: (0, j)),
            pl.BlockSpec((1, tn), lambda j: (0, j)),
        ],
        out_specs=pl.BlockSpec((rows, tn), lambda j: (0, j)),
        out_shape=jax.ShapeDtypeStruct((rows, n), F32),
        compiler_params=pltpu.CompilerParams(dimension_semantics=("arbitrary",), vmem_limit_bytes=VMEM_LIMIT),
        name="ada_mod",
    )(c, w_ada, b_ada.reshape(1, n))


def _ff_offset(step):
    return jnp.minimum(step * FF_TILE, FF_LAST_OFF)


def _ffn_kernel(n_side, *refs):
    x_ref, shift_ref, scale_ref, gate_ref, wv_ref, wg_ref, wd_ref, g_ref, b_ref = refs[:9]
    side_in = refs[9:9 + n_side]
    o_ref = refs[9 + n_side]
    side_out = refs[10 + n_side:10 + 2 * n_side]
    xm_ref, a_ref, y_ref = refs[10 + 2 * n_side:]
    s = pl.program_id(1)

    def cast_side():
        for src, dst in zip(side_in, side_out):
            dst[...] = src[...].astype(BF16)

    @pl.when(s == 0)
    def _():
        xm_ref[...] = (x_ref[...] * (1.0 + scale_ref[0]) + shift_ref[0]).astype(BF16)

    @pl.when(s < N_FF_STEPS)
    def _():
        cast_side()
        xm = xm_ref[...]
        for c in range(FF_TILE // MXU_COLS):
            sl = slice(c * MXU_COLS, (c + 1) * MXU_COLS)
            hv = _dot(xm, wv_ref[:, sl])
            hg = _dot(xm, wg_ref[:, sl])
            a = hg * jax.nn.sigmoid(hg) * hv
            if c * MXU_COLS < FF_OVERLAP:
                col = lax.broadcasted_iota(jnp.int32, a.shape, 1) + c * MXU_COLS
                a = jnp.where(jnp.logical_and(s == N_FF_STEPS - 1, col < FF_OVERLAP), 0.0, a)
            a_ref[s, :, sl] = a.astype(BF16)

    @pl.when(s >= N_FF_STEPS)
    def _():
        cast_side()
        acc = None
        for f in range(N_FF_STEPS):
            r0 = min(f * FF_TILE, FF_LAST_OFF)
            part = _dot(a_ref[f], wd_ref[r0:r0 + FF_TILE, :])
            acc = part if acc is None else acc + part
        y_ref[s - N_FF_STEPS] = acc

    @pl.when(s == pl.num_programs(1) - 1)
    def _():
        mixed = jnp.concatenate([y_ref[n] for n in range(N_OUT_STEPS)], axis=-1)
        y = ALPHA * x_ref[...] + 0.5 * gate_ref[0] * mixed
        o_ref[...] = _layer_norm(y, g_ref[...], b_ref[...])


def _ffn(x, shift, scale, gate, w_up, w_down, ln_g, ln_b, *, tm, rows_per_mod, side=()):
    m = x.shape[0]
    r = shift.shape[1]
    n_steps = N_FF_STEPS + N_OUT_STEPS
    tiles_per_mod = rows_per_mod // tm
    mod_spec = pl.BlockSpec((1, r, D_MODEL), lambda i, s: (i // tiles_per_mod, 0, 0))
    side_in_specs, side_out_specs, side_shapes = [], [], []
    for arr, lead, rows in side:
        n_rows, n_cols = arr.shape[-2:]
        n_blocks = n_rows // rows
        assert n_blocks * rows == n_rows and n_blocks <= (m // tm) * n_steps
        blk = lambda i, s, n_blocks=n_blocks: jnp.minimum(i * n_steps + s, n_blocks - 1)
        side_in_specs.append(pl.BlockSpec((None,) * len(lead) + (rows, n_cols),
                                          lambda i, s, lead=lead, blk=blk: (*lead, blk(i, s), 0)))
        side_out_specs.append(pl.BlockSpec((rows, n_cols), lambda i, s, blk=blk: (blk(i, s), 0)))
        side_shapes.append(jax.ShapeDtypeStruct((n_rows, n_cols), BF16))
    wd_block = lambda i, s: (0, jnp.where(s < N_FF_STEPS, N_OUT_STEPS - 1, s - N_FF_STEPS))
    outs = pl.pallas_call(
        functools.partial(_ffn_kernel, len(side)),
        grid=(m // tm, n_steps),
        in_specs=[
            pl.BlockSpec((tm, D_MODEL), lambda i, s: (i, 0)),
            mod_spec, mod_spec, mod_spec,
            pl.BlockSpec((pl.Element(D_MODEL), pl.Element(FF_TILE)),
                         lambda i, s: (0, pl.multiple_of(_ff_offset(s), LANES))),
            pl.BlockSpec((pl.Element(D_MODEL), pl.Element(FF_TILE)),
                         lambda i, s: (0, pl.multiple_of(D_FF + _ff_offset(s), LANES))),
            pl.BlockSpec((D_FF, OUT_TILE), wd_block),
            pl.BlockSpec((1, D_MODEL), lambda i, s: (0, 0)),
            pl.BlockSpec((1, D_MODEL), lambda i, s: (0, 0)),
            *side_in_specs,
        ],
        out_specs=[pl.BlockSpec((tm, D_MODEL), lambda i, s: (i, 0)), *side_out_specs],
        out_shape=[jax.ShapeDtypeStruct((m, D_MODEL), F32), *side_shapes],
        scratch_shapes=[
            pltpu.VMEM((tm, D_MODEL), BF16),
            pltpu.VMEM((N_FF_STEPS, tm, FF_TILE), BF16),
            pltpu.VMEM((N_OUT_STEPS, tm, OUT_TILE), F32),
        ],
        compiler_params=pltpu.CompilerParams(dimension_semantics=("arbitrary", "arbitrary"), vmem_limit_bytes=VMEM_LIMIT),
        name="swiglu_ln",
    )(x, shift, scale, gate, w_up, w_up, w_down, ln_g.reshape(1, D_MODEL), ln_b.reshape(1, D_MODEL),
      *[arr for arr, _, _ in side])
    return outs if side else outs[0]


def _inproj_kernel(x_ref, shift_ref, scale_ref, w_ref, o_ref, xm_ref):
    @pl.when(pl.program_id(1) == 0)
    def _():
        xm_ref[...] = (x_ref[...] * (1.0 + scale_ref[0]) + shift_ref[0]).astype(BF16)

    o_ref[...] = _dot(xm_ref[...], w_ref[...])


def _inproj(x, shift, scale, w_in, *, tm, rows_per_mod):
    m = x.shape[0]
    r = shift.shape[1]
    tn = 512
    tiles_per_mod = rows_per_mod // tm
    mod_spec = pl.BlockSpec((1, r, D_MODEL), lambda i, j: (i // tiles_per_mod, 0, 0))
    return pl.pallas_call(
        _inproj_kernel,
        grid=(m // tm, IN_WIDTH // tn),
        in_specs=[
            pl.BlockSpec((tm, D_MODEL), lambda i, j: (i, 0)),
            mod_spec, mod_spec,
            pl.BlockSpec((D_MODEL, tn), lambda i, j: (0, j)),
        ],
        out_specs=pl.BlockSpec((tm, tn), lambda i, j: (i, j)),
        out_shape=jax.ShapeDtypeStruct((m, IN_WIDTH), F32),
        scratch_shapes=[pltpu.VMEM((tm, D_MODEL), BF16)],
        compiler_params=pltpu.CompilerParams(dimension_semantics=("arbitrary", "arbitrary"), vmem_limit_bytes=VMEM_LIMIT),
        name="mixer_inproj",
    )(x, shift, scale, w_in)


def _outproj_kernel(m_ref, x_ref, gate_ref, w_ref, g_ref, b_ref, o_ref):
    mixed = _dot(m_ref[...], w_ref[...])
    y = ALPHA * x_ref[...] + gate_ref[0] * mixed
    o_ref[...] = _layer_norm(y, g_ref[...], b_ref[...])


def _outproj(merged, x, gate, w_o, ln_g, ln_b, *, tm, rows_per_mod):
    m = x.shape[0]
    r = gate.shape[1]
    tiles_per_mod = rows_per_mod // tm
    return pl.pallas_call(
        _outproj_kernel,
        grid=(m // tm,),
        in_specs=[
            pl.BlockSpec((tm, D_MODEL), lambda i: (i, 0)),
            pl.BlockSpec((tm, D_MODEL), lambda i: (i, 0)),
            pl.BlockSpec((1, r, D_MODEL), lambda i: (i // tiles_per_mod, 0, 0)),
            pl.BlockSpec((D_MODEL, D_MODEL), lambda i: (0, 0)),
            pl.BlockSpec((1, D_MODEL), lambda i: (0, 0)),
            pl.BlockSpec((1, D_MODEL), lambda i: (0, 0)),
        ],
        out_specs=pl.BlockSpec((tm, D_MODEL), lambda i: (i, 0)),
        out_shape=jax.ShapeDtypeStruct((m, D_MODEL), F32),
        compiler_params=pltpu.CompilerParams(dimension_semantics=("arbitrary",), vmem_limit_bytes=VMEM_LIMIT),
        name="mixer_outproj_ln",
    )(merged, x, gate, w_o, ln_g.reshape(1, D_MODEL), ln_b.reshape(1, D_MODEL))


def _gmlp_norm(pv, gng, gnb):
    v = _gelu_tanh(pv)
    outs = []
    for g in range(N_GMLP_GROUPS):
        sl = slice(g * LANES, (g + 1) * LANES)
        outs.append(_layer_norm(v[:, sl], gng[:, sl], gnb[:, sl]))
    return outs


def _half_lane_tiles(t):
    lo_lane = lax.broadcasted_iota(jnp.int32, (t.shape[0], LANES), 1) < HEAD_DIM
    lo, hi = [], []
    for c in range(KV_WIDTH // LANES):
        col = t[:, c * LANES:(c + 1) * LANES]
        swp = pltpu.roll(col, HEAD_DIM, 1)
        lo += [jnp.where(lo_lane, col, 0.0), jnp.where(lo_lane, swp, 0.0)]
        hi += [jnp.where(lo_lane, 0.0, swp), jnp.where(lo_lane, 0.0, col)]
    return jnp.concatenate(lo + hi, axis=-1).astype(BF16)


def _band_attention(q_s, kx_s, vx_s, attn_s, sinks_ref, nblk, first_tile):
    rows = 2 * WINDOW
    qi = lax.broadcasted_iota(jnp.int32, (rows, WINDOW), 0) & (WINDOW - 1)
    kj = lax.broadcasted_iota(jnp.int32, (rows, WINDOW), 1)
    own = kj <= qi
    first_mask = kj <= jnp.where(first_tile, qi, WINDOW)
    ones = jnp.ones((2 * WINDOW, LANES), BF16)
    units = [(wb, g, hf) for wb in range(nblk) for g in range(N_KV_HEADS) for hf in range(2)]

    def kv_cols(g, hf):
        return slice(hf * 4 * LANES + g * LANES, hf * 4 * LANES + (g + 1) * LANES)

    scores = []
    for wb, g, hf in units:
        r0 = wb * WINDOW
        q2 = jnp.concatenate([q_s[r0:r0 + WINDOW, (2 * g) * LANES:(2 * g + 1) * LANES],
                              q_s[r0:r0 + WINDOW, (2 * g + 1) * LANES:(2 * g + 2) * LANES]], axis=0)
        s = _dot_nt(q2, kx_s[r0:r0 + 2 * WINDOW, kv_cols(g, hf)])
        c = jnp.where(own, s[:, WINDOW:2 * WINDOW], s[:, 0:WINDOW])
        if wb == 0:
            c = jnp.where(first_mask, c, -jnp.inf)
        scores.append(c)

    probs = []
    for (wb, g, hf), c in zip(units, scores):
        sink = jnp.concatenate([jnp.full((WINDOW, LANES), sinks_ref[GQA_GROUP * g + hf], F32),
                                jnp.full((WINDOW, LANES), sinks_ref[GQA_GROUP * g + 2 + hf], F32)], axis=0)
        mx = jnp.maximum(jnp.broadcast_to(jnp.max(c, axis=-1, keepdims=True), c.shape), sink)
        p = jnp.exp(c - mx)
        pcat = jnp.concatenate([jnp.where(own, 0.0, p), jnp.where(own, p, 0.0)], axis=-1).astype(BF16)
        probs.append((pcat, jnp.exp(sink - mx)))

    outs = {}
    for (wb, g, hf), (pcat, esink) in zip(units, probs):
        r0 = wb * WINDOW
        v2 = jnp.concatenate([vx_s[r0:r0 + 2 * WINDOW, kv_cols(g, hf)], ones], axis=-1)
        o = _dot(pcat, v2)
        outs[wb, g, hf] = o[:, 0:LANES] / (o[:, LANES:2 * LANES] + esink)

    for wb in range(nblk):
        r0 = wb * WINDOW
        for g in range(N_KV_HEADS):
            both = outs[wb, g, 0] + outs[wb, g, 1]
            attn_s[r0:r0 + WINDOW, (2 * g) * LANES:(2 * g + 1) * LANES] = both[0:WINDOW]
            attn_s[r0:r0 + WINDOW, (2 * g + 1) * LANES:(2 * g + 2) * LANES] = both[WINDOW:2 * WINDOW]


def _pmix_kernel(sinks_ref, x_ref, shift_ref, scale_ref, gate_ref, win_ref, wo_ref, cos_ref, slo_ref, shi_ref,
                 gng_ref, gnb_ref, ws_ref, bs_ref, og_ref, lng_ref, lnb_ref,
                 o_ref, k_ref, v_ref, xm_s, p_s, q_s, kk_s, vv_s, kp_s, vp_s, attn_s, m_s, wm_s):
    t = pl.program_id(1)
    tm = x_ref.shape[0]
    nblk = tm // WINDOW

    @pl.when(jnp.logical_and(pl.program_id(0) == 0, t == 0))
    def _():
        row = lax.broadcasted_iota(jnp.int32, (CHUNK, CHUNK), 0)
        col = lax.broadcasted_iota(jnp.int32, (CHUNK, CHUNK), 1)
        for g in range(N_GMLP_GROUPS):
            wm_s[g] = jnp.where(col <= row, ws_ref[g], 0.0).astype(BF16)

    @pl.when(t == 0)
    def _():
        kp_s[...] = jnp.zeros(kp_s.shape, BF16)
        vp_s[...] = jnp.zeros(vp_s.shape, BF16)

    xm_s[...] = (x_ref[...] * (1.0 + scale_ref[0]) + shift_ref[0]).astype(BF16)
    for c in range(IN_WIDTH // PROJ_TILE):
        sl = slice(c * PROJ_TILE, (c + 1) * PROJ_TILE)
        p_s[:, sl] = _dot(xm_s[...], win_ref[:, sl])

    cos, slo, shi = cos_ref[...], slo_ref[...], shi_ref[...]
    q_s[...] = (_rope_lanes(p_s[:, 0:Q_END], cos, slo, shi) * (HEAD_DIM ** -0.5)).astype(BF16)
    k = _rope_lanes(p_s[:, Q_END:K_END], cos, slo, shi)
    v = p_s[:, K_END:V_END]
    k_ref[0] = k[tm - WINDOW:tm]
    v_ref[0] = v[tm - WINDOW:tm]
    kb, vb = _half_lane_tiles(k), _half_lane_tiles(v)
    kk_s[0:WINDOW, :] = kp_s[...]
    vv_s[0:WINDOW, :] = vp_s[...]
    kk_s[WINDOW:WINDOW + tm, :] = kb
    vv_s[WINDOW:WINDOW + tm, :] = vb
    kp_s[...] = kb[tm - WINDOW:tm]
    vp_s[...] = vb[tm - WINDOW:tm]

    _band_attention(q_s, kk_s, vv_s, attn_s, sinks_ref, nblk, t == 0)

    vn = _gmlp_norm(p_s[:, U_END:IN_WIDTH], gng_ref[...], gnb_ref[...])
    gated = []
    for g in range(N_GMLP_GROUPS):
        u = _gelu_tanh(p_s[:, V_END + g * LANES:V_END + (g + 1) * LANES])
        vg = vn[g].astype(BF16)
        cols = jnp.concatenate([vg[wb * CHUNK:(wb + 1) * CHUNK] for wb in range(nblk)], axis=-1)
        mix = _dot(wm_s[g], cols)
        mix = jnp.concatenate([mix[:, wb * LANES:(wb + 1) * LANES] for wb in range(nblk)], axis=0)
        bias = jnp.concatenate([bs_ref[:, g:g + 1]] * nblk, axis=0)
        gated.append(u * (mix + bias))
    gm = jnp.concatenate(gated, axis=-1)
    og = og_ref[...]
    m_s[:, 0:ATTN_WIDTH] = _rms_norm(attn_s[...], og[:, 0:ATTN_WIDTH]).astype(BF16)
    m_s[:, ATTN_WIDTH:] = _rms_norm(gm, og[:, ATTN_WIDTH:]).astype(BF16)

    for c in range(D_MODEL // PROJ_TILE):
        sl = slice(c * PROJ_TILE, (c + 1) * PROJ_TILE)
        o_ref[:, sl] = ALPHA * x_ref[:, sl] + gate_ref[0, :, sl] * _dot(m_s[...], wo_ref[:, sl])
    o_ref[...] = _layer_norm(o_ref[...], lng_ref[...], lnb_ref[...])


def _rope_tables(pos):
    half = ROPE_DIM // 2
    inv = ROPE_THETA ** (-(jnp.arange(half, dtype=F32) * 2.0) / ROPE_DIM)
    ang = pos.astype(F32)[:, None] * inv[None, :]
    cos, sin = jnp.cos(ang), jnp.sin(ang)
    n = pos.shape[0]
    one = jnp.ones((n, HEAD_DIM - ROPE_DIM), F32)
    zero = jnp.zeros((n, HEAD_DIM - ROPE_DIM), F32)
    z8 = jnp.zeros((n, half), F32)
    cos_t = jnp.concatenate([cos, cos, one], axis=-1)
    sin_lo = jnp.concatenate([-sin, z8, zero], axis=-1)
    sin_hi = jnp.concatenate([z8, sin, zero], axis=-1)
    rep = LANES // HEAD_DIM
    return jnp.tile(cos_t, (1, rep)), jnp.tile(sin_lo, (1, rep)), jnp.tile(sin_hi, (1, rep))


def _pmix(x, shift, scale, gate, w_in, w_o, sinks, tables, gn_g, gn_b, w_s, b_s, out_g, ln_g, ln_b, *, batch, tm):
    nt = SEQ // tm
    cos_t, sin_lo, sin_hi = tables
    tab_spec = pl.BlockSpec((tm, LANES), lambda b, j: (j, 0))
    full2 = lambda b, j: (0, 0)
    mod_spec = pl.BlockSpec((1, 1, D_MODEL), lambda b, j: (b, 0, 0))
    resident = dict(pipeline_mode=pl.Buffered(1))
    return pl.pallas_call(
        _pmix_kernel,
        grid=(batch, nt),
        in_specs=[
            pl.BlockSpec(memory_space=pltpu.SMEM),
            pl.BlockSpec((tm, D_MODEL), lambda b, j: (b * nt + j, 0)),
            mod_spec, mod_spec, mod_spec,
            pl.BlockSpec((D_MODEL, IN_WIDTH), full2, **resident),
            pl.BlockSpec((D_MODEL, D_MODEL), full2, **resident),
            tab_spec, tab_spec, tab_spec,
            pl.BlockSpec((1, GMLP_WIDTH), full2),
            pl.BlockSpec((1, GMLP_WIDTH), full2),
            pl.BlockSpec((N_GMLP_GROUPS, CHUNK, CHUNK), lambda b, j: (0, 0, 0)),
            pl.BlockSpec((CHUNK, N_GMLP_GROUPS), full2),
            pl.BlockSpec((1, D_MODEL), full2),
            pl.BlockSpec((1, D_MODEL), full2),
            pl.BlockSpec((1, D_MODEL), full2),
        ],
        out_specs=[
            pl.BlockSpec((tm, D_MODEL), lambda b, j: (b * nt + j, 0)),
            pl.BlockSpec((1, WINDOW, KV_WIDTH), lambda b, j: (b, 0, 0)),
            pl.BlockSpec((1, WINDOW, KV_WIDTH), lambda b, j: (b, 0, 0)),
        ],
        out_shape=[
            jax.ShapeDtypeStruct((batch * SEQ, D_MODEL), F32),
            jax.ShapeDtypeStruct((batch, WINDOW, KV_WIDTH), F32),
            jax.ShapeDtypeStruct((batch, WINDOW, KV_WIDTH), F32),
        ],
        scratch_shapes=[
            pltpu.VMEM((tm, D_MODEL), BF16),
            pltpu.VMEM((tm, IN_WIDTH), F32),
            pltpu.VMEM((tm, ATTN_WIDTH), BF16),
            pltpu.VMEM((WINDOW + tm, 4 * KV_WIDTH), BF16),
            pltpu.VMEM((WINDOW + tm, 4 * KV_WIDTH), BF16),
            pltpu.VMEM((WINDOW, 4 * KV_WIDTH), BF16),
            pltpu.VMEM((WINDOW, 4 * KV_WIDTH), BF16),
            pltpu.VMEM((tm, ATTN_WIDTH), F32),
            pltpu.VMEM((tm, D_MODEL), BF16),
            pltpu.VMEM((N_GMLP_GROUPS, CHUNK, CHUNK), BF16),
        ],
        compiler_params=pltpu.CompilerParams(dimension_semantics=("arbitrary", "arbitrary"), vmem_limit_bytes=VMEM_LIMIT),
        name="prompt_mixer",
    )(sinks, x, shift, scale, gate, w_in, w_o, cos_t, sin_lo, sin_hi, gn_g.reshape(1, GMLP_WIDTH),
      gn_b.reshape(1, GMLP_WIDTH), w_s, b_s.T, out_g.reshape(1, D_MODEL), ln_g.reshape(1, D_MODEL),
      ln_b.reshape(1, D_MODEL))


def _sattn_kernel(qe_ref, p_ref, ck_ref, cv_ref, sink_ref, cos_ref, slo_ref, shi_ref, o_ref, ko_ref, vo_ref):
    cos, slo, shi = cos_ref[...], slo_ref[...], shi_ref[...]
    qe = _rope_lanes(qe_ref[0], cos, slo, shi) * (HEAD_DIM ** -0.5)
    kn = _rope_lanes(p_ref[0, :, Q_END:K_END], cos, slo, shi)
    vn = p_ref[0, :, K_END:V_END]
    ck = ck_ref[0]
    cv = cv_ref[0]
    sink = sink_ref[...]
    s_c = _dot_nt(qe.astype(BF16), ck.astype(BF16))
    key = lax.broadcasted_iota(jnp.int32, s_c.shape, 1)
    s_c = jnp.where(key >= 1, s_c, -jnp.inf)
    s_n = jnp.sum(qe * kn, axis=-1, keepdims=True)
    mx = jnp.maximum(jnp.maximum(jnp.max(s_c, axis=-1, keepdims=True), s_n), sink)
    p_c = jnp.exp(s_c - mx)
    p_n = jnp.exp(s_n - mx)
    denom = jnp.sum(p_c, axis=-1, keepdims=True) + p_n + jnp.exp(sink - mx)
    o = (_dot(p_c.astype(BF16), cv.astype(BF16)) + p_n * vn) / denom
    head = lax.broadcasted_iota(jnp.int32, o.shape, 0)
    lane = lax.broadcasted_iota(jnp.int32, o.shape, 1)
    o_ref[0] = jnp.where(head // GQA_GROUP == lane // HEAD_DIM, o, 0.0)
    row = lax.broadcasted_iota(jnp.int32, ck.shape, 0)
    last = row == WINDOW - 1
    ko_ref[0] = jnp.where(last, kn, pltpu.roll(ck, WINDOW - 1, 0))
    vo_ref[0] = jnp.where(last, vn, pltpu.roll(cv, WINDOW - 1, 0))


def _sattn(qe, p3, ck, cv, sinks, tables):
    b = qe.shape[0]
    cos_t, sin_lo, sin_hi = tables
    tab_spec = pl.BlockSpec((1, LANES), lambda i: (0, 0))
    blk3 = lambda i: (i, 0, 0)
    return pl.pallas_call(
        _sattn_kernel,
        grid=(b,),
        in_specs=[
            pl.BlockSpec((1, N_HEADS, KV_WIDTH), blk3),
            pl.BlockSpec((1, 1, IN_WIDTH), blk3),
            pl.BlockSpec((1, WINDOW, KV_WIDTH), blk3),
            pl.BlockSpec((1, WINDOW, KV_WIDTH), blk3),
            pl.BlockSpec((N_HEADS, 1), lambda i: (0, 0)),
            tab_spec, tab_spec, tab_spec,
        ],
        out_specs=[
            pl.BlockSpec((1, N_HEADS, KV_WIDTH), blk3),
            pl.BlockSpec((1, WINDOW, KV_WIDTH), blk3),
            pl.BlockSpec((1, WINDOW, KV_WIDTH), blk3),
        ],
        out_shape=[
            jax.ShapeDtypeStruct((b, N_HEADS, KV_WIDTH), F32),
            jax.ShapeDtypeStruct((b, WINDOW, KV_WIDTH), F32),
            jax.ShapeDtypeStruct((b, WINDOW, KV_WIDTH), F32),
        ],
        compiler_params=pltpu.CompilerParams(dimension_semantics=("arbitrary",)),
        name="sample_attn",
    )(qe, p3, ck, cv, sinks.reshape(N_HEADS, 1), cos_t, sin_lo, sin_hi)


def _smerge_kernel(attn_ref, p_ref, gng_ref, gnb_ref, w0_ref, b0_ref, og_ref, m_ref, vn_ref):
    og = og_ref[...]
    vn = jnp.concatenate(_gmlp_norm(p_ref[:, U_END:IN_WIDTH], gng_ref[...], gnb_ref[...]), axis=-1)
    vn_ref[...] = vn
    u = _gelu_tanh(p_ref[:, V_END:U_END])
    gm = u * (w0_ref[...] * vn + b0_ref[...])
    m_ref[:, 0:ATTN_WIDTH] = _rms_norm(attn_ref[...], og[:, 0:ATTN_WIDTH]).astype(BF16)
    m_ref[:, ATTN_WIDTH:] = _rms_norm(gm, og[:, ATTN_WIDTH:]).astype(BF16)


def _smerge(attn, p, gn_g, gn_b, w_s, b_s, out_g):
    b = attn.shape[0]
    w0 = jnp.repeat(w_s[:, 0, 0], LANES).reshape(1, GMLP_WIDTH)
    b0 = jnp.repeat(b_s[:, 0], LANES).reshape(1, GMLP_WIDTH)
    return pl.pallas_call(
        _smerge_kernel,
        out_shape=[
            jax.ShapeDtypeStruct((b, D_MODEL), BF16),
            jax.ShapeDtypeStruct((b, GMLP_WIDTH), F32),
        ],
        name="sample_gmlp_merge",
    )(attn, p, gn_g.reshape(1, GMLP_WIDTH), gn_b.reshape(1, GMLP_WIDTH), w0, b0, out_g.reshape(1, D_MODEL))


def kernel(x_prompt, x_sample, cache_k_win, cache_v_win, c_prompt, c_sample, w_ada, b_ada, ln_g, ln_b,
           w_ffn_up, w_ffn_down, w_in, attn_sinks, gmlp_norm_g, gmlp_norm_b, w_spatial, b_spatial,
           out_norm_g, w_o):
    batch, seq, _ = x_prompt.shape
    dec_batch = x_sample.shape[0]
    buf = cache_k_win.shape[2]
    assert seq == SEQ and buf == WINDOW and x_sample.shape[1] == 1 and w_ada.shape[0] == DEPTH == 1

    xp = x_prompt.reshape(batch * seq, D_MODEL)
    xs = x_sample.reshape(dec_batch, D_MODEL)

    n_c = batch + dec_batch
    n_c_pad = ((n_c + 7) // 8) * 8
    c_all = jnp.concatenate([c_prompt, c_sample, jnp.zeros((n_c_pad - n_c, D_MODEL), F32)], axis=0)
    mod = _ada(c_all, w_ada[0], b_ada[0]).reshape(n_c_pad, N_SUB, 3, D_MODEL)
    mod_p = mod[:batch]
    mod_s = mod[batch:n_c]

    def pmod(i, j):
        return mod_p[:, i, j].reshape(batch, 1, D_MODEL)

    def smod(i, j):
        return mod_s[:, i, j].reshape(1, dec_batch, D_MODEL)

    tm_p = 512
    ffn_p = functools.partial(_ffn, tm=tm_p, rows_per_mod=seq)
    ffn_s = functools.partial(_ffn, tm=dec_batch, rows_per_mod=dec_batch)

    w_up0 = w_ffn_up[0, 0].astype(BF16)
    w_down0 = w_ffn_down[0, 0].astype(BF16)
    later = ((w_ffn_up, (0, 1), 16), (w_ffn_down, (0, 1), 32), (w_in, (0,), 16), (w_o, (0,), 16))
    xp, w_up1, w_down1, w_in_b, w_o_b = ffn_p(xp, pmod(0, 0), pmod(0, 1), pmod(0, 2), w_up0, w_down0,
                                              ln_g[0, 0], ln_b[0, 0], side=later)
    xs = ffn_s(xs, smod(0, 0), smod(0, 1), smod(0, 2), w_up0, w_down0, ln_g[0, 0], ln_b[0, 0])

    xp, k_p, v_p = _pmix(xp, pmod(1, 0), pmod(1, 1), pmod(1, 2), w_in_b, w_o_b, attn_sinks[0],
                         _rope_tables(jnp.arange(seq)), gmlp_norm_g[0], gmlp_norm_b[0], w_spatial[0], b_spatial[0],
                         out_norm_g[0], ln_g[0, 1], ln_b[0, 1], batch=batch, tm=MIX_TILE)

    ps = _inproj(xs, smod(1, 0), smod(1, 1), w_in_b, tm=dec_batch, rows_per_mod=dec_batch)
    eye = jnp.eye(N_KV_HEADS, dtype=F32)
    q5 = ps[:, :Q_END].reshape(dec_batch, N_KV_HEADS, GQA_GROUP, 1, HEAD_DIM)
    qe = (q5 * eye[None, :, None, :, None]).reshape(dec_batch, N_HEADS, KV_WIDTH)
    ck = cache_k_win[0].reshape(dec_batch, buf, KV_WIDTH)
    cv = cache_v_win[0].reshape(dec_batch, buf, KV_WIDTH)
    oe, k_s, v_s = _sattn(qe, ps.reshape(dec_batch, 1, IN_WIDTH), ck, cv, attn_sinks[0],
                          _rope_tables(jnp.full((1,), PAST_LEN)))
    attn_s = oe.reshape(dec_batch, N_KV_HEADS, GQA_GROUP, N_KV_HEADS, HEAD_DIM).sum(axis=3).reshape(dec_batch, ATTN_WIDTH)
    merged_s, vn_s = _smerge(attn_s, ps, gmlp_norm_g[0], gmlp_norm_b[0], w_spatial[0], b_spatial[0], out_norm_g[0])
    xs = _outproj(merged_s, xs, smod(1, 2), w_o_b, ln_g[0, 1], ln_b[0, 1], tm=dec_batch, rows_per_mod=dec_batch)

    xp = ffn_p(xp, pmod(2, 0), pmod(2, 1), pmod(2, 2), w_up1, w_down1, ln_g[0, 2], ln_b[0, 2])
    xs = ffn_s(xs, smod(2, 0), smod(2, 1), smod(2, 2), w_up1, w_down1, ln_g[0, 2], ln_b[0, 2])

    return (
        xp.reshape(batch, seq, D_MODEL),
        xs.reshape(dec_batch, 1, D_MODEL),
        k_p.reshape(1, batch, WINDOW, N_KV_HEADS, HEAD_DIM),
        v_p.reshape(1, batch, WINDOW, N_KV_HEADS, HEAD_DIM),
        k_s.reshape(1, dec_batch, buf, N_KV_HEADS, HEAD_DIM),
        v_s.reshape(1, dec_batch, buf, N_KV_HEADS, HEAD_DIM),
        vn_s.reshape(1, dec_batch, 1, GMLP_WIDTH),
    )
```

```python
import functools

import jax
import jax.numpy as jnp
import numpy as np
from jax import lax
from jax.experimental import pallas as pl
from jax.experimental.pallas import tpu as pltpu

D_MODEL = 2048
SEQ = 2048
PAST_LEN = 16384
ATTN_WIDTH = 1024
GMLP_WIDTH = 1024
HEAD_DIM = 64
N_HEADS = 16
N_KV_HEADS = 4
GQA_GROUP = 4
KV_WIDTH = 256
WINDOW = 128
ROPE_THETA = 500000.0
ROPE_DIM = 16
CHUNK = 128
N_GMLP_GROUPS = 8
D_FF = 5504
N_SUB = 3
DEPTH = 1
ALPHA = (2.0 * DEPTH) ** 0.25
LN_EPS = 1e-5
Q_END = ATTN_WIDTH
K_END = Q_END + KV_WIDTH
V_END = K_END + KV_WIDTH
U_END = V_END + GMLP_WIDTH
IN_WIDTH = U_END + GMLP_WIDTH

LANES = 128
MXU_COLS = 256
FF_TILE = 512
N_FF_STEPS = -(-D_FF // FF_TILE)
FF_LAST_OFF = D_FF - FF_TILE
FF_OVERLAP = N_FF_STEPS * FF_TILE - D_FF
OUT_TILE = 512
N_OUT_STEPS = D_MODEL // OUT_TILE
FFN_ROWS = 512
MIX_TILE = 256
PROJ_TILE = 512
VMEM_LIMIT = 60 * 1024 * 1024

BF16 = jnp.bfloat16
F32 = jnp.float32


def _dot(a, b):
    return jnp.dot(a, b, preferred_element_type=F32)


def _dot_nt(a, b):
    return lax.dot_general(a, b, (((1,), (1,)), ((), ())), preferred_element_type=F32)


def _layer_norm(y, g, b):
    mu = jnp.mean(y, axis=-1, keepdims=True)
    d = y - mu
    var = jnp.mean(d * d, axis=-1, keepdims=True)
    return d * lax.rsqrt(var + LN_EPS) * g + b


def _rms_norm(y, g):
    return y * lax.rsqrt(jnp.mean(y * y, axis=-1, keepdims=True) + LN_EPS) * g


def _gelu_tanh(x):
    c = np.float32(np.sqrt(2.0 / np.pi))
    return 0.5 * x * (1.0 + jnp.tanh(c * (x + 0.044715 * (x * x * x))))


def _rope_lanes(t, cos, sin_lo, sin_hi):
    pieces = []
    for c in range(t.shape[-1] // LANES):
        x = t[:, c * LANES:(c + 1) * LANES]
        pieces.append(x * cos + pltpu.roll(x, 8, 1) * sin_hi + pltpu.roll(x, LANES - 8, 1) * sin_lo)
    return pieces[0] if len(pieces) == 1 else jnp.concatenate(pieces, axis=-1)


def _ada_kernel(c_ref, w_ref, b_ref, o_ref):
    c = c_ref[...]
    h = (c * jax.nn.sigmoid(c)).astype(BF16)
    o_ref[...] = _dot(h, w_ref[...].astype(BF16)) + b_ref[...]


def _ada(c, w_ada, b_ada):
    rows = c.shape[0]
    n = w_ada.shape[1]
    tn = 1024
    return pl.pallas_call(
        _ada_kernel,
        grid=(n // tn,),
        in_specs=[
            pl.BlockSpec((rows, D_MODEL), lambda j: (0, 0)),
            pl.BlockSpec((D_MODEL, tn), lambda j: (0, j)),
            pl.BlockSpec((1, tn), lambda j: (0, j)),
        ],
        out_specs=pl.BlockSpec((rows, tn), lambda j: (0, j)),
        out_shape=jax.ShapeDtypeStruct((rows, n), F32),
        compiler_params=pltpu.CompilerParams(dimension_semantics=("arbitrary",), vmem_limit_bytes=VMEM_LIMIT),
        name="ada_mod",
    )(c, w_ada, b_ada.reshape(1, n))


def _ff_offset(step):
    return jnp.minimum(step * FF_TILE, FF_LAST_OFF)


def _ffn_kernel(n_side, n_extra, *refs):
    x_ref, shift_ref, scale_ref, gate_ref, wv_ref, wg_ref, wd_ref, g_ref, b_ref = refs[:9]
    n_in = 9
    if n_extra:
        xe_ref, shift_e_ref, scale_e_ref, gate_e_ref = refs[n_in:n_in + 4]
        n_in += 4
    side_in = refs[n_in:n_in + n_side]
    n_in += n_side
    o_ref = refs[n_in]
    n_out = 1
    if n_extra:
        oe_ref = refs[n_in + 1]
        n_out = 2
    side_out = refs[n_in + n_out:n_in + n_out + n_side]
    xm_ref, a_ref, y_ref = refs[n_in + n_out + n_side:]
    i = pl.program_id(0)
    s = pl.program_id(1)
    tm = x_ref.shape[0]
    last = pl.num_programs(1) - 1

    def cast_side():
        for src, dst in zip(side_in, side_out):
            dst[...] = src[...].astype(BF16)

    def up_chunk(rows):
        cast_side()
        xm = xm_ref[0:rows]
        for c in range(FF_TILE // MXU_COLS):
            sl = slice(c * MXU_COLS, (c + 1) * MXU_COLS)
            hv = _dot(xm, wv_ref[:, sl])
            hg = _dot(xm, wg_ref[:, sl])
            a = hg * jax.nn.sigmoid(hg) * hv
            if c * MXU_COLS < FF_OVERLAP:
                col = lax.broadcasted_iota(jnp.int32, a.shape, 1) + c * MXU_COLS
                a = jnp.where(jnp.logical_and(s == N_FF_STEPS - 1, col < FF_OVERLAP), 0.0, a)
            a_ref[s, 0:rows, sl] = a.astype(BF16)

    def down_slab(rows):
        cast_side()
        acc = None
        for f in range(N_FF_STEPS):
            r0 = min(f * FF_TILE, FF_LAST_OFF)
            part = _dot(a_ref[f, 0:rows], wd_ref[r0:r0 + FF_TILE, :])
            acc = part if acc is None else acc + part
        y_ref[s - N_FF_STEPS, 0:rows] = acc

    def finish(rows, x, gate, out_ref):
        mixed = jnp.concatenate([y_ref[n, rows] for n in range(N_OUT_STEPS)], axis=-1)
        out_ref[...] = _layer_norm(ALPHA * x + 0.5 * gate * mixed, g_ref[...], b_ref[...])

    @pl.when(s == 0)
    def _():
        xm_ref[0:tm] = (x_ref[...] * (1.0 + scale_ref[0]) + shift_ref[0]).astype(BF16)

    if n_extra:
        with_extra = i == 0

        @pl.when(jnp.logical_and(s == 0, with_extra))
        def _():
            xm_ref[tm:tm + n_extra] = (xe_ref[...] * (1.0 + scale_e_ref[...]) + shift_e_ref[...]).astype(BF16)

        for phase, body in ((s < N_FF_STEPS, up_chunk), (s >= N_FF_STEPS, down_slab)):
            pl.when(jnp.logical_and(phase, with_extra))(functools.partial(body, tm + n_extra))
            pl.when(jnp.logical_and(phase, jnp.logical_not(with_extra)))(functools.partial(body, tm))

        @pl.when(jnp.logical_and(s == last, with_extra))
        def _():
            finish(slice(tm, tm + n_extra), xe_ref[...], gate_e_ref[...], oe_ref)
    else:
        pl.when(s < N_FF_STEPS)(functools.partial(up_chunk, tm))
        pl.when(s >= N_FF_STEPS)(functools.partial(down_slab, tm))

    @pl.when(s == last)
    def _():
        finish(slice(0, tm), x_ref[...], gate_ref[0], o_ref)


def _ffn(x, shift, scale, gate, w_up, w_down, ln_g, ln_b, *, tm, rows_per_mod, extra=None, side=()):
    m = x.shape[0]
    r = shift.shape[1]
    n_steps = N_FF_STEPS + N_OUT_STEPS
    tiles_per_mod = rows_per_mod // tm
    n_extra = 0 if extra is None else extra[0].shape[0]
    mod_spec = pl.BlockSpec((1, r, D_MODEL), lambda i, s: (i // tiles_per_mod, 0, 0))
    const2 = lambda i, s: (0, 0)
    extra_spec = pl.BlockSpec((n_extra, D_MODEL), const2)
    side_in_specs, side_out_specs, side_shapes = [], [], []
    for arr, lead, rows in side:
        n_rows, n_cols = arr.shape[-2:]
        n_blocks = n_rows // rows
        assert n_blocks * rows == n_rows and n_blocks <= (m // tm) * n_steps
        blk = lambda i, s, n_blocks=n_blocks: jnp.minimum(i * n_steps + s, n_blocks - 1)
        side_in_specs.append(pl.BlockSpec((None,) * len(lead) + (rows, n_cols),
                                          lambda i, s, lead=lead, blk=blk: (*lead, blk(i, s), 0)))
        side_out_specs.append(pl.BlockSpec((rows, n_cols), lambda i, s, blk=blk: (blk(i, s), 0)))
        side_shapes.append(jax.ShapeDtypeStruct((n_rows, n_cols), BF16))
    wd_block = lambda i, s: (0, jnp.where(s < N_FF_STEPS, N_OUT_STEPS - 1, s - N_FF_STEPS))
    outs = pl.pallas_call(
        functools.partial(_ffn_kernel, len(side), n_extra),
        grid=(m // tm, n_steps),
        in_specs=[
            pl.BlockSpec((tm, D_MODEL), lambda i, s: (i, 0)),
            mod_spec, mod_spec, mod_spec,
            pl.BlockSpec((pl.Element(D_MODEL), pl.Element(FF_TILE)),
                         lambda i, s: (0, pl.multiple_of(_ff_offset(s), LANES))),
            pl.BlockSpec((pl.Element(D_MODEL), pl.Element(FF_TILE)),
                         lambda i, s: (0, pl.multiple_of(D_FF + _ff_offset(s), LANES))),
            pl.BlockSpec((D_FF, OUT_TILE), wd_block),
            pl.BlockSpec((1, D_MODEL), const2),
            pl.BlockSpec((1, D_MODEL), const2),
            *([extra_spec] * 4 if n_extra else []),
            *side_in_specs,
        ],
        out_specs=[pl.BlockSpec((tm, D_MODEL), lambda i, s: (i, 0)), *([extra_spec] if n_extra else []),
                   *side_out_specs],
        out_shape=[jax.ShapeDtypeStruct((m, D_MODEL), F32),
                   *([jax.ShapeDtypeStruct((n_extra, D_MODEL), F32)] if n_extra else []), *side_shapes],
        scratch_shapes=[
            pltpu.VMEM((tm + n_extra, D_MODEL), BF16),
            pltpu.VMEM((N_FF_STEPS, tm + n_extra, FF_TILE), BF16),
            pltpu.VMEM((N_OUT_STEPS, tm + n_extra, OUT_TILE), F32),
        ],
        compiler_params=pltpu.CompilerParams(dimension_semantics=("arbitrary", "arbitrary"), vmem_limit_bytes=VMEM_LIMIT),
        name="swiglu_ln",
    )(x, shift, scale, gate, w_up, w_up, w_down, ln_g.reshape(1, D_MODEL), ln_b.reshape(1, D_MODEL),
      *(extra or ()), *[arr for arr, _, _ in side])
    return outs


def _inproj_kernel(x_ref, shift_ref, scale_ref, w_ref, o_ref, xm_ref):
    @pl.when(pl.program_id(1) == 0)
    def _():
        xm_ref[...] = (x_ref[...] * (1.0 + scale_ref[0]) + shift_ref[0]).astype(BF16)

    o_ref[...] = _dot(xm_ref[...], w_ref[...])


def _inproj(x, shift, scale, w_in, *, tm, rows_per_mod):
    m = x.shape[0]
    r = shift.shape[1]
    tn = 512
    tiles_per_mod = rows_per_mod // tm
    mod_spec = pl.BlockSpec((1, r, D_MODEL), lambda i, j: (i // tiles_per_mod, 0, 0))
    return pl.pallas_call(
        _inproj_kernel,
        grid=(m // tm, IN_WIDTH // tn),
        in_specs=[
            pl.BlockSpec((tm, D_MODEL), lambda i, j: (i, 0)),
            mod_spec, mod_spec,
            pl.BlockSpec((D_MODEL, tn), lambda i, j: (0, j)),
        ],
        out_specs=pl.BlockSpec((tm, tn), lambda i, j: (i, j)),
        out_shape=jax.ShapeDtypeStruct((m, IN_WIDTH), F32),
        scratch_shapes=[pltpu.VMEM((tm, D_MODEL), BF16)],
        compiler_params=pltpu.CompilerParams(dimension_semantics=("arbitrary", "arbitrary"), vmem_limit_bytes=VMEM_LIMIT),
        name="mixer_inproj",
    )(x, shift, scale, w_in)


def _outproj_kernel(m_ref, x_ref, gate_ref, w_ref, g_ref, b_ref, o_ref):
    mixed = _dot(m_ref[...], w_ref[...])
    y = ALPHA * x_ref[...] + gate_ref[0] * mixed
    o_ref[...] = _layer_norm(y, g_ref[...], b_ref[...])


def _outproj(merged, x, gate, w_o, ln_g, ln_b, *, tm, rows_per_mod):
    m = x.shape[0]
    r = gate.shape[1]
    tiles_per_mod = rows_per_mod // tm
    return pl.pallas_call(
        _outproj_kernel,
        grid=(m // tm,),
        in_specs=[
            pl.BlockSpec((tm, D_MODEL), lambda i: (i, 0)),
            pl.BlockSpec((tm, D_MODEL), lambda i: (i, 0)),
            pl.BlockSpec((1, r, D_MODEL), lambda i: (i // tiles_per_mod, 0, 0)),
            pl.BlockSpec((D_MODEL, D_MODEL), lambda i: (0, 0)),
            pl.BlockSpec((1, D_MODEL), lambda i: (0, 0)),
            pl.BlockSpec((1, D_MODEL), lambda i: (0, 0)),
        ],
        out_specs=pl.BlockSpec((tm, D_MODEL), lambda i: (i, 0)),
        out_shape=jax.ShapeDtypeStruct((m, D_MODEL), F32),
        compiler_params=pltpu.CompilerParams(dimension_semantics=("arbitrary",), vmem_limit_bytes=VMEM_LIMIT),
        name="mixer_outproj_ln",
    )(merged, x, gate, w_o, ln_g.reshape(1, D_MODEL), ln_b.reshape(1, D_MODEL))


def _gmlp_norm(pv, gng, gnb):
    v = _gelu_tanh(pv)
    outs = []
    for g in range(N_GMLP_GROUPS):
        sl = slice(g * LANES, (g + 1) * LANES)
        outs.append(_layer_norm(v[:, sl], gng[:, sl], gnb[:, sl]))
    return outs


def _half_lane_tiles(t):
    lo_lane = lax.broadcasted_iota(jnp.int32, (t.shape[0], LANES), 1) < HEAD_DIM
    lo, hi = [], []
    for c in range(KV_WIDTH // LANES):
        col = t[:, c * LANES:(c + 1) * LANES]
        swp = pltpu.roll(col, HEAD_DIM, 1)
        lo += [jnp.where(lo_lane, col, 0.0), jnp.where(lo_lane, swp, 0.0)]
        hi += [jnp.where(lo_lane, 0.0, swp), jnp.where(lo_lane, 0.0, col)]
    return jnp.concatenate(lo + hi, axis=-1).astype(BF16)


def _band_attention(q_s, kx_s, vx_s, attn_s, sinks_ref, nblk, first_tile):
    rows = 2 * WINDOW
    qi = lax.broadcasted_iota(jnp.int32, (rows, WINDOW), 0) & (WINDOW - 1)
    kj = lax.broadcasted_iota(jnp.int32, (rows, WINDOW), 1)
    own = kj <= qi
    first_mask = kj <= jnp.where(first_tile, qi, WINDOW)
    ones = jnp.ones((2 * WINDOW, LANES), BF16)
    units = [(wb, g, hf) for wb in range(nblk) for g in range(N_KV_HEADS) for hf in range(2)]

    def kv_cols(g, hf):
        return slice(hf * 4 * LANES + g * LANES, hf * 4 * LANES + (g + 1) * LANES)

    scores = []
    for wb, g, hf in units:
        r0 = wb * WINDOW
        q2 = jnp.concatenate([q_s[r0:r0 + WINDOW, (2 * g) * LANES:(2 * g + 1) * LANES],
                              q_s[r0:r0 + WINDOW, (2 * g + 1) * LANES:(2 * g + 2) * LANES]], axis=0)
        s = _dot_nt(q2, kx_s[r0:r0 + 2 * WINDOW, kv_cols(g, hf)])
        c = jnp.where(own, s[:, WINDOW:2 * WINDOW], s[:, 0:WINDOW])
        if wb == 0:
            c = jnp.where(first_mask, c, -jnp.inf)
        scores.append(c)

    probs = []
    for (wb, g, hf), c in zip(units, scores):
        sink = jnp.concatenate([jnp.full((WINDOW, LANES), sinks_ref[GQA_GROUP * g + hf], F32),
                                jnp.full((WINDOW, LANES), sinks_ref[GQA_GROUP * g + 2 + hf], F32)], axis=0)
        mx = jnp.maximum(jnp.broadcast_to(jnp.max(c, axis=-1, keepdims=True), c.shape), sink)
        p = jnp.exp(c - mx)
        pcat = jnp.concatenate([jnp.where(own, 0.0, p), jnp.where(own, p, 0.0)], axis=-1).astype(BF16)
        probs.append((pcat, jnp.exp(sink - mx)))

    outs = {}
    for (wb, g, hf), (pcat, esink) in zip(units, probs):
        r0 = wb * WINDOW
        v2 = jnp.concatenate([vx_s[r0:r0 + 2 * WINDOW, kv_cols(g, hf)], ones], axis=-1)
        o = _dot(pcat, v2)
        outs[wb, g, hf] = o[:, 0:LANES] / (o[:, LANES:2 * LANES] + esink)

    for wb in range(nblk):
        r0 = wb * WINDOW
        for g in range(N_KV_HEADS):
            both = outs[wb, g, 0] + outs[wb, g, 1]
            attn_s[r0:r0 + WINDOW, (2 * g) * LANES:(2 * g + 1) * LANES] = both[0:WINDOW]
            attn_s[r0:r0 + WINDOW, (2 * g + 1) * LANES:(2 * g + 2) * LANES] = both[WINDOW:2 * WINDOW]


def _pmix_kernel(sinks_ref, x_ref, shift_ref, scale_ref, gate_ref, win_ref, wo_ref, cos_ref, slo_ref, shi_ref,
                 gng_ref, gnb_ref, ws_ref, bs_ref, og_ref, lng_ref, lnb_ref,
                 o_ref, k_ref, v_ref, xm_s, p_s, q_s, kk_s, vv_s, kp_s, vp_s, attn_s, m_s, wm_s):
    t = pl.program_id(1)
    tm = x_ref.shape[0]
    nblk = tm // WINDOW

    @pl.when(jnp.logical_and(pl.program_id(0) == 0, t == 0))
    def _():
        row = lax.broadcasted_iota(jnp.int32, (CHUNK, CHUNK), 0)
        col = lax.broadcasted_iota(jnp.int32, (CHUNK, CHUNK), 1)
        for g in range(N_GMLP_GROUPS):
            wm_s[g] = jnp.where(col <= row, ws_ref[g], 0.0).astype(BF16)

    @pl.when(t == 0)
    def _():
        kp_s[...] = jnp.zeros(kp_s.shape, BF16)
        vp_s[...] = jnp.zeros(vp_s.shape, BF16)

    xm_s[...] = (x_ref[...] * (1.0 + scale_ref[0]) + shift_ref[0]).astype(BF16)
    for c in range(IN_WIDTH // PROJ_TILE):
        sl = slice(c * PROJ_TILE, (c + 1) * PROJ_TILE)
        p_s[:, sl] = _dot(xm_s[...], win_ref[:, sl])

    cos, slo, shi = cos_ref[...], slo_ref[...], shi_ref[...]
    q_s[...] = (_rope_lanes(p_s[:, 0:Q_END], cos, slo, shi) * (HEAD_DIM ** -0.5)).astype(BF16)
    k = _rope_lanes(p_s[:, Q_END:K_END], cos, slo, shi)
    v = p_s[:, K_END:V_END]
    k_ref[0] = k[tm - WINDOW:tm]
    v_ref[0] = v[tm - WINDOW:tm]
    kb, vb = _half_lane_tiles(k), _half_lane_tiles(v)
    kk_s[0:WINDOW, :] = kp_s[...]
    vv_s[0:WINDOW, :] = vp_s[...]
    kk_s[WINDOW:WINDOW + tm, :] = kb
    vv_s[WINDOW:WINDOW + tm, :] = vb
    kp_s[...] = kb[tm - WINDOW:tm]
    vp_s[...] = vb[tm - WINDOW:tm]

    _band_attention(q_s, kk_s, vv_s, attn_s, sinks_ref, nblk, t == 0)

    vn = _gmlp_norm(p_s[:, U_END:IN_WIDTH], gng_ref[...], gnb_ref[...])
    gated = []
    for g in range(N_GMLP_GROUPS):
        u = _gelu_tanh(p_s[:, V_END + g * LANES:V_END + (g + 1) * LANES])
        vg = vn[g].astype(BF16)
        cols = jnp.concatenate([vg[wb * CHUNK:(wb + 1) * CHUNK] for wb in range(nblk)], axis=-1)
        mix = _dot(wm_s[g], cols)
        mix = jnp.concatenate([mix[:, wb * LANES:(wb + 1) * LANES] for wb in range(nblk)], axis=0)
        bias = jnp.concatenate([bs_ref[:, g:g + 1]] * nblk, axis=0)
        gated.append(u * (mix + bias))
    gm = jnp.concatenate(gated, axis=-1)
    og = og_ref[...]
    m_s[:, 0:ATTN_WIDTH] = _rms_norm(attn_s[...], og[:, 0:ATTN_WIDTH]).astype(BF16)
    m_s[:, ATTN_WIDTH:] = _rms_norm(gm, og[:, ATTN_WIDTH:]).astype(BF16)

    for c in range(D_MODEL // PROJ_TILE):
        sl = slice(c * PROJ_TILE, (c + 1) * PROJ_TILE)
        o_ref[:, sl] = ALPHA * x_ref[:, sl] + gate_ref[0, :, sl] * _dot(m_s[...], wo_ref[:, sl])
    o_ref[...] = _layer_norm(o_ref[...], lng_ref[...], lnb_ref[...])


def _rope_tables(pos):
    half = ROPE_DIM // 2
    inv = ROPE_THETA ** (-(jnp.arange(half, dtype=F32) * 2.0) / ROPE_DIM)
    ang = pos.astype(F32)[:, None] * inv[None, :]
    cos, sin = jnp.cos(ang), jnp.sin(ang)
    n = pos.shape[0]
    one = jnp.ones((n, HEAD_DIM - ROPE_DIM), F32)
    zero = jnp.zeros((n, HEAD_DIM - ROPE_DIM), F32)
    z8 = jnp.zeros((n, half), F32)
    cos_t = jnp.concatenate([cos, cos, one], axis=-1)
    sin_lo = jnp.concatenate([-sin, z8, zero], axis=-1)
    sin_hi = jnp.concatenate([z8, sin, zero], axis=-1)
    rep = LANES // HEAD_DIM
    return jnp.tile(cos_t, (1, rep)), jnp.tile(sin_lo, (1, rep)), jnp.tile(sin_hi, (1, rep))


def _pmix(x, shift, scale, gate, w_in, w_o, sinks, tables, gn_g, gn_b, w_s, b_s, out_g, ln_g, ln_b, *, batch, tm):
    nt = SEQ // tm
    cos_t, sin_lo, sin_hi = tables
    tab_spec = pl.BlockSpec((tm, LANES), lambda b, j: (j, 0))
    full2 = lambda b, j: (0, 0)
    mod_spec = pl.BlockSpec((1, 1, D_MODEL), lambda b, j: (b, 0, 0))
    resident = dict(pipeline_mode=pl.Buffered(1))
    return pl.pallas_call(
        _pmix_kernel,
        grid=(batch, nt),
        in_specs=[
            pl.BlockSpec(memory_space=pltpu.SMEM),
            pl.BlockSpec((tm, D_MODEL), lambda b, j: (b * nt + j, 0)),
            mod_spec, mod_spec, mod_spec,
            pl.BlockSpec((D_MODEL, IN_WIDTH), full2, **resident),
            pl.BlockSpec((D_MODEL, D_MODEL), full2, **resident),
            tab_spec, tab_spec, tab_spec,
            pl.BlockSpec((1, GMLP_WIDTH), full2),
            pl.BlockSpec((1, GMLP_WIDTH), full2),
            pl.BlockSpec((N_GMLP_GROUPS, CHUNK, CHUNK), lambda b, j: (0, 0, 0)),
            pl.BlockSpec((CHUNK, N_GMLP_GROUPS), full2),
            pl.BlockSpec((1, D_MODEL), full2),
            pl.BlockSpec((1, D_MODEL), full2),
            pl.BlockSpec((1, D_MODEL), full2),
        ],
        out_specs=[
            pl.BlockSpec((tm, D_MODEL), lambda b, j: (b * nt + j, 0)),
            pl.BlockSpec((1, WINDOW, KV_WIDTH), lambda b, j: (b, 0, 0)),
            pl.BlockSpec((1, WINDOW, KV_WIDTH), lambda b, j: (b, 0, 0)),
        ],
        out_shape=[
            jax.ShapeDtypeStruct((batch * SEQ, D_MODEL), F32),
            jax.ShapeDtypeStruct((batch, WINDOW, KV_WIDTH), F32),
            jax.ShapeDtypeStruct((batch, WINDOW, KV_WIDTH), F32),
        ],
        scratch_shapes=[
            pltpu.VMEM((tm, D_MODEL), BF16),
            pltpu.VMEM((tm, IN_WIDTH), F32),
            pltpu.VMEM((tm, ATTN_WIDTH), BF16),
            pltpu.VMEM((WINDOW + tm, 4 * KV_WIDTH), BF16),
            pltpu.VMEM((WINDOW + tm, 4 * KV_WIDTH), BF16),
            pltpu.VMEM((WINDOW, 4 * KV_WIDTH), BF16),
            pltpu.VMEM((WINDOW, 4 * KV_WIDTH), BF16),
            pltpu.VMEM((tm, ATTN_WIDTH), F32),
            pltpu.VMEM((tm, D_MODEL), BF16),
            pltpu.VMEM((N_GMLP_GROUPS, CHUNK, CHUNK), BF16),
        ],
        compiler_params=pltpu.CompilerParams(dimension_semantics=("arbitrary", "arbitrary"), vmem_limit_bytes=VMEM_LIMIT),
        name="prompt_mixer",
    )(sinks, x, shift, scale, gate, w_in, w_o, cos_t, sin_lo, sin_hi, gn_g.reshape(1, GMLP_WIDTH),
      gn_b.reshape(1, GMLP_WIDTH), w_s, b_s.T, out_g.reshape(1, D_MODEL), ln_g.reshape(1, D_MODEL),
      ln_b.reshape(1, D_MODEL))


def _sattn_kernel(qe_ref, p_ref, ck_ref, cv_ref, sink_ref, cos_ref, slo_ref, shi_ref, o_ref, ko_ref, vo_ref):
    cos, slo, shi = cos_ref[...], slo_ref[...], shi_ref[...]
    qe = _rope_lanes(qe_ref[0], cos, slo, shi) * (HEAD_DIM ** -0.5)
    kn = _rope_lanes(p_ref[0, :, Q_END:K_END], cos, slo, shi)
    vn = p_ref[0, :, K_END:V_END]
    ck = ck_ref[0]
    cv = cv_ref[0]
    sink = sink_ref[...]
    s_c = _dot_nt(qe.astype(BF16), ck.astype(BF16))
    key = lax.broadcasted_iota(jnp.int32, s_c.shape, 1)
    s_c = jnp.where(key >= 1, s_c, -jnp.inf)
    s_n = jnp.sum(qe * kn, axis=-1, keepdims=True)
    mx = jnp.maximum(jnp.maximum(jnp.max(s_c, axis=-1, keepdims=True), s_n), sink)
    p_c = jnp.exp(s_c - mx)
    p_n = jnp.exp(s_n - mx)
    denom = jnp.sum(p_c, axis=-1, keepdims=True) + p_n + jnp.exp(sink - mx)
    o = (_dot(p_c.astype(BF16), cv.astype(BF16)) + p_n * vn) / denom
    head = lax.broadcasted_iota(jnp.int32, o.shape, 0)
    lane = lax.broadcasted_iota(jnp.int32, o.shape, 1)
    o_ref[0] = jnp.where(head // GQA_GROUP == lane // HEAD_DIM, o, 0.0)
    row = lax.broadcasted_iota(jnp.int32, ck.shape, 0)
    last = row == WINDOW - 1
    ko_ref[0] = jnp.where(last, kn, pltpu.roll(ck, WINDOW - 1, 0))
    vo_ref[0] = jnp.where(last, vn, pltpu.roll(cv, WINDOW - 1, 0))


def _sattn(qe, p3, ck, cv, sinks, tables):
    b = qe.shape[0]
    cos_t, sin_lo, sin_hi = tables
    tab_spec = pl.BlockSpec((1, LANES), lambda i: (0, 0))
    blk3 = lambda i: (i, 0, 0)
    return pl.pallas_call(
        _sattn_kernel,
        grid=(b,),
        in_specs=[
            pl.BlockSpec((1, N_HEADS, KV_WIDTH), blk3),
            pl.BlockSpec((1, 1, IN_WIDTH), blk3),
            pl.BlockSpec((1, WINDOW, KV_WIDTH), blk3),
            pl.BlockSpec((1, WINDOW, KV_WIDTH), blk3),
            pl.BlockSpec((N_HEADS, 1), lambda i: (0, 0)),
            tab_spec, tab_spec, tab_spec,
        ],
        out_specs=[
            pl.BlockSpec((1, N_HEADS, KV_WIDTH), blk3),
            pl.BlockSpec((1, WINDOW, KV_WIDTH), blk3),
            pl.BlockSpec((1, WINDOW, KV_WIDTH), blk3),
        ],
        out_shape=[
            jax.ShapeDtypeStruct((b, N_HEADS, KV_WIDTH), F32),
            jax.ShapeDtypeStruct((b, WINDOW, KV_WIDTH), F32),
            jax.ShapeDtypeStruct((b, WINDOW, KV_WIDTH), F32),
        ],
        compiler_params=pltpu.CompilerParams(dimension_semantics=("arbitrary",)),
        name="sample_attn",
    )(qe, p3, ck, cv, sinks.reshape(N_HEADS, 1), cos_t, sin_lo, sin_hi)


def _smerge_kernel(attn_ref, p_ref, gng_ref, gnb_ref, w0_ref, b0_ref, og_ref, m_ref, vn_ref):
    og = og_ref[...]
    vn = jnp.concatenate(_gmlp_norm(p_ref[:, U_END:IN_WIDTH], gng_ref[...], gnb_ref[...]), axis=-1)
    vn_ref[...] = vn
    u = _gelu_tanh(p_ref[:, V_END:U_END])
    gm = u * (w0_ref[...] * vn + b0_ref[...])
    m_ref[:, 0:ATTN_WIDTH] = _rms_norm(attn_ref[...], og[:, 0:ATTN_WIDTH]).astype(BF16)
    m_ref[:, ATTN_WIDTH:] = _rms_norm(gm, og[:, ATTN_WIDTH:]).astype(BF16)


def _smerge(attn, p, gn_g, gn_b, w_s, b_s, out_g):
    b = attn.shape[0]
    w0 = jnp.repeat(w_s[:, 0, 0], LANES).reshape(1, GMLP_WIDTH)
    b0 = jnp.repeat(b_s[:, 0], LANES).reshape(1, GMLP_WIDTH)
    return pl.pallas_call(
        _smerge_kernel,
        out_shape=[
            jax.ShapeDtypeStruct((b, D_MODEL), BF16),
            jax.ShapeDtypeStruct((b, GMLP_WIDTH), F32),
        ],
        name="sample_gmlp_merge",
    )(attn, p, gn_g.reshape(1, GMLP_WIDTH), gn_b.reshape(1, GMLP_WIDTH), w0, b0, out_g.reshape(1, D_MODEL))


def kernel(x_prompt, x_sample, cache_k_win, cache_v_win, c_prompt, c_sample, w_ada, b_ada, ln_g, ln_b,
           w_ffn_up, w_ffn_down, w_in, attn_sinks, gmlp_norm_g, gmlp_norm_b, w_spatial, b_spatial,
           out_norm_g, w_o):
    batch, seq, _ = x_prompt.shape
    dec_batch = x_sample.shape[0]
    buf = cache_k_win.shape[2]
    assert seq == SEQ and buf == WINDOW and x_sample.shape[1] == 1 and w_ada.shape[0] == DEPTH == 1

    xp = x_prompt.reshape(batch * seq, D_MODEL)
    xs = x_sample.reshape(dec_batch, D_MODEL)

    n_c = batch + dec_batch
    n_c_pad = ((n_c + 7) // 8) * 8
    c_all = jnp.concatenate([c_prompt, c_sample, jnp.zeros((n_c_pad - n_c, D_MODEL), F32)], axis=0)
    mod = _ada(c_all, w_ada[0], b_ada[0]).reshape(n_c_pad, N_SUB, 3, D_MODEL)
    mod_p = mod[:batch]
    mod_s = mod[batch:n_c]

    def pmod(i, j):
        return mod_p[:, i, j].reshape(batch, 1, D_MODEL)

    def smod(i, j):
        return mod_s[:, i, j].reshape(1, dec_batch, D_MODEL)

    ffn = functools.partial(_ffn, tm=FFN_ROWS, rows_per_mod=seq)

    w_up0 = w_ffn_up[0, 0].astype(BF16)
    w_down0 = w_ffn_down[0, 0].astype(BF16)
    later = ((w_ffn_up, (0, 1), 16), (w_ffn_down, (0, 1), 32), (w_in, (0,), 16), (w_o, (0,), 16))
    xp, xs, w_up1, w_down1, w_in_b, w_o_b = ffn(
        xp, pmod(0, 0), pmod(0, 1), pmod(0, 2), w_up0, w_down0, ln_g[0, 0], ln_b[0, 0],
        extra=(xs, mod_s[:, 0, 0], mod_s[:, 0, 1], mod_s[:, 0, 2]), side=later)

    xp, k_p, v_p = _pmix(xp, pmod(1, 0), pmod(1, 1), pmod(1, 2), w_in_b, w_o_b, attn_sinks[0],
                         _rope_tables(jnp.arange(seq)), gmlp_norm_g[0], gmlp_norm_b[0], w_spatial[0], b_spatial[0],
                         out_norm_g[0], ln_g[0, 1], ln_b[0, 1], batch=batch, tm=MIX_TILE)

    ps = _inproj(xs, smod(1, 0), smod(1, 1), w_in_b, tm=dec_batch, rows_per_mod=dec_batch)
    eye = jnp.eye(N_KV_HEADS, dtype=F32)
    q5 = ps[:, :Q_END].reshape(dec_batch, N_KV_HEADS, GQA_GROUP, 1, HEAD_DIM)
    qe = (q5 * eye[None, :, None, :, None]).reshape(dec_batch, N_HEADS, KV_WIDTH)
    ck = cache_k_win[0].reshape(dec_batch, buf, KV_WIDTH)
    cv = cache_v_win[0].reshape(dec_batch, buf, KV_WIDTH)
    oe, k_s, v_s = _sattn(qe, ps.reshape(dec_batch, 1, IN_WIDTH), ck, cv, attn_sinks[0],
                          _rope_tables(jnp.full((1,), PAST_LEN)))
    attn_s = oe.reshape(dec_batch, N_KV_HEADS, GQA_GROUP, N_KV_HEADS, HEAD_DIM).sum(axis=3).reshape(dec_batch, ATTN_WIDTH)
    merged_s, vn_s = _smerge(attn_s, ps, gmlp_norm_g[0], gmlp_norm_b[0], w_spatial[0], b_spatial[0], out_norm_g[0])
    xs = _outproj(merged_s, xs, smod(1, 2), w_o_b, ln_g[0, 1], ln_b[0, 1], tm=dec_batch, rows_per_mod=dec_batch)

    xp, xs = ffn(xp, pmod(2, 0), pmod(2, 1), pmod(2, 2), w_up1, w_down1, ln_g[0, 2], ln_b[0, 2],
                 extra=(xs, mod_s[:, 2, 0], mod_s[:, 2, 1], mod_s[:, 2, 2]))

    return (
        xp.reshape(batch, seq, D_MODEL),
        xs.reshape(dec_batch, 1, D_MODEL),
        k_p.reshape(1, batch, WINDOW, N_KV_HEADS, HEAD_DIM),
        v_p.reshape(1, batch, WINDOW, N_KV_HEADS, HEAD_DIM),
        k_s.reshape(1, dec_batch, buf, N_KV_HEADS, HEAD_DIM),
        v_s.reshape(1, dec_batch, buf, N_KV_HEADS, HEAD_DIM),
        vn_s.reshape(1, dec_batch, 1, GMLP_WIDTH),
    )
```

```python
import functools

import jax
import jax.numpy as jnp
import numpy as np
from jax import lax
from jax.experimental import pallas as pl
from jax.experimental.pallas import tpu as pltpu

D_MODEL = 2048
SEQ = 2048
PAST_LEN = 16384
ATTN_WIDTH = 1024
GMLP_WIDTH = 1024
HEAD_DIM = 64
N_HEADS = 16
N_KV_HEADS = 4
GQA_GROUP = 4
KV_WIDTH = 256
WINDOW = 128
ROPE_THETA = 500000.0
ROPE_DIM = 16
CHUNK = 128
N_GMLP_GROUPS = 8
D_FF = 5504
N_SUB = 3
DEPTH = 1
ALPHA = (2.0 * DEPTH) ** 0.25
LN_EPS = 1e-5
Q_END = ATTN_WIDTH
K_END = Q_END + KV_WIDTH
V_END = K_END + KV_WIDTH
U_END = V_END + GMLP_WIDTH
IN_WIDTH = U_END + GMLP_WIDTH

LANES = 128
MXU_COLS = 256
FF_TILE = 512
N_FF_STEPS = -(-D_FF // FF_TILE)
FF_LAST_OFF = D_FF - FF_TILE
FF_OVERLAP = N_FF_STEPS * FF_TILE - D_FF
OUT_TILE = 512
N_OUT_STEPS = D_MODEL // OUT_TILE
FFN_ROWS = 512
MIX_TILE = 256
PROJ_TILE = 512
SAMPLE_ROWS_PER_STEP = 8
VMEM_LIMIT = 60 * 1024 * 1024

BF16 = jnp.bfloat16
F32 = jnp.float32


def _dot(a, b):
    return jnp.dot(a, b, preferred_element_type=F32)


def _dot_nt(a, b):
    return lax.dot_general(a, b, (((1,), (1,)), ((), ())), preferred_element_type=F32)


def _layer_norm(y, g, b):
    mu = jnp.mean(y, axis=-1, keepdims=True)
    d = y - mu
    var = jnp.mean(d * d, axis=-1, keepdims=True)
    return d * lax.rsqrt(var + LN_EPS) * g + b


def _rms_norm(y, g):
    return y * lax.rsqrt(jnp.mean(y * y, axis=-1, keepdims=True) + LN_EPS) * g


def _gelu_tanh(x):
    c = np.float32(np.sqrt(2.0 / np.pi))
    return 0.5 * x * (1.0 + jnp.tanh(c * (x + 0.044715 * (x * x * x))))


def _rope_lanes(t, cos, sin_lo, sin_hi):
    pieces = []
    for c in range(t.shape[-1] // LANES):
        x = t[:, c * LANES:(c + 1) * LANES]
        pieces.append(x * cos + pltpu.roll(x, 8, 1) * sin_hi + pltpu.roll(x, LANES - 8, 1) * sin_lo)
    return pieces[0] if len(pieces) == 1 else jnp.concatenate(pieces, axis=-1)


def _ada_kernel(c_ref, w_ref, b_ref, o_ref):
    c = c_ref[...]
    h = (c * jax.nn.sigmoid(c)).astype(BF16)
    o_ref[0] = _dot(h, w_ref[...].astype(BF16)) + b_ref[...]


def _ada(c, w_ada, b_ada):
    rows = c.shape[0]
    n = w_ada.shape[1]
    tn = 1024
    per_plane = D_MODEL // tn
    return pl.pallas_call(
        _ada_kernel,
        grid=(n // tn,),
        in_specs=[
            pl.BlockSpec((rows, D_MODEL), lambda j: (0, 0)),
            pl.BlockSpec((D_MODEL, tn), lambda j: (0, j)),
            pl.BlockSpec((1, tn), lambda j: (0, j)),
        ],
        out_specs=pl.BlockSpec((1, rows, tn), lambda j: (j // per_plane, 0, j % per_plane)),
        out_shape=jax.ShapeDtypeStruct((n // D_MODEL, rows, D_MODEL), F32),
        compiler_params=pltpu.CompilerParams(dimension_semantics=("arbitrary",), vmem_limit_bytes=VMEM_LIMIT),
        name="ada_mod",
    )(c, w_ada, b_ada.reshape(1, n))


def _ff_offset(step):
    return jnp.minimum(step * FF_TILE, FF_LAST_OFF)


def _ffn_kernel(n_side, n_extra, tiles_per_mod, extra_row0, *refs):
    x_ref, mod_ref, wv_ref, wg_ref, wd_ref, gb_ref = refs[:6]
    n_in = 6
    if n_extra:
        xe_ref = refs[n_in]
        n_in += 1
        erows = slice(extra_row0, extra_row0 + n_extra)
    side_in = refs[n_in:n_in + n_side]
    n_in += n_side
    o_ref = refs[n_in]
    n_out = 1
    if n_extra:
        oe_ref = refs[n_in + 1]
        n_out = 2
    side_out = refs[n_in + n_out:n_in + n_out + n_side]
    xm_ref, a_ref, y_ref = refs[n_in + n_out + n_side:]
    i = pl.program_id(0)
    s = pl.program_id(1)
    tm = x_ref.shape[0]
    last = pl.num_programs(1) - 1

    def cast_side():
        for src, dst in zip(side_in, side_out):
            dst[...] = src[...].astype(BF16)

    def up_chunk(rows):
        cast_side()
        xm = xm_ref[0:rows]
        for c in range(FF_TILE // MXU_COLS):
            sl = slice(c * MXU_COLS, (c + 1) * MXU_COLS)
            hv = _dot(xm, wv_ref[:, sl])
            hg = _dot(xm, wg_ref[:, sl])
            a = hg * jax.nn.sigmoid(hg) * hv
            if c * MXU_COLS < FF_OVERLAP:
                col = lax.broadcasted_iota(jnp.int32, a.shape, 1) + c * MXU_COLS
                a = jnp.where(jnp.logical_and(s == N_FF_STEPS - 1, col < FF_OVERLAP), 0.0, a)
            a_ref[s, 0:rows, sl] = a.astype(BF16)

    def down_slab(rows):
        cast_side()
        acc = None
        for f in range(N_FF_STEPS):
            r0 = min(f * FF_TILE, FF_LAST_OFF)
            part = _dot(a_ref[f, 0:rows], wd_ref[r0:r0 + FF_TILE, :])
            acc = part if acc is None else acc + part
        y_ref[s - N_FF_STEPS, 0:rows] = acc

    def finish(rows, x, gate, out_ref):
        mixed = jnp.concatenate([y_ref[n, rows] for n in range(N_OUT_STEPS)], axis=-1)
        out_ref[...] = _layer_norm(ALPHA * x + 0.5 * gate * mixed, gb_ref[0:1], gb_ref[1:2])

    seq = pl.ds(i // tiles_per_mod, 1)

    @pl.when(s == 0)
    def _():
        xm_ref[0:tm] = (x_ref[...] * (1.0 + mod_ref[1, seq, :]) + mod_ref[0, seq, :]).astype(BF16)

    if n_extra:
        with_extra = i == 0

        @pl.when(jnp.logical_and(s == 0, with_extra))
        def _():
            xm_ref[tm:tm + n_extra] = (xe_ref[...] * (1.0 + mod_ref[1, erows, :]) + mod_ref[0, erows, :]).astype(BF16)

        for phase, body in ((s < N_FF_STEPS, up_chunk), (s >= N_FF_STEPS, down_slab)):
            pl.when(jnp.logical_and(phase, with_extra))(functools.partial(body, tm + n_extra))
            pl.when(jnp.logical_and(phase, jnp.logical_not(with_extra)))(functools.partial(body, tm))

        @pl.when(jnp.logical_and(s == last, with_extra))
        def _():
            finish(slice(tm, tm + n_extra), xe_ref[...], mod_ref[2, erows, :], oe_ref)
    else:
        pl.when(s < N_FF_STEPS)(functools.partial(up_chunk, tm))
        pl.when(s >= N_FF_STEPS)(functools.partial(down_slab, tm))

    @pl.when(s == last)
    def _():
        finish(slice(0, tm), x_ref[...], mod_ref[2, seq, :], o_ref)


def _ffn(x, mod, sub, w_up, w_down, ln_gb, *, tm, rows_per_mod, extra=None, extra_row0=0, side=()):
    m = x.shape[0]
    n_steps = N_FF_STEPS + N_OUT_STEPS
    tiles_per_mod = rows_per_mod // tm
    n_extra = 0 if extra is None else extra.shape[0]
    const2 = lambda i, s: (0, 0)
    extra_spec = pl.BlockSpec((n_extra, D_MODEL), const2)
    side_in_specs, side_out_specs, side_shapes = [], [], []
    for arr, lead, rows in side:
        n_rows, n_cols = arr.shape[-2:]
        n_blocks = n_rows // rows
        assert n_blocks * rows == n_rows and n_blocks <= (m // tm) * n_steps
        blk = lambda i, s, n_blocks=n_blocks: jnp.minimum(i * n_steps + s, n_blocks - 1)
        side_in_specs.append(pl.BlockSpec((None,) * len(lead) + (rows, n_cols),
                                          lambda i, s, lead=lead, blk=blk: (*lead, blk(i, s), 0)))
        side_out_specs.append(pl.BlockSpec((rows, n_cols), lambda i, s, blk=blk: (blk(i, s), 0)))
        side_shapes.append(jax.ShapeDtypeStruct((n_rows, n_cols), BF16))
    wd_block = lambda i, s: (0, jnp.where(s < N_FF_STEPS, N_OUT_STEPS - 1, s - N_FF_STEPS))
    outs = pl.pallas_call(
        functools.partial(_ffn_kernel, len(side), n_extra, tiles_per_mod, extra_row0),
        grid=(m // tm, n_steps),
        in_specs=[
            pl.BlockSpec((tm, D_MODEL), lambda i, s: (i, 0)),
            pl.BlockSpec((3, mod.shape[1], D_MODEL), lambda i, s: (sub, 0, 0)),
            pl.BlockSpec((pl.Element(D_MODEL), pl.Element(FF_TILE)),
                         lambda i, s: (0, pl.multiple_of(_ff_offset(s), LANES))),
            pl.BlockSpec((pl.Element(D_MODEL), pl.Element(FF_TILE)),
                         lambda i, s: (0, pl.multiple_of(D_FF + _ff_offset(s), LANES))),
            pl.BlockSpec((D_FF, OUT_TILE), wd_block),
            pl.BlockSpec((2, D_MODEL), const2),
            *([extra_spec] if n_extra else []),
            *side_in_specs,
        ],
        out_specs=[pl.BlockSpec((tm, D_MODEL), lambda i, s: (i, 0)), *([extra_spec] if n_extra else []),
                   *side_out_specs],
        out_shape=[jax.ShapeDtypeStruct((m, D_MODEL), F32),
                   *([jax.ShapeDtypeStruct((n_extra, D_MODEL), F32)] if n_extra else []), *side_shapes],
        scratch_shapes=[
            pltpu.VMEM((tm + n_extra, D_MODEL), BF16),
            pltpu.VMEM((N_FF_STEPS, tm + n_extra, FF_TILE), BF16),
            pltpu.VMEM((N_OUT_STEPS, tm + n_extra, OUT_TILE), F32),
        ],
        compiler_params=pltpu.CompilerParams(dimension_semantics=("arbitrary", "arbitrary"), vmem_limit_bytes=VMEM_LIMIT),
        name="swiglu_ln",
    )(x, mod, w_up, w_up, w_down, ln_gb, *([extra] if n_extra else []), *[arr for arr, _, _ in side])
    return outs


def _inproj_kernel(x_ref, shift_ref, scale_ref, w_ref, o_ref, xm_ref):
    @pl.when(pl.program_id(1) == 0)
    def _():
        xm_ref[...] = (x_ref[...] * (1.0 + scale_ref[0]) + shift_ref[0]).astype(BF16)

    o_ref[...] = _dot(xm_ref[...], w_ref[...])


def _inproj(x, shift, scale, w_in, *, tm, rows_per_mod):
    m = x.shape[0]
    r = shift.shape[1]
    tn = 512
    tiles_per_mod = rows_per_mod // tm
    mod_spec = pl.BlockSpec((1, r, D_MODEL), lambda i, j: (i // tiles_per_mod, 0, 0))
    return pl.pallas_call(
        _inproj_kernel,
        grid=(m // tm, IN_WIDTH // tn),
        in_specs=[
            pl.BlockSpec((tm, D_MODEL), lambda i, j: (i, 0)),
            mod_spec, mod_spec,
            pl.BlockSpec((D_MODEL, tn), lambda i, j: (0, j)),
        ],
        out_specs=pl.BlockSpec((tm, tn), lambda i, j: (i, j)),
        out_shape=jax.ShapeDtypeStruct((m, IN_WIDTH), F32),
        scratch_shapes=[pltpu.VMEM((tm, D_MODEL), BF16)],
        compiler_params=pltpu.CompilerParams(dimension_semantics=("arbitrary", "arbitrary"), vmem_limit_bytes=VMEM_LIMIT),
        name="mixer_inproj",
    )(x, shift, scale, w_in)


def _outproj_kernel(m_ref, x_ref, gate_ref, w_ref, g_ref, b_ref, o_ref):
    mixed = _dot(m_ref[...], w_ref[...])
    y = ALPHA * x_ref[...] + gate_ref[0] * mixed
    o_ref[...] = _layer_norm(y, g_ref[...], b_ref[...])


def _outproj(merged, x, gate, w_o, ln_g, ln_b, *, tm, rows_per_mod):
    m = x.shape[0]
    r = gate.shape[1]
    tiles_per_mod = rows_per_mod // tm
    return pl.pallas_call(
        _outproj_kernel,
        grid=(m // tm,),
        in_specs=[
            pl.BlockSpec((tm, D_MODEL), lambda i: (i, 0)),
            pl.BlockSpec((tm, D_MODEL), lambda i: (i, 0)),
            pl.BlockSpec((1, r, D_MODEL), lambda i: (i // tiles_per_mod, 0, 0)),
            pl.BlockSpec((D_MODEL, D_MODEL), lambda i: (0, 0)),
            pl.BlockSpec((1, D_MODEL), lambda i: (0, 0)),
            pl.BlockSpec((1, D_MODEL), lambda i: (0, 0)),
        ],
        out_specs=pl.BlockSpec((tm, D_MODEL), lambda i: (i, 0)),
        out_shape=jax.ShapeDtypeStruct((m, D_MODEL), F32),
        compiler_params=pltpu.CompilerParams(dimension_semantics=("arbitrary",), vmem_limit_bytes=VMEM_LIMIT),
        name="mixer_outproj_ln",
    )(merged, x, gate, w_o, ln_g.reshape(1, D_MODEL), ln_b.reshape(1, D_MODEL))


def _gmlp_norm(pv, gng, gnb):
    v = _gelu_tanh(pv)
    outs = []
    for g in range(N_GMLP_GROUPS):
        sl = slice(g * LANES, (g + 1) * LANES)
        outs.append(_layer_norm(v[:, sl], gng[:, sl], gnb[:, sl]))
    return outs


def _half_lane_tiles(t):
    lo_lane = lax.broadcasted_iota(jnp.int32, (t.shape[0], LANES), 1) < HEAD_DIM
    lo, hi = [], []
    for c in range(KV_WIDTH // LANES):
        col = t[:, c * LANES:(c + 1) * LANES]
        swp = pltpu.roll(col, HEAD_DIM, 1)
        lo += [jnp.where(lo_lane, col, 0.0), jnp.where(lo_lane, swp, 0.0)]
        hi += [jnp.where(lo_lane, 0.0, swp), jnp.where(lo_lane, 0.0, col)]
    return jnp.concatenate(lo + hi, axis=-1).astype(BF16)


def _band_attention(q_s, kx_s, vx_s, attn_s, sinks_ref, nblk, first_tile):
    rows = 2 * WINDOW
    qi = lax.broadcasted_iota(jnp.int32, (rows, WINDOW), 0) & (WINDOW - 1)
    kj = lax.broadcasted_iota(jnp.int32, (rows, WINDOW), 1)
    own = kj <= qi
    first_mask = kj <= jnp.where(first_tile, qi, WINDOW)
    ones = jnp.ones((2 * WINDOW, LANES), BF16)
    units = [(wb, g, hf) for wb in range(nblk) for g in range(N_KV_HEADS) for hf in range(2)]

    def kv_cols(g, hf):
        return slice(hf * 4 * LANES + g * LANES, hf * 4 * LANES + (g + 1) * LANES)

    scores = []
    for wb, g, hf in units:
        r0 = wb * WINDOW
        q2 = jnp.concatenate([q_s[r0:r0 + WINDOW, (2 * g) * LANES:(2 * g + 1) * LANES],
                              q_s[r0:r0 + WINDOW, (2 * g + 1) * LANES:(2 * g + 2) * LANES]], axis=0)
        s = _dot_nt(q2, kx_s[r0:r0 + 2 * WINDOW, kv_cols(g, hf)])
        c = jnp.where(own, s[:, WINDOW:2 * WINDOW], s[:, 0:WINDOW])
        if wb == 0:
            c = jnp.where(first_mask, c, -jnp.inf)
        scores.append(c)

    probs = []
    for (wb, g, hf), c in zip(units, scores):
        sink = jnp.concatenate([jnp.full((WINDOW, LANES), sinks_ref[GQA_GROUP * g + hf], F32),
                                jnp.full((WINDOW, LANES), sinks_ref[GQA_GROUP * g + 2 + hf], F32)], axis=0)
        mx = jnp.maximum(jnp.broadcast_to(jnp.max(c, axis=-1, keepdims=True), c.shape), sink)
        p = jnp.exp(c - mx)
        pcat = jnp.concatenate([jnp.where(own, 0.0, p), jnp.where(own, p, 0.0)], axis=-1).astype(BF16)
        probs.append((pcat, jnp.exp(sink - mx)))

    outs = {}
    for (wb, g, hf), (pcat, esink) in zip(units, probs):
        r0 = wb * WINDOW
        v2 = jnp.concatenate([vx_s[r0:r0 + 2 * WINDOW, kv_cols(g, hf)], ones], axis=-1)
        o = _dot(pcat, v2)
        outs[wb, g, hf] = o[:, 0:LANES] / (o[:, LANES:2 * LANES] + esink)

    for wb in range(nblk):
        r0 = wb * WINDOW
        for g in range(N_KV_HEADS):
            both = outs[wb, g, 0] + outs[wb, g, 1]
            attn_s[r0:r0 + WINDOW, (2 * g) * LANES:(2 * g + 1) * LANES] = both[0:WINDOW]
            attn_s[r0:r0 + WINDOW, (2 * g + 1) * LANES:(2 * g + 2) * LANES] = both[WINDOW:2 * WINDOW]


def _pmix_kernel(sinks_ref, x_ref, mod_ref, win_ref, wo_ref, cos_ref, slo_ref, shi_ref,
                 gng_ref, gnb_ref, ws_ref, bs_ref, og_ref, lngb_ref,
                 o_ref, k_ref, v_ref, xm_s, p_s, q_s, kk_s, vv_s, kp_s, vp_s, attn_s, m_s, wm_s):
    t = pl.program_id(1)
    tm = x_ref.shape[0]
    nblk = tm // WINDOW

    @pl.when(jnp.logical_and(pl.program_id(0) == 0, t == 0))
    def _():
        row = lax.broadcasted_iota(jnp.int32, (CHUNK, CHUNK), 0)
        col = lax.broadcasted_iota(jnp.int32, (CHUNK, CHUNK), 1)
        for g in range(N_GMLP_GROUPS):
            wm_s[g] = jnp.where(col <= row, ws_ref[g], 0.0).astype(BF16)

    @pl.when(t == 0)
    def _():
        kp_s[...] = jnp.zeros(kp_s.shape, BF16)
        vp_s[...] = jnp.zeros(vp_s.shape, BF16)

    seq = pl.ds(pl.program_id(0), 1)
    xm_s[...] = (x_ref[...] * (1.0 + mod_ref[1, seq, :]) + mod_ref[0, seq, :]).astype(BF16)
    for c in range(IN_WIDTH // PROJ_TILE):
        sl = slice(c * PROJ_TILE, (c + 1) * PROJ_TILE)
        p_s[:, sl] = _dot(xm_s[...], win_ref[:, sl])

    cos, slo, shi = cos_ref[...], slo_ref[...], shi_ref[...]
    q_s[...] = (_rope_lanes(p_s[:, 0:Q_END], cos, slo, shi) * (HEAD_DIM ** -0.5)).astype(BF16)
    k = _rope_lanes(p_s[:, Q_END:K_END], cos, slo, shi)
    v = p_s[:, K_END:V_END]
    k_ref[0] = k[tm - WINDOW:tm]
    v_ref[0] = v[tm - WINDOW:tm]
    kb, vb = _half_lane_tiles(k), _half_lane_tiles(v)
    kk_s[0:WINDOW, :] = kp_s[...]
    vv_s[0:WINDOW, :] = vp_s[...]
    kk_s[WINDOW:WINDOW + tm, :] = kb
    vv_s[WINDOW:WINDOW + tm, :] = vb
    kp_s[...] = kb[tm - WINDOW:tm]
    vp_s[...] = vb[tm - WINDOW:tm]

    _band_attention(q_s, kk_s, vv_s, attn_s, sinks_ref, nblk, t == 0)

    vn = _gmlp_norm(p_s[:, U_END:IN_WIDTH], gng_ref[...], gnb_ref[...])
    gated = []
    for g in range(N_GMLP_GROUPS):
        u = _gelu_tanh(p_s[:, V_END + g * LANES:V_END + (g + 1) * LANES])
        vg = vn[g].astype(BF16)
        cols = jnp.concatenate([vg[wb * CHUNK:(wb + 1) * CHUNK] for wb in range(nblk)], axis=-1)
        mix = _dot(wm_s[g], cols)
        mix = jnp.concatenate([mix[:, wb * LANES:(wb + 1) * LANES] for wb in range(nblk)], axis=0)
        bias = jnp.concatenate([bs_ref[:, g:g + 1]] * nblk, axis=0)
        gated.append(u * (mix + bias))
    gm = jnp.concatenate(gated, axis=-1)
    og = og_ref[...]
    m_s[:, 0:ATTN_WIDTH] = _rms_norm(attn_s[...], og[:, 0:ATTN_WIDTH]).astype(BF16)
    m_s[:, ATTN_WIDTH:] = _rms_norm(gm, og[:, ATTN_WIDTH:]).astype(BF16)

    for c in range(D_MODEL // PROJ_TILE):
        sl = slice(c * PROJ_TILE, (c + 1) * PROJ_TILE)
        o_ref[:, sl] = ALPHA * x_ref[:, sl] + mod_ref[2, seq, sl] * _dot(m_s[...], wo_ref[:, sl])
    o_ref[...] = _layer_norm(o_ref[...], lngb_ref[0:1], lngb_ref[1:2])


def _rope_tables(pos):
    half = ROPE_DIM // 2
    inv = ROPE_THETA ** (-(jnp.arange(half, dtype=F32) * 2.0) / ROPE_DIM)
    ang = pos.astype(F32)[:, None] * inv[None, :]
    cos, sin = jnp.cos(ang), jnp.sin(ang)
    n = pos.shape[0]
    one = jnp.ones((n, HEAD_DIM - ROPE_DIM), F32)
    zero = jnp.zeros((n, HEAD_DIM - ROPE_DIM), F32)
    z8 = jnp.zeros((n, half), F32)
    cos_t = jnp.concatenate([cos, cos, one], axis=-1)
    sin_lo = jnp.concatenate([-sin, z8, zero], axis=-1)
    sin_hi = jnp.concatenate([z8, sin, zero], axis=-1)
    rep = LANES // HEAD_DIM
    return jnp.tile(cos_t, (1, rep)), jnp.tile(sin_lo, (1, rep)), jnp.tile(sin_hi, (1, rep))


def _pmix(x, mod, sub, w_in, w_o, sinks, tables, gn_g, gn_b, w_s, b_s, out_g, ln_gb, *, batch, tm):
    nt = SEQ // tm
    cos_t, sin_lo, sin_hi = tables
    tab_spec = pl.BlockSpec((tm, LANES), lambda b, j: (j, 0))
    full2 = lambda b, j: (0, 0)
    mod_spec = pl.BlockSpec((3, mod.shape[1], D_MODEL), lambda b, j: (sub, 0, 0))
    resident = dict(pipeline_mode=pl.Buffered(1))
    return pl.pallas_call(
        _pmix_kernel,
        grid=(batch, nt),
        in_specs=[
            pl.BlockSpec(memory_space=pltpu.SMEM),
            pl.BlockSpec((tm, D_MODEL), lambda b, j: (b * nt + j, 0)),
            mod_spec,
            pl.BlockSpec((D_MODEL, IN_WIDTH), full2, **resident),
            pl.BlockSpec((D_MODEL, D_MODEL), full2, **resident),
            tab_spec, tab_spec, tab_spec,
            pl.BlockSpec((1, GMLP_WIDTH), full2),
            pl.BlockSpec((1, GMLP_WIDTH), full2),
            pl.BlockSpec((N_GMLP_GROUPS, CHUNK, CHUNK), lambda b, j: (0, 0, 0)),
            pl.BlockSpec((CHUNK, N_GMLP_GROUPS), full2),
            pl.BlockSpec((1, D_MODEL), full2),
            pl.BlockSpec((2, D_MODEL), full2),
        ],
        out_specs=[
            pl.BlockSpec((tm, D_MODEL), lambda b, j: (b * nt + j, 0)),
            pl.BlockSpec((1, WINDOW, KV_WIDTH), lambda b, j: (b, 0, 0)),
            pl.BlockSpec((1, WINDOW, KV_WIDTH), lambda b, j: (b, 0, 0)),
        ],
        out_shape=[
            jax.ShapeDtypeStruct((batch * SEQ, D_MODEL), F32),
            jax.ShapeDtypeStruct((batch, WINDOW, KV_WIDTH), F32),
            jax.ShapeDtypeStruct((batch, WINDOW, KV_WIDTH), F32),
        ],
        scratch_shapes=[
            pltpu.VMEM((tm, D_MODEL), BF16),
            pltpu.VMEM((tm, IN_WIDTH), F32),
            pltpu.VMEM((tm, ATTN_WIDTH), BF16),
            pltpu.VMEM((WINDOW + tm, 4 * KV_WIDTH), BF16),
            pltpu.VMEM((WINDOW + tm, 4 * KV_WIDTH), BF16),
            pltpu.VMEM((WINDOW, 4 * KV_WIDTH), BF16),
            pltpu.VMEM((WINDOW, 4 * KV_WIDTH), BF16),
            pltpu.VMEM((tm, ATTN_WIDTH), F32),
            pltpu.VMEM((tm, D_MODEL), BF16),
            pltpu.VMEM((N_GMLP_GROUPS, CHUNK, CHUNK), BF16),
        ],
        compiler_params=pltpu.CompilerParams(dimension_semantics=("arbitrary", "arbitrary"), vmem_limit_bytes=VMEM_LIMIT),
        name="prompt_mixer",
    )(sinks, x, mod, w_in, w_o, cos_t, sin_lo, sin_hi, gn_g.reshape(1, GMLP_WIDTH),
      gn_b.reshape(1, GMLP_WIDTH), w_s, b_s.T, out_g.reshape(1, D_MODEL), ln_gb)


def _sattn_kernel(qe_ref, p_ref, ck_ref, cv_ref, sink_ref, cos_ref, slo_ref, shi_ref, o_ref, ko_ref, vo_ref):
    cos, slo, shi = cos_ref[...], slo_ref[...], shi_ref[...]
    sink = sink_ref[...]
    for r in range(qe_ref.shape[0]):
        qe = _rope_lanes(qe_ref[r], cos, slo, shi) * (HEAD_DIM ** -0.5)
        kn = _rope_lanes(p_ref[r, :, Q_END:K_END], cos, slo, shi)
        vn = p_ref[r, :, K_END:V_END]
        ck = ck_ref[r]
        cv = cv_ref[r]
        s_c = _dot_nt(qe.astype(BF16), ck.astype(BF16))
        key = lax.broadcasted_iota(jnp.int32, s_c.shape, 1)
        s_c = jnp.where(key >= 1, s_c, -jnp.inf)
        s_n = jnp.sum(qe * kn, axis=-1, keepdims=True)
        mx = jnp.maximum(jnp.maximum(jnp.max(s_c, axis=-1, keepdims=True), s_n), sink)
        p_c = jnp.exp(s_c - mx)
        p_n = jnp.exp(s_n - mx)
        denom = jnp.sum(p_c, axis=-1, keepdims=True) + p_n + jnp.exp(sink - mx)
        o = (_dot(p_c.astype(BF16), cv.astype(BF16)) + p_n * vn) / denom
        head = lax.broadcasted_iota(jnp.int32, o.shape, 0)
        lane = lax.broadcasted_iota(jnp.int32, o.shape, 1)
        o_ref[r] = jnp.where(head // GQA_GROUP == lane // HEAD_DIM, o, 0.0)
        row = lax.broadcasted_iota(jnp.int32, ck.shape, 0)
        newest = row == WINDOW - 1
        ko_ref[r] = jnp.where(newest, kn, pltpu.roll(ck, WINDOW - 1, 0))
        vo_ref[r] = jnp.where(newest, vn, pltpu.roll(cv, WINDOW - 1, 0))


def _sattn(qe, p3, ck, cv, sinks, tables):
    b = qe.shape[0]
    rows = SAMPLE_ROWS_PER_STEP
    cos_t, sin_lo, sin_hi = tables
    tab_spec = pl.BlockSpec((1, LANES), lambda i: (0, 0))
    blk3 = lambda i: (i, 0, 0)
    return pl.pallas_call(
        _sattn_kernel,
        grid=(b // rows,),
        in_specs=[
            pl.BlockSpec((rows, N_HEADS, KV_WIDTH), blk3),
            pl.BlockSpec((rows, 1, IN_WIDTH), blk3),
            pl.BlockSpec((rows, WINDOW, KV_WIDTH), blk3),
            pl.BlockSpec((rows, WINDOW, KV_WIDTH), blk3),
            pl.BlockSpec((N_HEADS, 1), lambda i: (0, 0)),
            tab_spec, tab_spec, tab_spec,
        ],
        out_specs=[
            pl.BlockSpec((rows, N_HEADS, KV_WIDTH), blk3),
            pl.BlockSpec((rows, WINDOW, KV_WIDTH), blk3),
            pl.BlockSpec((rows, WINDOW, KV_WIDTH), blk3),
        ],
        out_shape=[
            jax.ShapeDtypeStruct((b, N_HEADS, KV_WIDTH), F32),
            jax.ShapeDtypeStruct((b, WINDOW, KV_WIDTH), F32),
            jax.ShapeDtypeStruct((b, WINDOW, KV_WIDTH), F32),
        ],
        compiler_params=pltpu.CompilerParams(dimension_semantics=("arbitrary",)),
        name="sample_attn",
    )(qe, p3, ck, cv, sinks.reshape(N_HEADS, 1), cos_t, sin_lo, sin_hi)


def _smerge_kernel(attn_ref, p_ref, gng_ref, gnb_ref, w0_ref, b0_ref, og_ref, m_ref, vn_ref):
    og = og_ref[...]
    vn = jnp.concatenate(_gmlp_norm(p_ref[:, U_END:IN_WIDTH], gng_ref[...], gnb_ref[...]), axis=-1)
    vn_ref[...] = vn
    u = _gelu_tanh(p_ref[:, V_END:U_END])
    gm = u * (w0_ref[...] * vn + b0_ref[...])
    m_ref[:, 0:ATTN_WIDTH] = _rms_norm(attn_ref[...], og[:, 0:ATTN_WIDTH]).astype(BF16)
    m_ref[:, ATTN_WIDTH:] = _rms_norm(gm, og[:, ATTN_WIDTH:]).astype(BF16)


def _smerge(attn, p, gn_g, gn_b, w_s, b_s, out_g):
    b = attn.shape[0]
    w0 = jnp.repeat(w_s[:, 0, 0], LANES).reshape(1, GMLP_WIDTH)
    b0 = jnp.repeat(b_s[:, 0], LANES).reshape(1, GMLP_WIDTH)
    return pl.pallas_call(
        _smerge_kernel,
        out_shape=[
            jax.ShapeDtypeStruct((b, D_MODEL), BF16),
            jax.ShapeDtypeStruct((b, GMLP_WIDTH), F32),
        ],
        name="sample_gmlp_merge",
    )(attn, p, gn_g.reshape(1, GMLP_WIDTH), gn_b.reshape(1, GMLP_WIDTH), w0, b0, out_g.reshape(1, D_MODEL))


def kernel(x_prompt, x_sample, cache_k_win, cache_v_win, c_prompt, c_sample, w_ada, b_ada, ln_g, ln_b,
           w_ffn_up, w_ffn_down, w_in, attn_sinks, gmlp_norm_g, gmlp_norm_b, w_spatial, b_spatial,
           out_norm_g, w_o):
    batch, seq, _ = x_prompt.shape
    dec_batch = x_sample.shape[0]
    buf = cache_k_win.shape[2]
    assert seq == SEQ and buf == WINDOW and x_sample.shape[1] == 1 and w_ada.shape[0] == DEPTH == 1

    xp = x_prompt.reshape(batch * seq, D_MODEL)
    xs = x_sample.reshape(dec_batch, D_MODEL)

    n_c = batch + dec_batch
    n_c_pad = ((n_c + 7) // 8) * 8
    c_all = jnp.concatenate([c_prompt, c_sample, jnp.zeros((n_c_pad - n_c, D_MODEL), F32)], axis=0)
    mod = _ada(c_all, w_ada[0], b_ada[0])

    def smod(i, j):
        return mod[3 * i + j, batch:n_c].reshape(1, dec_batch, D_MODEL)

    ln_gb = jnp.stack([ln_g[0], ln_b[0]], axis=1)
    ffn = functools.partial(_ffn, tm=FFN_ROWS, rows_per_mod=seq, extra_row0=batch)

    w_up0 = w_ffn_up[0, 0].astype(BF16)
    w_down0 = w_ffn_down[0, 0].astype(BF16)
    later = ((w_ffn_up, (0, 1), 16), (w_ffn_down, (0, 1), 32), (w_in, (0,), 16), (w_o, (0,), 16))
    xp, xs, w_up1, w_down1, w_in_b, w_o_b = ffn(xp, mod, 0, w_up0, w_down0, ln_gb[0], extra=xs, side=later)

    xp, k_p, v_p = _pmix(xp, mod, 1, w_in_b, w_o_b, attn_sinks[0], _rope_tables(jnp.arange(seq)),
                         gmlp_norm_g[0], gmlp_norm_b[0], w_spatial[0], b_spatial[0], out_norm_g[0], ln_gb[1],
                         batch=batch, tm=MIX_TILE)

    ps = _inproj(xs, smod(1, 0), smod(1, 1), w_in_b, tm=dec_batch, rows_per_mod=dec_batch)
    eye = jnp.eye(N_KV_HEADS, dtype=F32)
    q5 = ps[:, :Q_END].reshape(dec_batch, N_KV_HEADS, GQA_GROUP, 1, HEAD_DIM)
    qe = (q5 * eye[None, :, None, :, None]).reshape(dec_batch, N_HEADS, KV_WIDTH)
    ck = cache_k_win[0].reshape(dec_batch, buf, KV_WIDTH)
    cv = cache_v_win[0].reshape(dec_batch, buf, KV_WIDTH)
    oe, k_s, v_s = _sattn(qe, ps.reshape(dec_batch, 1, IN_WIDTH), ck, cv, attn_sinks[0],
                          _rope_tables(jnp.full((1,), PAST_LEN)))
    attn_s = oe.reshape(dec_batch, N_KV_HEADS, GQA_GROUP, N_KV_HEADS, HEAD_DIM).sum(axis=3).reshape(dec_batch, ATTN_WIDTH)
    merged_s, vn_s = _smerge(attn_s, ps, gmlp_norm_g[0], gmlp_norm_b[0], w_spatial[0], b_spatial[0], out_norm_g[0])
    xs = _outproj(merged_s, xs, smod(1, 2), w_o_b, ln_g[0, 1], ln_b[0, 1], tm=dec_batch, rows_per_mod=dec_batch)

    xp, xs = ffn(xp, mod, 2, w_up1, w_down1, ln_gb[2], extra=xs)

    return (
        xp.reshape(batch, seq, D_MODEL),
        xs.reshape(dec_batch, 1, D_MODEL),
        k_p.reshape(1, batch, WINDOW, N_KV_HEADS, HEAD_DIM),
        v_p.reshape(1, batch, WINDOW, N_KV_HEADS, HEAD_DIM),
        k_s.reshape(1, dec_batch, buf, N_KV_HEADS, HEAD_DIM),
        v_s.reshape(1, dec_batch, buf, N_KV_HEADS, HEAD_DIM),
        vn_s.reshape(1, dec_batch, 1, GMLP_WIDTH),
    )
```

```python
import functools

import jax
import jax.numpy as jnp
import numpy as np
from jax import lax
from jax.experimental import pallas as pl
from jax.experimental.pallas import tpu as pltpu

D_MODEL = 2048
SEQ = 2048
PAST_LEN = 16384
ATTN_WIDTH = 1024
GMLP_WIDTH = 1024
HEAD_DIM = 64
N_HEADS = 16
N_KV_HEADS = 4
GQA_GROUP = 4
KV_WIDTH = 256
WINDOW = 128
ROPE_THETA = 500000.0
ROPE_DIM = 16
CHUNK = 128
N_GMLP_GROUPS = 8
D_FF = 5504
N_SUB = 3
DEPTH = 1
ALPHA = (2.0 * DEPTH) ** 0.25
LN_EPS = 1e-5
Q_END = ATTN_WIDTH
K_END = Q_END + KV_WIDTH
V_END = K_END + KV_WIDTH
U_END = V_END + GMLP_WIDTH
IN_WIDTH = U_END + GMLP_WIDTH

LANES = 128
MXU_COLS = 256
FF_TILE = 512
N_FF_STEPS = -(-D_FF // FF_TILE)
FF_LAST_OFF = D_FF - FF_TILE
FF_OVERLAP = N_FF_STEPS * FF_TILE - D_FF
OUT_TILE = 512
N_OUT_STEPS = D_MODEL // OUT_TILE
FFN_ROWS = 512
MIX_TILE = 256
PROJ_TILE = 512
SAMPLE_ROWS_PER_STEP = 8
VMEM_LIMIT = 60 * 1024 * 1024

BF16 = jnp.bfloat16
F32 = jnp.float32


def _dot(a, b):
    return jnp.dot(a, b, preferred_element_type=F32)


def _dot_nt(a, b):
    return lax.dot_general(a, b, (((1,), (1,)), ((), ())), preferred_element_type=F32)


def _layer_norm(y, g, b):
    mu = jnp.mean(y, axis=-1, keepdims=True)
    d = y - mu
    var = jnp.mean(d * d, axis=-1, keepdims=True)
    return d * lax.rsqrt(var + LN_EPS) * g + b


def _rms_norm(y, g):
    return y * lax.rsqrt(jnp.mean(y * y, axis=-1, keepdims=True) + LN_EPS) * g


def _gelu_tanh(x):
    c = np.float32(np.sqrt(2.0 / np.pi))
    return 0.5 * x * (1.0 + jnp.tanh(c * (x + 0.044715 * (x * x * x))))


def _rope_lanes(t, cos, sin_lo, sin_hi):
    pieces = []
    for c in range(t.shape[-1] // LANES):
        x = t[:, c * LANES:(c + 1) * LANES]
        pieces.append(x * cos + pltpu.roll(x, 8, 1) * sin_hi + pltpu.roll(x, LANES - 8, 1) * sin_lo)
    return pieces[0] if len(pieces) == 1 else jnp.concatenate(pieces, axis=-1)


def _ada_kernel(c_ref, w_ref, b_ref, o_ref):
    c = c_ref[...]
    h = (c * jax.nn.sigmoid(c)).astype(BF16)
    o_ref[0] = _dot(h, w_ref[...].astype(BF16)) + b_ref[...]


def _ada(c, w_ada, b_ada):
    rows = c.shape[0]
    n = w_ada.shape[1]
    tn = 1024
    per_plane = D_MODEL // tn
    return pl.pallas_call(
        _ada_kernel,
        grid=(n // tn,),
        in_specs=[
            pl.BlockSpec((rows, D_MODEL), lambda j: (0, 0)),
            pl.BlockSpec((D_MODEL, tn), lambda j: (0, j)),
            pl.BlockSpec((1, tn), lambda j: (0, j)),
        ],
        out_specs=pl.BlockSpec((1, rows, tn), lambda j: (j // per_plane, 0, j % per_plane)),
        out_shape=jax.ShapeDtypeStruct((n // D_MODEL, rows, D_MODEL), F32),
        compiler_params=pltpu.CompilerParams(dimension_semantics=("arbitrary",), vmem_limit_bytes=VMEM_LIMIT),
        name="ada_mod",
    )(c, w_ada, b_ada.reshape(1, n))


def _ff_offset(step):
    return jnp.minimum(step * FF_TILE, FF_LAST_OFF)


def _ffn_kernel(n_side, n_extra, tiles_per_mod, extra_row0, *refs):
    x_ref, mod_ref, wv_ref, wg_ref, wd_ref, gb_ref = refs[:6]
    n_in = 6
    if n_extra:
        xe_ref = refs[n_in]
        n_in += 1
        erows = slice(extra_row0, extra_row0 + n_extra)
    side_in = refs[n_in:n_in + n_side]
    n_in += n_side
    o_ref = refs[n_in]
    n_out = 1
    if n_extra:
        oe_ref = refs[n_in + 1]
        n_out = 2
    side_out = refs[n_in + n_out:n_in + n_out + n_side]
    xm_ref, a_ref, y_ref = refs[n_in + n_out + n_side:]
    i = pl.program_id(0)
    s = pl.program_id(1)
    tm = x_ref.shape[0]
    last = pl.num_programs(1) - 1

    def cast_side():
        for src, dst in zip(side_in, side_out):
            dst[...] = src[...].astype(BF16)

    def up_chunk(rows):
        cast_side()
        xm = xm_ref[0:rows]
        for c in range(FF_TILE // MXU_COLS):
            sl = slice(c * MXU_COLS, (c + 1) * MXU_COLS)
            hv = _dot(xm, wv_ref[:, sl])
            hg = _dot(xm, wg_ref[:, sl])
            a = hg * jax.nn.sigmoid(hg) * hv
            if c * MXU_COLS < FF_OVERLAP:
                col = lax.broadcasted_iota(jnp.int32, a.shape, 1) + c * MXU_COLS
                a = jnp.where(jnp.logical_and(s == N_FF_STEPS - 1, col < FF_OVERLAP), 0.0, a)
            a_ref[s, 0:rows, sl] = a.astype(BF16)

    def down_slab(rows):
        cast_side()
        acc = None
        for f in range(N_FF_STEPS):
            r0 = min(f * FF_TILE, FF_LAST_OFF)
            part = _dot(a_ref[f, 0:rows], wd_ref[r0:r0 + FF_TILE, :])
            acc = part if acc is None else acc + part
        y_ref[s - N_FF_STEPS, 0:rows] = acc

    def finish(rows, x, gate, out_ref):
        mixed = jnp.concatenate([y_ref[n, rows] for n in range(N_OUT_STEPS)], axis=-1)
        out_ref[...] = _layer_norm(ALPHA * x + 0.5 * gate * mixed, gb_ref[0:1], gb_ref[1:2])

    seq = pl.ds(i // tiles_per_mod, 1)

    @pl.when(s == 0)
    def _():
        xm_ref[0:tm] = (x_ref[...] * (1.0 + mod_ref[1, seq, :]) + mod_ref[0, seq, :]).astype(BF16)

    if n_extra:
        with_extra = i == 0

        @pl.when(jnp.logical_and(s == 0, with_extra))
        def _():
            xm_ref[tm:tm + n_extra] = (xe_ref[...] * (1.0 + mod_ref[1, erows, :]) + mod_ref[0, erows, :]).astype(BF16)

        for phase, body in ((s < N_FF_STEPS, up_chunk), (s >= N_FF_STEPS, down_slab)):
            pl.when(jnp.logical_and(phase, with_extra))(functools.partial(body, tm + n_extra))
            pl.when(jnp.logical_and(phase, jnp.logical_not(with_extra)))(functools.partial(body, tm))

        @pl.when(jnp.logical_and(s == last, with_extra))
        def _():
            finish(slice(tm, tm + n_extra), xe_ref[...], mod_ref[2, erows, :], oe_ref)
    else:
        pl.when(s < N_FF_STEPS)(functools.partial(up_chunk, tm))
        pl.when(s >= N_FF_STEPS)(functools.partial(down_slab, tm))

    @pl.when(s == last)
    def _():
        finish(slice(0, tm), x_ref[...], mod_ref[2, seq, :], o_ref)


def _ffn(x, mod, sub, w_up, w_down, ln_gb, *, tm, rows_per_mod, extra=None, extra_row0=0, side=()):
    m = x.shape[0]
    n_steps = N_FF_STEPS + N_OUT_STEPS
    tiles_per_mod = rows_per_mod // tm
    n_extra = 0 if extra is None else extra.shape[0]
    const2 = lambda i, s: (0, 0)
    extra_spec = pl.BlockSpec((n_extra, D_MODEL), const2)
    side_in_specs, side_out_specs, side_shapes = [], [], []
    for arr, lead, rows in side:
        n_rows, n_cols = arr.shape[-2:]
        n_blocks = n_rows // rows
        assert n_blocks * rows == n_rows and n_blocks <= (m // tm) * n_steps
        blk = lambda i, s, n_blocks=n_blocks: jnp.minimum(i * n_steps + s, n_blocks - 1)
        side_in_specs.append(pl.BlockSpec((None,) * len(lead) + (rows, n_cols),
                                          lambda i, s, lead=lead, blk=blk: (*lead, blk(i, s), 0)))
        side_out_specs.append(pl.BlockSpec((rows, n_cols), lambda i, s, blk=blk: (blk(i, s), 0)))
        side_shapes.append(jax.ShapeDtypeStruct((n_rows, n_cols), BF16))
    wd_block = lambda i, s: (0, jnp.where(s < N_FF_STEPS, N_OUT_STEPS - 1, s - N_FF_STEPS))
    outs = pl.pallas_call(
        functools.partial(_ffn_kernel, len(side), n_extra, tiles_per_mod, extra_row0),
        grid=(m // tm, n_steps),
        in_specs=[
            pl.BlockSpec((tm, D_MODEL), lambda i, s: (i, 0)),
            pl.BlockSpec((3, mod.shape[1], D_MODEL), lambda i, s: (sub, 0, 0)),
            pl.BlockSpec((pl.Element(D_MODEL), pl.Element(FF_TILE)),
                         lambda i, s: (0, pl.multiple_of(_ff_offset(s), LANES))),
            pl.BlockSpec((pl.Element(D_MODEL), pl.Element(FF_TILE)),
                         lambda i, s: (0, pl.multiple_of(D_FF + _ff_offset(s), LANES))),
            pl.BlockSpec((D_FF, OUT_TILE), wd_block),
            pl.BlockSpec((2, D_MODEL), const2),
            *([extra_spec] if n_extra else []),
            *side_in_specs,
        ],
        out_specs=[pl.BlockSpec((tm, D_MODEL), lambda i, s: (i, 0), pipeline_mode=pl.Buffered(1)),
                   *([extra_spec] if n_extra else []),
                   *side_out_specs],
        out_shape=[jax.ShapeDtypeStruct((m, D_MODEL), F32),
                   *([jax.ShapeDtypeStruct((n_extra, D_MODEL), F32)] if n_extra else []), *side_shapes],
        scratch_shapes=[
            pltpu.VMEM((tm + n_extra, D_MODEL), BF16),
            pltpu.VMEM((N_FF_STEPS, tm + n_extra, FF_TILE), BF16),
            pltpu.VMEM((N_OUT_STEPS, tm + n_extra, OUT_TILE), F32),
        ],
        compiler_params=pltpu.CompilerParams(dimension_semantics=("arbitrary", "arbitrary"), vmem_limit_bytes=VMEM_LIMIT),
        name="swiglu_ln",
    )(x, mod, w_up, w_up, w_down, ln_gb, *([extra] if n_extra else []), *[arr for arr, _, _ in side])
    return outs


def _inproj_kernel(x_ref, shift_ref, scale_ref, w_ref, o_ref, xm_ref):
    @pl.when(pl.program_id(1) == 0)
    def _():
        xm_ref[...] = (x_ref[...] * (1.0 + scale_ref[0]) + shift_ref[0]).astype(BF16)

    o_ref[...] = _dot(xm_ref[...], w_ref[...])


def _inproj(x, shift, scale, w_in, *, tm, rows_per_mod):
    m = x.shape[0]
    r = shift.shape[1]
    tn = 512
    tiles_per_mod = rows_per_mod // tm
    mod_spec = pl.BlockSpec((1, r, D_MODEL), lambda i, j: (i // tiles_per_mod, 0, 0))
    return pl.pallas_call(
        _inproj_kernel,
        grid=(m // tm, IN_WIDTH // tn),
        in_specs=[
            pl.BlockSpec((tm, D_MODEL), lambda i, j: (i, 0)),
            mod_spec, mod_spec,
            pl.BlockSpec((D_MODEL, tn), lambda i, j: (0, j)),
        ],
        out_specs=pl.BlockSpec((tm, tn), lambda i, j: (i, j)),
        out_shape=jax.ShapeDtypeStruct((m, IN_WIDTH), F32),
        scratch_shapes=[pltpu.VMEM((tm, D_MODEL), BF16)],
        compiler_params=pltpu.CompilerParams(dimension_semantics=("arbitrary", "arbitrary"), vmem_limit_bytes=VMEM_LIMIT),
        name="mixer_inproj",
    )(x, shift, scale, w_in)


def _outproj_kernel(m_ref, x_ref, gate_ref, w_ref, g_ref, b_ref, o_ref):
    mixed = _dot(m_ref[...], w_ref[...])
    y = ALPHA * x_ref[...] + gate_ref[0] * mixed
    o_ref[...] = _layer_norm(y, g_ref[...], b_ref[...])


def _outproj(merged, x, gate, w_o, ln_g, ln_b, *, tm, rows_per_mod):
    m = x.shape[0]
    r = gate.shape[1]
    tiles_per_mod = rows_per_mod // tm
    return pl.pallas_call(
        _outproj_kernel,
        grid=(m // tm,),
        in_specs=[
            pl.BlockSpec((tm, D_MODEL), lambda i: (i, 0)),
            pl.BlockSpec((tm, D_MODEL), lambda i: (i, 0)),
            pl.BlockSpec((1, r, D_MODEL), lambda i: (i // tiles_per_mod, 0, 0)),
            pl.BlockSpec((D_MODEL, D_MODEL), lambda i: (0, 0)),
            pl.BlockSpec((1, D_MODEL), lambda i: (0, 0)),
            pl.BlockSpec((1, D_MODEL), lambda i: (0, 0)),
        ],
        out_specs=pl.BlockSpec((tm, D_MODEL), lambda i: (i, 0)),
        out_shape=jax.ShapeDtypeStruct((m, D_MODEL), F32),
        compiler_params=pltpu.CompilerParams(dimension_semantics=("arbitrary",), vmem_limit_bytes=VMEM_LIMIT),
        name="mixer_outproj_ln",
    )(merged, x, gate, w_o, ln_g.reshape(1, D_MODEL), ln_b.reshape(1, D_MODEL))


def _gmlp_norm(pv, gng, gnb):
    v = _gelu_tanh(pv)
    outs = []
    for g in range(N_GMLP_GROUPS):
        sl = slice(g * LANES, (g + 1) * LANES)
        outs.append(_layer_norm(v[:, sl], gng[:, sl], gnb[:, sl]))
    return outs


def _half_lane_tiles(t):
    lo_lane = lax.broadcasted_iota(jnp.int32, (t.shape[0], LANES), 1) < HEAD_DIM
    lo, hi = [], []
    for c in range(KV_WIDTH // LANES):
        col = t[:, c * LANES:(c + 1) * LANES]
        swp = pltpu.roll(col, HEAD_DIM, 1)
        lo += [jnp.where(lo_lane, col, 0.0), jnp.where(lo_lane, swp, 0.0)]
        hi += [jnp.where(lo_lane, 0.0, swp), jnp.where(lo_lane, 0.0, col)]
    return jnp.concatenate(lo + hi, axis=-1).astype(BF16)


def _band_attention(q_s, kx_s, vx_s, attn_s, sinks_ref, nblk, first_tile):
    rows = 2 * WINDOW
    qi = lax.broadcasted_iota(jnp.int32, (rows, WINDOW), 0) & (WINDOW - 1)
    kj = lax.broadcasted_iota(jnp.int32, (rows, WINDOW), 1)
    own = kj <= qi
    first_mask = kj <= jnp.where(first_tile, qi, WINDOW)
    ones = jnp.ones((2 * WINDOW, LANES), BF16)
    units = [(wb, g, hf) for wb in range(nblk) for g in range(N_KV_HEADS) for hf in range(2)]

    def kv_cols(g, hf):
        return slice(hf * 4 * LANES + g * LANES, hf * 4 * LANES + (g + 1) * LANES)

    scores = []
    for wb, g, hf in units:
        r0 = wb * WINDOW
        q2 = jnp.concatenate([q_s[r0:r0 + WINDOW, (2 * g) * LANES:(2 * g + 1) * LANES],
                              q_s[r0:r0 + WINDOW, (2 * g + 1) * LANES:(2 * g + 2) * LANES]], axis=0)
        s = _dot_nt(q2, kx_s[r0:r0 + 2 * WINDOW, kv_cols(g, hf)])
        c = jnp.where(own, s[:, WINDOW:2 * WINDOW], s[:, 0:WINDOW])
        if wb == 0:
            c = jnp.where(first_mask, c, -jnp.inf)
        scores.append(c)

    probs = []
    for (wb, g, hf), c in zip(units, scores):
        sink = jnp.concatenate([jnp.full((WINDOW, LANES), sinks_ref[GQA_GROUP * g + hf], F32),
                                jnp.full((WINDOW, LANES), sinks_ref[GQA_GROUP * g + 2 + hf], F32)], axis=0)
        mx = jnp.maximum(jnp.broadcast_to(jnp.max(c, axis=-1, keepdims=True), c.shape), sink)
        p = jnp.exp(c - mx)
        pcat = jnp.concatenate([jnp.where(own, 0.0, p), jnp.where(own, p, 0.0)], axis=-1).astype(BF16)
        probs.append((pcat, jnp.exp(sink - mx)))

    outs = {}
    for (wb, g, hf), (pcat, esink) in zip(units, probs):
        r0 = wb * WINDOW
        v2 = jnp.concatenate([vx_s[r0:r0 + 2 * WINDOW, kv_cols(g, hf)], ones], axis=-1)
        o = _dot(pcat, v2)
        outs[wb, g, hf] = o[:, 0:LANES] / (o[:, LANES:2 * LANES] + esink)

    for wb in range(nblk):
        r0 = wb * WINDOW
        for g in range(N_KV_HEADS):
            both = outs[wb, g, 0] + outs[wb, g, 1]
            attn_s[r0:r0 + WINDOW, (2 * g) * LANES:(2 * g + 1) * LANES] = both[0:WINDOW]
            attn_s[r0:r0 + WINDOW, (2 * g + 1) * LANES:(2 * g + 2) * LANES] = both[WINDOW:2 * WINDOW]


def _pmix_kernel(sinks_ref, x_ref, mod_ref, win_ref, wo_ref, cos_ref, slo_ref, shi_ref,
                 gng_ref, gnb_ref, ws_ref, bs_ref, og_ref, lngb_ref,
                 o_ref, k_ref, v_ref, xm_s, p_s, q_s, kk_s, vv_s, kp_s, vp_s, attn_s, m_s, wm_s):
    t = pl.program_id(1)
    tm = x_ref.shape[0]
    nblk = tm // WINDOW

    @pl.when(jnp.logical_and(pl.program_id(0) == 0, t == 0))
    def _():
        row = lax.broadcasted_iota(jnp.int32, (CHUNK, CHUNK), 0)
        col = lax.broadcasted_iota(jnp.int32, (CHUNK, CHUNK), 1)
        for g in range(N_GMLP_GROUPS):
            wm_s[g] = jnp.where(col <= row, ws_ref[g], 0.0).astype(BF16)

    @pl.when(t == 0)
    def _():
        kp_s[...] = jnp.zeros(kp_s.shape, BF16)
        vp_s[...] = jnp.zeros(vp_s.shape, BF16)

    seq = pl.ds(pl.program_id(0), 1)
    xm_s[...] = (x_ref[...] * (1.0 + mod_ref[1, seq, :]) + mod_ref[0, seq, :]).astype(BF16)
    for c in range(IN_WIDTH // PROJ_TILE):
        sl = slice(c * PROJ_TILE, (c + 1) * PROJ_TILE)
        p_s[:, sl] = _dot(xm_s[...], win_ref[:, sl])

    cos, slo, shi = cos_ref[...], slo_ref[...], shi_ref[...]
    q_s[...] = (_rope_lanes(p_s[:, 0:Q_END], cos, slo, shi) * (HEAD_DIM ** -0.5)).astype(BF16)
    k = _rope_lanes(p_s[:, Q_END:K_END], cos, slo, shi)
    v = p_s[:, K_END:V_END]
    k_ref[0] = k[tm - WINDOW:tm]
    v_ref[0] = v[tm - WINDOW:tm]
    kb, vb = _half_lane_tiles(k), _half_lane_tiles(v)
    kk_s[0:WINDOW, :] = kp_s[...]
    vv_s[0:WINDOW, :] = vp_s[...]
    kk_s[WINDOW:WINDOW + tm, :] = kb
    vv_s[WINDOW:WINDOW + tm, :] = vb
    kp_s[...] = kb[tm - WINDOW:tm]
    vp_s[...] = vb[tm - WINDOW:tm]

    _band_attention(q_s, kk_s, vv_s, attn_s, sinks_ref, nblk, t == 0)

    vn = _gmlp_norm(p_s[:, U_END:IN_WIDTH], gng_ref[...], gnb_ref[...])
    gated = []
    for g in range(N_GMLP_GROUPS):
        u = _gelu_tanh(p_s[:, V_END + g * LANES:V_END + (g + 1) * LANES])
        vg = vn[g].astype(BF16)
        cols = jnp.concatenate([vg[wb * CHUNK:(wb + 1) * CHUNK] for wb in range(nblk)], axis=-1)
        mix = _dot(wm_s[g], cols)
        mix = jnp.concatenate([mix[:, wb * LANES:(wb + 1) * LANES] for wb in range(nblk)], axis=0)
        bias = jnp.concatenate([bs_ref[:, g:g + 1]] * nblk, axis=0)
        gated.append(u * (mix + bias))
    gm = jnp.concatenate(gated, axis=-1)
    og = og_ref[...]
    m_s[:, 0:ATTN_WIDTH] = _rms_norm(attn_s[...], og[:, 0:ATTN_WIDTH]).astype(BF16)
    m_s[:, ATTN_WIDTH:] = _rms_norm(gm, og[:, ATTN_WIDTH:]).astype(BF16)

    for c in range(D_MODEL // PROJ_TILE):
        sl = slice(c * PROJ_TILE, (c + 1) * PROJ_TILE)
        o_ref[:, sl] = ALPHA * x_ref[:, sl] + mod_ref[2, seq, sl] * _dot(m_s[...], wo_ref[:, sl])
    o_ref[...] = _layer_norm(o_ref[...], lngb_ref[0:1], lngb_ref[1:2])


def _rope_tables(pos):
    half = ROPE_DIM // 2
    inv = ROPE_THETA ** (-(jnp.arange(half, dtype=F32) * 2.0) / ROPE_DIM)
    ang = pos.astype(F32)[:, None] * inv[None, :]
    cos, sin = jnp.cos(ang), jnp.sin(ang)
    n = pos.shape[0]
    one = jnp.ones((n, HEAD_DIM - ROPE_DIM), F32)
    zero = jnp.zeros((n, HEAD_DIM - ROPE_DIM), F32)
    z8 = jnp.zeros((n, half), F32)
    cos_t = jnp.concatenate([cos, cos, one], axis=-1)
    sin_lo = jnp.concatenate([-sin, z8, zero], axis=-1)
    sin_hi = jnp.concatenate([z8, sin, zero], axis=-1)
    rep = LANES // HEAD_DIM
    return jnp.tile(cos_t, (1, rep)), jnp.tile(sin_lo, (1, rep)), jnp.tile(sin_hi, (1, rep))


def _pmix(x, mod, sub, w_in, w_o, sinks, tables, gn_g, gn_b, w_s, b_s, out_g, ln_gb, *, batch, tm):
    nt = SEQ // tm
    cos_t, sin_lo, sin_hi = tables
    tab_spec = pl.BlockSpec((tm, LANES), lambda b, j: (j, 0))
    full2 = lambda b, j: (0, 0)
    mod_spec = pl.BlockSpec((3, mod.shape[1], D_MODEL), lambda b, j: (sub, 0, 0))
    resident = dict(pipeline_mode=pl.Buffered(1))
    return pl.pallas_call(
        _pmix_kernel,
        grid=(batch, nt),
        in_specs=[
            pl.BlockSpec(memory_space=pltpu.SMEM),
            pl.BlockSpec((tm, D_MODEL), lambda b, j: (b * nt + j, 0)),
            mod_spec,
            pl.BlockSpec((D_MODEL, IN_WIDTH), full2, **resident),
            pl.BlockSpec((D_MODEL, D_MODEL), full2, **resident),
            tab_spec, tab_spec, tab_spec,
            pl.BlockSpec((1, GMLP_WIDTH), full2),
            pl.BlockSpec((1, GMLP_WIDTH), full2),
            pl.BlockSpec((N_GMLP_GROUPS, CHUNK, CHUNK), lambda b, j: (0, 0, 0)),
            pl.BlockSpec((CHUNK, N_GMLP_GROUPS), full2),
            pl.BlockSpec((1, D_MODEL), full2),
            pl.BlockSpec((2, D_MODEL), full2),
        ],
        out_specs=[
            pl.BlockSpec((tm, D_MODEL), lambda b, j: (b * nt + j, 0)),
            pl.BlockSpec((1, WINDOW, KV_WIDTH), lambda b, j: (b, 0, 0)),
            pl.BlockSpec((1, WINDOW, KV_WIDTH), lambda b, j: (b, 0, 0)),
        ],
        out_shape=[
            jax.ShapeDtypeStruct((batch * SEQ, D_MODEL), F32),
            jax.ShapeDtypeStruct((batch, WINDOW, KV_WIDTH), F32),
            jax.ShapeDtypeStruct((batch, WINDOW, KV_WIDTH), F32),
        ],
        scratch_shapes=[
            pltpu.VMEM((tm, D_MODEL), BF16),
            pltpu.VMEM((tm, IN_WIDTH), F32),
            pltpu.VMEM((tm, ATTN_WIDTH), BF16),
            pltpu.VMEM((WINDOW + tm, 4 * KV_WIDTH), BF16),
            pltpu.VMEM((WINDOW + tm, 4 * KV_WIDTH), BF16),
            pltpu.VMEM((WINDOW, 4 * KV_WIDTH), BF16),
            pltpu.VMEM((WINDOW, 4 * KV_WIDTH), BF16),
            pltpu.VMEM((tm, ATTN_WIDTH), F32),
            pltpu.VMEM((tm, D_MODEL), BF16),
            pltpu.VMEM((N_GMLP_GROUPS, CHUNK, CHUNK), BF16),
        ],
        compiler_params=pltpu.CompilerParams(dimension_semantics=("arbitrary", "arbitrary"), vmem_limit_bytes=VMEM_LIMIT),
        name="prompt_mixer",
    )(sinks, x, mod, w_in, w_o, cos_t, sin_lo, sin_hi, gn_g.reshape(1, GMLP_WIDTH),
      gn_b.reshape(1, GMLP_WIDTH), w_s, b_s.T, out_g.reshape(1, D_MODEL), ln_gb)


def _sattn_kernel(qe_ref, p_ref, ck_ref, cv_ref, sink_ref, cos_ref, slo_ref, shi_ref, o_ref, ko_ref, vo_ref):
    cos, slo, shi = cos_ref[...], slo_ref[...], shi_ref[...]
    sink = sink_ref[...]
    for r in range(qe_ref.shape[0]):
        qe = _rope_lanes(qe_ref[r], cos, slo, shi) * (HEAD_DIM ** -0.5)
        kn = _rope_lanes(p_ref[r, :, Q_END:K_END], cos, slo, shi)
        vn = p_ref[r, :, K_END:V_END]
        ck = ck_ref[r]
        cv = cv_ref[r]
        s_c = _dot_nt(qe.astype(BF16), ck.astype(BF16))
        key = lax.broadcasted_iota(jnp.int32, s_c.shape, 1)
        s_c = jnp.where(key >= 1, s_c, -jnp.inf)
        s_n = jnp.sum(qe * kn, axis=-1, keepdims=True)
        mx = jnp.maximum(jnp.maximum(jnp.max(s_c, axis=-1, keepdims=True), s_n), sink)
        p_c = jnp.exp(s_c - mx)
        p_n = jnp.exp(s_n - mx)
        denom = jnp.sum(p_c, axis=-1, keepdims=True) + p_n + jnp.exp(sink - mx)
        o = (_dot(p_c.astype(BF16), cv.astype(BF16)) + p_n * vn) / denom
        head = lax.broadcasted_iota(jnp.int32, o.shape, 0)
        lane = lax.broadcasted_iota(jnp.int32, o.shape, 1)
        o_ref[r] = jnp.where(head // GQA_GROUP == lane // HEAD_DIM, o, 0.0)
        row = lax.broadcasted_iota(jnp.int32, ck.shape, 0)
        newest = row == WINDOW - 1
        ko_ref[r] = jnp.where(newest, kn, pltpu.roll(ck, WINDOW - 1, 0))
        vo_ref[r] = jnp.where(newest, vn, pltpu.roll(cv, WINDOW - 1, 0))


def _sattn(qe, p3, ck, cv, sinks, tables):
    b = qe.shape[0]
    rows = SAMPLE_ROWS_PER_STEP
    cos_t, sin_lo, sin_hi = tables
    tab_spec = pl.BlockSpec((1, LANES), lambda i: (0, 0))
    blk3 = lambda i: (i, 0, 0)
    return pl.pallas_call(
        _sattn_kernel,
        grid=(b // rows,),
        in_specs=[
            pl.BlockSpec((rows, N_HEADS, KV_WIDTH), blk3),
            pl.BlockSpec((rows, 1, IN_WIDTH), blk3),
            pl.BlockSpec((rows, WINDOW, KV_WIDTH), blk3),
            pl.BlockSpec((rows, WINDOW, KV_WIDTH), blk3),
            pl.BlockSpec((N_HEADS, 1), lambda i: (0, 0)),
            tab_spec, tab_spec, tab_spec,
        ],
        out_specs=[
            pl.BlockSpec((rows, N_HEADS, KV_WIDTH), blk3),
            pl.BlockSpec((rows, WINDOW, KV_WIDTH), blk3),
            pl.BlockSpec((rows, WINDOW, KV_WIDTH), blk3),
        ],
        out_shape=[
            jax.ShapeDtypeStruct((b, N_HEADS, KV_WIDTH), F32),
            jax.ShapeDtypeStruct((b, WINDOW, KV_WIDTH), F32),
            jax.ShapeDtypeStruct((b, WINDOW, KV_WIDTH), F32),
        ],
        compiler_params=pltpu.CompilerParams(dimension_semantics=("arbitrary",)),
        name="sample_attn",
    )(qe, p3, ck, cv, sinks.reshape(N_HEADS, 1), cos_t, sin_lo, sin_hi)


def _smerge_kernel(attn_ref, p_ref, gng_ref, gnb_ref, w0_ref, b0_ref, og_ref, m_ref, vn_ref):
    og = og_ref[...]
    vn = jnp.concatenate(_gmlp_norm(p_ref[:, U_END:IN_WIDTH], gng_ref[...], gnb_ref[...]), axis=-1)
    vn_ref[...] = vn
    u = _gelu_tanh(p_ref[:, V_END:U_END])
    gm = u * (w0_ref[...] * vn + b0_ref[...])
    m_ref[:, 0:ATTN_WIDTH] = _rms_norm(attn_ref[...], og[:, 0:ATTN_WIDTH]).astype(BF16)
    m_ref[:, ATTN_WIDTH:] = _rms_norm(gm, og[:, ATTN_WIDTH:]).astype(BF16)


def _smerge(attn, p, gn_g, gn_b, w_s, b_s, out_g):
    b = attn.shape[0]
    w0 = jnp.repeat(w_s[:, 0, 0], LANES).reshape(1, GMLP_WIDTH)
    b0 = jnp.repeat(b_s[:, 0], LANES).reshape(1, GMLP_WIDTH)
    return pl.pallas_call(
        _smerge_kernel,
        out_shape=[
            jax.ShapeDtypeStruct((b, D_MODEL), BF16),
            jax.ShapeDtypeStruct((b, GMLP_WIDTH), F32),
        ],
        name="sample_gmlp_merge",
    )(attn, p, gn_g.reshape(1, GMLP_WIDTH), gn_b.reshape(1, GMLP_WIDTH), w0, b0, out_g.reshape(1, D_MODEL))


def kernel(x_prompt, x_sample, cache_k_win, cache_v_win, c_prompt, c_sample, w_ada, b_ada, ln_g, ln_b,
           w_ffn_up, w_ffn_down, w_in, attn_sinks, gmlp_norm_g, gmlp_norm_b, w_spatial, b_spatial,
           out_norm_g, w_o):
    batch, seq, _ = x_prompt.shape
    dec_batch = x_sample.shape[0]
    buf = cache_k_win.shape[2]
    assert seq == SEQ and buf == WINDOW and x_sample.shape[1] == 1 and w_ada.shape[0] == DEPTH == 1

    xp = x_prompt.reshape(batch * seq, D_MODEL)
    xs = x_sample.reshape(dec_batch, D_MODEL)

    n_c = batch + dec_batch
    n_c_pad = ((n_c + 7) // 8) * 8
    c_all = jnp.concatenate([c_prompt, c_sample, jnp.zeros((n_c_pad - n_c, D_MODEL), F32)], axis=0)
    mod = _ada(c_all, w_ada[0], b_ada[0])

    def smod(i, j):
        return mod[3 * i + j, batch:n_c].reshape(1, dec_batch, D_MODEL)

    ln_gb = jnp.stack([ln_g[0], ln_b[0]], axis=1)
    ffn = functools.partial(_ffn, tm=FFN_ROWS, rows_per_mod=seq, extra_row0=batch)

    w_up0 = w_ffn_up[0, 0].astype(BF16)
    w_down0 = w_ffn_down[0, 0].astype(BF16)
    later = ((w_ffn_up, (0, 1), 16), (w_ffn_down, (0, 1), 32), (w_in, (0,), 16), (w_o, (0,), 16))
    xp, xs, w_up1, w_down1, w_in_b, w_o_b = ffn(xp, mod, 0, w_up0, w_down0, ln_gb[0], extra=xs, side=later)

    xp, k_p, v_p = _pmix(xp, mod, 1, w_in_b, w_o_b, attn_sinks[0], _rope_tables(jnp.arange(seq)),
                         gmlp_norm_g[0], gmlp_norm_b[0], w_spatial[0], b_spatial[0], out_norm_g[0], ln_gb[1],
                         batch=batch, tm=MIX_TILE)

    ps = _inproj(xs, smod(1, 0), smod(1, 1), w_in_b, tm=dec_batch, rows_per_mod=dec_batch)
    eye = jnp.eye(N_KV_HEADS, dtype=F32)
    q5 = ps[:, :Q_END].reshape(dec_batch, N_KV_HEADS, GQA_GROUP, 1, HEAD_DIM)
    qe = (q5 * eye[None, :, None, :, None]).reshape(dec_batch, N_HEADS, KV_WIDTH)
    ck = cache_k_win[0].reshape(dec_batch, buf, KV_WIDTH)
    cv = cache_v_win[0].reshape(dec_batch, buf, KV_WIDTH)
    oe, k_s, v_s = _sattn(qe, ps.reshape(dec_batch, 1, IN_WIDTH), ck, cv, attn_sinks[0],
                          _rope_tables(jnp.full((1,), PAST_LEN)))
    attn_s = oe.reshape(dec_batch, N_KV_HEADS, GQA_GROUP, N_KV_HEADS, HEAD_DIM).sum(axis=3).reshape(dec_batch, ATTN_WIDTH)
    merged_s, vn_s = _smerge(attn_s, ps, gmlp_norm_g[0], gmlp_norm_b[0], w_spatial[0], b_spatial[0], out_norm_g[0])
    xs = _outproj(merged_s, xs, smod(1, 2), w_o_b, ln_g[0, 1], ln_b[0, 1], tm=dec_batch, rows_per_mod=dec_batch)

    xp, xs = ffn(xp, mod, 2, w_up1, w_down1, ln_gb[2], extra=xs)

    return (
        xp.reshape(batch, seq, D_MODEL),
        xs.reshape(dec_batch, 1, D_MODEL),
        k_p.reshape(1, batch, WINDOW, N_KV_HEADS, HEAD_DIM),
        v_p.reshape(1, batch, WINDOW, N_KV_HEADS, HEAD_DIM),
        k_s.reshape(1, dec_batch, buf, N_KV_HEADS, HEAD_DIM),
        v_s.reshape(1, dec_batch, buf, N_KV_HEADS, HEAD_DIM),
        vn_s.reshape(1, dec_batch, 1, GMLP_WIDTH),
    )
```

```python
import functools

import jax
import jax.numpy as jnp
import numpy as np
from jax import lax
from jax.experimental import pallas as pl
from jax.experimental.pallas import tpu as pltpu

D_MODEL = 2048
SEQ = 2048
PAST_LEN = 16384
ATTN_WIDTH = 1024
GMLP_WIDTH = 1024
HEAD_DIM = 64
N_HEADS = 16
N_KV_HEADS = 4
GQA_GROUP = 4
KV_WIDTH = 256
WINDOW = 128
ROPE_THETA = 500000.0
ROPE_DIM = 16
CHUNK = 128
N_GMLP_GROUPS = 8
D_FF = 5504
N_SUB = 3
DEPTH = 1
ALPHA = (2.0 * DEPTH) ** 0.25
LN_EPS = 1e-5
Q_END = ATTN_WIDTH
K_END = Q_END + KV_WIDTH
V_END = K_END + KV_WIDTH
U_END = V_END + GMLP_WIDTH
IN_WIDTH = U_END + GMLP_WIDTH

LANES = 128
MXU_COLS = 256
FF_TILE = 512
N_FF_STEPS = -(-D_FF // FF_TILE)
FF_LAST_OFF = D_FF - FF_TILE
FF_OVERLAP = N_FF_STEPS * FF_TILE - D_FF
OUT_TILE = 512
N_OUT_STEPS = D_MODEL // OUT_TILE
FFN_ROWS = 512
LN_PIECE_ROWS = 64
MIX_TILE = 256
PROJ_TILE = 512
SAMPLE_ROWS_PER_STEP = 8
VMEM_LIMIT = 60 * 1024 * 1024

BF16 = jnp.bfloat16
F32 = jnp.float32


def _dot(a, b):
    return jnp.dot(a, b, preferred_element_type=F32)


def _dot_nt(a, b):
    return lax.dot_general(a, b, (((1,), (1,)), ((), ())), preferred_element_type=F32)


def _layer_norm(y, g, b):
    mu = jnp.mean(y, axis=-1, keepdims=True)
    d = y - mu
    var = jnp.mean(d * d, axis=-1, keepdims=True)
    return d * lax.rsqrt(var + LN_EPS) * g + b


def _rms_norm(y, g):
    return y * lax.rsqrt(jnp.mean(y * y, axis=-1, keepdims=True) + LN_EPS) * g


def _gelu_tanh(x):
    c = np.float32(np.sqrt(2.0 / np.pi))
    return 0.5 * x * (1.0 + jnp.tanh(c * (x + 0.044715 * (x * x * x))))


def _rope_lanes(t, cos, sin_lo, sin_hi):
    pieces = []
    for c in range(t.shape[-1] // LANES):
        x = t[:, c * LANES:(c + 1) * LANES]
        pieces.append(x * cos + pltpu.roll(x, 8, 1) * sin_hi + pltpu.roll(x, LANES - 8, 1) * sin_lo)
    return pieces[0] if len(pieces) == 1 else jnp.concatenate(pieces, axis=-1)


def _ada_kernel(c_ref, w_ref, b_ref, o_ref):
    c = c_ref[...]
    h = (c * jax.nn.sigmoid(c)).astype(BF16)
    o_ref[0] = _dot(h, w_ref[...].astype(BF16)) + b_ref[...]


def _ada(c, w_ada, b_ada):
    rows = c.shape[0]
    n = w_ada.shape[1]
    tn = 1024
    per_plane = D_MODEL // tn
    return pl.pallas_call(
        _ada_kernel,
        grid=(n // tn,),
        in_specs=[
            pl.BlockSpec((rows, D_MODEL), lambda j: (0, 0)),
            pl.BlockSpec((D_MODEL, tn), lambda j: (0, j)),
            pl.BlockSpec((1, tn), lambda j: (0, j)),
        ],
        out_specs=pl.BlockSpec((1, rows, tn), lambda j: (j // per_plane, 0, j % per_plane)),
        out_shape=jax.ShapeDtypeStruct((n // D_MODEL, rows, D_MODEL), F32),
        compiler_params=pltpu.CompilerParams(dimension_semantics=("arbitrary",), vmem_limit_bytes=VMEM_LIMIT),
        name="ada_mod",
    )(c, w_ada, b_ada.reshape(1, n))


def _ff_offset(step):
    return jnp.minimum(step * FF_TILE, FF_LAST_OFF)


def _ffn_kernel(n_side, n_extra, tiles_per_mod, extra_row0, n_tiles, *refs):
    x_ref, mod_ref, wv_ref, wg_ref, wd_ref, gb_ref = refs[:6]
    n_in = 6
    if n_extra:
        xe_ref = refs[n_in]
        n_in += 1
        erows = slice(extra_row0, extra_row0 + n_extra)
    side_in = refs[n_in:n_in + n_side]
    n_in += n_side
    o_ref = refs[n_in]
    n_out = 1
    if n_extra:
        oe_ref = refs[n_in + 1]
        n_out = 2
    side_out = refs[n_in + n_out:n_in + n_out + n_side]
    xm_ref, a_ref, y_ref = refs[n_in + n_out + n_side:]
    i = pl.program_id(0)
    s = pl.program_id(1)
    tm = x_ref.shape[0]
    last = pl.num_programs(1) - 1

    def cast_side():
        for src, dst in zip(side_in, side_out):
            dst[...] = src[...].astype(BF16)

    def norm_piece():
        r0 = pl.multiple_of(jnp.minimum(s, tm // LN_PIECE_ROWS - 1) * LN_PIECE_ROWS, LN_PIECE_ROWS)
        piece = pl.ds(r0, LN_PIECE_ROWS)
        y = jnp.concatenate([y_ref[n, piece, :] for n in range(N_OUT_STEPS)], axis=-1)
        o_ref[piece, :] = _layer_norm(y, gb_ref[0:1], gb_ref[1:2])

    def up_chunk(rows, norm_previous):
        cast_side()
        if norm_previous:
            norm_piece()
        xm = xm_ref[0:rows]
        for c in range(FF_TILE // MXU_COLS):
            sl = slice(c * MXU_COLS, (c + 1) * MXU_COLS)
            hv = _dot(xm, wv_ref[:, sl])
            hg = _dot(xm, wg_ref[:, sl])
            a = hg * jax.nn.sigmoid(hg) * hv
            if c * MXU_COLS < FF_OVERLAP:
                col = lax.broadcasted_iota(jnp.int32, a.shape, 1) + c * MXU_COLS
                a = jnp.where(jnp.logical_and(s == N_FF_STEPS - 1, col < FF_OVERLAP), 0.0, a)
            a_ref[s, 0:rows, sl] = a.astype(BF16)

    def down_slab(rows):
        cast_side()
        acc = None
        for f in range(N_FF_STEPS):
            r0 = min(f * FF_TILE, FF_LAST_OFF)
            part = _dot(a_ref[f, 0:rows], wd_ref[r0:r0 + FF_TILE, :])
            acc = part if acc is None else acc + part
        y_ref[s - N_FF_STEPS, 0:rows] = acc

    first = i == 0
    later = i > 0
    up_phase = s < N_FF_STEPS
    down_phase = s >= N_FF_STEPS
    seq = pl.ds(i // tiles_per_mod, 1)

    @pl.when(s == 0)
    def _():
        xm_ref[0:tm] = (x_ref[...] * (1.0 + mod_ref[1, seq, :]) + mod_ref[0, seq, :]).astype(BF16)

    if n_extra:
        @pl.when(jnp.logical_and(s == 0, first))
        def _():
            xm_ref[tm:tm + n_extra] = (xe_ref[...] * (1.0 + mod_ref[1, erows, :]) + mod_ref[0, erows, :]).astype(BF16)

    pl.when(jnp.logical_and(up_phase, first))(functools.partial(up_chunk, tm + n_extra, False))
    pl.when(jnp.logical_and(up_phase, later))(functools.partial(up_chunk, tm, True))
    pl.when(jnp.logical_and(down_phase, first))(functools.partial(down_slab, tm + n_extra))
    pl.when(jnp.logical_and(down_phase, later))(functools.partial(down_slab, tm))

    @pl.when(s == last)
    def _():
        for n in range(N_OUT_STEPS):
            sl = slice(n * OUT_TILE, (n + 1) * OUT_TILE)
            y_ref[n, 0:tm] = ALPHA * x_ref[:, sl] + 0.5 * mod_ref[2, seq, sl] * y_ref[n, 0:tm]

    @pl.when(jnp.logical_and(s == last, i == n_tiles - 1))
    def _():
        y = jnp.concatenate([y_ref[n, 0:tm] for n in range(N_OUT_STEPS)], axis=-1)
        o_ref[...] = _layer_norm(y, gb_ref[0:1], gb_ref[1:2])

    if n_extra:
        @pl.when(jnp.logical_and(s == last, first))
        def _():
            mixed = jnp.concatenate([y_ref[n, tm:tm + n_extra] for n in range(N_OUT_STEPS)], axis=-1)
            oe_ref[...] = _layer_norm(ALPHA * xe_ref[...] + 0.5 * mod_ref[2, erows, :] * mixed, gb_ref[0:1], gb_ref[1:2])


def _ffn(x, mod, sub, w_up, w_down, ln_gb, *, tm, rows_per_mod, extra=None, extra_row0=0, side=()):
    m = x.shape[0]
    n_steps = N_FF_STEPS + N_OUT_STEPS
    tiles_per_mod = rows_per_mod // tm
    n_extra = 0 if extra is None else extra.shape[0]
    const2 = lambda i, s: (0, 0)
    extra_spec = pl.BlockSpec((n_extra, D_MODEL), const2)
    side_in_specs, side_out_specs, side_shapes = [], [], []
    for arr, lead, rows in side:
        n_rows, n_cols = arr.shape[-2:]
        n_blocks = n_rows // rows
        assert n_blocks * rows == n_rows and n_blocks <= (m // tm) * n_steps
        blk = lambda i, s, n_blocks=n_blocks: jnp.minimum(i * n_steps + s, n_blocks - 1)
        side_in_specs.append(pl.BlockSpec((None,) * len(lead) + (rows, n_cols),
                                          lambda i, s, lead=lead, blk=blk: (*lead, blk(i, s), 0)))
        side_out_specs.append(pl.BlockSpec((rows, n_cols), lambda i, s, blk=blk: (blk(i, s), 0)))
        side_shapes.append(jax.ShapeDtypeStruct((n_rows, n_cols), BF16))
    n_tiles = m // tm
    assert tm % LN_PIECE_ROWS == 0 and tm // LN_PIECE_ROWS <= N_FF_STEPS
    wd_block = lambda i, s: (0, jnp.where(s < N_FF_STEPS, N_OUT_STEPS - 1, s - N_FF_STEPS))
    out_block = lambda i, s: (jnp.where(s < N_FF_STEPS, jnp.maximum(i - 1, 0), i), 0)
    outs = pl.pallas_call(
        functools.partial(_ffn_kernel, len(side), n_extra, tiles_per_mod, extra_row0, n_tiles),
        grid=(n_tiles, n_steps),
        in_specs=[
            pl.BlockSpec((tm, D_MODEL), lambda i, s: (i, 0)),
            pl.BlockSpec((3, mod.shape[1], D_MODEL), lambda i, s: (sub, 0, 0)),
            pl.BlockSpec((pl.Element(D_MODEL), pl.Element(FF_TILE)),
                         lambda i, s: (0, pl.multiple_of(_ff_offset(s), LANES))),
            pl.BlockSpec((pl.Element(D_MODEL), pl.Element(FF_TILE)),
                         lambda i, s: (0, pl.multiple_of(D_FF + _ff_offset(s), LANES))),
            pl.BlockSpec((D_FF, OUT_TILE), wd_block),
            pl.BlockSpec((2, D_MODEL), const2),
            *([extra_spec] if n_extra else []),
            *side_in_specs,
        ],
        out_specs=[pl.BlockSpec((tm, D_MODEL), out_block), *([extra_spec] if n_extra else []),
                   *side_out_specs],
        out_shape=[jax.ShapeDtypeStruct((m, D_MODEL), F32),
                   *([jax.ShapeDtypeStruct((n_extra, D_MODEL), F32)] if n_extra else []), *side_shapes],
        scratch_shapes=[
            pltpu.VMEM((tm + n_extra, D_MODEL), BF16),
            pltpu.VMEM((N_FF_STEPS, tm + n_extra, FF_TILE), BF16),
            pltpu.VMEM((N_OUT_STEPS, tm + n_extra, OUT_TILE), F32),
        ],
        compiler_params=pltpu.CompilerParams(dimension_semantics=("arbitrary", "arbitrary"), vmem_limit_bytes=VMEM_LIMIT),
        name="swiglu_ln",
    )(x, mod, w_up, w_up, w_down, ln_gb, *([extra] if n_extra else []), *[arr for arr, _, _ in side])
    return outs


def _inproj_kernel(x_ref, shift_ref, scale_ref, w_ref, o_ref, xm_ref):
    @pl.when(pl.program_id(1) == 0)
    def _():
        xm_ref[...] = (x_ref[...] * (1.0 + scale_ref[0]) + shift_ref[0]).astype(BF16)

    o_ref[...] = _dot(xm_ref[...], w_ref[...])


def _inproj(x, shift, scale, w_in, *, tm, rows_per_mod):
    m = x.shape[0]
    r = shift.shape[1]
    tn = 512
    tiles_per_mod = rows_per_mod // tm
    mod_spec = pl.BlockSpec((1, r, D_MODEL), lambda i, j: (i // tiles_per_mod, 0, 0))
    return pl.pallas_call(
        _inproj_kernel,
        grid=(m // tm, IN_WIDTH // tn),
        in_specs=[
            pl.BlockSpec((tm, D_MODEL), lambda i, j: (i, 0)),
            mod_spec, mod_spec,
            pl.BlockSpec((D_MODEL, tn), lambda i, j: (0, j)),
        ],
        out_specs=pl.BlockSpec((tm, tn), lambda i, j: (i, j)),
        out_shape=jax.ShapeDtypeStruct((m, IN_WIDTH), F32),
        scratch_shapes=[pltpu.VMEM((tm, D_MODEL), BF16)],
        compiler_params=pltpu.CompilerParams(dimension_semantics=("arbitrary", "arbitrary"), vmem_limit_bytes=VMEM_LIMIT),
        name="mixer_inproj",
    )(x, shift, scale, w_in)


def _outproj_kernel(m_ref, x_ref, gate_ref, w_ref, g_ref, b_ref, o_ref):
    mixed = _dot(m_ref[...], w_ref[...])
    y = ALPHA * x_ref[...] + gate_ref[0] * mixed
    o_ref[...] = _layer_norm(y, g_ref[...], b_ref[...])


def _outproj(merged, x, gate, w_o, ln_g, ln_b, *, tm, rows_per_mod):
    m = x.shape[0]
    r = gate.shape[1]
    tiles_per_mod = rows_per_mod // tm
    return pl.pallas_call(
        _outproj_kernel,
        grid=(m // tm,),
        in_specs=[
            pl.BlockSpec((tm, D_MODEL), lambda i: (i, 0)),
            pl.BlockSpec((tm, D_MODEL), lambda i: (i, 0)),
            pl.BlockSpec((1, r, D_MODEL), lambda i: (i // tiles_per_mod, 0, 0)),
            pl.BlockSpec((D_MODEL, D_MODEL), lambda i: (0, 0)),
            pl.BlockSpec((1, D_MODEL), lambda i: (0, 0)),
            pl.BlockSpec((1, D_MODEL), lambda i: (0, 0)),
        ],
        out_specs=pl.BlockSpec((tm, D_MODEL), lambda i: (i, 0)),
        out_shape=jax.ShapeDtypeStruct((m, D_MODEL), F32),
        compiler_params=pltpu.CompilerParams(dimension_semantics=("arbitrary",), vmem_limit_bytes=VMEM_LIMIT),
        name="mixer_outproj_ln",
    )(merged, x, gate, w_o, ln_g.reshape(1, D_MODEL), ln_b.reshape(1, D_MODEL))


def _gmlp_norm(pv, gng, gnb):
    v = _gelu_tanh(pv)
    outs = []
    for g in range(N_GMLP_GROUPS):
        sl = slice(g * LANES, (g + 1) * LANES)
        outs.append(_layer_norm(v[:, sl], gng[:, sl], gnb[:, sl]))
    return outs


def _half_lane_tiles(t):
    lo_lane = lax.broadcasted_iota(jnp.int32, (t.shape[0], LANES), 1) < HEAD_DIM
    lo, hi = [], []
    for c in range(KV_WIDTH // LANES):
        col = t[:, c * LANES:(c + 1) * LANES]
        swp = pltpu.roll(col, HEAD_DIM, 1)
        lo += [jnp.where(lo_lane, col, 0.0), jnp.where(lo_lane, swp, 0.0)]
        hi += [jnp.where(lo_lane, 0.0, swp), jnp.where(lo_lane, 0.0, col)]
    return jnp.concatenate(lo + hi, axis=-1).astype(BF16)


def _band_attention(q_s, kx_s, vx_s, attn_s, sinks_ref, nblk, first_tile):
    rows = 2 * WINDOW
    qi = lax.broadcasted_iota(jnp.int32, (rows, WINDOW), 0) & (WINDOW - 1)
    kj = lax.broadcasted_iota(jnp.int32, (rows, WINDOW), 1)
    own = kj <= qi
    first_mask = kj <= jnp.where(first_tile, qi, WINDOW)
    ones = jnp.ones((2 * WINDOW, LANES), BF16)
    units = [(wb, g, hf) for wb in range(nblk) for g in range(N_KV_HEADS) for hf in range(2)]

    def kv_cols(g, hf):
        return slice(hf * 4 * LANES + g * LANES, hf * 4 * LANES + (g + 1) * LANES)

    scores = []
    for wb, g, hf in units:
        r0 = wb * WINDOW
        q2 = jnp.concatenate([q_s[r0:r0 + WINDOW, (2 * g) * LANES:(2 * g + 1) * LANES],
                              q_s[r0:r0 + WINDOW, (2 * g + 1) * LANES:(2 * g + 2) * LANES]], axis=0)
        s = _dot_nt(q2, kx_s[r0:r0 + 2 * WINDOW, kv_cols(g, hf)])
        c = jnp.where(own, s[:, WINDOW:2 * WINDOW], s[:, 0:WINDOW])
        if wb == 0:
            c = jnp.where(first_mask, c, -jnp.inf)
        scores.append(c)

    probs = []
    for (wb, g, hf), c in zip(units, scores):
        sink = jnp.concatenate([jnp.full((WINDOW, LANES), sinks_ref[GQA_GROUP * g + hf], F32),
                                jnp.full((WINDOW, LANES), sinks_ref[GQA_GROUP * g + 2 + hf], F32)], axis=0)
        mx = jnp.maximum(jnp.broadcast_to(jnp.max(c, axis=-1, keepdims=True), c.shape), sink)
        p = jnp.exp(c - mx)
        pcat = jnp.concatenate([jnp.where(own, 0.0, p), jnp.where(own, p, 0.0)], axis=-1).astype(BF16)
        probs.append((pcat, jnp.exp(sink - mx)))

    outs = {}
    for (wb, g, hf), (pcat, esink) in zip(units, probs):
        r0 = wb * WINDOW
        v2 = jnp.concatenate([vx_s[r0:r0 + 2 * WINDOW, kv_cols(g, hf)], ones], axis=-1)
        o = _dot(pcat, v2)
        outs[wb, g, hf] = o[:, 0:LANES] / (o[:, LANES:2 * LANES] + esink)

    for wb in range(nblk):
        r0 = wb * WINDOW
        for g in range(N_KV_HEADS):
            both = outs[wb, g, 0] + outs[wb, g, 1]
            attn_s[r0:r0 + WINDOW, (2 * g) * LANES:(2 * g + 1) * LANES] = both[0:WINDOW]
            attn_s[r0:r0 + WINDOW, (2 * g + 1) * LANES:(2 * g + 2) * LANES] = both[WINDOW:2 * WINDOW]


def _pmix_kernel(sinks_ref, x_ref, mod_ref, win_ref, wo_ref, cos_ref, slo_ref, shi_ref,
                 gng_ref, gnb_ref, ws_ref, bs_ref, og_ref, lngb_ref,
                 o_ref, k_ref, v_ref, xm_s, p_s, q_s, kk_s, vv_s, kp_s, vp_s, attn_s, m_s, wm_s):
    t = pl.program_id(1)
    tm = x_ref.shape[0]
    nblk = tm // WINDOW

    @pl.when(jnp.logical_and(pl.program_id(0) == 0, t == 0))
    def _():
        row = lax.broadcasted_iota(jnp.int32, (CHUNK, CHUNK), 0)
        col = lax.broadcasted_iota(jnp.int32, (CHUNK, CHUNK), 1)
        for g in range(N_GMLP_GROUPS):
            wm_s[g] = jnp.where(col <= row, ws_ref[g], 0.0).astype(BF16)

    @pl.when(t == 0)
    def _():
        kp_s[...] = jnp.zeros(kp_s.shape, BF16)
        vp_s[...] = jnp.zeros(vp_s.shape, BF16)

    seq = pl.ds(pl.program_id(0), 1)
    xm_s[...] = (x_ref[...] * (1.0 + mod_ref[1, seq, :]) + mod_ref[0, seq, :]).astype(BF16)
    for c in range(IN_WIDTH // PROJ_TILE):
        sl = slice(c * PROJ_TILE, (c + 1) * PROJ_TILE)
        p_s[:, sl] = _dot(xm_s[...], win_ref[:, sl])

    cos, slo, shi = cos_ref[...], slo_ref[...], shi_ref[...]
    q_s[...] = (_rope_lanes(p_s[:, 0:Q_END], cos, slo, shi) * (HEAD_DIM ** -0.5)).astype(BF16)
    k = _rope_lanes(p_s[:, Q_END:K_END], cos, slo, shi)
    v = p_s[:, K_END:V_END]
    k_ref[0] = k[tm - WINDOW:tm]
    v_ref[0] = v[tm - WINDOW:tm]
    kb, vb = _half_lane_tiles(k), _half_lane_tiles(v)
    kk_s[0:WINDOW, :] = kp_s[...]
    vv_s[0:WINDOW, :] = vp_s[...]
    kk_s[WINDOW:WINDOW + tm, :] = kb
    vv_s[WINDOW:WINDOW + tm, :] = vb
    kp_s[...] = kb[tm - WINDOW:tm]
    vp_s[...] = vb[tm - WINDOW:tm]

    _band_attention(q_s, kk_s, vv_s, attn_s, sinks_ref, nblk, t == 0)

    vn = _gmlp_norm(p_s[:, U_END:IN_WIDTH], gng_ref[...], gnb_ref[...])
    gated = []
    for g in range(N_GMLP_GROUPS):
        u = _gelu_tanh(p_s[:, V_END + g * LANES:V_END + (g + 1) * LANES])
        vg = vn[g].astype(BF16)
        cols = jnp.concatenate([vg[wb * CHUNK:(wb + 1) * CHUNK] for wb in range(nblk)], axis=-1)
        mix = _dot(wm_s[g], cols)
        mix = jnp.concatenate([mix[:, wb * LANES:(wb + 1) * LANES] for wb in range(nblk)], axis=0)
        bias = jnp.concatenate([bs_ref[:, g:g + 1]] * nblk, axis=0)
        gated.append(u * (mix + bias))
    gm = jnp.concatenate(gated, axis=-1)
    og = og_ref[...]
    m_s[:, 0:ATTN_WIDTH] = _rms_norm(attn_s[...], og[:, 0:ATTN_WIDTH]).astype(BF16)
    m_s[:, ATTN_WIDTH:] = _rms_norm(gm, og[:, ATTN_WIDTH:]).astype(BF16)

    for c in range(D_MODEL // PROJ_TILE):
        sl = slice(c * PROJ_TILE, (c + 1) * PROJ_TILE)
        o_ref[:, sl] = ALPHA * x_ref[:, sl] + mod_ref[2, seq, sl] * _dot(m_s[...], wo_ref[:, sl])
    o_ref[...] = _layer_norm(o_ref[...], lngb_ref[0:1], lngb_ref[1:2])


def _rope_tables(pos):
    half = ROPE_DIM // 2
    inv = ROPE_THETA ** (-(np.arange(half, dtype=np.float64) * 2.0) / ROPE_DIM)
    ang = np.asarray(pos, np.float64)[:, None] * inv[None, :]
    cos, sin = np.cos(ang), np.sin(ang)
    n = ang.shape[0]
    one = np.ones((n, HEAD_DIM - ROPE_DIM))
    zero = np.zeros((n, HEAD_DIM - ROPE_DIM))
    z8 = np.zeros((n, half))
    cos_t = np.concatenate([cos, cos, one], axis=-1)
    sin_lo = np.concatenate([-sin, z8, zero], axis=-1)
    sin_hi = np.concatenate([z8, sin, zero], axis=-1)
    rep = LANES // HEAD_DIM
    return tuple(jnp.asarray(np.tile(t, (1, rep)), F32) for t in (cos_t, sin_lo, sin_hi))


def _pmix(x, mod, sub, w_in, w_o, sinks, tables, gn_g, gn_b, w_s, b_s, out_g, ln_gb, *, batch, tm):
    nt = SEQ // tm
    cos_t, sin_lo, sin_hi = tables
    tab_spec = pl.BlockSpec((tm, LANES), lambda b, j: (j, 0))
    full2 = lambda b, j: (0, 0)
    mod_spec = pl.BlockSpec((3, mod.shape[1], D_MODEL), lambda b, j: (sub, 0, 0))
    resident = dict(pipeline_mode=pl.Buffered(1))
    return pl.pallas_call(
        _pmix_kernel,
        grid=(batch, nt),
        in_specs=[
            pl.BlockSpec(memory_space=pltpu.SMEM),
            pl.BlockSpec((tm, D_MODEL), lambda b, j: (b * nt + j, 0)),
            mod_spec,
            pl.BlockSpec((D_MODEL, IN_WIDTH), full2, **resident),
            pl.BlockSpec((D_MODEL, D_MODEL), full2, **resident),
            tab_spec, tab_spec, tab_spec,
            pl.BlockSpec((1, GMLP_WIDTH), full2),
            pl.BlockSpec((1, GMLP_WIDTH), full2),
            pl.BlockSpec((N_GMLP_GROUPS, CHUNK, CHUNK), lambda b, j: (0, 0, 0)),
            pl.BlockSpec((CHUNK, N_GMLP_GROUPS), full2),
            pl.BlockSpec((1, D_MODEL), full2),
            pl.BlockSpec((2, D_MODEL), full2),
        ],
        out_specs=[
            pl.BlockSpec((tm, D_MODEL), lambda b, j: (b * nt + j, 0)),
            pl.BlockSpec((1, WINDOW, KV_WIDTH), lambda b, j: (b, 0, 0)),
            pl.BlockSpec((1, WINDOW, KV_WIDTH), lambda b, j: (b, 0, 0)),
        ],
        out_shape=[
            jax.ShapeDtypeStruct((batch * SEQ, D_MODEL), F32),
            jax.ShapeDtypeStruct((batch, WINDOW, KV_WIDTH), F32),
            jax.ShapeDtypeStruct((batch, WINDOW, KV_WIDTH), F32),
        ],
        scratch_shapes=[
            pltpu.VMEM((tm, D_MODEL), BF16),
            pltpu.VMEM((tm, IN_WIDTH), F32),
            pltpu.VMEM((tm, ATTN_WIDTH), BF16),
            pltpu.VMEM((WINDOW + tm, 4 * KV_WIDTH), BF16),
            pltpu.VMEM((WINDOW + tm, 4 * KV_WIDTH), BF16),
            pltpu.VMEM((WINDOW, 4 * KV_WIDTH), BF16),
            pltpu.VMEM((WINDOW, 4 * KV_WIDTH), BF16),
            pltpu.VMEM((tm, ATTN_WIDTH), F32),
            pltpu.VMEM((tm, D_MODEL), BF16),
            pltpu.VMEM((N_GMLP_GROUPS, CHUNK, CHUNK), BF16),
        ],
        compiler_params=pltpu.CompilerParams(dimension_semantics=("arbitrary", "arbitrary"), vmem_limit_bytes=VMEM_LIMIT),
        name="prompt_mixer",
    )(sinks, x, mod, w_in, w_o, cos_t, sin_lo, sin_hi, gn_g.reshape(1, GMLP_WIDTH),
      gn_b.reshape(1, GMLP_WIDTH), w_s, b_s.T, out_g.reshape(1, D_MODEL), ln_gb)


def _sattn_kernel(qe_ref, p_ref, ck_ref, cv_ref, sink_ref, cos_ref, slo_ref, shi_ref, o_ref, ko_ref, vo_ref):
    cos, slo, shi = cos_ref[...], slo_ref[...], shi_ref[...]
    sink = sink_ref[...]
    for r in range(qe_ref.shape[0]):
        qe = _rope_lanes(qe_ref[r], cos, slo, shi) * (HEAD_DIM ** -0.5)
        kn = _rope_lanes(p_ref[r, :, Q_END:K_END], cos, slo, shi)
        vn = p_ref[r, :, K_END:V_END]
        ck = ck_ref[r]
        cv = cv_ref[r]
        s_c = _dot_nt(qe.astype(BF16), ck.astype(BF16))
        key = lax.broadcasted_iota(jnp.int32, s_c.shape, 1)
        s_c = jnp.where(key >= 1, s_c, -jnp.inf)
        s_n = jnp.sum(qe * kn, axis=-1, keepdims=True)
        mx = jnp.maximum(jnp.maximum(jnp.max(s_c, axis=-1, keepdims=True), s_n), sink)
        p_c = jnp.exp(s_c - mx)
        p_n = jnp.exp(s_n - mx)
        denom = jnp.sum(p_c, axis=-1, keepdims=True) + p_n + jnp.exp(sink - mx)
        o = (_dot(p_c.astype(BF16), cv.astype(BF16)) + p_n * vn) / denom
        head = lax.broadcasted_iota(jnp.int32, o.shape, 0)
        lane = lax.broadcasted_iota(jnp.int32, o.shape, 1)
        o_ref[r] = jnp.where(head // GQA_GROUP == lane // HEAD_DIM, o, 0.0)
        row = lax.broadcasted_iota(jnp.int32, ck.shape, 0)
        newest = row == WINDOW - 1
        ko_ref[r] = jnp.where(newest, kn, pltpu.roll(ck, WINDOW - 1, 0))
        vo_ref[r] = jnp.where(newest, vn, pltpu.roll(cv, WINDOW - 1, 0))


def _sattn(qe, p3, ck, cv, sinks, tables):
    b = qe.shape[0]
    rows = SAMPLE_ROWS_PER_STEP
    cos_t, sin_lo, sin_hi = tables
    tab_spec = pl.BlockSpec((1, LANES), lambda i: (0, 0))
    blk3 = lambda i: (i, 0, 0)
    return pl.pallas_call(
        _sattn_kernel,
        grid=(b // rows,),
        in_specs=[
            pl.BlockSpec((rows, N_HEADS, KV_WIDTH), blk3),
            pl.BlockSpec((rows, 1, IN_WIDTH), blk3),
            pl.BlockSpec((rows, WINDOW, KV_WIDTH), blk3),
            pl.BlockSpec((rows, WINDOW, KV_WIDTH), blk3),
            pl.BlockSpec((N_HEADS, 1), lambda i: (0, 0)),
            tab_spec, tab_spec, tab_spec,
        ],
        out_specs=[
            pl.BlockSpec((rows, N_HEADS, KV_WIDTH), blk3),
            pl.BlockSpec((rows, WINDOW, KV_WIDTH), blk3),
            pl.BlockSpec((rows, WINDOW, KV_WIDTH), blk3),
        ],
        out_shape=[
            jax.ShapeDtypeStruct((b, N_HEADS, KV_WIDTH), F32),
            jax.ShapeDtypeStruct((b, WINDOW, KV_WIDTH), F32),
            jax.ShapeDtypeStruct((b, WINDOW, KV_WIDTH), F32),
        ],
        compiler_params=pltpu.CompilerParams(dimension_semantics=("arbitrary",)),
        name="sample_attn",
    )(qe, p3, ck, cv, sinks.reshape(N_HEADS, 1), cos_t, sin_lo, sin_hi)


def _smerge_kernel(attn_ref, p_ref, gng_ref, gnb_ref, w0_ref, b0_ref, og_ref, m_ref, vn_ref):
    og = og_ref[...]
    vn = jnp.concatenate(_gmlp_norm(p_ref[:, U_END:IN_WIDTH], gng_ref[...], gnb_ref[...]), axis=-1)
    vn_ref[...] = vn
    u = _gelu_tanh(p_ref[:, V_END:U_END])
    gm = u * (w0_ref[...] * vn + b0_ref[...])
    m_ref[:, 0:ATTN_WIDTH] = _rms_norm(attn_ref[...], og[:, 0:ATTN_WIDTH]).astype(BF16)
    m_ref[:, ATTN_WIDTH:] = _rms_norm(gm, og[:, ATTN_WIDTH:]).astype(BF16)


def _smerge(attn, p, gn_g, gn_b, w_s, b_s, out_g):
    b = attn.shape[0]
    w0 = jnp.repeat(w_s[:, 0, 0], LANES).reshape(1, GMLP_WIDTH)
    b0 = jnp.repeat(b_s[:, 0], LANES).reshape(1, GMLP_WIDTH)
    return pl.pallas_call(
        _smerge_kernel,
        out_shape=[
            jax.ShapeDtypeStruct((b, D_MODEL), BF16),
            jax.ShapeDtypeStruct((b, GMLP_WIDTH), F32),
        ],
        name="sample_gmlp_merge",
    )(attn, p, gn_g.reshape(1, GMLP_WIDTH), gn_b.reshape(1, GMLP_WIDTH), w0, b0, out_g.reshape(1, D_MODEL))


def kernel(x_prompt, x_sample, cache_k_win, cache_v_win, c_prompt, c_sample, w_ada, b_ada, ln_g, ln_b,
           w_ffn_up, w_ffn_down, w_in, attn_sinks, gmlp_norm_g, gmlp_norm_b, w_spatial, b_spatial,
           out_norm_g, w_o):
    batch, seq, _ = x_prompt.shape
    dec_batch = x_sample.shape[0]
    buf = cache_k_win.shape[2]
    assert seq == SEQ and buf == WINDOW and x_sample.shape[1] == 1 and w_ada.shape[0] == DEPTH == 1

    xp = x_prompt.reshape(batch * seq, D_MODEL)
    xs = x_sample.reshape(dec_batch, D_MODEL)

    n_c = batch + dec_batch
    n_c_pad = ((n_c + 7) // 8) * 8
    c_all = jnp.concatenate([c_prompt, c_sample, jnp.zeros((n_c_pad - n_c, D_MODEL), F32)], axis=0)
    mod = _ada(c_all, w_ada[0], b_ada[0])

    def smod(i, j):
        return mod[3 * i + j, batch:n_c].reshape(1, dec_batch, D_MODEL)

    ln_gb = jnp.stack([ln_g[0], ln_b[0]], axis=1)
    ffn = functools.partial(_ffn, tm=FFN_ROWS, rows_per_mod=seq, extra_row0=batch)

    w_up0 = w_ffn_up[0, 0].astype(BF16)
    w_down0 = w_ffn_down[0, 0].astype(BF16)
    later = ((w_ffn_up, (0, 1), 16), (w_ffn_down, (0, 1), 32), (w_in, (0,), 16), (w_o, (0,), 16))
    xp, xs, w_up1, w_down1, w_in_b, w_o_b = ffn(xp, mod, 0, w_up0, w_down0, ln_gb[0], extra=xs, side=later)

    xp, k_p, v_p = _pmix(xp, mod, 1, w_in_b, w_o_b, attn_sinks[0], _rope_tables(np.arange(seq)),
                         gmlp_norm_g[0], gmlp_norm_b[0], w_spatial[0], b_spatial[0], out_norm_g[0], ln_gb[1],
                         batch=batch, tm=MIX_TILE)

    ps = _inproj(xs, smod(1, 0), smod(1, 1), w_in_b, tm=dec_batch, rows_per_mod=dec_batch)
    eye = jnp.eye(N_KV_HEADS, dtype=F32)
    q5 = ps[:, :Q_END].reshape(dec_batch, N_KV_HEADS, GQA_GROUP, 1, HEAD_DIM)
    qe = (q5 * eye[None, :, None, :, None]).reshape(dec_batch, N_HEADS, KV_WIDTH)
    ck = cache_k_win[0].reshape(dec_batch, buf, KV_WIDTH)
    cv = cache_v_win[0].reshape(dec_batch, buf, KV_WIDTH)
    oe, k_s, v_s = _sattn(qe, ps.reshape(dec_batch, 1, IN_WIDTH), ck, cv, attn_sinks[0],
                          _rope_tables(np.full((1,), PAST_LEN)))
    attn_s = oe.reshape(dec_batch, N_KV_HEADS, GQA_GROUP, N_KV_HEADS, HEAD_DIM).sum(axis=3).reshape(dec_batch, ATTN_WIDTH)
    merged_s, vn_s = _smerge(attn_s, ps, gmlp_norm_g[0], gmlp_norm_b[0], w_spatial[0], b_spatial[0], out_norm_g[0])
    xs = _outproj(merged_s, xs, smod(1, 2), w_o_b, ln_g[0, 1], ln_b[0, 1], tm=dec_batch, rows_per_mod=dec_batch)

    xp, xs = ffn(xp, mod, 2, w_up1, w_down1, ln_gb[2], extra=xs)

    return (
        xp.reshape(batch, seq, D_MODEL),
        xs.reshape(dec_batch, 1, D_MODEL),
        k_p.reshape(1, batch, WINDOW, N_KV_HEADS, HEAD_DIM),
        v_p.reshape(1, batch, WINDOW, N_KV_HEADS, HEAD_DIM),
        k_s.reshape(1, dec_batch, buf, N_KV_HEADS, HEAD_DIM),
        v_s.reshape(1, dec_batch, buf, N_KV_HEADS, HEAD_DIM),
        vn_s.reshape(1, dec_batch, 1, GMLP_WIDTH),
    )
```

```python
import functools

import jax
import jax.numpy as jnp
import numpy as np
from jax import lax
from jax.experimental import pallas as pl
from jax.experimental.pallas import tpu as pltpu

D_MODEL = 2048
SEQ = 2048
PAST_LEN = 16384
ATTN_WIDTH = 1024
GMLP_WIDTH = 1024
HEAD_DIM = 64
N_HEADS = 16
N_KV_HEADS = 4
GQA_GROUP = 4
KV_WIDTH = 256
WINDOW = 128
ROPE_THETA = 500000.0
ROPE_DIM = 16
CHUNK = 128
N_GMLP_GROUPS = 8
D_FF = 5504
N_SUB = 3
DEPTH = 1
ALPHA = (2.0 * DEPTH) ** 0.25
LN_EPS = 1e-5
Q_END = ATTN_WIDTH
K_END = Q_END + KV_WIDTH
V_END = K_END + KV_WIDTH
U_END = V_END + GMLP_WIDTH
IN_WIDTH = U_END + GMLP_WIDTH

LANES = 128
MXU_COLS = 256
FF_TILE = 512
N_FF_STEPS = -(-D_FF // FF_TILE)
FF_LAST_OFF = D_FF - FF_TILE
FF_OVERLAP = N_FF_STEPS * FF_TILE - D_FF
OUT_TILE = 512
N_OUT_STEPS = D_MODEL // OUT_TILE
FFN_ROWS = 512
LN_PIECE_ROWS = 64
MIX_TILE = 256
PROJ_TILE = 512
SAMPLE_ROWS_PER_STEP = 8
VMEM_LIMIT = 60 * 1024 * 1024

BF16 = jnp.bfloat16
F32 = jnp.float32


def _dot(a, b):
    return jnp.dot(a, b, preferred_element_type=F32)


def _dot_nt(a, b):
    return lax.dot_general(a, b, (((1,), (1,)), ((), ())), preferred_element_type=F32)


def _layer_norm(y, g, b):
    mu = jnp.mean(y, axis=-1, keepdims=True)
    d = y - mu
    var = jnp.mean(d * d, axis=-1, keepdims=True)
    return d * lax.rsqrt(var + LN_EPS) * g + b


def _rms_norm(y, g):
    return y * lax.rsqrt(jnp.mean(y * y, axis=-1, keepdims=True) + LN_EPS) * g


def _gelu_tanh(x):
    c = np.float32(np.sqrt(2.0 / np.pi))
    return 0.5 * x * (1.0 + jnp.tanh(c * (x + 0.044715 * (x * x * x))))


def _rope_lanes(t, cos, sin_lo, sin_hi):
    pieces = []
    for c in range(t.shape[-1] // LANES):
        x = t[:, c * LANES:(c + 1) * LANES]
        pieces.append(x * cos + pltpu.roll(x, 8, 1) * sin_hi + pltpu.roll(x, LANES - 8, 1) * sin_lo)
    return pieces[0] if len(pieces) == 1 else jnp.concatenate(pieces, axis=-1)


def _ada_kernel(c_ref, w_ref, b_ref, o_ref):
    c = c_ref[...]
    h = (c * jax.nn.sigmoid(c)).astype(BF16)
    o_ref[0] = _dot(h, w_ref[...].astype(BF16)) + b_ref[...]


def _ada(c, w_ada, b_ada):
    rows = c.shape[0]
    n = w_ada.shape[1]
    tn = 1024
    per_plane = D_MODEL // tn
    return pl.pallas_call(
        _ada_kernel,
        grid=(n // tn,),
        in_specs=[
            pl.BlockSpec((rows, D_MODEL), lambda j: (0, 0)),
            pl.BlockSpec((D_MODEL, tn), lambda j: (0, j)),
            pl.BlockSpec((1, tn), lambda j: (0, j)),
        ],
        out_specs=pl.BlockSpec((1, rows, tn), lambda j: (j // per_plane, 0, j % per_plane)),
        out_shape=jax.ShapeDtypeStruct((n // D_MODEL, rows, D_MODEL), F32),
        compiler_params=pltpu.CompilerParams(dimension_semantics=("arbitrary",), vmem_limit_bytes=VMEM_LIMIT),
        name="ada_mod",
    )(c, w_ada, b_ada.reshape(1, n))


def _ff_offset(step):
    return jnp.minimum(step * FF_TILE, FF_LAST_OFF)


def _ffn_kernel(n_side, n_extra, tiles_per_mod, extra_row0, n_tiles, *refs):
    x_ref, mod_ref, wv_ref, wg_ref, wd_ref, gb_ref = refs[:6]
    n_in = 6
    if n_extra:
        xe_ref = refs[n_in]
        n_in += 1
        erows = slice(extra_row0, extra_row0 + n_extra)
    side_in = refs[n_in:n_in + n_side]
    n_in += n_side
    o_ref = refs[n_in]
    n_out = 1
    if n_extra:
        oe_ref = refs[n_in + 1]
        n_out = 2
    side_out = refs[n_in + n_out:n_in + n_out + n_side]
    xm_ref, a_ref, y_ref = refs[n_in + n_out + n_side:]
    i = pl.program_id(0)
    s = pl.program_id(1)
    tm = x_ref.shape[0]
    last = pl.num_programs(1) - 1

    def cast_side():
        for src, dst in zip(side_in, side_out):
            dst[...] = src[...].astype(BF16)

    def norm_piece():
        r0 = pl.multiple_of(jnp.minimum(s, tm // LN_PIECE_ROWS - 1) * LN_PIECE_ROWS, LN_PIECE_ROWS)
        piece = pl.ds(r0, LN_PIECE_ROWS)
        y = jnp.concatenate([y_ref[n, piece, :] for n in range(N_OUT_STEPS)], axis=-1)
        o_ref[piece, :] = _layer_norm(y, gb_ref[0:1], gb_ref[1:2])

    def up_chunk(rows, norm_previous):
        cast_side()
        if norm_previous:
            norm_piece()
        xm = xm_ref[0:rows]
        for c in range(FF_TILE // MXU_COLS):
            sl = slice(c * MXU_COLS, (c + 1) * MXU_COLS)
            hv = _dot(xm, wv_ref[:, sl])
            hg = _dot(xm, wg_ref[:, sl])
            a = hg * jax.nn.sigmoid(hg) * hv
            if c * MXU_COLS < FF_OVERLAP:
                col = lax.broadcasted_iota(jnp.int32, a.shape, 1) + c * MXU_COLS
                a = jnp.where(jnp.logical_and(s == N_FF_STEPS - 1, col < FF_OVERLAP), 0.0, a)
            a_ref[s, 0:rows, sl] = a.astype(BF16)

    def down_slab(rows):
        cast_side()
        acc = None
        for f in range(N_FF_STEPS):
            r0 = min(f * FF_TILE, FF_LAST_OFF)
            part = _dot(a_ref[f, 0:rows], wd_ref[r0:r0 + FF_TILE, :])
            acc = part if acc is None else acc + part
        y_ref[s - N_FF_STEPS, 0:rows] = acc

    first = i == 0
    later = i > 0
    up_phase = s < N_FF_STEPS
    down_phase = s >= N_FF_STEPS
    seq = pl.ds(i // tiles_per_mod, 1)

    @pl.when(s == 0)
    def _():
        xm_ref[0:tm] = (x_ref[...] * (1.0 + mod_ref[1, seq, :]) + mod_ref[0, seq, :]).astype(BF16)

    if n_extra:
        @pl.when(jnp.logical_and(s == 0, first))
        def _():
            xm_ref[tm:tm + n_extra] = (xe_ref[...] * (1.0 + mod_ref[1, erows, :]) + mod_ref[0, erows, :]).astype(BF16)

    pl.when(jnp.logical_and(up_phase, first))(functools.partial(up_chunk, tm + n_extra, False))
    pl.when(jnp.logical_and(up_phase, later))(functools.partial(up_chunk, tm, True))
    pl.when(jnp.logical_and(down_phase, first))(functools.partial(down_slab, tm + n_extra))
    pl.when(jnp.logical_and(down_phase, later))(functools.partial(down_slab, tm))

    @pl.when(s == last)
    def _():
        for n in range(N_OUT_STEPS):
            sl = slice(n * OUT_TILE, (n + 1) * OUT_TILE)
            y_ref[n, 0:tm] = ALPHA * x_ref[:, sl] + 0.5 * mod_ref[2, seq, sl] * y_ref[n, 0:tm]

    @pl.when(jnp.logical_and(s == last, i == n_tiles - 1))
    def _():
        y = jnp.concatenate([y_ref[n, 0:tm] for n in range(N_OUT_STEPS)], axis=-1)
        o_ref[...] = _layer_norm(y, gb_ref[0:1], gb_ref[1:2])

    if n_extra:
        @pl.when(jnp.logical_and(s == last, first))
        def _():
            mixed = jnp.concatenate([y_ref[n, tm:tm + n_extra] for n in range(N_OUT_STEPS)], axis=-1)
            oe_ref[...] = _layer_norm(ALPHA * xe_ref[...] + 0.5 * mod_ref[2, erows, :] * mixed, gb_ref[0:1], gb_ref[1:2])


def _ffn(x, mod, sub, w_up, w_down, ln_gb, *, tm, rows_per_mod, extra=None, extra_row0=0, side=()):
    m = x.shape[0]
    n_steps = N_FF_STEPS + N_OUT_STEPS
    tiles_per_mod = rows_per_mod // tm
    n_extra = 0 if extra is None else extra.shape[0]
    const2 = lambda i, s: (0, 0)
    extra_spec = pl.BlockSpec((n_extra, D_MODEL), const2)
    side_in_specs, side_out_specs, side_shapes = [], [], []
    for arr, lead, rows in side:
        n_rows, n_cols = arr.shape[-2:]
        n_blocks = n_rows // rows
        assert n_blocks * rows == n_rows and n_blocks <= (m // tm) * n_steps
        blk = lambda i, s, n_blocks=n_blocks: jnp.minimum(i * n_steps + s, n_blocks - 1)
        side_in_specs.append(pl.BlockSpec((None,) * len(lead) + (rows, n_cols),
                                          lambda i, s, lead=lead, blk=blk: (*lead, blk(i, s), 0)))
        side_out_specs.append(pl.BlockSpec((rows, n_cols), lambda i, s, blk=blk: (blk(i, s), 0)))
        side_shapes.append(jax.ShapeDtypeStruct((n_rows, n_cols), BF16))
    n_tiles = m // tm
    assert tm % LN_PIECE_ROWS == 0 and tm // LN_PIECE_ROWS <= N_FF_STEPS
    wd_block = lambda i, s: (0, jnp.where(s < N_FF_STEPS, N_OUT_STEPS - 1, s - N_FF_STEPS))
    out_block = lambda i, s: (jnp.where(s < N_FF_STEPS, jnp.maximum(i - 1, 0), i), 0)
    outs = pl.pallas_call(
        functools.partial(_ffn_kernel, len(side), n_extra, tiles_per_mod, extra_row0, n_tiles),
        grid=(n_tiles, n_steps),
        in_specs=[
            pl.BlockSpec((tm, D_MODEL), lambda i, s: (i, 0)),
            pl.BlockSpec((3, mod.shape[1], D_MODEL), lambda i, s: (sub, 0, 0)),
            pl.BlockSpec((pl.Element(D_MODEL), pl.Element(FF_TILE)),
                         lambda i, s: (0, pl.multiple_of(_ff_offset(s), LANES))),
            pl.BlockSpec((pl.Element(D_MODEL), pl.Element(FF_TILE)),
                         lambda i, s: (0, pl.multiple_of(D_FF + _ff_offset(s), LANES))),
            pl.BlockSpec((D_FF, OUT_TILE), wd_block),
            pl.BlockSpec((2, D_MODEL), const2),
            *([extra_spec] if n_extra else []),
            *side_in_specs,
        ],
        out_specs=[pl.BlockSpec((tm, D_MODEL), out_block), *([extra_spec] if n_extra else []),
                   *side_out_specs],
        out_shape=[jax.ShapeDtypeStruct((m, D_MODEL), F32),
                   *([jax.ShapeDtypeStruct((n_extra, D_MODEL), F32)] if n_extra else []), *side_shapes],
        scratch_shapes=[
            pltpu.VMEM((tm + n_extra, D_MODEL), BF16),
            pltpu.VMEM((N_FF_STEPS, tm + n_extra, FF_TILE), BF16),
            pltpu.VMEM((N_OUT_STEPS, tm + n_extra, OUT_TILE), F32),
        ],
        compiler_params=pltpu.CompilerParams(dimension_semantics=("arbitrary", "arbitrary"), vmem_limit_bytes=VMEM_LIMIT),
        name="swiglu_ln",
    )(x, mod, w_up, w_up, w_down, ln_gb, *([extra] if n_extra else []), *[arr for arr, _, _ in side])
    return outs


def _inproj_kernel(x_ref, shift_ref, scale_ref, w_ref, o_ref, xm_ref):
    @pl.when(pl.program_id(1) == 0)
    def _():
        xm_ref[...] = (x_ref[...] * (1.0 + scale_ref[0]) + shift_ref[0]).astype(BF16)

    o_ref[...] = _dot(xm_ref[...], w_ref[...])


def _inproj(x, shift, scale, w_in, *, tm, rows_per_mod):
    m = x.shape[0]
    r = shift.shape[1]
    tn = 512
    tiles_per_mod = rows_per_mod // tm
    mod_spec = pl.BlockSpec((1, r, D_MODEL), lambda i, j: (i // tiles_per_mod, 0, 0))
    return pl.pallas_call(
        _inproj_kernel,
        grid=(m // tm, IN_WIDTH // tn),
        in_specs=[
            pl.BlockSpec((tm, D_MODEL), lambda i, j: (i, 0)),
            mod_spec, mod_spec,
            pl.BlockSpec((D_MODEL, tn), lambda i, j: (0, j)),
        ],
        out_specs=pl.BlockSpec((tm, tn), lambda i, j: (i, j)),
        out_shape=jax.ShapeDtypeStruct((m, IN_WIDTH), F32),
        scratch_shapes=[pltpu.VMEM((tm, D_MODEL), BF16)],
        compiler_params=pltpu.CompilerParams(dimension_semantics=("arbitrary", "arbitrary"), vmem_limit_bytes=VMEM_LIMIT),
        name="mixer_inproj",
    )(x, shift, scale, w_in)


def _outproj_kernel(m_ref, x_ref, gate_ref, w_ref, g_ref, b_ref, o_ref):
    mixed = _dot(m_ref[...], w_ref[...])
    y = ALPHA * x_ref[...] + gate_ref[0] * mixed
    o_ref[...] = _layer_norm(y, g_ref[...], b_ref[...])


def _outproj(merged, x, gate, w_o, ln_g, ln_b, *, tm, rows_per_mod):
    m = x.shape[0]
    r = gate.shape[1]
    tiles_per_mod = rows_per_mod // tm
    return pl.pallas_call(
        _outproj_kernel,
        grid=(m // tm,),
        in_specs=[
            pl.BlockSpec((tm, D_MODEL), lambda i: (i, 0)),
            pl.BlockSpec((tm, D_MODEL), lambda i: (i, 0)),
            pl.BlockSpec((1, r, D_MODEL), lambda i: (i // tiles_per_mod, 0, 0)),
            pl.BlockSpec((D_MODEL, D_MODEL), lambda i: (0, 0)),
            pl.BlockSpec((1, D_MODEL), lambda i: (0, 0)),
            pl.BlockSpec((1, D_MODEL), lambda i: (0, 0)),
        ],
        out_specs=pl.BlockSpec((tm, D_MODEL), lambda i: (i, 0)),
        out_shape=jax.ShapeDtypeStruct((m, D_MODEL), F32),
        compiler_params=pltpu.CompilerParams(dimension_semantics=("arbitrary",), vmem_limit_bytes=VMEM_LIMIT),
        name="mixer_outproj_ln",
    )(merged, x, gate, w_o, ln_g.reshape(1, D_MODEL), ln_b.reshape(1, D_MODEL))


def _gmlp_norm(pv, gng, gnb):
    v = _gelu_tanh(pv)
    outs = []
    for g in range(N_GMLP_GROUPS):
        sl = slice(g * LANES, (g + 1) * LANES)
        outs.append(_layer_norm(v[:, sl], gng[:, sl], gnb[:, sl]))
    return outs


def _half_lane_tiles(t):
    lo_lane = lax.broadcasted_iota(jnp.int32, (t.shape[0], LANES), 1) < HEAD_DIM
    lo, hi = [], []
    for c in range(KV_WIDTH // LANES):
        col = t[:, c * LANES:(c + 1) * LANES]
        swp = pltpu.roll(col, HEAD_DIM, 1)
        lo += [jnp.where(lo_lane, col, 0.0), jnp.where(lo_lane, swp, 0.0)]
        hi += [jnp.where(lo_lane, 0.0, swp), jnp.where(lo_lane, 0.0, col)]
    return jnp.concatenate(lo + hi, axis=-1).astype(BF16)


def _band_attention(q_s, kx_s, vx_s, attn_s, sinks_ref, nblk, first_tile):
    rows = 2 * WINDOW
    qi = lax.broadcasted_iota(jnp.int32, (rows, WINDOW), 0) & (WINDOW - 1)
    kj = lax.broadcasted_iota(jnp.int32, (rows, WINDOW), 1)
    own = kj <= qi
    first_mask = kj <= jnp.where(first_tile, qi, WINDOW)
    ones = jnp.ones((2 * WINDOW, LANES), BF16)
    units = [(wb, g, hf) for wb in range(nblk) for g in range(N_KV_HEADS) for hf in range(2)]

    def kv_cols(g, hf):
        return slice(hf * 4 * LANES + g * LANES, hf * 4 * LANES + (g + 1) * LANES)

    scores = []
    for wb, g, hf in units:
        r0 = wb * WINDOW
        q2 = jnp.concatenate([q_s[r0:r0 + WINDOW, (2 * g) * LANES:(2 * g + 1) * LANES],
                              q_s[r0:r0 + WINDOW, (2 * g + 1) * LANES:(2 * g + 2) * LANES]], axis=0)
        s = _dot_nt(q2, kx_s[r0:r0 + 2 * WINDOW, kv_cols(g, hf)])
        c = jnp.where(own, s[:, WINDOW:2 * WINDOW], s[:, 0:WINDOW])
        if wb == 0:
            c = jnp.where(first_mask, c, -jnp.inf)
        scores.append(c)

    probs = []
    for (wb, g, hf), c in zip(units, scores):
        sink = jnp.concatenate([jnp.full((WINDOW, LANES), sinks_ref[GQA_GROUP * g + hf], F32),
                                jnp.full((WINDOW, LANES), sinks_ref[GQA_GROUP * g + 2 + hf], F32)], axis=0)
        mx = jnp.maximum(jnp.broadcast_to(jnp.max(c, axis=-1, keepdims=True), c.shape), sink)
        p = jnp.exp(c - mx)
        pcat = jnp.concatenate([jnp.where(own, 0.0, p), jnp.where(own, p, 0.0)], axis=-1).astype(BF16)
        probs.append((pcat, jnp.exp(sink - mx)))

    outs = {}
    for (wb, g, hf), (pcat, esink) in zip(units, probs):
        r0 = wb * WINDOW
        v2 = jnp.concatenate([vx_s[r0:r0 + 2 * WINDOW, kv_cols(g, hf)], ones], axis=-1)
        o = _dot(pcat, v2)
        outs[wb, g, hf] = o[:, 0:LANES] / (o[:, LANES:2 * LANES] + esink)

    for wb in range(nblk):
        r0 = wb * WINDOW
        for g in range(N_KV_HEADS):
            both = outs[wb, g, 0] + outs[wb, g, 1]
            attn_s[r0:r0 + WINDOW, (2 * g) * LANES:(2 * g + 1) * LANES] = both[0:WINDOW]
            attn_s[r0:r0 + WINDOW, (2 * g + 1) * LANES:(2 * g + 2) * LANES] = both[WINDOW:2 * WINDOW]


def _pmix_kernel(tiles_per_seq, n_tiles, sinks_ref, x_ref, mod_ref, win_ref, wo_ref, cos_ref, slo_ref, shi_ref,
                 gng_ref, gnb_ref, ws_ref, bs_ref, og_ref, lngb_ref,
                 o_ref, k_ref, v_ref, p_s, q_s, kk_s, vv_s, kp_s, vp_s, attn_s, m_s, wm_s, y_s):
    n = pl.program_id(0)
    t = n % tiles_per_seq
    tm = x_ref.shape[0]
    nblk = tm // WINDOW

    @pl.when(n == 0)
    def _():
        y_s[...] = jnp.zeros(y_s.shape, F32)
        row = lax.broadcasted_iota(jnp.int32, (CHUNK, CHUNK), 0)
        col = lax.broadcasted_iota(jnp.int32, (CHUNK, CHUNK), 1)
        for g in range(N_GMLP_GROUPS):
            wm_s[g] = jnp.where(col <= row, ws_ref[g], 0.0).astype(BF16)

    @pl.when(t == 0)
    def _():
        kp_s[...] = jnp.zeros(kp_s.shape, BF16)
        vp_s[...] = jnp.zeros(vp_s.shape, BF16)

    @pl.when(n == n_tiles)
    def _():
        o_ref[...] = _layer_norm(y_s[...], lngb_ref[0:1], lngb_ref[1:2])

    pl.when(n < n_tiles)(functools.partial(
        _pmix_tile, t, pl.ds(n // tiles_per_seq, 1), sinks_ref, x_ref, mod_ref, win_ref, wo_ref, cos_ref, slo_ref,
        shi_ref, gng_ref, gnb_ref, bs_ref, og_ref, lngb_ref, o_ref, k_ref, v_ref, p_s, q_s, kk_s, vv_s, kp_s, vp_s,
        attn_s, m_s, wm_s, y_s))


def _pmix_tile(t, seq, sinks_ref, x_ref, mod_ref, win_ref, wo_ref, cos_ref, slo_ref, shi_ref, gng_ref, gnb_ref,
               bs_ref, og_ref, lngb_ref, o_ref, k_ref, v_ref, p_s, q_s, kk_s, vv_s, kp_s, vp_s, attn_s, m_s, wm_s, y_s):
    tm = x_ref.shape[0]
    nblk = tm // WINDOW

    o_ref[...] = _layer_norm(y_s[...], lngb_ref[0:1], lngb_ref[1:2])

    m_s[...] = (x_ref[...] * (1.0 + mod_ref[1, seq, :]) + mod_ref[0, seq, :]).astype(BF16)
    cos, slo, shi = cos_ref[...], slo_ref[...], shi_ref[...]
    for c in range(IN_WIDTH // PROJ_TILE):
        lo_col = c * PROJ_TILE
        h = _dot(m_s[...], win_ref[:, lo_col:lo_col + PROJ_TILE])
        if lo_col + PROJ_TILE <= Q_END:
            q_s[:, lo_col:lo_col + PROJ_TILE] = (_rope_lanes(h, cos, slo, shi) * (HEAD_DIM ** -0.5)).astype(BF16)
        elif lo_col == Q_END and PROJ_TILE == 2 * KV_WIDTH:
            k = _rope_lanes(h[:, 0:KV_WIDTH], cos, slo, shi)
            v = h[:, KV_WIDTH:2 * KV_WIDTH]
            k_ref[0] = k[tm - WINDOW:tm]
            v_ref[0] = v[tm - WINDOW:tm]
            kb, vb = _half_lane_tiles(k), _half_lane_tiles(v)
            kk_s[0:WINDOW, :] = kp_s[...]
            vv_s[0:WINDOW, :] = vp_s[...]
            kk_s[WINDOW:WINDOW + tm, :] = kb
            vv_s[WINDOW:WINDOW + tm, :] = vb
            kp_s[...] = kb[tm - WINDOW:tm]
            vp_s[...] = vb[tm - WINDOW:tm]
        else:
            p_s[:, lo_col - V_END:lo_col - V_END + PROJ_TILE] = h

    _band_attention(q_s, kk_s, vv_s, attn_s, sinks_ref, nblk, t == 0)

    vn = _gmlp_norm(p_s[:, GMLP_WIDTH:2 * GMLP_WIDTH], gng_ref[...], gnb_ref[...])
    gated = []
    for g in range(N_GMLP_GROUPS):
        u = _gelu_tanh(p_s[:, g * LANES:(g + 1) * LANES])
        vg = vn[g].astype(BF16)
        cols = jnp.concatenate([vg[wb * CHUNK:(wb + 1) * CHUNK] for wb in range(nblk)], axis=-1)
        mix = _dot(wm_s[g], cols)
        mix = jnp.concatenate([mix[:, wb * LANES:(wb + 1) * LANES] for wb in range(nblk)], axis=0)
        bias = jnp.concatenate([bs_ref[:, g:g + 1]] * nblk, axis=0)
        gated.append(u * (mix + bias))
    gm = jnp.concatenate(gated, axis=-1)
    og = og_ref[...]
    m_s[:, 0:ATTN_WIDTH] = _rms_norm(attn_s[...], og[:, 0:ATTN_WIDTH]).astype(BF16)
    m_s[:, ATTN_WIDTH:] = _rms_norm(gm, og[:, ATTN_WIDTH:]).astype(BF16)

    for c in range(D_MODEL // PROJ_TILE):
        sl = slice(c * PROJ_TILE, (c + 1) * PROJ_TILE)
        y_s[:, sl] = ALPHA * x_ref[:, sl] + mod_ref[2, seq, sl] * _dot(m_s[...], wo_ref[:, sl])


def _rope_tables(pos):
    half = ROPE_DIM // 2
    inv = ROPE_THETA ** (-(np.arange(half, dtype=np.float64) * 2.0) / ROPE_DIM)
    ang = np.asarray(pos, np.float64)[:, None] * inv[None, :]
    cos, sin = np.cos(ang), np.sin(ang)
    n = ang.shape[0]
    one = np.ones((n, HEAD_DIM - ROPE_DIM))
    zero = np.zeros((n, HEAD_DIM - ROPE_DIM))
    z8 = np.zeros((n, half))
    cos_t = np.concatenate([cos, cos, one], axis=-1)
    sin_lo = np.concatenate([-sin, z8, zero], axis=-1)
    sin_hi = np.concatenate([z8, sin, zero], axis=-1)
    rep = LANES // HEAD_DIM
    return tuple(jnp.asarray(np.tile(t, (1, rep)), F32) for t in (cos_t, sin_lo, sin_hi))


def _pmix(x, mod, sub, w_in, w_o, sinks, tables, gn_g, gn_b, w_s, b_s, out_g, ln_gb, *, batch, tm):
    nt = SEQ // tm
    n_tiles = batch * nt
    tile = lambda n: jnp.minimum(n, n_tiles - 1)
    cos_t, sin_lo, sin_hi = tables
    tab_spec = pl.BlockSpec((tm, LANES), lambda n: (tile(n) % nt, 0))
    full2 = lambda n: (0, 0)
    mod_spec = pl.BlockSpec((3, mod.shape[1], D_MODEL), lambda n: (sub, 0, 0))
    kv_spec = pl.BlockSpec((1, WINDOW, KV_WIDTH), lambda n: (tile(n) // nt, 0, 0))
    resident = dict(pipeline_mode=pl.Buffered(1))
    return pl.pallas_call(
        functools.partial(_pmix_kernel, nt, n_tiles),
        grid=(n_tiles + 1,),
        in_specs=[
            pl.BlockSpec(memory_space=pltpu.SMEM),
            pl.BlockSpec((tm, D_MODEL), lambda n: (tile(n), 0)),
            mod_spec,
            pl.BlockSpec((D_MODEL, IN_WIDTH), full2, **resident),
            pl.BlockSpec((D_MODEL, D_MODEL), full2, **resident),
            tab_spec, tab_spec, tab_spec,
            pl.BlockSpec((1, GMLP_WIDTH), full2),
            pl.BlockSpec((1, GMLP_WIDTH), full2),
            pl.BlockSpec((N_GMLP_GROUPS, CHUNK, CHUNK), lambda n: (0, 0, 0)),
            pl.BlockSpec((CHUNK, N_GMLP_GROUPS), full2),
            pl.BlockSpec((1, D_MODEL), full2),
            pl.BlockSpec((2, D_MODEL), full2),
        ],
        out_specs=[
            pl.BlockSpec((tm, D_MODEL), lambda n: (jnp.maximum(n - 1, 0), 0)),
            kv_spec, kv_spec,
        ],
        out_shape=[
            jax.ShapeDtypeStruct((batch * SEQ, D_MODEL), F32),
            jax.ShapeDtypeStruct((batch, WINDOW, KV_WIDTH), F32),
            jax.ShapeDtypeStruct((batch, WINDOW, KV_WIDTH), F32),
        ],
        scratch_shapes=[
            pltpu.VMEM((tm, 2 * GMLP_WIDTH), F32),
            pltpu.VMEM((tm, ATTN_WIDTH), BF16),
            pltpu.VMEM((WINDOW + tm, 4 * KV_WIDTH), BF16),
            pltpu.VMEM((WINDOW + tm, 4 * KV_WIDTH), BF16),
            pltpu.VMEM((WINDOW, 4 * KV_WIDTH), BF16),
            pltpu.VMEM((WINDOW, 4 * KV_WIDTH), BF16),
            pltpu.VMEM((tm, ATTN_WIDTH), F32),
            pltpu.VMEM((tm, D_MODEL), BF16),
            pltpu.VMEM((N_GMLP_GROUPS, CHUNK, CHUNK), BF16),
            pltpu.VMEM((tm, D_MODEL), F32),
        ],
        compiler_params=pltpu.CompilerParams(dimension_semantics=("arbitrary",), vmem_limit_bytes=VMEM_LIMIT),
        name="prompt_mixer",
    )(sinks, x, mod, w_in, w_o, cos_t, sin_lo, sin_hi, gn_g.reshape(1, GMLP_WIDTH),
      gn_b.reshape(1, GMLP_WIDTH), w_s, b_s.T, out_g.reshape(1, D_MODEL), ln_gb)


def _sattn_kernel(qe_ref, p_ref, ck_ref, cv_ref, sink_ref, cos_ref, slo_ref, shi_ref, o_ref, ko_ref, vo_ref):
    cos, slo, shi = cos_ref[...], slo_ref[...], shi_ref[...]
    sink = sink_ref[...]
    for r in range(qe_ref.shape[0]):
        qe = _rope_lanes(qe_ref[r], cos, slo, shi) * (HEAD_DIM ** -0.5)
        kn = _rope_lanes(p_ref[r, :, Q_END:K_END], cos, slo, shi)
        vn = p_ref[r, :, K_END:V_END]
        ck = ck_ref[r]
        cv = cv_ref[r]
        s_c = _dot_nt(qe.astype(BF16), ck.astype(BF16))
        key = lax.broadcasted_iota(jnp.int32, s_c.shape, 1)
        s_c = jnp.where(key >= 1, s_c, -jnp.inf)
        s_n = jnp.sum(qe * kn, axis=-1, keepdims=True)
        mx = jnp.maximum(jnp.maximum(jnp.max(s_c, axis=-1, keepdims=True), s_n), sink)
        p_c = jnp.exp(s_c - mx)
        p_n = jnp.exp(s_n - mx)
        denom = jnp.sum(p_c, axis=-1, keepdims=True) + p_n + jnp.exp(sink - mx)
        o = (_dot(p_c.astype(BF16), cv.astype(BF16)) + p_n * vn) / denom
        head = lax.broadcasted_iota(jnp.int32, o.shape, 0)
        lane = lax.broadcasted_iota(jnp.int32, o.shape, 1)
        o_ref[r] = jnp.where(head // GQA_GROUP == lane // HEAD_DIM, o, 0.0)
        row = lax.broadcasted_iota(jnp.int32, ck.shape, 0)
        newest = row == WINDOW - 1
        ko_ref[r] = jnp.where(newest, kn, pltpu.roll(ck, WINDOW - 1, 0))
        vo_ref[r] = jnp.where(newest, vn, pltpu.roll(cv, WINDOW - 1, 0))


def _sattn(qe, p3, ck, cv, sinks, tables):
    b = qe.shape[0]
    rows = SAMPLE_ROWS_PER_STEP
    cos_t, sin_lo, sin_hi = tables
    tab_spec = pl.BlockSpec((1, LANES), lambda i: (0, 0))
    blk3 = lambda i: (i, 0, 0)
    return pl.pallas_call(
        _sattn_kernel,
        grid=(b // rows,),
        in_specs=[
            pl.BlockSpec((rows, N_HEADS, KV_WIDTH), blk3),
            pl.BlockSpec((rows, 1, IN_WIDTH), blk3),
            pl.BlockSpec((rows, WINDOW, KV_WIDTH), blk3),
            pl.BlockSpec((rows, WINDOW, KV_WIDTH), blk3),
            pl.BlockSpec((N_HEADS, 1), lambda i: (0, 0)),
            tab_spec, tab_spec, tab_spec,
        ],
        out_specs=[
            pl.BlockSpec((rows, N_HEADS, KV_WIDTH), blk3),
            pl.BlockSpec((rows, WINDOW, KV_WIDTH), blk3),
            pl.BlockSpec((rows, WINDOW, KV_WIDTH), blk3),
        ],
        out_shape=[
            jax.ShapeDtypeStruct((b, N_HEADS, KV_WIDTH), F32),
            jax.ShapeDtypeStruct((b, WINDOW, KV_WIDTH), F32),
            jax.ShapeDtypeStruct((b, WINDOW, KV_WIDTH), F32),
        ],
        compiler_params=pltpu.CompilerParams(dimension_semantics=("arbitrary",)),
        name="sample_attn",
    )(qe, p3, ck, cv, sinks.reshape(N_HEADS, 1), cos_t, sin_lo, sin_hi)


def _smerge_kernel(attn_ref, p_ref, gng_ref, gnb_ref, w0_ref, b0_ref, og_ref, m_ref, vn_ref):
    og = og_ref[...]
    vn = jnp.concatenate(_gmlp_norm(p_ref[:, U_END:IN_WIDTH], gng_ref[...], gnb_ref[...]), axis=-1)
    vn_ref[...] = vn
    u = _gelu_tanh(p_ref[:, V_END:U_END])
    gm = u * (w0_ref[...] * vn + b0_ref[...])
    m_ref[:, 0:ATTN_WIDTH] = _rms_norm(attn_ref[...], og[:, 0:ATTN_WIDTH]).astype(BF16)
    m_ref[:, ATTN_WIDTH:] = _rms_norm(gm, og[:, ATTN_WIDTH:]).astype(BF16)


def _smerge(attn, p, gn_g, gn_b, w_s, b_s, out_g):
    b = attn.shape[0]
    w0 = jnp.repeat(w_s[:, 0, 0], LANES).reshape(1, GMLP_WIDTH)
    b0 = jnp.repeat(b_s[:, 0], LANES).reshape(1, GMLP_WIDTH)
    return pl.pallas_call(
        _smerge_kernel,
        out_shape=[
            jax.ShapeDtypeStruct((b, D_MODEL), BF16),
            jax.ShapeDtypeStruct((b, GMLP_WIDTH), F32),
        ],
        name="sample_gmlp_merge",
    )(attn, p, gn_g.reshape(1, GMLP_WIDTH), gn_b.reshape(1, GMLP_WIDTH), w0, b0, out_g.reshape(1, D_MODEL))


def kernel(x_prompt, x_sample, cache_k_win, cache_v_win, c_prompt, c_sample, w_ada, b_ada, ln_g, ln_b,
           w_ffn_up, w_ffn_down, w_in, attn_sinks, gmlp_norm_g, gmlp_norm_b, w_spatial, b_spatial,
           out_norm_g, w_o):
    batch, seq, _ = x_prompt.shape
    dec_batch = x_sample.shape[0]
    buf = cache_k_win.shape[2]
    assert seq == SEQ and buf == WINDOW and x_sample.shape[1] == 1 and w_ada.shape[0] == DEPTH == 1

    xp = x_prompt.reshape(batch * seq, D_MODEL)
    xs = x_sample.reshape(dec_batch, D_MODEL)

    n_c = batch + dec_batch
    n_c_pad = ((n_c + 7) // 8) * 8
    c_all = jnp.concatenate([c_prompt, c_sample, jnp.zeros((n_c_pad - n_c, D_MODEL), F32)], axis=0)
    mod = _ada(c_all, w_ada[0], b_ada[0])

    def smod(i, j):
        return mod[3 * i + j, batch:n_c].reshape(1, dec_batch, D_MODEL)

    ln_gb = jnp.stack([ln_g[0], ln_b[0]], axis=1)
    ffn = functools.partial(_ffn, tm=FFN_ROWS, rows_per_mod=seq, extra_row0=batch)

    w_up0 = w_ffn_up[0, 0].astype(BF16)
    w_down0 = w_ffn_down[0, 0].astype(BF16)
    later = ((w_ffn_up, (0, 1), 16), (w_ffn_down, (0, 1), 32), (w_in, (0,), 16), (w_o, (0,), 16))
    xp, xs, w_up1, w_down1, w_in_b, w_o_b = ffn(xp, mod, 0, w_up0, w_down0, ln_gb[0], extra=xs, side=later)

    xp, k_p, v_p = _pmix(xp, mod, 1, w_in_b, w_o_b, attn_sinks[0], _rope_tables(np.arange(seq)),
                         gmlp_norm_g[0], gmlp_norm_b[0], w_spatial[0], b_spatial[0], out_norm_g[0], ln_gb[1],
                         batch=batch, tm=MIX_TILE)

    ps = _inproj(xs, smod(1, 0), smod(1, 1), w_in_b, tm=dec_batch, rows_per_mod=dec_batch)
    eye = jnp.eye(N_KV_HEADS, dtype=F32)
    q5 = ps[:, :Q_END].reshape(dec_batch, N_KV_HEADS, GQA_GROUP, 1, HEAD_DIM)
    qe = (q5 * eye[None, :, None, :, None]).reshape(dec_batch, N_HEADS, KV_WIDTH)
    ck = cache_k_win[0].reshape(dec_batch, buf, KV_WIDTH)
    cv = cache_v_win[0].reshape(dec_batch, buf, KV_WIDTH)
    oe, k_s, v_s = _sattn(qe, ps.reshape(dec_batch, 1, IN_WIDTH), ck, cv, attn_sinks[0],
                          _rope_tables(np.full((1,), PAST_LEN)))
    attn_s = oe.reshape(dec_batch, N_KV_HEADS, GQA_GROUP, N_KV_HEADS, HEAD_DIM).sum(axis=3).reshape(dec_batch, ATTN_WIDTH)
    merged_s, vn_s = _smerge(attn_s, ps, gmlp_norm_g[0], gmlp_norm_b[0], w_spatial[0], b_spatial[0], out_norm_g[0])
    xs = _outproj(merged_s, xs, smod(1, 2), w_o_b, ln_g[0, 1], ln_b[0, 1], tm=dec_batch, rows_per_mod=dec_batch)

    xp, xs = ffn(xp, mod, 2, w_up1, w_down1, ln_gb[2], extra=xs)

    return (
        xp.reshape(batch, seq, D_MODEL),
        xs.reshape(dec_batch, 1, D_MODEL),
        k_p.reshape(1, batch, WINDOW, N_KV_HEADS, HEAD_DIM),
        v_p.reshape(1, batch, WINDOW, N_KV_HEADS, HEAD_DIM),
        k_s.reshape(1, dec_batch, buf, N_KV_HEADS, HEAD_DIM),
        v_s.reshape(1, dec_batch, buf, N_KV_HEADS, HEAD_DIM),
        vn_s.reshape(1, dec_batch, 1, GMLP_WIDTH),
    )
```

```python
import functools

import jax
import jax.numpy as jnp
import numpy as np
from jax import lax
from jax.experimental import pallas as pl
from jax.experimental.pallas import tpu as pltpu

D_MODEL = 2048
SEQ = 2048
PAST_LEN = 16384
ATTN_WIDTH = 1024
GMLP_WIDTH = 1024
HEAD_DIM = 64
N_HEADS = 16
N_KV_HEADS = 4
GQA_GROUP = 4
KV_WIDTH = 256
WINDOW = 128
ROPE_THETA = 500000.0
ROPE_DIM = 16
CHUNK = 128
N_GMLP_GROUPS = 8
D_FF = 5504
N_SUB = 3
DEPTH = 1
ALPHA = (2.0 * DEPTH) ** 0.25
LN_EPS = 1e-5
Q_END = ATTN_WIDTH
K_END = Q_END + KV_WIDTH
V_END = K_END + KV_WIDTH
U_END = V_END + GMLP_WIDTH
IN_WIDTH = U_END + GMLP_WIDTH

LANES = 128
MXU_COLS = 256
FF_TILE = 512
N_FULL_CHUNKS = D_FF // FF_TILE
FF_TAIL = D_FF - N_FULL_CHUNKS * FF_TILE
N_FF_STEPS = N_FULL_CHUNKS + 1
OUT_TILE = 512
N_OUT_STEPS = D_MODEL // OUT_TILE
FFN_ROWS = 512
LN_PIECE_ROWS = 64
MIX_TILE = 256
PROJ_TILE = 512
SAMPLE_ROWS_PER_STEP = 8
VMEM_LIMIT = 60 * 1024 * 1024

BF16 = jnp.bfloat16
F32 = jnp.float32


def _dot(a, b):
    return jnp.dot(a, b, preferred_element_type=F32)


def _dot_nt(a, b):
    return lax.dot_general(a, b, (((1,), (1,)), ((), ())), preferred_element_type=F32)


def _layer_norm(y, g, b):
    mu = jnp.mean(y, axis=-1, keepdims=True)
    d = y - mu
    var = jnp.mean(d * d, axis=-1, keepdims=True)
    return d * lax.rsqrt(var + LN_EPS) * g + b


def _rms_norm(y, g):
    return y * lax.rsqrt(jnp.mean(y * y, axis=-1, keepdims=True) + LN_EPS) * g


def _gelu_tanh(x):
    c = np.float32(np.sqrt(2.0 / np.pi))
    return 0.5 * x * (1.0 + jnp.tanh(c * (x + 0.044715 * (x * x * x))))


def _rope_lanes(t, cos, sin_lo, sin_hi):
    pieces = []
    for c in range(t.shape[-1] // LANES):
        x = t[:, c * LANES:(c + 1) * LANES]
        pieces.append(x * cos + pltpu.roll(x, 8, 1) * sin_hi + pltpu.roll(x, LANES - 8, 1) * sin_lo)
    return pieces[0] if len(pieces) == 1 else jnp.concatenate(pieces, axis=-1)


def _ada_kernel(c_ref, w_ref, b_ref, o_ref):
    c = c_ref[...]
    h = (c * jax.nn.sigmoid(c)).astype(BF16)
    o_ref[0] = _dot(h, w_ref[...].astype(BF16)) + b_ref[...]


def _ada(c, w_ada, b_ada):
    rows = c.shape[0]
    n = w_ada.shape[1]
    tn = D_MODEL
    per_plane = D_MODEL // tn
    return pl.pallas_call(
        _ada_kernel,
        grid=(n // tn,),
        in_specs=[
            pl.BlockSpec((rows, D_MODEL), lambda j: (0, 0)),
            pl.BlockSpec((D_MODEL, tn), lambda j: (0, j)),
            pl.BlockSpec((1, tn), lambda j: (0, j)),
        ],
        out_specs=pl.BlockSpec((1, rows, tn), lambda j: (j // per_plane, 0, j % per_plane)),
        out_shape=jax.ShapeDtypeStruct((n // D_MODEL, rows, D_MODEL), F32),
        compiler_params=pltpu.CompilerParams(dimension_semantics=("arbitrary",), vmem_limit_bytes=VMEM_LIMIT),
        name="ada_mod",
    )(c, w_ada, b_ada.reshape(1, n))


def _ff_offset(step):
    return jnp.minimum(step, N_FULL_CHUNKS - 1) * FF_TILE


def _ffn_kernel(n_side, n_extra, tiles_per_mod, extra_row0, n_tiles, *refs):
    x_ref, mod_ref, wv_ref, wg_ref, wvt_ref, wgt_ref, wd_ref, gb_ref = refs[:8]
    n_in = 8
    if n_extra:
        xe_ref = refs[n_in]
        n_in += 1
        erows = slice(extra_row0, extra_row0 + n_extra)
    side_in = refs[n_in:n_in + n_side]
    n_in += n_side
    o_ref = refs[n_in]
    n_out = 1
    if n_extra:
        oe_ref = refs[n_in + 1]
        n_out = 2
    side_out = refs[n_in + n_out:n_in + n_out + n_side]
    xm_ref, a_ref, y_ref = refs[n_in + n_out + n_side:]
    i = pl.program_id(0)
    s = pl.program_id(1)
    tm = x_ref.shape[0]
    last = pl.num_programs(1) - 1

    def cast_side():
        for src, dst in zip(side_in, side_out):
            dst[...] = src[...].astype(BF16)

    def norm_piece():
        r0 = pl.multiple_of(jnp.minimum(s, tm // LN_PIECE_ROWS - 1) * LN_PIECE_ROWS, LN_PIECE_ROWS)
        piece = pl.ds(r0, LN_PIECE_ROWS)
        y = jnp.concatenate([y_ref[n, piece, :] for n in range(N_OUT_STEPS)], axis=-1)
        o_ref[piece, :] = _layer_norm(y, gb_ref[0:1], gb_ref[1:2])

    def swiglu(hv, hg):
        return (hg * jax.nn.sigmoid(hg) * hv).astype(BF16)

    def up_chunk(rows, norm_previous):
        cast_side()
        if norm_previous:
            norm_piece()
        xm = xm_ref[0:rows]
        for c in range(FF_TILE // MXU_COLS):
            sl = slice(c * MXU_COLS, (c + 1) * MXU_COLS)
            a_ref[s, 0:rows, sl] = swiglu(_dot(xm, wv_ref[:, sl]), _dot(xm, wg_ref[:, sl]))

    def up_tail(rows, norm_previous):
        cast_side()
        if norm_previous:
            norm_piece()
        xm = xm_ref[0:rows]
        n_wide = FF_TAIL // MXU_COLS
        for c in range(n_wide):
            sl = slice(c * MXU_COLS, (c + 1) * MXU_COLS)
            a_ref[s, 0:rows, sl] = swiglu(_dot(xm, wvt_ref[:, sl]), _dot(xm, wgt_ref[:, sl]))
        rest = slice(n_wide * MXU_COLS, FF_TAIL)
        width = FF_TAIL - n_wide * MXU_COLS
        h = _dot(xm, jnp.concatenate([wvt_ref[:, rest], wgt_ref[:, rest]], axis=-1))
        a_ref[s, 0:rows, rest] = swiglu(h[:, 0:width], h[:, width:2 * width])

    def down_slab(rows):
        cast_side()
        acc = None
        for f in range(N_FF_STEPS):
            width = FF_TILE if f < N_FULL_CHUNKS else FF_TAIL
            part = _dot(a_ref[f, 0:rows, 0:width], wd_ref[f * FF_TILE:f * FF_TILE + width, :])
            acc = part if acc is None else acc + part
        y_ref[s - N_FF_STEPS, 0:rows] = acc

    first = i == 0
    later = i > 0
    up_phase = s < N_FULL_CHUNKS
    tail_phase = s == N_FULL_CHUNKS
    down_phase = s >= N_FF_STEPS
    seq = pl.ds(i // tiles_per_mod, 1)

    @pl.when(s == 0)
    def _():
        xm_ref[0:tm] = (x_ref[...] * (1.0 + mod_ref[1, seq, :]) + mod_ref[0, seq, :]).astype(BF16)

    if n_extra:
        @pl.when(jnp.logical_and(s == 0, first))
        def _():
            xm_ref[tm:tm + n_extra] = (xe_ref[...] * (1.0 + mod_ref[1, erows, :]) + mod_ref[0, erows, :]).astype(BF16)

    pl.when(jnp.logical_and(up_phase, first))(functools.partial(up_chunk, tm + n_extra, False))
    pl.when(jnp.logical_and(up_phase, later))(functools.partial(up_chunk, tm, True))
    pl.when(jnp.logical_and(tail_phase, first))(functools.partial(up_tail, tm + n_extra, False))
    pl.when(jnp.logical_and(tail_phase, later))(functools.partial(up_tail, tm, True))
    pl.when(jnp.logical_and(down_phase, first))(functools.partial(down_slab, tm + n_extra))
    pl.when(jnp.logical_and(down_phase, later))(functools.partial(down_slab, tm))

    @pl.when(s == last)
    def _():
        for n in range(N_OUT_STEPS):
            sl = slice(n * OUT_TILE, (n + 1) * OUT_TILE)
            y_ref[n, 0:tm] = ALPHA * x_ref[:, sl] + 0.5 * mod_ref[2, seq, sl] * y_ref[n, 0:tm]

    @pl.when(jnp.logical_and(s == last, i == n_tiles - 1))
    def _():
        y = jnp.concatenate([y_ref[n, 0:tm] for n in range(N_OUT_STEPS)], axis=-1)
        o_ref[...] = _layer_norm(y, gb_ref[0:1], gb_ref[1:2])

    if n_extra:
        @pl.when(jnp.logical_and(s == last, first))
        def _():
            mixed = jnp.concatenate([y_ref[n, tm:tm + n_extra] for n in range(N_OUT_STEPS)], axis=-1)
            oe_ref[...] = _layer_norm(ALPHA * xe_ref[...] + 0.5 * mod_ref[2, erows, :] * mixed, gb_ref[0:1], gb_ref[1:2])


def _ffn(x, mod, sub, w_up, w_down, ln_gb, *, tm, rows_per_mod, extra=None, extra_row0=0, side=()):
    m = x.shape[0]
    n_steps = N_FF_STEPS + N_OUT_STEPS
    tiles_per_mod = rows_per_mod // tm
    n_extra = 0 if extra is None else extra.shape[0]
    const2 = lambda i, s: (0, 0)
    extra_spec = pl.BlockSpec((n_extra, D_MODEL), const2)
    side_in_specs, side_out_specs, side_shapes = [], [], []
    for arr, lead, rows in side:
        n_rows, n_cols = arr.shape[-2:]
        n_blocks = n_rows // rows
        assert n_blocks * rows == n_rows and n_blocks <= (m // tm) * n_steps
        blk = lambda i, s, n_blocks=n_blocks: jnp.minimum(i * n_steps + s, n_blocks - 1)
        side_in_specs.append(pl.BlockSpec((None,) * len(lead) + (rows, n_cols),
                                          lambda i, s, lead=lead, blk=blk: (*lead, blk(i, s), 0)))
        side_out_specs.append(pl.BlockSpec((rows, n_cols), lambda i, s, blk=blk: (blk(i, s), 0)))
        side_shapes.append(jax.ShapeDtypeStruct((n_rows, n_cols), BF16))
    n_tiles = m // tm
    assert tm % LN_PIECE_ROWS == 0 and tm // LN_PIECE_ROWS <= N_FF_STEPS
    wd_block = lambda i, s: (0, jnp.where(s < N_FF_STEPS, N_OUT_STEPS - 1, s - N_FF_STEPS))
    out_block = lambda i, s: (jnp.where(s < N_FF_STEPS, jnp.maximum(i - 1, 0), i), 0)
    outs = pl.pallas_call(
        functools.partial(_ffn_kernel, len(side), n_extra, tiles_per_mod, extra_row0, n_tiles),
        grid=(n_tiles, n_steps),
        in_specs=[
            pl.BlockSpec((tm, D_MODEL), lambda i, s: (i, 0)),
            pl.BlockSpec((3, mod.shape[1], D_MODEL), lambda i, s: (sub, 0, 0)),
            pl.BlockSpec((pl.Element(D_MODEL), pl.Element(FF_TILE)),
                         lambda i, s: (0, pl.multiple_of(_ff_offset(s), LANES))),
            pl.BlockSpec((pl.Element(D_MODEL), pl.Element(FF_TILE)),
                         lambda i, s: (0, pl.multiple_of(D_FF + _ff_offset(s), LANES))),
            pl.BlockSpec((pl.Element(D_MODEL), pl.Element(FF_TAIL)), lambda i, s: (0, N_FULL_CHUNKS * FF_TILE),
                         pipeline_mode=pl.Buffered(1)),
            pl.BlockSpec((pl.Element(D_MODEL), pl.Element(FF_TAIL)), lambda i, s: (0, D_FF + N_FULL_CHUNKS * FF_TILE),
                         pipeline_mode=pl.Buffered(1)),
            pl.BlockSpec((D_FF, OUT_TILE), wd_block),
            pl.BlockSpec((2, D_MODEL), const2),
            *([extra_spec] if n_extra else []),
            *side_in_specs,
        ],
        out_specs=[pl.BlockSpec((tm, D_MODEL), out_block), *([extra_spec] if n_extra else []),
                   *side_out_specs],
        out_shape=[jax.ShapeDtypeStruct((m, D_MODEL), F32),
                   *([jax.ShapeDtypeStruct((n_extra, D_MODEL), F32)] if n_extra else []), *side_shapes],
        scratch_shapes=[
            pltpu.VMEM((tm + n_extra, D_MODEL), BF16),
            pltpu.VMEM((N_FF_STEPS, tm + n_extra, FF_TILE), BF16),
            pltpu.VMEM((N_OUT_STEPS, tm + n_extra, OUT_TILE), F32),
        ],
        compiler_params=pltpu.CompilerParams(dimension_semantics=("arbitrary", "arbitrary"), vmem_limit_bytes=VMEM_LIMIT),
        name="swiglu_ln",
    )(x, mod, w_up, w_up, w_up, w_up, w_down, ln_gb, *([extra] if n_extra else []), *[arr for arr, _, _ in side])
    return outs


def _inproj_kernel(x_ref, shift_ref, scale_ref, w_ref, o_ref, xm_ref):
    @pl.when(pl.program_id(1) == 0)
    def _():
        xm_ref[...] = (x_ref[...] * (1.0 + scale_ref[0]) + shift_ref[0]).astype(BF16)

    o_ref[...] = _dot(xm_ref[...], w_ref[...])


def _inproj(x, shift, scale, w_in, *, tm, rows_per_mod):
    m = x.shape[0]
    r = shift.shape[1]
    tn = 512
    tiles_per_mod = rows_per_mod // tm
    mod_spec = pl.BlockSpec((1, r, D_MODEL), lambda i, j: (i // tiles_per_mod, 0, 0))
    return pl.pallas_call(
        _inproj_kernel,
        grid=(m // tm, IN_WIDTH // tn),
        in_specs=[
            pl.BlockSpec((tm, D_MODEL), lambda i, j: (i, 0)),
            mod_spec, mod_spec,
            pl.BlockSpec((D_MODEL, tn), lambda i, j: (0, j)),
        ],
        out_specs=pl.BlockSpec((tm, tn), lambda i, j: (i, j)),
        out_shape=jax.ShapeDtypeStruct((m, IN_WIDTH), F32),
        scratch_shapes=[pltpu.VMEM((tm, D_MODEL), BF16)],
        compiler_params=pltpu.CompilerParams(dimension_semantics=("arbitrary", "arbitrary"), vmem_limit_bytes=VMEM_LIMIT),
        name="mixer_inproj",
    )(x, shift, scale, w_in)


def _outproj_kernel(m_ref, x_ref, gate_ref, w_ref, g_ref, b_ref, o_ref):
    mixed = _dot(m_ref[...], w_ref[...])
    y = ALPHA * x_ref[...] + gate_ref[0] * mixed
    o_ref[...] = _layer_norm(y, g_ref[...], b_ref[...])


def _outproj(merged, x, gate, w_o, ln_g, ln_b, *, tm, rows_per_mod):
    m = x.shape[0]
    r = gate.shape[1]
    tiles_per_mod = rows_per_mod // tm
    return pl.pallas_call(
        _outproj_kernel,
        grid=(m // tm,),
        in_specs=[
            pl.BlockSpec((tm, D_MODEL), lambda i: (i, 0)),
            pl.BlockSpec((tm, D_MODEL), lambda i: (i, 0)),
            pl.BlockSpec((1, r, D_MODEL), lambda i: (i // tiles_per_mod, 0, 0)),
            pl.BlockSpec((D_MODEL, D_MODEL), lambda i: (0, 0)),
            pl.BlockSpec((1, D_MODEL), lambda i: (0, 0)),
            pl.BlockSpec((1, D_MODEL), lambda i: (0, 0)),
        ],
        out_specs=pl.BlockSpec((tm, D_MODEL), lambda i: (i, 0)),
        out_shape=jax.ShapeDtypeStruct((m, D_MODEL), F32),
        compiler_params=pltpu.CompilerParams(dimension_semantics=("arbitrary",), vmem_limit_bytes=VMEM_LIMIT),
        name="mixer_outproj_ln",
    )(merged, x, gate, w_o, ln_g.reshape(1, D_MODEL), ln_b.reshape(1, D_MODEL))


def _gmlp_norm(pv, gng, gnb):
    v = _gelu_tanh(pv)
    outs = []
    for g in range(N_GMLP_GROUPS):
        sl = slice(g * LANES, (g + 1) * LANES)
        outs.append(_layer_norm(v[:, sl], gng[:, sl], gnb[:, sl]))
    return outs


def _half_lane_tiles(t):
    lo_lane = lax.broadcasted_iota(jnp.int32, (t.shape[0], LANES), 1) < HEAD_DIM
    lo, hi = [], []
    for c in range(KV_WIDTH // LANES):
        col = t[:, c * LANES:(c + 1) * LANES]
        swp = pltpu.roll(col, HEAD_DIM, 1)
        lo += [jnp.where(lo_lane, col, 0.0), jnp.where(lo_lane, swp, 0.0)]
        hi += [jnp.where(lo_lane, 0.0, swp), jnp.where(lo_lane, 0.0, col)]
    return jnp.concatenate(lo + hi, axis=-1).astype(BF16)


def _band_attention(q_s, kx_s, vx_s, attn_s, sinks_ref, nblk, first_tile):
    rows = 2 * WINDOW
    qi = lax.broadcasted_iota(jnp.int32, (rows, WINDOW), 0) & (WINDOW - 1)
    kj = lax.broadcasted_iota(jnp.int32, (rows, WINDOW), 1)
    own = kj <= qi
    first_mask = kj <= jnp.where(first_tile, qi, WINDOW)
    ones = jnp.ones((2 * WINDOW, LANES), BF16)
    units = [(wb, g, hf) for wb in range(nblk) for g in range(N_KV_HEADS) for hf in range(2)]

    def kv_cols(g, hf):
        return slice(hf * 4 * LANES + g * LANES, hf * 4 * LANES + (g + 1) * LANES)

    scores = []
    for wb, g, hf in units:
        r0 = wb * WINDOW
        q2 = jnp.concatenate([q_s[r0:r0 + WINDOW, (2 * g) * LANES:(2 * g + 1) * LANES],
                              q_s[r0:r0 + WINDOW, (2 * g + 1) * LANES:(2 * g + 2) * LANES]], axis=0)
        s = _dot_nt(q2, kx_s[r0:r0 + 2 * WINDOW, kv_cols(g, hf)])
        c = jnp.where(own, s[:, WINDOW:2 * WINDOW], s[:, 0:WINDOW])
        if wb == 0:
            c = jnp.where(first_mask, c, -jnp.inf)
        scores.append(c)

    probs = []
    for (wb, g, hf), c in zip(units, scores):
        sink = jnp.concatenate([jnp.full((WINDOW, LANES), sinks_ref[GQA_GROUP * g + hf], F32),
                                jnp.full((WINDOW, LANES), sinks_ref[GQA_GROUP * g + 2 + hf], F32)], axis=0)
        mx = jnp.maximum(jnp.broadcast_to(jnp.max(c, axis=-1, keepdims=True), c.shape), sink)
        p = jnp.exp(c - mx)
        pcat = jnp.concatenate([jnp.where(own, 0.0, p), jnp.where(own, p, 0.0)], axis=-1).astype(BF16)
        probs.append((pcat, jnp.exp(sink - mx)))

    outs = {}
    for (wb, g, hf), (pcat, esink) in zip(units, probs):
        r0 = wb * WINDOW
        v2 = jnp.concatenate([vx_s[r0:r0 + 2 * WINDOW, kv_cols(g, hf)], ones], axis=-1)
        o = _dot(pcat, v2)
        outs[wb, g, hf] = o[:, 0:LANES] / (o[:, LANES:2 * LANES] + esink)

    for wb in range(nblk):
        r0 = wb * WINDOW
        for g in range(N_KV_HEADS):
            both = outs[wb, g, 0] + outs[wb, g, 1]
            attn_s[r0:r0 + WINDOW, (2 * g) * LANES:(2 * g + 1) * LANES] = both[0:WINDOW]
            attn_s[r0:r0 + WINDOW, (2 * g + 1) * LANES:(2 * g + 2) * LANES] = both[WINDOW:2 * WINDOW]


def _pmix_kernel(tiles_per_seq, n_tiles, sinks_ref, x_ref, mod_ref, win_ref, wo_ref, cos_ref, slo_ref, shi_ref,
                 gng_ref, gnb_ref, ws_ref, bs_ref, og_ref, lngb_ref,
                 o_ref, k_ref, v_ref, p_s, q_s, kk_s, vv_s, kp_s, vp_s, attn_s, m_s, wm_s, y_s):
    n = pl.program_id(0)
    t = n % tiles_per_seq
    tm = x_ref.shape[0]
    nblk = tm // WINDOW

    @pl.when(n == 0)
    def _():
        y_s[...] = jnp.zeros(y_s.shape, F32)
        row = lax.broadcasted_iota(jnp.int32, (CHUNK, CHUNK), 0)
        col = lax.broadcasted_iota(jnp.int32, (CHUNK, CHUNK), 1)
        for g in range(N_GMLP_GROUPS):
            wm_s[g] = jnp.where(col <= row, ws_ref[g], 0.0).astype(BF16)

    @pl.when(t == 0)
    def _():
        kp_s[...] = jnp.zeros(kp_s.shape, BF16)
        vp_s[...] = jnp.zeros(vp_s.shape, BF16)

    @pl.when(n == n_tiles)
    def _():
        o_ref[...] = _layer_norm(y_s[...], lngb_ref[0:1], lngb_ref[1:2])

    pl.when(n < n_tiles)(functools.partial(
        _pmix_tile, t, pl.ds(n // tiles_per_seq, 1), sinks_ref, x_ref, mod_ref, win_ref, wo_ref, cos_ref, slo_ref,
        shi_ref, gng_ref, gnb_ref, bs_ref, og_ref, lngb_ref, o_ref, k_ref, v_ref, p_s, q_s, kk_s, vv_s, kp_s, vp_s,
        attn_s, m_s, wm_s, y_s))


def _pmix_tile(t, seq, sinks_ref, x_ref, mod_ref, win_ref, wo_ref, cos_ref, slo_ref, shi_ref, gng_ref, gnb_ref,
               bs_ref, og_ref, lngb_ref, o_ref, k_ref, v_ref, p_s, q_s, kk_s, vv_s, kp_s, vp_s, attn_s, m_s, wm_s, y_s):
    tm = x_ref.shape[0]
    nblk = tm // WINDOW

    o_ref[...] = _layer_norm(y_s[...], lngb_ref[0:1], lngb_ref[1:2])

    m_s[...] = (x_ref[...] * (1.0 + mod_ref[1, seq, :]) + mod_ref[0, seq, :]).astype(BF16)
    cos, slo, shi = cos_ref[...], slo_ref[...], shi_ref[...]
    for c in range(IN_WIDTH // PROJ_TILE):
        lo_col = c * PROJ_TILE
        h = _dot(m_s[...], win_ref[:, lo_col:lo_col + PROJ_TILE])
        if lo_col + PROJ_TILE <= Q_END:
            q_s[:, lo_col:lo_col + PROJ_TILE] = (_rope_lanes(h, cos, slo, shi) * (HEAD_DIM ** -0.5)).astype(BF16)
        elif lo_col == Q_END and PROJ_TILE == 2 * KV_WIDTH:
            k = _rope_lanes(h[:, 0:KV_WIDTH], cos, slo, shi)
            v = h[:, KV_WIDTH:2 * KV_WIDTH]
            k_ref[0] = k[tm - WINDOW:tm]
            v_ref[0] = v[tm - WINDOW:tm]
            kb, vb = _half_lane_tiles(k), _half_lane_tiles(v)
            kk_s[0:WINDOW, :] = kp_s[...]
            vv_s[0:WINDOW, :] = vp_s[...]
            kk_s[WINDOW:WINDOW + tm, :] = kb
            vv_s[WINDOW:WINDOW + tm, :] = vb
            kp_s[...] = kb[tm - WINDOW:tm]
            vp_s[...] = vb[tm - WINDOW:tm]
        else:
            p_s[:, lo_col - V_END:lo_col - V_END + PROJ_TILE] = h

    _band_attention(q_s, kk_s, vv_s, attn_s, sinks_ref, nblk, t == 0)

    vn = _gmlp_norm(p_s[:, GMLP_WIDTH:2 * GMLP_WIDTH], gng_ref[...], gnb_ref[...])
    gated = []
    for g in range(N_GMLP_GROUPS):
        u = _gelu_tanh(p_s[:, g * LANES:(g + 1) * LANES])
        vg = vn[g].astype(BF16)
        cols = jnp.concatenate([vg[wb * CHUNK:(wb + 1) * CHUNK] for wb in range(nblk)], axis=-1)
        mix = _dot(wm_s[g], cols)
        mix = jnp.concatenate([mix[:, wb * LANES:(wb + 1) * LANES] for wb in range(nblk)], axis=0)
        bias = jnp.concatenate([bs_ref[:, g:g + 1]] * nblk, axis=0)
        gated.append(u * (mix + bias))
    gm = jnp.concatenate(gated, axis=-1)
    og = og_ref[...]
    m_s[:, 0:ATTN_WIDTH] = _rms_norm(attn_s[...], og[:, 0:ATTN_WIDTH]).astype(BF16)
    m_s[:, ATTN_WIDTH:] = _rms_norm(gm, og[:, ATTN_WIDTH:]).astype(BF16)

    for c in range(D_MODEL // PROJ_TILE):
        sl = slice(c * PROJ_TILE, (c + 1) * PROJ_TILE)
        y_s[:, sl] = ALPHA * x_ref[:, sl] + mod_ref[2, seq, sl] * _dot(m_s[...], wo_ref[:, sl])


def _rope_tables(pos):
    half = ROPE_DIM // 2
    inv = ROPE_THETA ** (-(np.arange(half, dtype=np.float64) * 2.0) / ROPE_DIM)
    ang = np.asarray(pos, np.float64)[:, None] * inv[None, :]
    cos, sin = np.cos(ang), np.sin(ang)
    n = ang.shape[0]
    one = np.ones((n, HEAD_DIM - ROPE_DIM))
    zero = np.zeros((n, HEAD_DIM - ROPE_DIM))
    z8 = np.zeros((n, half))
    cos_t = np.concatenate([cos, cos, one], axis=-1)
    sin_lo = np.concatenate([-sin, z8, zero], axis=-1)
    sin_hi = np.concatenate([z8, sin, zero], axis=-1)
    rep = LANES // HEAD_DIM
    return tuple(jnp.asarray(np.tile(t, (1, rep)), F32) for t in (cos_t, sin_lo, sin_hi))


def _pmix(x, mod, sub, w_in, w_o, sinks, tables, gn_g, gn_b, w_s, b_s, out_g, ln_gb, *, batch, tm):
    nt = SEQ // tm
    n_tiles = batch * nt
    tile = lambda n: jnp.minimum(n, n_tiles - 1)
    cos_t, sin_lo, sin_hi = tables
    tab_spec = pl.BlockSpec((tm, LANES), lambda n: (tile(n) % nt, 0))
    full2 = lambda n: (0, 0)
    mod_spec = pl.BlockSpec((3, mod.shape[1], D_MODEL), lambda n: (sub, 0, 0))
    kv_spec = pl.BlockSpec((1, WINDOW, KV_WIDTH), lambda n: (tile(n) // nt, 0, 0))
    resident = dict(pipeline_mode=pl.Buffered(1))
    return pl.pallas_call(
        functools.partial(_pmix_kernel, nt, n_tiles),
        grid=(n_tiles + 1,),
        in_specs=[
            pl.BlockSpec(memory_space=pltpu.SMEM),
            pl.BlockSpec((tm, D_MODEL), lambda n: (tile(n), 0)),
            mod_spec,
            pl.BlockSpec((D_MODEL, IN_WIDTH), full2, **resident),
            pl.BlockSpec((D_MODEL, D_MODEL), full2, **resident),
            tab_spec, tab_spec, tab_spec,
            pl.BlockSpec((1, GMLP_WIDTH), full2),
            pl.BlockSpec((1, GMLP_WIDTH), full2),
            pl.BlockSpec((N_GMLP_GROUPS, CHUNK, CHUNK), lambda n: (0, 0, 0)),
            pl.BlockSpec((CHUNK, N_GMLP_GROUPS), full2),
            pl.BlockSpec((1, D_MODEL), full2),
            pl.BlockSpec((2, D_MODEL), full2),
        ],
        out_specs=[
            pl.BlockSpec((tm, D_MODEL), lambda n: (jnp.maximum(n - 1, 0), 0)),
            kv_spec, kv_spec,
        ],
        out_shape=[
            jax.ShapeDtypeStruct((batch * SEQ, D_MODEL), F32),
            jax.ShapeDtypeStruct((batch, WINDOW, KV_WIDTH), F32),
            jax.ShapeDtypeStruct((batch, WINDOW, KV_WIDTH), F32),
        ],
        scratch_shapes=[
            pltpu.VMEM((tm, 2 * GMLP_WIDTH), F32),
            pltpu.VMEM((tm, ATTN_WIDTH), BF16),
            pltpu.VMEM((WINDOW + tm, 4 * KV_WIDTH), BF16),
            pltpu.VMEM((WINDOW + tm, 4 * KV_WIDTH), BF16),
            pltpu.VMEM((WINDOW, 4 * KV_WIDTH), BF16),
            pltpu.VMEM((WINDOW, 4 * KV_WIDTH), BF16),
            pltpu.VMEM((tm, ATTN_WIDTH), F32),
            pltpu.VMEM((tm, D_MODEL), BF16),
            pltpu.VMEM((N_GMLP_GROUPS, CHUNK, CHUNK), BF16),
            pltpu.VMEM((tm, D_MODEL), F32),
        ],
        compiler_params=pltpu.CompilerParams(dimension_semantics=("arbitrary",), vmem_limit_bytes=VMEM_LIMIT),
        name="prompt_mixer",
    )(sinks, x, mod, w_in, w_o, cos_t, sin_lo, sin_hi, gn_g.reshape(1, GMLP_WIDTH),
      gn_b.reshape(1, GMLP_WIDTH), w_s, b_s.T, out_g.reshape(1, D_MODEL), ln_gb)


def _sattn_kernel(qe_ref, p_ref, ck_ref, cv_ref, sink_ref, cos_ref, slo_ref, shi_ref, o_ref, ko_ref, vo_ref):
    cos, slo, shi = cos_ref[...], slo_ref[...], shi_ref[...]
    sink = sink_ref[...]
    for r in range(qe_ref.shape[0]):
        qe = _rope_lanes(qe_ref[r], cos, slo, shi) * (HEAD_DIM ** -0.5)
        kn = _rope_lanes(p_ref[r, :, Q_END:K_END], cos, slo, shi)
        vn = p_ref[r, :, K_END:V_END]
        ck = ck_ref[r]
        cv = cv_ref[r]
        s_c = _dot_nt(qe.astype(BF16), ck.astype(BF16))
        key = lax.broadcasted_iota(jnp.int32, s_c.shape, 1)
        s_c = jnp.where(key >= 1, s_c, -jnp.inf)
        s_n = jnp.sum(qe * kn, axis=-1, keepdims=True)
        mx = jnp.maximum(jnp.maximum(jnp.max(s_c, axis=-1, keepdims=True), s_n), sink)
        p_c = jnp.exp(s_c - mx)
        p_n = jnp.exp(s_n - mx)
        denom = jnp.sum(p_c, axis=-1, keepdims=True) + p_n + jnp.exp(sink - mx)
        o = (_dot(p_c.astype(BF16), cv.astype(BF16)) + p_n * vn) / denom
        head = lax.broadcasted_iota(jnp.int32, o.shape, 0)
        lane = lax.broadcasted_iota(jnp.int32, o.shape, 1)
        o_ref[r] = jnp.where(head // GQA_GROUP == lane // HEAD_DIM, o, 0.0)
        row = lax.broadcasted_iota(jnp.int32, ck.shape, 0)
        newest = row == WINDOW - 1
        ko_ref[r] = jnp.where(newest, kn, pltpu.roll(ck, WINDOW - 1, 0))
        vo_ref[r] = jnp.where(newest, vn, pltpu.roll(cv, WINDOW - 1, 0))


def _sattn(qe, p3, ck, cv, sinks, tables):
    b = qe.shape[0]
    rows = SAMPLE_ROWS_PER_STEP
    cos_t, sin_lo, sin_hi = tables
    tab_spec = pl.BlockSpec((1, LANES), lambda i: (0, 0))
    blk3 = lambda i: (i, 0, 0)
    return pl.pallas_call(
        _sattn_kernel,
        grid=(b // rows,),
        in_specs=[
            pl.BlockSpec((rows, N_HEADS, KV_WIDTH), blk3),
            pl.BlockSpec((rows, 1, IN_WIDTH), blk3),
            pl.BlockSpec((rows, WINDOW, KV_WIDTH), blk3),
            pl.BlockSpec((rows, WINDOW, KV_WIDTH), blk3),
            pl.BlockSpec((N_HEADS, 1), lambda i: (0, 0)),
            tab_spec, tab_spec, tab_spec,
        ],
        out_specs=[
            pl.BlockSpec((rows, N_HEADS, KV_WIDTH), blk3),
            pl.BlockSpec((rows, WINDOW, KV_WIDTH), blk3),
            pl.BlockSpec((rows, WINDOW, KV_WIDTH), blk3),
        ],
        out_shape=[
            jax.ShapeDtypeStruct((b, N_HEADS, KV_WIDTH), F32),
            jax.ShapeDtypeStruct((b, WINDOW, KV_WIDTH), F32),
            jax.ShapeDtypeStruct((b, WINDOW, KV_WIDTH), F32),
        ],
        compiler_params=pltpu.CompilerParams(dimension_semantics=("arbitrary",)),
        name="sample_attn",
    )(qe, p3, ck, cv, sinks.reshape(N_HEADS, 1), cos_t, sin_lo, sin_hi)


def _smerge_kernel(attn_ref, p_ref, gng_ref, gnb_ref, w0_ref, b0_ref, og_ref, m_ref, vn_ref):
    og = og_ref[...]
    vn = jnp.concatenate(_gmlp_norm(p_ref[:, U_END:IN_WIDTH], gng_ref[...], gnb_ref[...]), axis=-1)
    vn_ref[...] = vn
    u = _gelu_tanh(p_ref[:, V_END:U_END])
    gm = u * (w0_ref[...] * vn + b0_ref[...])
    m_ref[:, 0:ATTN_WIDTH] = _rms_norm(attn_ref[...], og[:, 0:ATTN_WIDTH]).astype(BF16)
    m_ref[:, ATTN_WIDTH:] = _rms_norm(gm, og[:, ATTN_WIDTH:]).astype(BF16)


def _smerge(attn, p, gn_g, gn_b, w_s, b_s, out_g):
    b = attn.shape[0]
    w0 = jnp.repeat(w_s[:, 0, 0], LANES).reshape(1, GMLP_WIDTH)
    b0 = jnp.repeat(b_s[:, 0], LANES).reshape(1, GMLP_WIDTH)
    return pl.pallas_call(
        _smerge_kernel,
        out_shape=[
            jax.ShapeDtypeStruct((b, D_MODEL), BF16),
            jax.ShapeDtypeStruct((b, GMLP_WIDTH), F32),
        ],
        name="sample_gmlp_merge",
    )(attn, p, gn_g.reshape(1, GMLP_WIDTH), gn_b.reshape(1, GMLP_WIDTH), w0, b0, out_g.reshape(1, D_MODEL))


def kernel(x_prompt, x_sample, cache_k_win, cache_v_win, c_prompt, c_sample, w_ada, b_ada, ln_g, ln_b,
           w_ffn_up, w_ffn_down, w_in, attn_sinks, gmlp_norm_g, gmlp_norm_b, w_spatial, b_spatial,
           out_norm_g, w_o):
    batch, seq, _ = x_prompt.shape
    dec_batch = x_sample.shape[0]
    buf = cache_k_win.shape[2]
    assert seq == SEQ and buf == WINDOW and x_sample.shape[1] == 1 and w_ada.shape[0] == DEPTH == 1

    xp = x_prompt.reshape(batch * seq, D_MODEL)
    xs = x_sample.reshape(dec_batch, D_MODEL)

    n_c = batch + dec_batch
    n_c_pad = ((n_c + 7) // 8) * 8
    c_all = jnp.concatenate([c_prompt, c_sample, jnp.zeros((n_c_pad - n_c, D_MODEL), F32)], axis=0)
    mod = _ada(c_all, w_ada[0], b_ada[0])

    def smod(i, j):
        return mod[3 * i + j, batch:n_c].reshape(1, dec_batch, D_MODEL)

    ln_gb = jnp.stack([ln_g[0], ln_b[0]], axis=1)
    ffn = functools.partial(_ffn, tm=FFN_ROWS, rows_per_mod=seq, extra_row0=batch)

    w_up0 = w_ffn_up[0, 0].astype(BF16)
    w_down0 = w_ffn_down[0, 0].astype(BF16)
    later = ((w_ffn_up, (0, 1), 16), (w_ffn_down, (0, 1), 32), (w_in, (0,), 16), (w_o, (0,), 16))
    xp, xs, w_up1, w_down1, w_in_b, w_o_b = ffn(xp, mod, 0, w_up0, w_down0, ln_gb[0], extra=xs, side=later)

    xp, k_p, v_p = _pmix(xp, mod, 1, w_in_b, w_o_b, attn_sinks[0], _rope_tables(np.arange(seq)),
                         gmlp_norm_g[0], gmlp_norm_b[0], w_spatial[0], b_spatial[0], out_norm_g[0], ln_gb[1],
                         batch=batch, tm=MIX_TILE)

    ps = _inproj(xs, smod(1, 0), smod(1, 1), w_in_b, tm=dec_batch, rows_per_mod=dec_batch)
    eye = jnp.eye(N_KV_HEADS, dtype=F32)
    q5 = ps[:, :Q_END].reshape(dec_batch, N_KV_HEADS, GQA_GROUP, 1, HEAD_DIM)
    qe = (q5 * eye[None, :, None, :, None]).reshape(dec_batch, N_HEADS, KV_WIDTH)
    ck = cache_k_win[0].reshape(dec_batch, buf, KV_WIDTH)
    cv = cache_v_win[0].reshape(dec_batch, buf, KV_WIDTH)
    oe, k_s, v_s = _sattn(qe, ps.reshape(dec_batch, 1, IN_WIDTH), ck, cv, attn_sinks[0],
                          _rope_tables(np.full((1,), PAST_LEN)))
    attn_s = oe.reshape(dec_batch, N_KV_HEADS, GQA_GROUP, N_KV_HEADS, HEAD_DIM).sum(axis=3).reshape(dec_batch, ATTN_WIDTH)
    merged_s, vn_s = _smerge(attn_s, ps, gmlp_norm_g[0], gmlp_norm_b[0], w_spatial[0], b_spatial[0], out_norm_g[0])
    xs = _outproj(merged_s, xs, smod(1, 2), w_o_b, ln_g[0, 1], ln_b[0, 1], tm=dec_batch, rows_per_mod=dec_batch)

    xp, xs = ffn(xp, mod, 2, w_up1, w_down1, ln_gb[2], extra=xs)

    return (
        xp.reshape(batch, seq, D_MODEL),
        xs.reshape(dec_batch, 1, D_MODEL),
        k_p.reshape(1, batch, WINDOW, N_KV_HEADS, HEAD_DIM),
        v_p.reshape(1, batch, WINDOW, N_KV_HEADS, HEAD_DIM),
        k_s.reshape(1, dec_batch, buf, N_KV_HEADS, HEAD_DIM),
        v_s.reshape(1, dec_batch, buf, N_KV_HEADS, HEAD_DIM),
        vn_s.reshape(1, dec_batch, 1, GMLP_WIDTH),
    )
```

```python
import functools

import jax
import jax.numpy as jnp
import numpy as np
from jax import lax
from jax.experimental import pallas as pl
from jax.experimental.pallas import tpu as pltpu

D_MODEL = 2048
SEQ = 2048
PAST_LEN = 16384
ATTN_WIDTH = 1024
GMLP_WIDTH = 1024
HEAD_DIM = 64
N_HEADS = 16
N_KV_HEADS = 4
GQA_GROUP = 4
KV_WIDTH = 256
WINDOW = 128
ROPE_THETA = 500000.0
ROPE_DIM = 16
CHUNK = 128
N_GMLP_GROUPS = 8
D_FF = 5504
N_SUB = 3
DEPTH = 1
ALPHA = (2.0 * DEPTH) ** 0.25
LN_EPS = 1e-5
Q_END = ATTN_WIDTH
K_END = Q_END + KV_WIDTH
V_END = K_END + KV_WIDTH
U_END = V_END + GMLP_WIDTH
IN_WIDTH = U_END + GMLP_WIDTH

LANES = 128
MXU_COLS = 256
FF_TILE = 512
N_FF_CHUNKS = -(-D_FF // FF_TILE)
FF_LAST_OFF = D_FF - FF_TILE
FF_OVERLAP = N_FF_CHUNKS * FF_TILE - D_FF
N_PAIR_STEPS = (N_FF_CHUNKS - 1) // 2
N_UP_STEPS = N_PAIR_STEPS + 1
OUT_TILE = 512
N_OUT_STEPS = D_MODEL // OUT_TILE
FFN_ROWS = 512
LN_PIECE_ROWS = 128
MIX_TILE = 256
PROJ_TILE = 512
SAMPLE_ROWS_PER_STEP = 8
VMEM_LIMIT = 62 * 1024 * 1024

BF16 = jnp.bfloat16
F32 = jnp.float32


def _dot(a, b):
    return jnp.dot(a, b, preferred_element_type=F32)


def _dot_nt(a, b):
    return lax.dot_general(a, b, (((1,), (1,)), ((), ())), preferred_element_type=F32)


def _layer_norm(y, g, b):
    mu = jnp.mean(y, axis=-1, keepdims=True)
    d = y - mu
    var = jnp.mean(d * d, axis=-1, keepdims=True)
    return d * lax.rsqrt(var + LN_EPS) * g + b


def _rms_norm(y, g):
    return y * lax.rsqrt(jnp.mean(y * y, axis=-1, keepdims=True) + LN_EPS) * g


def _gelu_tanh(x):
    c = np.float32(np.sqrt(2.0 / np.pi))
    return 0.5 * x * (1.0 + jnp.tanh(c * (x + 0.044715 * (x * x * x))))


def _rope_lanes(t, cos, sin_lo, sin_hi):
    pieces = []
    for c in range(t.shape[-1] // LANES):
        x = t[:, c * LANES:(c + 1) * LANES]
        pieces.append(x * cos + pltpu.roll(x, 8, 1) * sin_hi + pltpu.roll(x, LANES - 8, 1) * sin_lo)
    return pieces[0] if len(pieces) == 1 else jnp.concatenate(pieces, axis=-1)


def _ada_kernel(c_ref, w_ref, b_ref, o_ref):
    c = c_ref[...]
    h = (c * jax.nn.sigmoid(c)).astype(BF16)
    o_ref[0] = _dot(h, w_ref[...].astype(BF16)) + b_ref[...]


def _ada(c, w_ada, b_ada):
    rows = c.shape[0]
    n = w_ada.shape[1]
    tn = 1024
    per_plane = D_MODEL // tn
    return pl.pallas_call(
        _ada_kernel,
        grid=(n // tn,),
        in_specs=[
            pl.BlockSpec((rows, D_MODEL), lambda j: (0, 0)),
            pl.BlockSpec((D_MODEL, tn), lambda j: (0, j)),
            pl.BlockSpec((1, tn), lambda j: (0, j)),
        ],
        out_specs=pl.BlockSpec((1, rows, tn), lambda j: (j // per_plane, 0, j % per_plane)),
        out_shape=jax.ShapeDtypeStruct((n // D_MODEL, rows, D_MODEL), F32),
        compiler_params=pltpu.CompilerParams(dimension_semantics=("arbitrary",), vmem_limit_bytes=VMEM_LIMIT),
        name="ada_mod",
    )(c, w_ada, b_ada.reshape(1, n))


def _ffn_kernel(n_side, n_extra, tiles_per_mod, extra_row0, n_tiles, *refs):
    x_ref, mod_ref, wva_ref, wga_ref, wvb_ref, wgb_ref, wd_ref, gb_ref = refs[:8]
    n_in = 8
    if n_extra:
        xe_ref = refs[n_in]
        n_in += 1
        erows = slice(extra_row0, extra_row0 + n_extra)
    side_in = refs[n_in:n_in + n_side]
    n_in += n_side
    o_ref = refs[n_in]
    n_out = 1
    if n_extra:
        oe_ref = refs[n_in + 1]
        n_out = 2
    side_out = refs[n_in + n_out:n_in + n_out + n_side]
    xm_ref, a_ref = refs[n_in + n_out + n_side:]
    i = pl.program_id(0)
    s = pl.program_id(1)
    tm = x_ref.shape[0]
    last = pl.num_programs(1) - 1
    seq = pl.ds(i // tiles_per_mod, 1)

    def cast_side():
        for src, dst in zip(side_in, side_out):
            dst[...] = src[...].astype(BF16)

    def norm_piece():
        n_pieces = tm // LN_PIECE_ROWS
        r0 = pl.multiple_of(jnp.minimum(s, n_pieces - 1) * LN_PIECE_ROWS, LN_PIECE_ROWS)
        piece = pl.ds(r0, LN_PIECE_ROWS)
        y = o_ref[piece, :]
        o_ref[piece, :] = jnp.where(s < n_pieces, _layer_norm(y, gb_ref[0:1], gb_ref[1:2]), y)

    def swiglu(xm, wv_ref, wg_ref, sl):
        hv = _dot(xm, wv_ref[:, sl])
        hg = _dot(xm, wg_ref[:, sl])
        return hg * jax.nn.sigmoid(hg) * hv

    def up_pair(rows, norm_previous):
        cast_side()
        if norm_previous:
            norm_piece()
        xm = xm_ref[0:rows]
        for half, (wv_ref, wg_ref) in enumerate(((wva_ref, wga_ref), (wvb_ref, wgb_ref))):
            for c in range(FF_TILE // MXU_COLS):
                sl = slice(c * MXU_COLS, (c + 1) * MXU_COLS)
                a_ref[2 * s + half, 0:rows, sl] = swiglu(xm, wv_ref, wg_ref, sl).astype(BF16)

    def up_last(rows, norm_previous):
        cast_side()
        if norm_previous:
            norm_piece()
        xm = xm_ref[0:rows]
        for c in range(FF_TILE // MXU_COLS):
            sl = slice(c * MXU_COLS, (c + 1) * MXU_COLS)
            a = swiglu(xm, wva_ref, wga_ref, sl)
            if c * MXU_COLS < FF_OVERLAP:
                col = lax.broadcasted_iota(jnp.int32, a.shape, 1) + c * MXU_COLS
                a = jnp.where(col < FF_OVERLAP, 0.0, a)
            a_ref[N_FF_CHUNKS - 1, 0:rows, sl] = a.astype(BF16)

    def down_slab(n, rows):
        cast_side()
        acc = None
        for f in range(N_FF_CHUNKS):
            r0 = min(f * FF_TILE, FF_LAST_OFF)
            part = _dot(a_ref[f, 0:rows], wd_ref[r0:r0 + FF_TILE, :])
            acc = part if acc is None else acc + part
        sl = slice(n * OUT_TILE, (n + 1) * OUT_TILE)
        o_ref[:, sl] = ALPHA * x_ref[:, sl] + 0.5 * mod_ref[2, seq, sl] * acc[0:tm]
        if rows > tm:
            oe_ref[:, sl] = ALPHA * xe_ref[:, sl] + 0.5 * mod_ref[2, erows, sl] * acc[tm:rows]

    first = i == 0
    later = i > 0

    @pl.when(s == 0)
    def _():
        xm_ref[0:tm] = (x_ref[...] * (1.0 + mod_ref[1, seq, :]) + mod_ref[0, seq, :]).astype(BF16)

    if n_extra:
        @pl.when(jnp.logical_and(s == 0, first))
        def _():
            xm_ref[tm:tm + n_extra] = (xe_ref[...] * (1.0 + mod_ref[1, erows, :]) + mod_ref[0, erows, :]).astype(BF16)

    pair_step = s < N_PAIR_STEPS
    last_step = s == N_PAIR_STEPS
    pl.when(jnp.logical_and(pair_step, first))(functools.partial(up_pair, tm + n_extra, False))
    pl.when(jnp.logical_and(pair_step, later))(functools.partial(up_pair, tm, True))
    pl.when(jnp.logical_and(last_step, first))(functools.partial(up_last, tm + n_extra, False))
    pl.when(jnp.logical_and(last_step, later))(functools.partial(up_last, tm, True))
    for n in range(N_OUT_STEPS):
        slab_step = s == N_UP_STEPS + n
        pl.when(jnp.logical_and(slab_step, first))(functools.partial(down_slab, n, tm + n_extra))
        pl.when(jnp.logical_and(slab_step, later))(functools.partial(down_slab, n, tm))

    @pl.when(jnp.logical_and(s == last, i == n_tiles - 1))
    def _():
        o_ref[...] = _layer_norm(o_ref[...], gb_ref[0:1], gb_ref[1:2])

    if n_extra:
        @pl.when(jnp.logical_and(s == last, first))
        def _():
            oe_ref[...] = _layer_norm(oe_ref[...], gb_ref[0:1], gb_ref[1:2])


def _ffn(x, mod, sub, w_up, w_down, ln_gb, *, tm, rows_per_mod, extra=None, extra_row0=0, side=()):
    m = x.shape[0]
    n_steps = N_UP_STEPS + N_OUT_STEPS
    n_tiles = m // tm
    tiles_per_mod = rows_per_mod // tm
    n_extra = 0 if extra is None else extra.shape[0]
    assert tm % LN_PIECE_ROWS == 0 and tm // LN_PIECE_ROWS <= N_UP_STEPS
    const2 = lambda i, s: (0, 0)
    extra_spec = pl.BlockSpec((n_extra, D_MODEL), const2)
    side_in_specs, side_out_specs, side_shapes = [], [], []
    for arr, lead, rows in side:
        n_rows, n_cols = arr.shape[-2:]
        n_blocks = n_rows // rows
        assert n_blocks * rows == n_rows and n_blocks <= n_tiles * n_steps
        blk = lambda i, s, n_blocks=n_blocks: jnp.minimum(i * n_steps + s, n_blocks - 1)
        side_in_specs.append(pl.BlockSpec((None,) * len(lead) + (rows, n_cols),
                                          lambda i, s, lead=lead, blk=blk: (*lead, blk(i, s), 0)))
        side_out_specs.append(pl.BlockSpec((rows, n_cols), lambda i, s, blk=blk: (blk(i, s), 0)))
        side_shapes.append(jax.ShapeDtypeStruct((n_rows, n_cols), BF16))

    def off_a(s):
        return jnp.where(s < N_PAIR_STEPS, 2 * s * FF_TILE, FF_LAST_OFF)

    def off_b(s):
        return (2 * jnp.minimum(s, N_PAIR_STEPS - 1) + 1) * FF_TILE

    def window(off, base):
        return pl.BlockSpec((pl.Element(D_MODEL), pl.Element(FF_TILE)),
                            lambda i, s: (0, pl.multiple_of(base + off(s), LANES)))

    wd_block = lambda i, s: (0, jnp.where(s < N_UP_STEPS, N_OUT_STEPS - 1, s - N_UP_STEPS))
    out_block = lambda i, s: (jnp.where(s < N_UP_STEPS, jnp.maximum(i - 1, 0), i), 0)
    outs = pl.pallas_call(
        functools.partial(_ffn_kernel, len(side), n_extra, tiles_per_mod, extra_row0, n_tiles),
        grid=(n_tiles, n_steps),
        in_specs=[
            pl.BlockSpec((tm, D_MODEL), lambda i, s: (i, 0)),
            pl.BlockSpec((3, mod.shape[1], D_MODEL), lambda i, s: (sub, 0, 0)),
            window(off_a, 0), window(off_a, D_FF), window(off_b, 0), window(off_b, D_FF),
            pl.BlockSpec((D_FF, OUT_TILE), wd_block),
            pl.BlockSpec((2, D_MODEL), const2),
            *([extra_spec] if n_extra else []),
            *side_in_specs,
        ],
        out_specs=[pl.BlockSpec((tm, D_MODEL), out_block), *([extra_spec] if n_extra else []),
                   *side_out_specs],
        out_shape=[jax.ShapeDtypeStruct((m, D_MODEL), F32),
                   *([jax.ShapeDtypeStruct((n_extra, D_MODEL), F32)] if n_extra else []), *side_shapes],
        scratch_shapes=[
            pltpu.VMEM((tm + n_extra, D_MODEL), BF16),
            pltpu.VMEM((N_FF_CHUNKS, tm + n_extra, FF_TILE), BF16),
        ],
        compiler_params=pltpu.CompilerParams(dimension_semantics=("arbitrary", "arbitrary"), vmem_limit_bytes=VMEM_LIMIT),
        name="swiglu_ln",
    )(x, mod, w_up, w_up, w_up, w_up, w_down, ln_gb, *([extra] if n_extra else []), *[arr for arr, _, _ in side])
    return outs


def _inproj_kernel(x_ref, shift_ref, scale_ref, w_ref, o_ref, xm_ref):
    @pl.when(pl.program_id(1) == 0)
    def _():
        xm_ref[...] = (x_ref[...] * (1.0 + scale_ref[0]) + shift_ref[0]).astype(BF16)

    o_ref[...] = _dot(xm_ref[...], w_ref[...])


def _inproj(x, shift, scale, w_in, *, tm, rows_per_mod):
    m = x.shape[0]
    r = shift.shape[1]
    tn = 512
    tiles_per_mod = rows_per_mod // tm
    mod_spec = pl.BlockSpec((1, r, D_MODEL), lambda i, j: (i // tiles_per_mod, 0, 0))
    return pl.pallas_call(
        _inproj_kernel,
        grid=(m // tm, IN_WIDTH // tn),
        in_specs=[
            pl.BlockSpec((tm, D_MODEL), lambda i, j: (i, 0)),
            mod_spec, mod_spec,
            pl.BlockSpec((D_MODEL, tn), lambda i, j: (0, j)),
        ],
        out_specs=pl.BlockSpec((tm, tn), lambda i, j: (i, j)),
        out_shape=jax.ShapeDtypeStruct((m, IN_WIDTH), F32),
        scratch_shapes=[pltpu.VMEM((tm, D_MODEL), BF16)],
        compiler_params=pltpu.CompilerParams(dimension_semantics=("arbitrary", "arbitrary"), vmem_limit_bytes=VMEM_LIMIT),
        name="mixer_inproj",
    )(x, shift, scale, w_in)


def _outproj_kernel(m_ref, x_ref, gate_ref, w_ref, g_ref, b_ref, o_ref):
    mixed = _dot(m_ref[...], w_ref[...])
    y = ALPHA * x_ref[...] + gate_ref[0] * mixed
    o_ref[...] = _layer_norm(y, g_ref[...], b_ref[...])


def _outproj(merged, x, gate, w_o, ln_g, ln_b, *, tm, rows_per_mod):
    m = x.shape[0]
    r = gate.shape[1]
    tiles_per_mod = rows_per_mod // tm
    return pl.pallas_call(
        _outproj_kernel,
        grid=(m // tm,),
        in_specs=[
            pl.BlockSpec((tm, D_MODEL), lambda i: (i, 0)),
            pl.BlockSpec((tm, D_MODEL), lambda i: (i, 0)),
            pl.BlockSpec((1, r, D_MODEL), lambda i: (i // tiles_per_mod, 0, 0)),
            pl.BlockSpec((D_MODEL, D_MODEL), lambda i: (0, 0)),
            pl.BlockSpec((1, D_MODEL), lambda i: (0, 0)),
            pl.BlockSpec((1, D_MODEL), lambda i: (0, 0)),
        ],
        out_specs=pl.BlockSpec((tm, D_MODEL), lambda i: (i, 0)),
        out_shape=jax.ShapeDtypeStruct((m, D_MODEL), F32),
        compiler_params=pltpu.CompilerParams(dimension_semantics=("arbitrary",), vmem_limit_bytes=VMEM_LIMIT),
        name="mixer_outproj_ln",
    )(merged, x, gate, w_o, ln_g.reshape(1, D_MODEL), ln_b.reshape(1, D_MODEL))


def _gmlp_norm(pv, gng, gnb):
    v = _gelu_tanh(pv)
    outs = []
    for g in range(N_GMLP_GROUPS):
        sl = slice(g * LANES, (g + 1) * LANES)
        outs.append(_layer_norm(v[:, sl], gng[:, sl], gnb[:, sl]))
    return outs


def _half_lane_tiles(t):
    lo_lane = lax.broadcasted_iota(jnp.int32, (t.shape[0], LANES), 1) < HEAD_DIM
    lo, hi = [], []
    for c in range(KV_WIDTH // LANES):
        col = t[:, c * LANES:(c + 1) * LANES]
        swp = pltpu.roll(col, HEAD_DIM, 1)
        lo += [jnp.where(lo_lane, col, 0.0), jnp.where(lo_lane, swp, 0.0)]
        hi += [jnp.where(lo_lane, 0.0, swp), jnp.where(lo_lane, 0.0, col)]
    return jnp.concatenate(lo + hi, axis=-1).astype(BF16)


def _band_attention(q_s, kx_s, vx_s, attn_s, sinks_ref, nblk, first_tile):
    rows = 2 * WINDOW
    qi = lax.broadcasted_iota(jnp.int32, (rows, WINDOW), 0) & (WINDOW - 1)
    kj = lax.broadcasted_iota(jnp.int32, (rows, WINDOW), 1)
    own = kj <= qi
    first_mask = kj <= jnp.where(first_tile, qi, WINDOW)
    ones = jnp.ones((2 * WINDOW, LANES), BF16)
    units = [(wb, g, hf) for wb in range(nblk) for g in range(N_KV_HEADS) for hf in range(2)]

    def kv_cols(g, hf):
        return slice(hf * 4 * LANES + g * LANES, hf * 4 * LANES + (g + 1) * LANES)

    scores = []
    for wb, g, hf in units:
        r0 = wb * WINDOW
        q2 = jnp.concatenate([q_s[r0:r0 + WINDOW, (2 * g) * LANES:(2 * g + 1) * LANES],
                              q_s[r0:r0 + WINDOW, (2 * g + 1) * LANES:(2 * g + 2) * LANES]], axis=0)
        s = _dot_nt(q2, kx_s[r0:r0 + 2 * WINDOW, kv_cols(g, hf)])
        c = jnp.where(own, s[:, WINDOW:2 * WINDOW], s[:, 0:WINDOW])
        if wb == 0:
            c = jnp.where(first_mask, c, -jnp.inf)
        scores.append(c)

    probs = []
    for (wb, g, hf), c in zip(units, scores):
        sink = jnp.concatenate([jnp.full((WINDOW, LANES), sinks_ref[GQA_GROUP * g + hf], F32),
                                jnp.full((WINDOW, LANES), sinks_ref[GQA_GROUP * g + 2 + hf], F32)], axis=0)
        mx = jnp.maximum(jnp.broadcast_to(jnp.max(c, axis=-1, keepdims=True), c.shape), sink)
        p = jnp.exp(c - mx)
        pcat = jnp.concatenate([jnp.where(own, 0.0, p), jnp.where(own, p, 0.0)], axis=-1).astype(BF16)
        probs.append((pcat, jnp.exp(sink - mx)))

    outs = {}
    for (wb, g, hf), (pcat, esink) in zip(units, probs):
        r0 = wb * WINDOW
        v2 = jnp.concatenate([vx_s[r0:r0 + 2 * WINDOW, kv_cols(g, hf)], ones], axis=-1)
        o = _dot(pcat, v2)
        outs[wb, g, hf] = o[:, 0:LANES] / (o[:, LANES:2 * LANES] + esink)

    for wb in range(nblk):
        r0 = wb * WINDOW
        for g in range(N_KV_HEADS):
            both = outs[wb, g, 0] + outs[wb, g, 1]
            attn_s[r0:r0 + WINDOW, (2 * g) * LANES:(2 * g + 1) * LANES] = both[0:WINDOW]
            attn_s[r0:r0 + WINDOW, (2 * g + 1) * LANES:(2 * g + 2) * LANES] = both[WINDOW:2 * WINDOW]


def _pmix_kernel(tiles_per_seq, n_tiles, sinks_ref, x_ref, mod_ref, win_ref, wo_ref, cos_ref, slo_ref, shi_ref,
                 gng_ref, gnb_ref, ws_ref, bs_ref, og_ref, lngb_ref,
                 o_ref, k_ref, v_ref, p_s, q_s, kk_s, vv_s, kp_s, vp_s, attn_s, m_s, wm_s, y_s):
    n = pl.program_id(0)
    t = n % tiles_per_seq
    tm = x_ref.shape[0]
    nblk = tm // WINDOW

    @pl.when(n == 0)
    def _():
        y_s[...] = jnp.zeros(y_s.shape, F32)
        row = lax.broadcasted_iota(jnp.int32, (CHUNK, CHUNK), 0)
        col = lax.broadcasted_iota(jnp.int32, (CHUNK, CHUNK), 1)
        for g in range(N_GMLP_GROUPS):
            wm_s[g] = jnp.where(col <= row, ws_ref[g], 0.0).astype(BF16)

    @pl.when(t == 0)
    def _():
        kp_s[...] = jnp.zeros(kp_s.shape, BF16)
        vp_s[...] = jnp.zeros(vp_s.shape, BF16)

    @pl.when(n == n_tiles)
    def _():
        o_ref[...] = _layer_norm(y_s[...], lngb_ref[0:1], lngb_ref[1:2])

    pl.when(n < n_tiles)(functools.partial(
        _pmix_tile, t, pl.ds(n // tiles_per_seq, 1), sinks_ref, x_ref, mod_ref, win_ref, wo_ref, cos_ref, slo_ref,
        shi_ref, gng_ref, gnb_ref, bs_ref, og_ref, lngb_ref, o_ref, k_ref, v_ref, p_s, q_s, kk_s, vv_s, kp_s, vp_s,
        attn_s, m_s, wm_s, y_s))


def _pmix_tile(t, seq, sinks_ref, x_ref, mod_ref, win_ref, wo_ref, cos_ref, slo_ref, shi_ref, gng_ref, gnb_ref,
               bs_ref, og_ref, lngb_ref, o_ref, k_ref, v_ref, p_s, q_s, kk_s, vv_s, kp_s, vp_s, attn_s, m_s, wm_s, y_s):
    tm = x_ref.shape[0]
    nblk = tm // WINDOW

    o_ref[...] = _layer_norm(y_s[...], lngb_ref[0:1], lngb_ref[1:2])

    m_s[...] = (x_ref[...] * (1.0 + mod_ref[1, seq, :]) + mod_ref[0, seq, :]).astype(BF16)
    cos, slo, shi = cos_ref[...], slo_ref[...], shi_ref[...]
    for c in range(IN_WIDTH // PROJ_TILE):
        lo_col = c * PROJ_TILE
        h = _dot(m_s[...], win_ref[:, lo_col:lo_col + PROJ_TILE])
        if lo_col + PROJ_TILE <= Q_END:
            q_s[:, lo_col:lo_col + PROJ_TILE] = (_rope_lanes(h, cos, slo, shi) * (HEAD_DIM ** -0.5)).astype(BF16)
        elif lo_col == Q_END and PROJ_TILE == 2 * KV_WIDTH:
            k = _rope_lanes(h[:, 0:KV_WIDTH], cos, slo, shi)
            v = h[:, KV_WIDTH:2 * KV_WIDTH]
            k_ref[0] = k[tm - WINDOW:tm]
            v_ref[0] = v[tm - WINDOW:tm]
            kb, vb = _half_lane_tiles(k), _half_lane_tiles(v)
            kk_s[0:WINDOW, :] = kp_s[...]
            vv_s[0:WINDOW, :] = vp_s[...]
            kk_s[WINDOW:WINDOW + tm, :] = kb
            vv_s[WINDOW:WINDOW + tm, :] = vb
            kp_s[...] = kb[tm - WINDOW:tm]
            vp_s[...] = vb[tm - WINDOW:tm]
        else:
            p_s[:, lo_col - V_END:lo_col - V_END + PROJ_TILE] = h

    _band_attention(q_s, kk_s, vv_s, attn_s, sinks_ref, nblk, t == 0)

    vn = _gmlp_norm(p_s[:, GMLP_WIDTH:2 * GMLP_WIDTH], gng_ref[...], gnb_ref[...])
    gated = []
    for g in range(N_GMLP_GROUPS):
        u = _gelu_tanh(p_s[:, g * LANES:(g + 1) * LANES])
        vg = vn[g].astype(BF16)
        cols = jnp.concatenate([vg[wb * CHUNK:(wb + 1) * CHUNK] for wb in range(nblk)], axis=-1)
        mix = _dot(wm_s[g], cols)
        mix = jnp.concatenate([mix[:, wb * LANES:(wb + 1) * LANES] for wb in range(nblk)], axis=0)
        bias = jnp.concatenate([bs_ref[:, g:g + 1]] * nblk, axis=0)
        gated.append(u * (mix + bias))
    gm = jnp.concatenate(gated, axis=-1)
    og = og_ref[...]
    m_s[:, 0:ATTN_WIDTH] = _rms_norm(attn_s[...], og[:, 0:ATTN_WIDTH]).astype(BF16)
    m_s[:, ATTN_WIDTH:] = _rms_norm(gm, og[:, ATTN_WIDTH:]).astype(BF16)

    for c in range(D_MODEL // PROJ_TILE):
        sl = slice(c * PROJ_TILE, (c + 1) * PROJ_TILE)
        y_s[:, sl] = ALPHA * x_ref[:, sl] + mod_ref[2, seq, sl] * _dot(m_s[...], wo_ref[:, sl])


def _rope_tables(pos):
    half = ROPE_DIM // 2
    inv = ROPE_THETA ** (-(np.arange(half, dtype=np.float64) * 2.0) / ROPE_DIM)
    ang = np.asarray(pos, np.float64)[:, None] * inv[None, :]
    cos, sin = np.cos(ang), np.sin(ang)
    n = ang.shape[0]
    one = np.ones((n, HEAD_DIM - ROPE_DIM))
    zero = np.zeros((n, HEAD_DIM - ROPE_DIM))
    z8 = np.zeros((n, half))
    cos_t = np.concatenate([cos, cos, one], axis=-1)
    sin_lo = np.concatenate([-sin, z8, zero], axis=-1)
    sin_hi = np.concatenate([z8, sin, zero], axis=-1)
    rep = LANES // HEAD_DIM
    return tuple(jnp.asarray(np.tile(t, (1, rep)), F32) for t in (cos_t, sin_lo, sin_hi))


def _pmix(x, mod, sub, w_in, w_o, sinks, tables, gn_g, gn_b, w_s, b_s, out_g, ln_gb, *, batch, tm):
    nt = SEQ // tm
    n_tiles = batch * nt
    tile = lambda n: jnp.minimum(n, n_tiles - 1)
    cos_t, sin_lo, sin_hi = tables
    tab_spec = pl.BlockSpec((tm, LANES), lambda n: (tile(n) % nt, 0))
    full2 = lambda n: (0, 0)
    mod_spec = pl.BlockSpec((3, mod.shape[1], D_MODEL), lambda n: (sub, 0, 0))
    kv_spec = pl.BlockSpec((1, WINDOW, KV_WIDTH), lambda n: (tile(n) // nt, 0, 0))
    resident = dict(pipeline_mode=pl.Buffered(1))
    return pl.pallas_call(
        functools.partial(_pmix_kernel, nt, n_tiles),
        grid=(n_tiles + 1,),
        in_specs=[
            pl.BlockSpec(memory_space=pltpu.SMEM),
            pl.BlockSpec((tm, D_MODEL), lambda n: (tile(n), 0)),
            mod_spec,
            pl.BlockSpec((D_MODEL, IN_WIDTH), full2, **resident),
            pl.BlockSpec((D_MODEL, D_MODEL), full2, **resident),
            tab_spec, tab_spec, tab_spec,
            pl.BlockSpec((1, GMLP_WIDTH), full2),
            pl.BlockSpec((1, GMLP_WIDTH), full2),
            pl.BlockSpec((N_GMLP_GROUPS, CHUNK, CHUNK), lambda n: (0, 0, 0)),
            pl.BlockSpec((CHUNK, N_GMLP_GROUPS), full2),
            pl.BlockSpec((1, D_MODEL), full2),
            pl.BlockSpec((2, D_MODEL), full2),
        ],
        out_specs=[
            pl.BlockSpec((tm, D_MODEL), lambda n: (jnp.maximum(n - 1, 0), 0)),
            kv_spec, kv_spec,
        ],
        out_shape=[
            jax.ShapeDtypeStruct((batch * SEQ, D_MODEL), F32),
            jax.ShapeDtypeStruct((batch, WINDOW, KV_WIDTH), F32),
            jax.ShapeDtypeStruct((batch, WINDOW, KV_WIDTH), F32),
        ],
        scratch_shapes=[
            pltpu.VMEM((tm, 2 * GMLP_WIDTH), F32),
            pltpu.VMEM((tm, ATTN_WIDTH), BF16),
            pltpu.VMEM((WINDOW + tm, 4 * KV_WIDTH), BF16),
            pltpu.VMEM((WINDOW + tm, 4 * KV_WIDTH), BF16),
            pltpu.VMEM((WINDOW, 4 * KV_WIDTH), BF16),
            pltpu.VMEM((WINDOW, 4 * KV_WIDTH), BF16),
            pltpu.VMEM((tm, ATTN_WIDTH), F32),
            pltpu.VMEM((tm, D_MODEL), BF16),
            pltpu.VMEM((N_GMLP_GROUPS, CHUNK, CHUNK), BF16),
            pltpu.VMEM((tm, D_MODEL), F32),
        ],
        compiler_params=pltpu.CompilerParams(dimension_semantics=("arbitrary",), vmem_limit_bytes=VMEM_LIMIT),
        name="prompt_mixer",
    )(sinks, x, mod, w_in, w_o, cos_t, sin_lo, sin_hi, gn_g.reshape(1, GMLP_WIDTH),
      gn_b.reshape(1, GMLP_WIDTH), w_s, b_s.T, out_g.reshape(1, D_MODEL), ln_gb)


def _sattn_kernel(qe_ref, p_ref, ck_ref, cv_ref, sink_ref, cos_ref, slo_ref, shi_ref, o_ref, ko_ref, vo_ref):
    cos, slo, shi = cos_ref[...], slo_ref[...], shi_ref[...]
    sink = sink_ref[...]
    for r in range(qe_ref.shape[0]):
        qe = _rope_lanes(qe_ref[r], cos, slo, shi) * (HEAD_DIM ** -0.5)
        kn = _rope_lanes(p_ref[r, :, Q_END:K_END], cos, slo, shi)
        vn = p_ref[r, :, K_END:V_END]
        ck = ck_ref[r]
        cv = cv_ref[r]
        s_c = _dot_nt(qe.astype(BF16), ck.astype(BF16))
        key = lax.broadcasted_iota(jnp.int32, s_c.shape, 1)
        s_c = jnp.where(key >= 1, s_c, -jnp.inf)
        s_n = jnp.sum(qe * kn, axis=-1, keepdims=True)
        mx = jnp.maximum(jnp.maximum(jnp.max(s_c, axis=-1, keepdims=True), s_n), sink)
        p_c = jnp.exp(s_c - mx)
        p_n = jnp.exp(s_n - mx)
        denom = jnp.sum(p_c, axis=-1, keepdims=True) + p_n + jnp.exp(sink - mx)
        o = (_dot(p_c.astype(BF16), cv.astype(BF16)) + p_n * vn) / denom
        head = lax.broadcasted_iota(jnp.int32, o.shape, 0)
        lane = lax.broadcasted_iota(jnp.int32, o.shape, 1)
        o_ref[r] = jnp.where(head // GQA_GROUP == lane // HEAD_DIM, o, 0.0)
        row = lax.broadcasted_iota(jnp.int32, ck.shape, 0)
        newest = row == WINDOW - 1
        ko_ref[r] = jnp.where(newest, kn, pltpu.roll(ck, WINDOW - 1, 0))
        vo_ref[r] = jnp.where(newest, vn, pltpu.roll(cv, WINDOW - 1, 0))


def _sattn(qe, p3, ck, cv, sinks, tables):
    b = qe.shape[0]
    rows = SAMPLE_ROWS_PER_STEP
    cos_t, sin_lo, sin_hi = tables
    tab_spec = pl.BlockSpec((1, LANES), lambda i: (0, 0))
    blk3 = lambda i: (i, 0, 0)
    return pl.pallas_call(
        _sattn_kernel,
        grid=(b // rows,),
        in_specs=[
            pl.BlockSpec((rows, N_HEADS, KV_WIDTH), blk3),
            pl.BlockSpec((rows, 1, IN_WIDTH), blk3),
            pl.BlockSpec((rows, WINDOW, KV_WIDTH), blk3),
            pl.BlockSpec((rows, WINDOW, KV_WIDTH), blk3),
            pl.BlockSpec((N_HEADS, 1), lambda i: (0, 0)),
            tab_spec, tab_spec, tab_spec,
        ],
        out_specs=[
            pl.BlockSpec((rows, N_HEADS, KV_WIDTH), blk3),
            pl.BlockSpec((rows, WINDOW, KV_WIDTH), blk3),
            pl.BlockSpec((rows, WINDOW, KV_WIDTH), blk3),
        ],
        out_shape=[
            jax.ShapeDtypeStruct((b, N_HEADS, KV_WIDTH), F32),
            jax.ShapeDtypeStruct((b, WINDOW, KV_WIDTH), F32),
            jax.ShapeDtypeStruct((b, WINDOW, KV_WIDTH), F32),
        ],
        compiler_params=pltpu.CompilerParams(dimension_semantics=("arbitrary",)),
        name="sample_attn",
    )(qe, p3, ck, cv, sinks.reshape(N_HEADS, 1), cos_t, sin_lo, sin_hi)


def _smerge_kernel(attn_ref, p_ref, gng_ref, gnb_ref, w0_ref, b0_ref, og_ref, m_ref, vn_ref):
    og = og_ref[...]
    vn = jnp.concatenate(_gmlp_norm(p_ref[:, U_END:IN_WIDTH], gng_ref[...], gnb_ref[...]), axis=-1)
    vn_ref[...] = vn
    u = _gelu_tanh(p_ref[:, V_END:U_END])
    gm = u * (w0_ref[...] * vn + b0_ref[...])
    m_ref[:, 0:ATTN_WIDTH] = _rms_norm(attn_ref[...], og[:, 0:ATTN_WIDTH]).astype(BF16)
    m_ref[:, ATTN_WIDTH:] = _rms_norm(gm, og[:, ATTN_WIDTH:]).astype(BF16)


def _smerge(attn, p, gn_g, gn_b, w_s, b_s, out_g):
    b = attn.shape[0]
    w0 = jnp.repeat(w_s[:, 0, 0], LANES).reshape(1, GMLP_WIDTH)
    b0 = jnp.repeat(b_s[:, 0], LANES).reshape(1, GMLP_WIDTH)
    return pl.pallas_call(
        _smerge_kernel,
        out_shape=[
            jax.ShapeDtypeStruct((b, D_MODEL), BF16),
            jax.ShapeDtypeStruct((b, GMLP_WIDTH), F32),
        ],
        name="sample_gmlp_merge",
    )(attn, p, gn_g.reshape(1, GMLP_WIDTH), gn_b.reshape(1, GMLP_WIDTH), w0, b0, out_g.reshape(1, D_MODEL))


def kernel(x_prompt, x_sample, cache_k_win, cache_v_win, c_prompt, c_sample, w_ada, b_ada, ln_g, ln_b,
           w_ffn_up, w_ffn_down, w_in, attn_sinks, gmlp_norm_g, gmlp_norm_b, w_spatial, b_spatial,
           out_norm_g, w_o):
    batch, seq, _ = x_prompt.shape
    dec_batch = x_sample.shape[0]
    buf = cache_k_win.shape[2]
    assert seq == SEQ and buf == WINDOW and x_sample.shape[1] == 1 and w_ada.shape[0] == DEPTH == 1

    xp = x_prompt.reshape(batch * seq, D_MODEL)
    xs = x_sample.reshape(dec_batch, D_MODEL)

    n_c = batch + dec_batch
    n_c_pad = ((n_c + 7) // 8) * 8
    c_all = jnp.concatenate([c_prompt, c_sample, jnp.zeros((n_c_pad - n_c, D_MODEL), F32)], axis=0)
    mod = _ada(c_all, w_ada[0], b_ada[0])

    def smod(i, j):
        return mod[3 * i + j, batch:n_c].reshape(1, dec_batch, D_MODEL)

    ln_gb = jnp.stack([ln_g[0], ln_b[0]], axis=1)
    ffn = functools.partial(_ffn, tm=FFN_ROWS, rows_per_mod=seq, extra_row0=batch)

    w_up0 = w_ffn_up[0, 0].astype(BF16)
    w_down0 = w_ffn_down[0, 0].astype(BF16)
    later = ((w_ffn_up, (0, 1), 16), (w_ffn_down, (0, 1), 64), (w_in, (0,), 16), (w_o, (0,), 16))
    xp, xs, w_up1, w_down1, w_in_b, w_o_b = ffn(xp, mod, 0, w_up0, w_down0, ln_gb[0], extra=xs, side=later)

    xp, k_p, v_p = _pmix(xp, mod, 1, w_in_b, w_o_b, attn_sinks[0], _rope_tables(np.arange(seq)),
                         gmlp_norm_g[0], gmlp_norm_b[0], w_spatial[0], b_spatial[0], out_norm_g[0], ln_gb[1],
                         batch=batch, tm=MIX_TILE)

    ps = _inproj(xs, smod(1, 0), smod(1, 1), w_in_b, tm=dec_batch, rows_per_mod=dec_batch)
    eye = jnp.eye(N_KV_HEADS, dtype=F32)
    q5 = ps[:, :Q_END].reshape(dec_batch, N_KV_HEADS, GQA_GROUP, 1, HEAD_DIM)
    qe = (q5 * eye[None, :, None, :, None]).reshape(dec_batch, N_HEADS, KV_WIDTH)
    ck = cache_k_win[0].reshape(dec_batch, buf, KV_WIDTH)
    cv = cache_v_win[0].reshape(dec_batch, buf, KV_WIDTH)
    oe, k_s, v_s = _sattn(qe, ps.reshape(dec_batch, 1, IN_WIDTH), ck, cv, attn_sinks[0],
                          _rope_tables(np.full((1,), PAST_LEN)))
    attn_s = oe.reshape(dec_batch, N_KV_HEADS, GQA_GROUP, N_KV_HEADS, HEAD_DIM).sum(axis=3).reshape(dec_batch, ATTN_WIDTH)
    merged_s, vn_s = _smerge(attn_s, ps, gmlp_norm_g[0], gmlp_norm_b[0], w_spatial[0], b_spatial[0], out_norm_g[0])
    xs = _outproj(merged_s, xs, smod(1, 2), w_o_b, ln_g[0, 1], ln_b[0, 1], tm=dec_batch, rows_per_mod=dec_batch)

    xp, xs = ffn(xp, mod, 2, w_up1, w_down1, ln_gb[2], extra=xs)

    return (
        xp.reshape(batch, seq, D_MODEL),
        xs.reshape(dec_batch, 1, D_MODEL),
        k_p.reshape(1, batch, WINDOW, N_KV_HEADS, HEAD_DIM),
        v_p.reshape(1, batch, WINDOW, N_KV_HEADS, HEAD_DIM),
        k_s.reshape(1, dec_batch, buf, N_KV_HEADS, HEAD_DIM),
        v_s.reshape(1, dec_batch, buf, N_KV_HEADS, HEAD_DIM),
        vn_s.reshape(1, dec_batch, 1, GMLP_WIDTH),
    )
```

```python
import functools

import jax
import jax.numpy as jnp
import numpy as np
from jax import lax
from jax.experimental import pallas as pl
from jax.experimental.pallas import tpu as pltpu

D_MODEL = 2048
SEQ = 2048
PAST_LEN = 16384
ATTN_WIDTH = 1024
GMLP_WIDTH = 1024
HEAD_DIM = 64
N_HEADS = 16
N_KV_HEADS = 4
GQA_GROUP = 4
KV_WIDTH = 256
WINDOW = 128
ROPE_THETA = 500000.0
ROPE_DIM = 16
CHUNK = 128
N_GMLP_GROUPS = 8
D_FF = 5504
N_SUB = 3
DEPTH = 1
ALPHA = (2.0 * DEPTH) ** 0.25
LN_EPS = 1e-5
Q_END = ATTN_WIDTH
K_END = Q_END + KV_WIDTH
V_END = K_END + KV_WIDTH
U_END = V_END + GMLP_WIDTH
IN_WIDTH = U_END + GMLP_WIDTH

LANES = 128
MXU_COLS = 256
FF_TILE = 512
N_FF_STEPS = -(-D_FF // FF_TILE)
FF_LAST_OFF = D_FF - FF_TILE
FF_OVERLAP = N_FF_STEPS * FF_TILE - D_FF
OUT_TILE = 512
N_OUT_STEPS = D_MODEL // OUT_TILE
FFN_ROWS = 512
LN_PIECE_ROWS = 64
MIX_TILE = 128
PROJ_TILE = 512
SAMPLE_ROWS_PER_STEP = 8
VMEM_LIMIT = 60 * 1024 * 1024

BF16 = jnp.bfloat16
F32 = jnp.float32


def _dot(a, b):
    return jnp.dot(a, b, preferred_element_type=F32)


def _dot_nt(a, b):
    return lax.dot_general(a, b, (((1,), (1,)), ((), ())), preferred_element_type=F32)


def _layer_norm(y, g, b):
    mu = jnp.mean(y, axis=-1, keepdims=True)
    d = y - mu
    var = jnp.mean(d * d, axis=-1, keepdims=True)
    return d * lax.rsqrt(var + LN_EPS) * g + b


def _rms_norm(y, g):
    return y * lax.rsqrt(jnp.mean(y * y, axis=-1, keepdims=True) + LN_EPS) * g


def _gelu_tanh(x):
    c = np.float32(np.sqrt(2.0 / np.pi))
    return 0.5 * x * (1.0 + jnp.tanh(c * (x + 0.044715 * (x * x * x))))


def _rope_lanes(t, cos, sin_lo, sin_hi):
    pieces = []
    for c in range(t.shape[-1] // LANES):
        x = t[:, c * LANES:(c + 1) * LANES]
        pieces.append(x * cos + pltpu.roll(x, 8, 1) * sin_hi + pltpu.roll(x, LANES - 8, 1) * sin_lo)
    return pieces[0] if len(pieces) == 1 else jnp.concatenate(pieces, axis=-1)


def _ada_kernel(c_ref, w_ref, b_ref, o_ref):
    c = c_ref[...]
    h = (c * jax.nn.sigmoid(c)).astype(BF16)
    o_ref[0] = _dot(h, w_ref[...].astype(BF16)) + b_ref[...]


def _ada(c, w_ada, b_ada):
    rows = c.shape[0]
    n = w_ada.shape[1]
    tn = 1024
    per_plane = D_MODEL // tn
    return pl.pallas_call(
        _ada_kernel,
        grid=(n // tn,),
        in_specs=[
            pl.BlockSpec((rows, D_MODEL), lambda j: (0, 0)),
            pl.BlockSpec((D_MODEL, tn), lambda j: (0, j)),
            pl.BlockSpec((1, tn), lambda j: (0, j)),
        ],
        out_specs=pl.BlockSpec((1, rows, tn), lambda j: (j // per_plane, 0, j % per_plane)),
        out_shape=jax.ShapeDtypeStruct((n // D_MODEL, rows, D_MODEL), F32),
        compiler_params=pltpu.CompilerParams(dimension_semantics=("arbitrary",), vmem_limit_bytes=VMEM_LIMIT),
        name="ada_mod",
    )(c, w_ada, b_ada.reshape(1, n))


def _ff_offset(step):
    return jnp.minimum(step * FF_TILE, FF_LAST_OFF)


def _ffn_kernel(n_side, n_extra, tiles_per_mod, extra_row0, n_tiles, *refs):
    x_ref, mod_ref, wv_ref, wg_ref, wd_ref, gb_ref = refs[:6]
    n_in = 6
    if n_extra:
        xe_ref = refs[n_in]
        n_in += 1
        erows = slice(extra_row0, extra_row0 + n_extra)
    side_in = refs[n_in:n_in + n_side]
    n_in += n_side
    o_ref = refs[n_in]
    n_out = 1
    if n_extra:
        oe_ref = refs[n_in + 1]
        n_out = 2
    side_out = refs[n_in + n_out:n_in + n_out + n_side]
    xm_ref, a_ref, y_ref = refs[n_in + n_out + n_side:]
    i = pl.program_id(0)
    s = pl.program_id(1)
    tm = x_ref.shape[0]
    last = pl.num_programs(1) - 1

    def cast_side():
        for src, dst in zip(side_in, side_out):
            dst[...] = src[...].astype(BF16)

    def norm_piece():
        r0 = pl.multiple_of(jnp.minimum(s, tm // LN_PIECE_ROWS - 1) * LN_PIECE_ROWS, LN_PIECE_ROWS)
        piece = pl.ds(r0, LN_PIECE_ROWS)
        y = jnp.concatenate([y_ref[n, piece, :] for n in range(N_OUT_STEPS)], axis=-1)
        o_ref[piece, :] = _layer_norm(y, gb_ref[0:1], gb_ref[1:2])

    def up_chunk(rows, norm_previous):
        cast_side()
        if norm_previous:
            norm_piece()
        xm = xm_ref[0:rows]
        for c in range(FF_TILE // MXU_COLS):
            sl = slice(c * MXU_COLS, (c + 1) * MXU_COLS)
            hv = _dot(xm, wv_ref[:, sl])
            hg = _dot(xm, wg_ref[:, sl])
            a = hg * jax.nn.sigmoid(hg) * hv
            if c * MXU_COLS < FF_OVERLAP:
                col = lax.broadcasted_iota(jnp.int32, a.shape, 1) + c * MXU_COLS
                a = jnp.where(jnp.logical_and(s == N_FF_STEPS - 1, col < FF_OVERLAP), 0.0, a)
            a_ref[s, 0:rows, sl] = a.astype(BF16)

    def down_slab(rows):
        cast_side()
        acc = None
        for f in range(N_FF_STEPS):
            r0 = min(f * FF_TILE, FF_LAST_OFF)
            part = _dot(a_ref[f, 0:rows], wd_ref[r0:r0 + FF_TILE, :])
            acc = part if acc is None else acc + part
        y_ref[s - N_FF_STEPS, 0:rows] = acc

    first = i == 0
    later = i > 0
    up_phase = s < N_FF_STEPS
    down_phase = s >= N_FF_STEPS
    seq = pl.ds(i // tiles_per_mod, 1)

    @pl.when(s == 0)
    def _():
        xm_ref[0:tm] = (x_ref[...] * (1.0 + mod_ref[1, seq, :]) + mod_ref[0, seq, :]).astype(BF16)

    if n_extra:
        @pl.when(jnp.logical_and(s == 0, first))
        def _():
            xm_ref[tm:tm + n_extra] = (xe_ref[...] * (1.0 + mod_ref[1, erows, :]) + mod_ref[0, erows, :]).astype(BF16)

    pl.when(jnp.logical_and(up_phase, first))(functools.partial(up_chunk, tm + n_extra, False))
    pl.when(jnp.logical_and(up_phase, later))(functools.partial(up_chunk, tm, True))
    pl.when(jnp.logical_and(down_phase, first))(functools.partial(down_slab, tm + n_extra))
    pl.when(jnp.logical_and(down_phase, later))(functools.partial(down_slab, tm))

    @pl.when(s == last)
    def _():
        for n in range(N_OUT_STEPS):
            sl = slice(n * OUT_TILE, (n + 1) * OUT_TILE)
            y_ref[n, 0:tm] = ALPHA * x_ref[:, sl] + 0.5 * mod_ref[2, seq, sl] * y_ref[n, 0:tm]

    @pl.when(jnp.logical_and(s == last, i == n_tiles - 1))
    def _():
        y = jnp.concatenate([y_ref[n, 0:tm] for n in range(N_OUT_STEPS)], axis=-1)
        o_ref[...] = _layer_norm(y, gb_ref[0:1], gb_ref[1:2])

    if n_extra:
        @pl.when(jnp.logical_and(s == last, first))
        def _():
            mixed = jnp.concatenate([y_ref[n, tm:tm + n_extra] for n in range(N_OUT_STEPS)], axis=-1)
            oe_ref[...] = _layer_norm(ALPHA * xe_ref[...] + 0.5 * mod_ref[2, erows, :] * mixed, gb_ref[0:1], gb_ref[1:2])


def _ffn(x, mod, sub, w_up, w_down, ln_gb, *, tm, rows_per_mod, extra=None, extra_row0=0, side=()):
    m = x.shape[0]
    n_steps = N_FF_STEPS + N_OUT_STEPS
    tiles_per_mod = rows_per_mod // tm
    n_extra = 0 if extra is None else extra.shape[0]
    const2 = lambda i, s: (0, 0)
    extra_spec = pl.BlockSpec((n_extra, D_MODEL), const2)
    side_in_specs, side_out_specs, side_shapes = [], [], []
    for arr, lead, rows in side:
        n_rows, n_cols = arr.shape[-2:]
        n_blocks = n_rows // rows
        assert n_blocks * rows == n_rows and n_blocks <= (m // tm) * n_steps
        blk = lambda i, s, n_blocks=n_blocks: jnp.minimum(i * n_steps + s, n_blocks - 1)
        side_in_specs.append(pl.BlockSpec((None,) * len(lead) + (rows, n_cols),
                                          lambda i, s, lead=lead, blk=blk: (*lead, blk(i, s), 0)))
        side_out_specs.append(pl.BlockSpec((rows, n_cols), lambda i, s, blk=blk: (blk(i, s), 0)))
        side_shapes.append(jax.ShapeDtypeStruct((n_rows, n_cols), BF16))
    n_tiles = m // tm
    assert tm % LN_PIECE_ROWS == 0 and tm // LN_PIECE_ROWS <= N_FF_STEPS
    wd_block = lambda i, s: (0, jnp.where(s < N_FF_STEPS, N_OUT_STEPS - 1, s - N_FF_STEPS))
    out_block = lambda i, s: (jnp.where(s < N_FF_STEPS, jnp.maximum(i - 1, 0), i), 0)
    outs = pl.pallas_call(
        functools.partial(_ffn_kernel, len(side), n_extra, tiles_per_mod, extra_row0, n_tiles),
        grid=(n_tiles, n_steps),
        in_specs=[
            pl.BlockSpec((tm, D_MODEL), lambda i, s: (i, 0)),
            pl.BlockSpec((3, mod.shape[1], D_MODEL), lambda i, s: (sub, 0, 0)),
            pl.BlockSpec((pl.Element(D_MODEL), pl.Element(FF_TILE)),
                         lambda i, s: (0, pl.multiple_of(_ff_offset(s), LANES))),
            pl.BlockSpec((pl.Element(D_MODEL), pl.Element(FF_TILE)),
                         lambda i, s: (0, pl.multiple_of(D_FF + _ff_offset(s), LANES))),
            pl.BlockSpec((D_FF, OUT_TILE), wd_block),
            pl.BlockSpec((2, D_MODEL), const2),
            *([extra_spec] if n_extra else []),
            *side_in_specs,
        ],
        out_specs=[pl.BlockSpec((tm, D_MODEL), out_block), *([extra_spec] if n_extra else []),
                   *side_out_specs],
        out_shape=[jax.ShapeDtypeStruct((m, D_MODEL), F32),
                   *([jax.ShapeDtypeStruct((n_extra, D_MODEL), F32)] if n_extra else []), *side_shapes],
        scratch_shapes=[
            pltpu.VMEM((tm + n_extra, D_MODEL), BF16),
            pltpu.VMEM((N_FF_STEPS, tm + n_extra, FF_TILE), BF16),
            pltpu.VMEM((N_OUT_STEPS, tm + n_extra, OUT_TILE), F32),
        ],
        compiler_params=pltpu.CompilerParams(dimension_semantics=("arbitrary", "arbitrary"), vmem_limit_bytes=VMEM_LIMIT),
        name="swiglu_ln",
    )(x, mod, w_up, w_up, w_down, ln_gb, *([extra] if n_extra else []), *[arr for arr, _, _ in side])
    return outs


def _inproj_kernel(x_ref, shift_ref, scale_ref, w_ref, o_ref, xm_ref):
    @pl.when(pl.program_id(1) == 0)
    def _():
        xm_ref[...] = (x_ref[...] * (1.0 + scale_ref[0]) + shift_ref[0]).astype(BF16)

    o_ref[...] = _dot(xm_ref[...], w_ref[...])


def _inproj(x, shift, scale, w_in, *, tm, rows_per_mod):
    m = x.shape[0]
    r = shift.shape[1]
    tn = 512
    tiles_per_mod = rows_per_mod // tm
    mod_spec = pl.BlockSpec((1, r, D_MODEL), lambda i, j: (i // tiles_per_mod, 0, 0))
    return pl.pallas_call(
        _inproj_kernel,
        grid=(m // tm, IN_WIDTH // tn),
        in_specs=[
            pl.BlockSpec((tm, D_MODEL), lambda i, j: (i, 0)),
            mod_spec, mod_spec,
            pl.BlockSpec((D_MODEL, tn), lambda i, j: (0, j)),
        ],
        out_specs=pl.BlockSpec((tm, tn), lambda i, j: (i, j)),
        out_shape=jax.ShapeDtypeStruct((m, IN_WIDTH), F32),
        scratch_shapes=[pltpu.VMEM((tm, D_MODEL), BF16)],
        compiler_params=pltpu.CompilerParams(dimension_semantics=("arbitrary", "arbitrary"), vmem_limit_bytes=VMEM_LIMIT),
        name="mixer_inproj",
    )(x, shift, scale, w_in)


def _outproj_kernel(m_ref, x_ref, gate_ref, w_ref, g_ref, b_ref, o_ref):
    mixed = _dot(m_ref[...], w_ref[...])
    y = ALPHA * x_ref[...] + gate_ref[0] * mixed
    o_ref[...] = _layer_norm(y, g_ref[...], b_ref[...])


def _outproj(merged, x, gate, w_o, ln_g, ln_b, *, tm, rows_per_mod):
    m = x.shape[0]
    r = gate.shape[1]
    tiles_per_mod = rows_per_mod // tm
    return pl.pallas_call(
        _outproj_kernel,
        grid=(m // tm,),
        in_specs=[
            pl.BlockSpec((tm, D_MODEL), lambda i: (i, 0)),
            pl.BlockSpec((tm, D_MODEL), lambda i: (i, 0)),
            pl.BlockSpec((1, r, D_MODEL), lambda i: (i // tiles_per_mod, 0, 0)),
            pl.BlockSpec((D_MODEL, D_MODEL), lambda i: (0, 0)),
            pl.BlockSpec((1, D_MODEL), lambda i: (0, 0)),
            pl.BlockSpec((1, D_MODEL), lambda i: (0, 0)),
        ],
        out_specs=pl.BlockSpec((tm, D_MODEL), lambda i: (i, 0)),
        out_shape=jax.ShapeDtypeStruct((m, D_MODEL), F32),
        compiler_params=pltpu.CompilerParams(dimension_semantics=("arbitrary",), vmem_limit_bytes=VMEM_LIMIT),
        name="mixer_outproj_ln",
    )(merged, x, gate, w_o, ln_g.reshape(1, D_MODEL), ln_b.reshape(1, D_MODEL))


def _gmlp_norm(pv, gng, gnb):
    v = _gelu_tanh(pv)
    outs = []
    for g in range(N_GMLP_GROUPS):
        sl = slice(g * LANES, (g + 1) * LANES)
        outs.append(_layer_norm(v[:, sl], gng[:, sl], gnb[:, sl]))
    return outs


def _half_lane_tiles(t):
    lo_lane = lax.broadcasted_iota(jnp.int32, (t.shape[0], LANES), 1) < HEAD_DIM
    lo, hi = [], []
    for c in range(KV_WIDTH // LANES):
        col = t[:, c * LANES:(c + 1) * LANES]
        swp = pltpu.roll(col, HEAD_DIM, 1)
        lo += [jnp.where(lo_lane, col, 0.0), jnp.where(lo_lane, swp, 0.0)]
        hi += [jnp.where(lo_lane, 0.0, swp), jnp.where(lo_lane, 0.0, col)]
    return jnp.concatenate(lo + hi, axis=-1).astype(BF16)


def _band_attention(q_s, kx_s, vx_s, attn_s, sinks_ref, nblk, first_tile):
    rows = 2 * WINDOW
    qi = lax.broadcasted_iota(jnp.int32, (rows, WINDOW), 0) & (WINDOW - 1)
    kj = lax.broadcasted_iota(jnp.int32, (rows, WINDOW), 1)
    own = kj <= qi
    first_mask = kj <= jnp.where(first_tile, qi, WINDOW)
    ones = jnp.ones((2 * WINDOW, LANES), BF16)
    units = [(wb, g, hf) for wb in range(nblk) for g in range(N_KV_HEADS) for hf in range(2)]

    def kv_cols(g, hf):
        return slice(hf * 4 * LANES + g * LANES, hf * 4 * LANES + (g + 1) * LANES)

    scores = []
    for wb, g, hf in units:
        r0 = wb * WINDOW
        q2 = jnp.concatenate([q_s[r0:r0 + WINDOW, (2 * g) * LANES:(2 * g + 1) * LANES],
                              q_s[r0:r0 + WINDOW, (2 * g + 1) * LANES:(2 * g + 2) * LANES]], axis=0)
        s = _dot_nt(q2, kx_s[r0:r0 + 2 * WINDOW, kv_cols(g, hf)])
        c = jnp.where(own, s[:, WINDOW:2 * WINDOW], s[:, 0:WINDOW])
        if wb == 0:
            c = jnp.where(first_mask, c, -jnp.inf)
        scores.append(c)

    probs = []
    for (wb, g, hf), c in zip(units, scores):
        sink = jnp.concatenate([jnp.full((WINDOW, LANES), sinks_ref[GQA_GROUP * g + hf], F32),
                                jnp.full((WINDOW, LANES), sinks_ref[GQA_GROUP * g + 2 + hf], F32)], axis=0)
        mx = jnp.maximum(jnp.broadcast_to(jnp.max(c, axis=-1, keepdims=True), c.shape), sink)
        p = jnp.exp(c - mx)
        pcat = jnp.concatenate([jnp.where(own, 0.0, p), jnp.where(own, p, 0.0)], axis=-1).astype(BF16)
        probs.append((pcat, jnp.exp(sink - mx)))

    outs = {}
    for (wb, g, hf), (pcat, esink) in zip(units, probs):
        r0 = wb * WINDOW
        v2 = jnp.concatenate([vx_s[r0:r0 + 2 * WINDOW, kv_cols(g, hf)], ones], axis=-1)
        o = _dot(pcat, v2)
        outs[wb, g, hf] = o[:, 0:LANES] / (o[:, LANES:2 * LANES] + esink)

    for wb in range(nblk):
        r0 = wb * WINDOW
        for g in range(N_KV_HEADS):
            both = outs[wb, g, 0] + outs[wb, g, 1]
            attn_s[r0:r0 + WINDOW, (2 * g) * LANES:(2 * g + 1) * LANES] = both[0:WINDOW]
            attn_s[r0:r0 + WINDOW, (2 * g + 1) * LANES:(2 * g + 2) * LANES] = both[WINDOW:2 * WINDOW]


def _pmix_kernel(tiles_per_seq, n_tiles, sinks_ref, x_ref, mod_ref, win_ref, wo_ref, cos_ref, slo_ref, shi_ref,
                 gng_ref, gnb_ref, ws_ref, bs_ref, og_ref, lngb_ref,
                 o_ref, k_ref, v_ref, p_s, q_s, kk_s, vv_s, kp_s, vp_s, attn_s, m_s, wm_s, y_s):
    n = pl.program_id(0)
    t = n % tiles_per_seq
    tm = x_ref.shape[0]
    nblk = tm // WINDOW

    @pl.when(n == 0)
    def _():
        y_s[...] = jnp.zeros(y_s.shape, F32)
        row = lax.broadcasted_iota(jnp.int32, (CHUNK, CHUNK), 0)
        col = lax.broadcasted_iota(jnp.int32, (CHUNK, CHUNK), 1)
        for g in range(N_GMLP_GROUPS):
            wm_s[g] = jnp.where(col <= row, ws_ref[g], 0.0).astype(BF16)

    @pl.when(t == 0)
    def _():
        kp_s[...] = jnp.zeros(kp_s.shape, BF16)
        vp_s[...] = jnp.zeros(vp_s.shape, BF16)

    @pl.when(n == n_tiles)
    def _():
        o_ref[...] = _layer_norm(y_s[...], lngb_ref[0:1], lngb_ref[1:2])

    pl.when(n < n_tiles)(functools.partial(
        _pmix_tile, t, pl.ds(n // tiles_per_seq, 1), sinks_ref, x_ref, mod_ref, win_ref, wo_ref, cos_ref, slo_ref,
        shi_ref, gng_ref, gnb_ref, bs_ref, og_ref, lngb_ref, o_ref, k_ref, v_ref, p_s, q_s, kk_s, vv_s, kp_s, vp_s,
        attn_s, m_s, wm_s, y_s))


def _pmix_tile(t, seq, sinks_ref, x_ref, mod_ref, win_ref, wo_ref, cos_ref, slo_ref, shi_ref, gng_ref, gnb_ref,
               bs_ref, og_ref, lngb_ref, o_ref, k_ref, v_ref, p_s, q_s, kk_s, vv_s, kp_s, vp_s, attn_s, m_s, wm_s, y_s):
    tm = x_ref.shape[0]
    nblk = tm // WINDOW

    o_ref[...] = _layer_norm(y_s[...], lngb_ref[0:1], lngb_ref[1:2])

    m_s[...] = (x_ref[...] * (1.0 + mod_ref[1, seq, :]) + mod_ref[0, seq, :]).astype(BF16)
    cos, slo, shi = cos_ref[...], slo_ref[...], shi_ref[...]
    for c in range(IN_WIDTH // PROJ_TILE):
        lo_col = c * PROJ_TILE
        h = _dot(m_s[...], win_ref[:, lo_col:lo_col + PROJ_TILE])
        if lo_col + PROJ_TILE <= Q_END:
            q_s[:, lo_col:lo_col + PROJ_TILE] = (_rope_lanes(h, cos, slo, shi) * (HEAD_DIM ** -0.5)).astype(BF16)
        elif lo_col == Q_END and PROJ_TILE == 2 * KV_WIDTH:
            k = _rope_lanes(h[:, 0:KV_WIDTH], cos, slo, shi)
            v = h[:, KV_WIDTH:2 * KV_WIDTH]
            k_ref[0] = k[tm - WINDOW:tm]
            v_ref[0] = v[tm - WINDOW:tm]
            kb, vb = _half_lane_tiles(k), _half_lane_tiles(v)
            kk_s[0:WINDOW, :] = kp_s[...]
            vv_s[0:WINDOW, :] = vp_s[...]
            kk_s[WINDOW:WINDOW + tm, :] = kb
            vv_s[WINDOW:WINDOW + tm, :] = vb
            kp_s[...] = kb[tm - WINDOW:tm]
            vp_s[...] = vb[tm - WINDOW:tm]
        else:
            p_s[:, lo_col - V_END:lo_col - V_END + PROJ_TILE] = h

    _band_attention(q_s, kk_s, vv_s, attn_s, sinks_ref, nblk, t == 0)

    vn = _gmlp_norm(p_s[:, GMLP_WIDTH:2 * GMLP_WIDTH], gng_ref[...], gnb_ref[...])
    gated = []
    for g in range(N_GMLP_GROUPS):
        u = _gelu_tanh(p_s[:, g * LANES:(g + 1) * LANES])
        vg = vn[g].astype(BF16)
        cols = jnp.concatenate([vg[wb * CHUNK:(wb + 1) * CHUNK] for wb in range(nblk)], axis=-1)
        mix = _dot(wm_s[g], cols)
        mix = jnp.concatenate([mix[:, wb * LANES:(wb + 1) * LANES] for wb in range(nblk)], axis=0)
        bias = jnp.concatenate([bs_ref[:, g:g + 1]] * nblk, axis=0)
        gated.append(u * (mix + bias))
    gm = jnp.concatenate(gated, axis=-1)
    og = og_ref[...]
    m_s[:, 0:ATTN_WIDTH] = _rms_norm(attn_s[...], og[:, 0:ATTN_WIDTH]).astype(BF16)
    m_s[:, ATTN_WIDTH:] = _rms_norm(gm, og[:, ATTN_WIDTH:]).astype(BF16)

    for c in range(D_MODEL // PROJ_TILE):
        sl = slice(c * PROJ_TILE, (c + 1) * PROJ_TILE)
        y_s[:, sl] = ALPHA * x_ref[:, sl] + mod_ref[2, seq, sl] * _dot(m_s[...], wo_ref[:, sl])


def _rope_tables(pos):
    half = ROPE_DIM // 2
    inv = ROPE_THETA ** (-(np.arange(half, dtype=np.float64) * 2.0) / ROPE_DIM)
    ang = np.asarray(pos, np.float64)[:, None] * inv[None, :]
    cos, sin = np.cos(ang), np.sin(ang)
    n = ang.shape[0]
    one = np.ones((n, HEAD_DIM - ROPE_DIM))
    zero = np.zeros((n, HEAD_DIM - ROPE_DIM))
    z8 = np.zeros((n, half))
    cos_t = np.concatenate([cos, cos, one], axis=-1)
    sin_lo = np.concatenate([-sin, z8, zero], axis=-1)
    sin_hi = np.concatenate([z8, sin, zero], axis=-1)
    rep = LANES // HEAD_DIM
    return tuple(jnp.asarray(np.tile(t, (1, rep)), F32) for t in (cos_t, sin_lo, sin_hi))


def _pmix(x, mod, sub, w_in, w_o, sinks, tables, gn_g, gn_b, w_s, b_s, out_g, ln_gb, *, batch, tm):
    nt = SEQ // tm
    n_tiles = batch * nt
    tile = lambda n: jnp.minimum(n, n_tiles - 1)
    cos_t, sin_lo, sin_hi = tables
    tab_spec = pl.BlockSpec((tm, LANES), lambda n: (tile(n) % nt, 0))
    full2 = lambda n: (0, 0)
    mod_spec = pl.BlockSpec((3, mod.shape[1], D_MODEL), lambda n: (sub, 0, 0))
    kv_spec = pl.BlockSpec((1, WINDOW, KV_WIDTH), lambda n: (tile(n) // nt, 0, 0))
    resident = dict(pipeline_mode=pl.Buffered(1))
    return pl.pallas_call(
        functools.partial(_pmix_kernel, nt, n_tiles),
        grid=(n_tiles + 1,),
        in_specs=[
            pl.BlockSpec(memory_space=pltpu.SMEM),
            pl.BlockSpec((tm, D_MODEL), lambda n: (tile(n), 0)),
            mod_spec,
            pl.BlockSpec((D_MODEL, IN_WIDTH), full2, **resident),
            pl.BlockSpec((D_MODEL, D_MODEL), full2, **resident),
            tab_spec, tab_spec, tab_spec,
            pl.BlockSpec((1, GMLP_WIDTH), full2),
            pl.BlockSpec((1, GMLP_WIDTH), full2),
            pl.BlockSpec((N_GMLP_GROUPS, CHUNK, CHUNK), lambda n: (0, 0, 0)),
            pl.BlockSpec((CHUNK, N_GMLP_GROUPS), full2),
            pl.BlockSpec((1, D_MODEL), full2),
            pl.BlockSpec((2, D_MODEL), full2),
        ],
        out_specs=[
            pl.BlockSpec((tm, D_MODEL), lambda n: (jnp.maximum(n - 1, 0), 0)),
            kv_spec, kv_spec,
        ],
        out_shape=[
            jax.ShapeDtypeStruct((batch * SEQ, D_MODEL), F32),
            jax.ShapeDtypeStruct((batch, WINDOW, KV_WIDTH), F32),
            jax.ShapeDtypeStruct((batch, WINDOW, KV_WIDTH), F32),
        ],
        scratch_shapes=[
            pltpu.VMEM((tm, 2 * GMLP_WIDTH), F32),
            pltpu.VMEM((tm, ATTN_WIDTH), BF16),
            pltpu.VMEM((WINDOW + tm, 4 * KV_WIDTH), BF16),
            pltpu.VMEM((WINDOW + tm, 4 * KV_WIDTH), BF16),
            pltpu.VMEM((WINDOW, 4 * KV_WIDTH), BF16),
            pltpu.VMEM((WINDOW, 4 * KV_WIDTH), BF16),
            pltpu.VMEM((tm, ATTN_WIDTH), F32),
            pltpu.VMEM((tm, D_MODEL), BF16),
            pltpu.VMEM((N_GMLP_GROUPS, CHUNK, CHUNK), BF16),
            pltpu.VMEM((tm, D_MODEL), F32),
        ],
        compiler_params=pltpu.CompilerParams(dimension_semantics=("arbitrary",), vmem_limit_bytes=VMEM_LIMIT),
        name="prompt_mixer",
    )(sinks, x, mod, w_in, w_o, cos_t, sin_lo, sin_hi, gn_g.reshape(1, GMLP_WIDTH),
      gn_b.reshape(1, GMLP_WIDTH), w_s, b_s.T, out_g.reshape(1, D_MODEL), ln_gb)


def _sattn_kernel(qe_ref, p_ref, ck_ref, cv_ref, sink_ref, cos_ref, slo_ref, shi_ref, o_ref, ko_ref, vo_ref):
    cos, slo, shi = cos_ref[...], slo_ref[...], shi_ref[...]
    sink = sink_ref[...]
    for r in range(qe_ref.shape[0]):
        qe = _rope_lanes(qe_ref[r], cos, slo, shi) * (HEAD_DIM ** -0.5)
        kn = _rope_lanes(p_ref[r, :, Q_END:K_END], cos, slo, shi)
        vn = p_ref[r, :, K_END:V_END]
        ck = ck_ref[r]
        cv = cv_ref[r]
        s_c = _dot_nt(qe.astype(BF16), ck.astype(BF16))
        key = lax.broadcasted_iota(jnp.int32, s_c.shape, 1)
        s_c = jnp.where(key >= 1, s_c, -jnp.inf)
        s_n = jnp.sum(qe * kn, axis=-1, keepdims=True)
        mx = jnp.maximum(jnp.maximum(jnp.max(s_c, axis=-1, keepdims=True), s_n), sink)
        p_c = jnp.exp(s_c - mx)
        p_n = jnp.exp(s_n - mx)
        denom = jnp.sum(p_c, axis=-1, keepdims=True) + p_n + jnp.exp(sink - mx)
        o = (_dot(p_c.astype(BF16), cv.astype(BF16)) + p_n * vn) / denom
        head = lax.broadcasted_iota(jnp.int32, o.shape, 0)
        lane = lax.broadcasted_iota(jnp.int32, o.shape, 1)
        o_ref[r] = jnp.where(head // GQA_GROUP == lane // HEAD_DIM, o, 0.0)
        row = lax.broadcasted_iota(jnp.int32, ck.shape, 0)
        newest = row == WINDOW - 1
        ko_ref[r] = jnp.where(newest, kn, pltpu.roll(ck, WINDOW - 1, 0))
        vo_ref[r] = jnp.where(newest, vn, pltpu.roll(cv, WINDOW - 1, 0))


def _sattn(qe, p3, ck, cv, sinks, tables):
    b = qe.shape[0]
    rows = SAMPLE_ROWS_PER_STEP
    cos_t, sin_lo, sin_hi = tables
    tab_spec = pl.BlockSpec((1, LANES), lambda i: (0, 0))
    blk3 = lambda i: (i, 0, 0)
    return pl.pallas_call(
        _sattn_kernel,
        grid=(b // rows,),
        in_specs=[
            pl.BlockSpec((rows, N_HEADS, KV_WIDTH), blk3),
            pl.BlockSpec((rows, 1, IN_WIDTH), blk3),
            pl.BlockSpec((rows, WINDOW, KV_WIDTH), blk3),
            pl.BlockSpec((rows, WINDOW, KV_WIDTH), blk3),
            pl.BlockSpec((N_HEADS, 1), lambda i: (0, 0)),
            tab_spec, tab_spec, tab_spec,
        ],
        out_specs=[
            pl.BlockSpec((rows, N_HEADS, KV_WIDTH), blk3),
            pl.BlockSpec((rows, WINDOW, KV_WIDTH), blk3),
            pl.BlockSpec((rows, WINDOW, KV_WIDTH), blk3),
        ],
        out_shape=[
            jax.ShapeDtypeStruct((b, N_HEADS, KV_WIDTH), F32),
            jax.ShapeDtypeStruct((b, WINDOW, KV_WIDTH), F32),
            jax.ShapeDtypeStruct((b, WINDOW, KV_WIDTH), F32),
        ],
        compiler_params=pltpu.CompilerParams(dimension_semantics=("arbitrary",)),
        name="sample_attn",
    )(qe, p3, ck, cv, sinks.reshape(N_HEADS, 1), cos_t, sin_lo, sin_hi)


def _smerge_kernel(attn_ref, p_ref, gng_ref, gnb_ref, w0_ref, b0_ref, og_ref, m_ref, vn_ref):
    og = og_ref[...]
    vn = jnp.concatenate(_gmlp_norm(p_ref[:, U_END:IN_WIDTH], gng_ref[...], gnb_ref[...]), axis=-1)
    vn_ref[...] = vn
    u = _gelu_tanh(p_ref[:, V_END:U_END])
    gm = u * (w0_ref[...] * vn + b0_ref[...])
    m_ref[:, 0:ATTN_WIDTH] = _rms_norm(attn_ref[...], og[:, 0:ATTN_WIDTH]).astype(BF16)
    m_ref[:, ATTN_WIDTH:] = _rms_norm(gm, og[:, ATTN_WIDTH:]).astype(BF16)


def _smerge(attn, p, gn_g, gn_b, w_s, b_s, out_g):
    b = attn.shape[0]
    w0 = jnp.repeat(w_s[:, 0, 0], LANES).reshape(1, GMLP_WIDTH)
    b0 = jnp.repeat(b_s[:, 0], LANES).reshape(1, GMLP_WIDTH)
    return pl.pallas_call(
        _smerge_kernel,
        out_shape=[
            jax.ShapeDtypeStruct((b, D_MODEL), BF16),
            jax.ShapeDtypeStruct((b, GMLP_WIDTH), F32),
        ],
        name="sample_gmlp_merge",
    )(attn, p, gn_g.reshape(1, GMLP_WIDTH), gn_b.reshape(1, GMLP_WIDTH), w0, b0, out_g.reshape(1, D_MODEL))


def kernel(x_prompt, x_sample, cache_k_win, cache_v_win, c_prompt, c_sample, w_ada, b_ada, ln_g, ln_b,
           w_ffn_up, w_ffn_down, w_in, attn_sinks, gmlp_norm_g, gmlp_norm_b, w_spatial, b_spatial,
           out_norm_g, w_o):
    batch, seq, _ = x_prompt.shape
    dec_batch = x_sample.shape[0]
    buf = cache_k_win.shape[2]
    assert seq == SEQ and buf == WINDOW and x_sample.shape[1] == 1 and w_ada.shape[0] == DEPTH == 1

    xp = x_prompt.reshape(batch * seq, D_MODEL)
    xs = x_sample.reshape(dec_batch, D_MODEL)

    n_c = batch + dec_batch
    n_c_pad = ((n_c + 7) // 8) * 8
    c_all = jnp.concatenate([c_prompt, c_sample, jnp.zeros((n_c_pad - n_c, D_MODEL), F32)], axis=0)
    mod = _ada(c_all, w_ada[0], b_ada[0])

    def smod(i, j):
        return mod[3 * i + j, batch:n_c].reshape(1, dec_batch, D_MODEL)

    ln_gb = jnp.stack([ln_g[0], ln_b[0]], axis=1)
    ffn = functools.partial(_ffn, tm=FFN_ROWS, rows_per_mod=seq, extra_row0=batch)

    w_up0 = w_ffn_up[0, 0].astype(BF16)
    w_down0 = w_ffn_down[0, 0].astype(BF16)
    later = ((w_ffn_up, (0, 1), 16), (w_ffn_down, (0, 1), 32), (w_in, (0,), 16), (w_o, (0,), 16))
    xp, xs, w_up1, w_down1, w_in_b, w_o_b = ffn(xp, mod, 0, w_up0, w_down0, ln_gb[0], extra=xs, side=later)

    xp, k_p, v_p = _pmix(xp, mod, 1, w_in_b, w_o_b, attn_sinks[0], _rope_tables(np.arange(seq)),
                         gmlp_norm_g[0], gmlp_norm_b[0], w_spatial[0], b_spatial[0], out_norm_g[0], ln_gb[1],
                         batch=batch, tm=MIX_TILE)

    ps = _inproj(xs, smod(1, 0), smod(1, 1), w_in_b, tm=dec_batch, rows_per_mod=dec_batch)
    eye = jnp.eye(N_KV_HEADS, dtype=F32)
    q5 = ps[:, :Q_END].reshape(dec_batch, N_KV_HEADS, GQA_GROUP, 1, HEAD_DIM)
    qe = (q5 * eye[None, :, None, :, None]).reshape(dec_batch, N_HEADS, KV_WIDTH)
    ck = cache_k_win[0].reshape(dec_batch, buf, KV_WIDTH)
    cv = cache_v_win[0].reshape(dec_batch, buf, KV_WIDTH)
    oe, k_s, v_s = _sattn(qe, ps.reshape(dec_batch, 1, IN_WIDTH), ck, cv, attn_sinks[0],
                          _rope_tables(np.full((1,), PAST_LEN)))
    attn_s = oe.reshape(dec_batch, N_KV_HEADS, GQA_GROUP, N_KV_HEADS, HEAD_DIM).sum(axis=3).reshape(dec_batch, ATTN_WIDTH)
    merged_s, vn_s = _smerge(attn_s, ps, gmlp_norm_g[0], gmlp_norm_b[0], w_spatial[0], b_spatial[0], out_norm_g[0])
    xs = _outproj(merged_s, xs, smod(1, 2), w_o_b, ln_g[0, 1], ln_b[0, 1], tm=dec_batch, rows_per_mod=dec_batch)

    xp, xs = ffn(xp, mod, 2, w_up1, w_down1, ln_gb[2], extra=xs)

    return (
        xp.reshape(batch, seq, D_MODEL),
        xs.reshape(dec_batch, 1, D_MODEL),
        k_p.reshape(1, batch, WINDOW, N_KV_HEADS, HEAD_DIM),
        v_p.reshape(1, batch, WINDOW, N_KV_HEADS, HEAD_DIM),
        k_s.reshape(1, dec_batch, buf, N_KV_HEADS, HEAD_DIM),
        v_s.reshape(1, dec_batch, buf, N_KV_HEADS, HEAD_DIM),
        vn_s.reshape(1, dec_batch, 1, GMLP_WIDTH),
    )
```

```python
import functools

import jax
import jax.numpy as jnp
import numpy as np
from jax import lax
from jax.experimental import pallas as pl
from jax.experimental.pallas import tpu as pltpu

D_MODEL = 2048
SEQ = 2048
PAST_LEN = 16384
ATTN_WIDTH = 1024
GMLP_WIDTH = 1024
HEAD_DIM = 64
N_HEADS = 16
N_KV_HEADS = 4
GQA_GROUP = 4
KV_WIDTH = 256
WINDOW = 128
ROPE_THETA = 500000.0
ROPE_DIM = 16
CHUNK = 128
N_GMLP_GROUPS = 8
D_FF = 5504
N_SUB = 3
DEPTH = 1
ALPHA = (2.0 * DEPTH) ** 0.25
LN_EPS = 1e-5
Q_END = ATTN_WIDTH
K_END = Q_END + KV_WIDTH
V_END = K_END + KV_WIDTH
U_END = V_END + GMLP_WIDTH
IN_WIDTH = U_END + GMLP_WIDTH

LANES = 128
MXU_COLS = 256
FF_TILE = 512
N_FF_STEPS = -(-D_FF // FF_TILE)
FF_LAST_OFF = D_FF - FF_TILE
FF_OVERLAP = N_FF_STEPS * FF_TILE - D_FF
OUT_TILE = 512
N_OUT_STEPS = D_MODEL // OUT_TILE
FFN_ROWS = 512
LN_PIECE_ROWS = 64
MIX_TILE = 256
PROJ_TILE = 512
SAMPLE_ROWS_PER_STEP = 8
VMEM_LIMIT = 60 * 1024 * 1024

BF16 = jnp.bfloat16
F32 = jnp.float32


def _dot(a, b):
    return jnp.dot(a, b, preferred_element_type=F32)


def _dot_nt(a, b):
    return lax.dot_general(a, b, (((1,), (1,)), ((), ())), preferred_element_type=F32)


def _layer_norm(y, g, b):
    mu = jnp.mean(y, axis=-1, keepdims=True)
    d = y - mu
    var = jnp.mean(d * d, axis=-1, keepdims=True)
    return d * lax.rsqrt(var + LN_EPS) * g + b


def _rms_norm(y, g):
    return y * lax.rsqrt(jnp.mean(y * y, axis=-1, keepdims=True) + LN_EPS) * g


def _gelu_tanh(x):
    c = np.float32(np.sqrt(2.0 / np.pi))
    return 0.5 * x * (1.0 + jnp.tanh(c * (x + 0.044715 * (x * x * x))))


def _rope_lanes(t, cos, sin_lo, sin_hi):
    pieces = []
    for c in range(t.shape[-1] // LANES):
        x = t[:, c * LANES:(c + 1) * LANES]
        pieces.append(x * cos + pltpu.roll(x, 8, 1) * sin_hi + pltpu.roll(x, LANES - 8, 1) * sin_lo)
    return pieces[0] if len(pieces) == 1 else jnp.concatenate(pieces, axis=-1)


def _ada_kernel(c_ref, w_ref, b_ref, o_ref):
    c = c_ref[...]
    h = (c * jax.nn.sigmoid(c)).astype(BF16)
    o_ref[0] = _dot(h, w_ref[...].astype(BF16)) + b_ref[...]


def _ada(c, w_ada, b_ada):
    rows = c.shape[0]
    n = w_ada.shape[1]
    tn = 1024
    per_plane = D_MODEL // tn
    return pl.pallas_call(
        _ada_kernel,
        grid=(n // tn,),
        in_specs=[
            pl.BlockSpec((rows, D_MODEL), lambda j: (0, 0)),
            pl.BlockSpec((D_MODEL, tn), lambda j: (0, j)),
            pl.BlockSpec((1, tn), lambda j: (0, j)),
        ],
        out_specs=pl.BlockSpec((1, rows, tn), lambda j: (j // per_plane, 0, j % per_plane)),
        out_shape=jax.ShapeDtypeStruct((n // D_MODEL, rows, D_MODEL), F32),
        compiler_params=pltpu.CompilerParams(dimension_semantics=("arbitrary",), vmem_limit_bytes=VMEM_LIMIT),
        name="ada_mod",
    )(c, w_ada, b_ada.reshape(1, n))


def _ff_offset(step):
    return jnp.minimum(step * FF_TILE, FF_LAST_OFF)


def _ffn_kernel(n_side, n_extra, tiles_per_mod, extra_row0, n_tiles, *refs):
    x_ref, mod_ref, wv_ref, wg_ref, wd_ref, gb_ref = refs[:6]
    n_in = 6
    if n_extra:
        xe_ref = refs[n_in]
        n_in += 1
        erows = slice(extra_row0, extra_row0 + n_extra)
    side_in = refs[n_in:n_in + n_side]
    n_in += n_side
    o_ref = refs[n_in]
    n_out = 1
    if n_extra:
        oe_ref = refs[n_in + 1]
        n_out = 2
    side_out = refs[n_in + n_out:n_in + n_out + n_side]
    xm_ref, a_ref, y_ref = refs[n_in + n_out + n_side:]
    i = pl.program_id(0)
    s = pl.program_id(1)
    tm = x_ref.shape[0]
    last = pl.num_programs(1) - 1

    def cast_side():
        for src, dst in zip(side_in, side_out):
            dst[...] = src[...].astype(BF16)

    def norm_piece():
        r0 = pl.multiple_of(jnp.minimum(s, tm // LN_PIECE_ROWS - 1) * LN_PIECE_ROWS, LN_PIECE_ROWS)
        piece = pl.ds(r0, LN_PIECE_ROWS)
        y = jnp.concatenate([y_ref[n, piece, :] for n in range(N_OUT_STEPS)], axis=-1)
        o_ref[piece, :] = _layer_norm(y, gb_ref[0:1], gb_ref[1:2])

    def up_chunk(rows, norm_previous):
        cast_side()
        if norm_previous:
            norm_piece()
        xm = xm_ref[0:rows]
        for c in range(FF_TILE // MXU_COLS):
            sl = slice(c * MXU_COLS, (c + 1) * MXU_COLS)
            hv = _dot(xm, wv_ref[:, sl])
            hg = _dot(xm, wg_ref[:, sl])
            a = hg * jax.nn.sigmoid(hg) * hv
            if c * MXU_COLS < FF_OVERLAP:
                col = lax.broadcasted_iota(jnp.int32, a.shape, 1) + c * MXU_COLS
                a = jnp.where(jnp.logical_and(s == N_FF_STEPS - 1, col < FF_OVERLAP), 0.0, a)
            a_ref[s, 0:rows, sl] = a.astype(BF16)

    def down_slab(rows):
        cast_side()
        acc = None
        for f in range(N_FF_STEPS):
            r0 = min(f * FF_TILE, FF_LAST_OFF)
            part = _dot(a_ref[f, 0:rows], wd_ref[r0:r0 + FF_TILE, :])
            acc = part if acc is None else acc + part
        y_ref[s - N_FF_STEPS, 0:rows] = acc

    first = i == 0
    later = i > 0
    up_phase = s < N_FF_STEPS
    down_phase = s >= N_FF_STEPS
    seq = pl.ds(i // tiles_per_mod, 1)

    @pl.when(s == 0)
    def _():
        xm_ref[0:tm] = (x_ref[...] * (1.0 + mod_ref[1, seq, :]) + mod_ref[0, seq, :]).astype(BF16)

    if n_extra:
        @pl.when(jnp.logical_and(s == 0, first))
        def _():
            xm_ref[tm:tm + n_extra] = (xe_ref[...] * (1.0 + mod_ref[1, erows, :]) + mod_ref[0, erows, :]).astype(BF16)

    pl.when(jnp.logical_and(up_phase, first))(functools.partial(up_chunk, tm + n_extra, False))
    pl.when(jnp.logical_and(up_phase, later))(functools.partial(up_chunk, tm, True))
    pl.when(jnp.logical_and(down_phase, first))(functools.partial(down_slab, tm + n_extra))
    pl.when(jnp.logical_and(down_phase, later))(functools.partial(down_slab, tm))

    @pl.when(s == last)
    def _():
        for n in range(N_OUT_STEPS):
            sl = slice(n * OUT_TILE, (n + 1) * OUT_TILE)
            y_ref[n, 0:tm] = ALPHA * x_ref[:, sl] + 0.5 * mod_ref[2, seq, sl] * y_ref[n, 0:tm]

    @pl.when(jnp.logical_and(s == last, i == n_tiles - 1))
    def _():
        y = jnp.concatenate([y_ref[n, 0:tm] for n in range(N_OUT_STEPS)], axis=-1)
        o_ref[...] = _layer_norm(y, gb_ref[0:1], gb_ref[1:2])

    if n_extra:
        @pl.when(jnp.logical_and(s == last, first))
        def _():
            mixed = jnp.concatenate([y_ref[n, tm:tm + n_extra] for n in range(N_OUT_STEPS)], axis=-1)
            oe_ref[...] = _layer_norm(ALPHA * xe_ref[...] + 0.5 * mod_ref[2, erows, :] * mixed, gb_ref[0:1], gb_ref[1:2])


def _ffn(x, mod, sub, w_up, w_down, ln_gb, *, tm, rows_per_mod, extra=None, extra_row0=0, side=()):
    m = x.shape[0]
    n_steps = N_FF_STEPS + N_OUT_STEPS
    tiles_per_mod = rows_per_mod // tm
    n_extra = 0 if extra is None else extra.shape[0]
    const2 = lambda i, s: (0, 0)
    extra_spec = pl.BlockSpec((n_extra, D_MODEL), const2)
    side_in_specs, side_out_specs, side_shapes = [], [], []
    for arr, lead, rows in side:
        n_rows, n_cols = arr.shape[-2:]
        n_blocks = n_rows // rows
        assert n_blocks * rows == n_rows and n_blocks <= (m // tm) * n_steps
        blk = lambda i, s, n_blocks=n_blocks: jnp.minimum(i * n_steps + s, n_blocks - 1)
        side_in_specs.append(pl.BlockSpec((None,) * len(lead) + (rows, n_cols),
                                          lambda i, s, lead=lead, blk=blk: (*lead, blk(i, s), 0)))
        side_out_specs.append(pl.BlockSpec((rows, n_cols), lambda i, s, blk=blk: (blk(i, s), 0)))
        side_shapes.append(jax.ShapeDtypeStruct((n_rows, n_cols), BF16))
    n_tiles = m // tm
    assert tm % LN_PIECE_ROWS == 0 and tm // LN_PIECE_ROWS <= N_FF_STEPS
    wd_block = lambda i, s: (0, jnp.where(s < N_FF_STEPS, N_OUT_STEPS - 1, s - N_FF_STEPS))
    out_block = lambda i, s: (jnp.where(s < N_FF_STEPS, jnp.maximum(i - 1, 0), i), 0)
    outs = pl.pallas_call(
        functools.partial(_ffn_kernel, len(side), n_extra, tiles_per_mod, extra_row0, n_tiles),
        grid=(n_tiles, n_steps),
        in_specs=[
            pl.BlockSpec((tm, D_MODEL), lambda i, s: (i, 0)),
            pl.BlockSpec((3, mod.shape[1], D_MODEL), lambda i, s: (sub, 0, 0)),
            pl.BlockSpec((pl.Element(D_MODEL), pl.Element(FF_TILE)),
                         lambda i, s: (0, pl.multiple_of(_ff_offset(s), LANES))),
            pl.BlockSpec((pl.Element(D_MODEL), pl.Element(FF_TILE)),
                         lambda i, s: (0, pl.multiple_of(D_FF + _ff_offset(s), LANES))),
            pl.BlockSpec((D_FF, OUT_TILE), wd_block),
            pl.BlockSpec((2, D_MODEL), const2),
            *([extra_spec] if n_extra else []),
            *side_in_specs,
        ],
        out_specs=[pl.BlockSpec((tm, D_MODEL), out_block), *([extra_spec] if n_extra else []),
                   *side_out_specs],
        out_shape=[jax.ShapeDtypeStruct((m, D_MODEL), F32),
                   *([jax.ShapeDtypeStruct((n_extra, D_MODEL), F32)] if n_extra else []), *side_shapes],
        scratch_shapes=[
            pltpu.VMEM((tm + n_extra, D_MODEL), BF16),
            pltpu.VMEM((N_FF_STEPS, tm + n_extra, FF_TILE), BF16),
            pltpu.VMEM((N_OUT_STEPS, tm + n_extra, OUT_TILE), F32),
        ],
        compiler_params=pltpu.CompilerParams(dimension_semantics=("arbitrary", "arbitrary"), vmem_limit_bytes=VMEM_LIMIT),
        name="swiglu_ln",
    )(x, mod, w_up, w_up, w_down, ln_gb, *([extra] if n_extra else []), *[arr for arr, _, _ in side])
    return outs


def _inproj_kernel(row0, x_ref, mod_ref, w_ref, o_ref, xm_ref):
    rows = slice(row0, row0 + x_ref.shape[0])

    @pl.when(pl.program_id(0) == 0)
    def _():
        xm_ref[...] = (x_ref[...] * (1.0 + mod_ref[1, rows, :]) + mod_ref[0, rows, :]).astype(BF16)

    o_ref[...] = _dot(xm_ref[...], w_ref[...])


def _inproj(x, mod, sub, row0, w_in):
    m = x.shape[0]
    tn = 512
    return pl.pallas_call(
        functools.partial(_inproj_kernel, row0),
        grid=(IN_WIDTH // tn,),
        in_specs=[
            pl.BlockSpec((m, D_MODEL), lambda j: (0, 0)),
            pl.BlockSpec((3, mod.shape[1], D_MODEL), lambda j: (sub, 0, 0)),
            pl.BlockSpec((D_MODEL, tn), lambda j: (0, j)),
        ],
        out_specs=pl.BlockSpec((m, tn), lambda j: (0, j)),
        out_shape=jax.ShapeDtypeStruct((m, IN_WIDTH), F32),
        scratch_shapes=[pltpu.VMEM((m, D_MODEL), BF16)],
        compiler_params=pltpu.CompilerParams(dimension_semantics=("arbitrary",), vmem_limit_bytes=VMEM_LIMIT),
        name="mixer_inproj",
    )(x, mod, w_in)


def _outproj_kernel(row0, m_ref, x_ref, mod_ref, w_ref, gb_ref, o_ref):
    rows = slice(row0, row0 + x_ref.shape[0])
    mixed = _dot(m_ref[...], w_ref[...])
    y = ALPHA * x_ref[...] + mod_ref[2, rows, :] * mixed
    o_ref[...] = _layer_norm(y, gb_ref[0:1], gb_ref[1:2])


def _outproj(merged, x, mod, sub, row0, w_o, ln_gb):
    m = x.shape[0]
    return pl.pallas_call(
        functools.partial(_outproj_kernel, row0),
        grid=(1,),
        in_specs=[
            pl.BlockSpec((m, D_MODEL), lambda i: (0, 0)),
            pl.BlockSpec((m, D_MODEL), lambda i: (0, 0)),
            pl.BlockSpec((3, mod.shape[1], D_MODEL), lambda i: (sub, 0, 0)),
            pl.BlockSpec((D_MODEL, D_MODEL), lambda i: (0, 0)),
            pl.BlockSpec((2, D_MODEL), lambda i: (0, 0)),
        ],
        out_specs=pl.BlockSpec((m, D_MODEL), lambda i: (0, 0)),
        out_shape=jax.ShapeDtypeStruct((m, D_MODEL), F32),
        compiler_params=pltpu.CompilerParams(dimension_semantics=("arbitrary",), vmem_limit_bytes=VMEM_LIMIT),
        name="mixer_outproj_ln",
    )(merged, x, mod, w_o, ln_gb)


def _gmlp_norm(pv, gng, gnb):
    v = _gelu_tanh(pv)
    outs = []
    for g in range(N_GMLP_GROUPS):
        sl = slice(g * LANES, (g + 1) * LANES)
        outs.append(_layer_norm(v[:, sl], gng[:, sl], gnb[:, sl]))
    return outs


def _half_lane_tiles(t):
    lo_lane = lax.broadcasted_iota(jnp.int32, (t.shape[0], LANES), 1) < HEAD_DIM
    lo, hi = [], []
    for c in range(KV_WIDTH // LANES):
        col = t[:, c * LANES:(c + 1) * LANES]
        swp = pltpu.roll(col, HEAD_DIM, 1)
        lo += [jnp.where(lo_lane, col, 0.0), jnp.where(lo_lane, swp, 0.0)]
        hi += [jnp.where(lo_lane, 0.0, swp), jnp.where(lo_lane, 0.0, col)]
    return jnp.concatenate(lo + hi, axis=-1).astype(BF16)


def _band_attention(q_s, kx_s, vx_s, attn_s, sinks_ref, nblk, first_tile):
    rows = 2 * WINDOW
    qi = lax.broadcasted_iota(jnp.int32, (rows, WINDOW), 0) & (WINDOW - 1)
    kj = lax.broadcasted_iota(jnp.int32, (rows, WINDOW), 1)
    own = kj <= qi
    first_mask = kj <= jnp.where(first_tile, qi, WINDOW)
    ones = jnp.ones((2 * WINDOW, LANES), BF16)
    units = [(wb, g, hf) for wb in range(nblk) for g in range(N_KV_HEADS) for hf in range(2)]

    def kv_cols(g, hf):
        return slice(hf * 4 * LANES + g * LANES, hf * 4 * LANES + (g + 1) * LANES)

    scores = []
    for wb, g, hf in units:
        r0 = wb * WINDOW
        q2 = jnp.concatenate([q_s[r0:r0 + WINDOW, (2 * g) * LANES:(2 * g + 1) * LANES],
                              q_s[r0:r0 + WINDOW, (2 * g + 1) * LANES:(2 * g + 2) * LANES]], axis=0)
        s = _dot_nt(q2, kx_s[r0:r0 + 2 * WINDOW, kv_cols(g, hf)])
        c = jnp.where(own, s[:, WINDOW:2 * WINDOW], s[:, 0:WINDOW])
        if wb == 0:
            c = jnp.where(first_mask, c, -jnp.inf)
        scores.append(c)

    probs = []
    for (wb, g, hf), c in zip(units, scores):
        sink = jnp.concatenate([jnp.full((WINDOW, LANES), sinks_ref[GQA_GROUP * g + hf], F32),
                                jnp.full((WINDOW, LANES), sinks_ref[GQA_GROUP * g + 2 + hf], F32)], axis=0)
        mx = jnp.maximum(jnp.broadcast_to(jnp.max(c, axis=-1, keepdims=True), c.shape), sink)
        p = jnp.exp(c - mx)
        pcat = jnp.concatenate([jnp.where(own, 0.0, p), jnp.where(own, p, 0.0)], axis=-1).astype(BF16)
        probs.append((pcat, jnp.exp(sink - mx)))

    outs = {}
    for (wb, g, hf), (pcat, esink) in zip(units, probs):
        r0 = wb * WINDOW
        v2 = jnp.concatenate([vx_s[r0:r0 + 2 * WINDOW, kv_cols(g, hf)], ones], axis=-1)
        o = _dot(pcat, v2)
        outs[wb, g, hf] = o[:, 0:LANES] / (o[:, LANES:2 * LANES] + esink)

    for wb in range(nblk):
        r0 = wb * WINDOW
        for g in range(N_KV_HEADS):
            both = outs[wb, g, 0] + outs[wb, g, 1]
            attn_s[r0:r0 + WINDOW, (2 * g) * LANES:(2 * g + 1) * LANES] = both[0:WINDOW]
            attn_s[r0:r0 + WINDOW, (2 * g + 1) * LANES:(2 * g + 2) * LANES] = both[WINDOW:2 * WINDOW]


def _pmix_kernel(tiles_per_seq, n_tiles, sinks_ref, x_ref, mod_ref, win_ref, wo_ref, cos_ref, slo_ref, shi_ref,
                 gng_ref, gnb_ref, ws_ref, bs_ref, og_ref, lngb_ref,
                 o_ref, k_ref, v_ref, p_s, q_s, kk_s, vv_s, kp_s, vp_s, attn_s, m_s, wm_s, y_s):
    n = pl.program_id(0)
    t = n % tiles_per_seq
    tm = x_ref.shape[0]
    nblk = tm // WINDOW

    @pl.when(n == 0)
    def _():
        y_s[...] = jnp.zeros(y_s.shape, F32)
        row = lax.broadcasted_iota(jnp.int32, (CHUNK, CHUNK), 0)
        col = lax.broadcasted_iota(jnp.int32, (CHUNK, CHUNK), 1)
        for g in range(N_GMLP_GROUPS):
            wm_s[g] = jnp.where(col <= row, ws_ref[g], 0.0).astype(BF16)

    @pl.when(t == 0)
    def _():
        kp_s[...] = jnp.zeros(kp_s.shape, BF16)
        vp_s[...] = jnp.zeros(vp_s.shape, BF16)

    @pl.when(n == n_tiles)
    def _():
        o_ref[...] = _layer_norm(y_s[...], lngb_ref[0:1], lngb_ref[1:2])

    pl.when(n < n_tiles)(functools.partial(
        _pmix_tile, t, pl.ds(n // tiles_per_seq, 1), sinks_ref, x_ref, mod_ref, win_ref, wo_ref, cos_ref, slo_ref,
        shi_ref, gng_ref, gnb_ref, bs_ref, og_ref, lngb_ref, o_ref, k_ref, v_ref, p_s, q_s, kk_s, vv_s, kp_s, vp_s,
        attn_s, m_s, wm_s, y_s))


def _pmix_tile(t, seq, sinks_ref, x_ref, mod_ref, win_ref, wo_ref, cos_ref, slo_ref, shi_ref, gng_ref, gnb_ref,
               bs_ref, og_ref, lngb_ref, o_ref, k_ref, v_ref, p_s, q_s, kk_s, vv_s, kp_s, vp_s, attn_s, m_s, wm_s, y_s):
    tm = x_ref.shape[0]
    nblk = tm // WINDOW

    o_ref[...] = _layer_norm(y_s[...], lngb_ref[0:1], lngb_ref[1:2])

    m_s[...] = (x_ref[...] * (1.0 + mod_ref[1, seq, :]) + mod_ref[0, seq, :]).astype(BF16)
    cos, slo, shi = cos_ref[...], slo_ref[...], shi_ref[...]
    for c in range(IN_WIDTH // PROJ_TILE):
        lo_col = c * PROJ_TILE
        h = _dot(m_s[...], win_ref[:, lo_col:lo_col + PROJ_TILE])
        if lo_col + PROJ_TILE <= Q_END:
            q_s[:, lo_col:lo_col + PROJ_TILE] = (_rope_lanes(h, cos, slo, shi) * (HEAD_DIM ** -0.5)).astype(BF16)
        elif lo_col == Q_END and PROJ_TILE == 2 * KV_WIDTH:
            k = _rope_lanes(h[:, 0:KV_WIDTH], cos, slo, shi)
            v = h[:, KV_WIDTH:2 * KV_WIDTH]
            k_ref[0] = k[tm - WINDOW:tm]
            v_ref[0] = v[tm - WINDOW:tm]
            kb, vb = _half_lane_tiles(k), _half_lane_tiles(v)
            kk_s[0:WINDOW, :] = kp_s[...]
            vv_s[0:WINDOW, :] = vp_s[...]
            kk_s[WINDOW:WINDOW + tm, :] = kb
            vv_s[WINDOW:WINDOW + tm, :] = vb
            kp_s[...] = kb[tm - WINDOW:tm]
            vp_s[...] = vb[tm - WINDOW:tm]
        else:
            p_s[:, lo_col - V_END:lo_col - V_END + PROJ_TILE] = h

    _band_attention(q_s, kk_s, vv_s, attn_s, sinks_ref, nblk, t == 0)

    vn = _gmlp_norm(p_s[:, GMLP_WIDTH:2 * GMLP_WIDTH], gng_ref[...], gnb_ref[...])
    gated = []
    for g in range(N_GMLP_GROUPS):
        u = _gelu_tanh(p_s[:, g * LANES:(g + 1) * LANES])
        vg = vn[g].astype(BF16)
        cols = jnp.concatenate([vg[wb * CHUNK:(wb + 1) * CHUNK] for wb in range(nblk)], axis=-1)
        mix = _dot(wm_s[g], cols)
        mix = jnp.concatenate([mix[:, wb * LANES:(wb + 1) * LANES] for wb in range(nblk)], axis=0)
        bias = jnp.concatenate([bs_ref[:, g:g + 1]] * nblk, axis=0)
        gated.append(u * (mix + bias))
    gm = jnp.concatenate(gated, axis=-1)
    og = og_ref[...]
    m_s[:, 0:ATTN_WIDTH] = _rms_norm(attn_s[...], og[:, 0:ATTN_WIDTH]).astype(BF16)
    m_s[:, ATTN_WIDTH:] = _rms_norm(gm, og[:, ATTN_WIDTH:]).astype(BF16)

    for c in range(D_MODEL // PROJ_TILE):
        sl = slice(c * PROJ_TILE, (c + 1) * PROJ_TILE)
        y_s[:, sl] = ALPHA * x_ref[:, sl] + mod_ref[2, seq, sl] * _dot(m_s[...], wo_ref[:, sl])


def _rope_tables(pos):
    half = ROPE_DIM // 2
    inv = ROPE_THETA ** (-(np.arange(half, dtype=np.float64) * 2.0) / ROPE_DIM)
    ang = np.asarray(pos, np.float64)[:, None] * inv[None, :]
    cos, sin = np.cos(ang), np.sin(ang)
    n = ang.shape[0]
    one = np.ones((n, HEAD_DIM - ROPE_DIM))
    zero = np.zeros((n, HEAD_DIM - ROPE_DIM))
    z8 = np.zeros((n, half))
    cos_t = np.concatenate([cos, cos, one], axis=-1)
    sin_lo = np.concatenate([-sin, z8, zero], axis=-1)
    sin_hi = np.concatenate([z8, sin, zero], axis=-1)
    rep = LANES // HEAD_DIM
    return tuple(jnp.asarray(np.tile(t, (1, rep)), F32) for t in (cos_t, sin_lo, sin_hi))


def _pmix(x, mod, sub, w_in, w_o, sinks, tables, gn_g, gn_b, w_s, b_s, out_g, ln_gb, *, batch, tm):
    nt = SEQ // tm
    n_tiles = batch * nt
    tile = lambda n: jnp.minimum(n, n_tiles - 1)
    cos_t, sin_lo, sin_hi = tables
    tab_spec = pl.BlockSpec((tm, LANES), lambda n: (tile(n) % nt, 0))
    full2 = lambda n: (0, 0)
    mod_spec = pl.BlockSpec((3, mod.shape[1], D_MODEL), lambda n: (sub, 0, 0))
    kv_spec = pl.BlockSpec((1, WINDOW, KV_WIDTH), lambda n: (tile(n) // nt, 0, 0))
    resident = dict(pipeline_mode=pl.Buffered(1))
    return pl.pallas_call(
        functools.partial(_pmix_kernel, nt, n_tiles),
        grid=(n_tiles + 1,),
        in_specs=[
            pl.BlockSpec(memory_space=pltpu.SMEM),
            pl.BlockSpec((tm, D_MODEL), lambda n: (tile(n), 0)),
            mod_spec,
            pl.BlockSpec((D_MODEL, IN_WIDTH), full2, **resident),
            pl.BlockSpec((D_MODEL, D_MODEL), full2, **resident),
            tab_spec, tab_spec, tab_spec,
            pl.BlockSpec((1, GMLP_WIDTH), full2),
            pl.BlockSpec((1, GMLP_WIDTH), full2),
            pl.BlockSpec((N_GMLP_GROUPS, CHUNK, CHUNK), lambda n: (0, 0, 0)),
            pl.BlockSpec((CHUNK, N_GMLP_GROUPS), full2),
            pl.BlockSpec((1, D_MODEL), full2),
            pl.BlockSpec((2, D_MODEL), full2),
        ],
        out_specs=[
            pl.BlockSpec((tm, D_MODEL), lambda n: (jnp.maximum(n - 1, 0), 0)),
            kv_spec, kv_spec,
        ],
        out_shape=[
            jax.ShapeDtypeStruct((batch * SEQ, D_MODEL), F32),
            jax.ShapeDtypeStruct((batch, WINDOW, KV_WIDTH), F32),
            jax.ShapeDtypeStruct((batch, WINDOW, KV_WIDTH), F32),
        ],
        scratch_shapes=[
            pltpu.VMEM((tm, 2 * GMLP_WIDTH), F32),
            pltpu.VMEM((tm, ATTN_WIDTH), BF16),
            pltpu.VMEM((WINDOW + tm, 4 * KV_WIDTH), BF16),
            pltpu.VMEM((WINDOW + tm, 4 * KV_WIDTH), BF16),
            pltpu.VMEM((WINDOW, 4 * KV_WIDTH), BF16),
            pltpu.VMEM((WINDOW, 4 * KV_WIDTH), BF16),
            pltpu.VMEM((tm, ATTN_WIDTH), F32),
            pltpu.VMEM((tm, D_MODEL), BF16),
            pltpu.VMEM((N_GMLP_GROUPS, CHUNK, CHUNK), BF16),
            pltpu.VMEM((tm, D_MODEL), F32),
        ],
        compiler_params=pltpu.CompilerParams(dimension_semantics=("arbitrary",), vmem_limit_bytes=VMEM_LIMIT),
        name="prompt_mixer",
    )(sinks, x, mod, w_in, w_o, cos_t, sin_lo, sin_hi, gn_g.reshape(1, GMLP_WIDTH),
      gn_b.reshape(1, GMLP_WIDTH), w_s, b_s.T, out_g.reshape(1, D_MODEL), ln_gb)


def _sattn_kernel(qe_ref, p_ref, ck_ref, cv_ref, sink_ref, cos_ref, slo_ref, shi_ref, o_ref, ko_ref, vo_ref):
    cos, slo, shi = cos_ref[...], slo_ref[...], shi_ref[...]
    sink = sink_ref[...]
    for r in range(qe_ref.shape[0]):
        qe = _rope_lanes(qe_ref[r], cos, slo, shi) * (HEAD_DIM ** -0.5)
        kn = _rope_lanes(p_ref[r:r + 1, Q_END:K_END], cos, slo, shi)
        vn = p_ref[r:r + 1, K_END:V_END]
        ck = ck_ref[r].reshape(WINDOW, KV_WIDTH)
        cv = cv_ref[r].reshape(WINDOW, KV_WIDTH)
        s_c = _dot_nt(qe.astype(BF16), ck.astype(BF16))
        key = lax.broadcasted_iota(jnp.int32, s_c.shape, 1)
        s_c = jnp.where(key >= 1, s_c, -jnp.inf)
        s_n = jnp.sum(qe * kn, axis=-1, keepdims=True)
        mx = jnp.maximum(jnp.maximum(jnp.max(s_c, axis=-1, keepdims=True), s_n), sink)
        p_c = jnp.exp(s_c - mx)
        p_n = jnp.exp(s_n - mx)
        denom = jnp.sum(p_c, axis=-1, keepdims=True) + p_n + jnp.exp(sink - mx)
        o = (_dot(p_c.astype(BF16), cv.astype(BF16)) + p_n * vn) / denom
        head = lax.broadcasted_iota(jnp.int32, o.shape, 0)
        lane = lax.broadcasted_iota(jnp.int32, o.shape, 1)
        o_ref[r] = jnp.where(head // GQA_GROUP == lane // HEAD_DIM, o, 0.0)
        row = lax.broadcasted_iota(jnp.int32, ck.shape, 0)
        newest = row == WINDOW - 1
        ko_ref[r] = jnp.where(newest, kn, pltpu.roll(ck, WINDOW - 1, 0)).reshape(WINDOW, N_KV_HEADS, HEAD_DIM)
        vo_ref[r] = jnp.where(newest, vn, pltpu.roll(cv, WINDOW - 1, 0)).reshape(WINDOW, N_KV_HEADS, HEAD_DIM)


def _sattn(qe, p3, ck, cv, sinks, tables):
    b = qe.shape[0]
    rows = SAMPLE_ROWS_PER_STEP
    cos_t, sin_lo, sin_hi = tables
    tab_spec = pl.BlockSpec((1, LANES), lambda i: (0, 0))
    blk3 = lambda i: (i, 0, 0)
    return pl.pallas_call(
        _sattn_kernel,
        grid=(b // rows,),
        in_specs=[
            pl.BlockSpec((rows, N_HEADS, KV_WIDTH), blk3),
            pl.BlockSpec((rows, IN_WIDTH), lambda i: (i, 0)),
            pl.BlockSpec((rows, WINDOW, N_KV_HEADS, HEAD_DIM), lambda i: (i, 0, 0, 0)),
            pl.BlockSpec((rows, WINDOW, N_KV_HEADS, HEAD_DIM), lambda i: (i, 0, 0, 0)),
            pl.BlockSpec((N_HEADS, 1), lambda i: (0, 0)),
            tab_spec, tab_spec, tab_spec,
        ],
        out_specs=[
            pl.BlockSpec((rows, N_HEADS, KV_WIDTH), blk3),
            pl.BlockSpec((rows, WINDOW, N_KV_HEADS, HEAD_DIM), lambda i: (i, 0, 0, 0)),
            pl.BlockSpec((rows, WINDOW, N_KV_HEADS, HEAD_DIM), lambda i: (i, 0, 0, 0)),
        ],
        out_shape=[
            jax.ShapeDtypeStruct((b, N_HEADS, KV_WIDTH), F32),
            jax.ShapeDtypeStruct((b, WINDOW, N_KV_HEADS, HEAD_DIM), F32),
            jax.ShapeDtypeStruct((b, WINDOW, N_KV_HEADS, HEAD_DIM), F32),
        ],
        compiler_params=pltpu.CompilerParams(dimension_semantics=("arbitrary",)),
        name="sample_attn",
    )(qe, p3, ck, cv, sinks.reshape(N_HEADS, 1), cos_t, sin_lo, sin_hi)


def _smerge_kernel(attn_ref, p_ref, gng_ref, gnb_ref, w0_ref, b0_ref, og_ref, m_ref, vn_ref):
    og = og_ref[...]
    vn = jnp.concatenate(_gmlp_norm(p_ref[:, U_END:IN_WIDTH], gng_ref[...], gnb_ref[...]), axis=-1)
    vn_ref[...] = vn
    u = _gelu_tanh(p_ref[:, V_END:U_END])
    gm = u * (w0_ref[...] * vn + b0_ref[...])
    m_ref[:, 0:ATTN_WIDTH] = _rms_norm(attn_ref[...], og[:, 0:ATTN_WIDTH]).astype(BF16)
    m_ref[:, ATTN_WIDTH:] = _rms_norm(gm, og[:, ATTN_WIDTH:]).astype(BF16)


def _smerge(attn, p, gn_g, gn_b, w_s, b_s, out_g):
    b = attn.shape[0]
    w0 = jnp.repeat(w_s[:, 0, 0], LANES).reshape(1, GMLP_WIDTH)
    b0 = jnp.repeat(b_s[:, 0], LANES).reshape(1, GMLP_WIDTH)
    return pl.pallas_call(
        _smerge_kernel,
        out_shape=[
            jax.ShapeDtypeStruct((b, D_MODEL), BF16),
            jax.ShapeDtypeStruct((b, GMLP_WIDTH), F32),
        ],
        name="sample_gmlp_merge",
    )(attn, p, gn_g.reshape(1, GMLP_WIDTH), gn_b.reshape(1, GMLP_WIDTH), w0, b0, out_g.reshape(1, D_MODEL))


def kernel(x_prompt, x_sample, cache_k_win, cache_v_win, c_prompt, c_sample, w_ada, b_ada, ln_g, ln_b,
           w_ffn_up, w_ffn_down, w_in, attn_sinks, gmlp_norm_g, gmlp_norm_b, w_spatial, b_spatial,
           out_norm_g, w_o):
    batch, seq, _ = x_prompt.shape
    dec_batch = x_sample.shape[0]
    buf = cache_k_win.shape[2]
    assert seq == SEQ and buf == WINDOW and x_sample.shape[1] == 1 and w_ada.shape[0] == DEPTH == 1

    xp = x_prompt.reshape(batch * seq, D_MODEL)
    xs = x_sample.reshape(dec_batch, D_MODEL)

    n_c = batch + dec_batch
    n_c_pad = ((n_c + 7) // 8) * 8
    c_all = jnp.concatenate([c_prompt, c_sample, jnp.zeros((n_c_pad - n_c, D_MODEL), F32)], axis=0)
    mod = _ada(c_all, w_ada[0], b_ada[0])

    ln_gb = jnp.stack([ln_g[0], ln_b[0]], axis=1)
    ffn = functools.partial(_ffn, tm=FFN_ROWS, rows_per_mod=seq, extra_row0=batch)

    w_up0 = w_ffn_up[0, 0].astype(BF16)
    w_down0 = w_ffn_down[0, 0].astype(BF16)
    later = ((w_ffn_up, (0, 1), 16), (w_ffn_down, (0, 1), 32), (w_in, (0,), 16), (w_o, (0,), 16))
    xp, xs, w_up1, w_down1, w_in_b, w_o_b = ffn(xp, mod, 0, w_up0, w_down0, ln_gb[0], extra=xs, side=later)

    xp, k_p, v_p = _pmix(xp, mod, 1, w_in_b, w_o_b, attn_sinks[0], _rope_tables(np.arange(seq)),
                         gmlp_norm_g[0], gmlp_norm_b[0], w_spatial[0], b_spatial[0], out_norm_g[0], ln_gb[1],
                         batch=batch, tm=MIX_TILE)

    ps = _inproj(xs, mod, 1, batch, w_in_b)
    eye = jnp.eye(N_KV_HEADS, dtype=F32)
    q5 = ps[:, :Q_END].reshape(dec_batch, N_KV_HEADS, GQA_GROUP, 1, HEAD_DIM)
    qe = (q5 * eye[None, :, None, :, None]).reshape(dec_batch, N_HEADS, KV_WIDTH)
    ck = cache_k_win[0]
    cv = cache_v_win[0]
    oe, k_s, v_s = _sattn(qe, ps, ck, cv, attn_sinks[0],
                          _rope_tables(np.full((1,), PAST_LEN)))
    attn_s = oe.reshape(dec_batch, N_KV_HEADS, GQA_GROUP, N_KV_HEADS, HEAD_DIM).sum(axis=3).reshape(dec_batch, ATTN_WIDTH)
    merged_s, vn_s = _smerge(attn_s, ps, gmlp_norm_g[0], gmlp_norm_b[0], w_spatial[0], b_spatial[0], out_norm_g[0])
    xs = _outproj(merged_s, xs, mod, 1, batch, w_o_b, ln_gb[1])

    xp, xs = ffn(xp, mod, 2, w_up1, w_down1, ln_gb[2], extra=xs)

    return (
        xp.reshape(batch, seq, D_MODEL),
        xs.reshape(dec_batch, 1, D_MODEL),
        k_p.reshape(1, batch, WINDOW, N_KV_HEADS, HEAD_DIM),
        v_p.reshape(1, batch, WINDOW, N_KV_HEADS, HEAD_DIM),
        k_s.reshape(1, dec_batch, buf, N_KV_HEADS, HEAD_DIM),
        v_s.reshape(1, dec_batch, buf, N_KV_HEADS, HEAD_DIM),
        vn_s.reshape(1, dec_batch, 1, GMLP_WIDTH),
    )
```

```python
import functools

import jax
import jax.numpy as jnp
import numpy as np
from jax import lax
from jax.experimental import pallas as pl
from jax.experimental.pallas import tpu as pltpu

D_MODEL = 2048
SEQ = 2048
PAST_LEN = 16384
ATTN_WIDTH = 1024
GMLP_WIDTH = 1024
HEAD_DIM = 64
N_HEADS = 16
N_KV_HEADS = 4
GQA_GROUP = 4
KV_WIDTH = 256
WINDOW = 128
ROPE_THETA = 500000.0
ROPE_DIM = 16
CHUNK = 128
N_GMLP_GROUPS = 8
D_FF = 5504
N_SUB = 3
DEPTH = 1
ALPHA = (2.0 * DEPTH) ** 0.25
LN_EPS = 1e-5
Q_END = ATTN_WIDTH
K_END = Q_END + KV_WIDTH
V_END = K_END + KV_WIDTH
U_END = V_END + GMLP_WIDTH
IN_WIDTH = U_END + GMLP_WIDTH

LANES = 128
MXU_COLS = 256
FF_TILE = 512
N_FF_STEPS = -(-D_FF // FF_TILE)
FF_LAST_OFF = D_FF - FF_TILE
FF_OVERLAP = N_FF_STEPS * FF_TILE - D_FF
OUT_TILE = 512
N_OUT_STEPS = D_MODEL // OUT_TILE
FFN_ROWS = 512
LN_PIECE_ROWS = 64
MIX_TILE = 256
PROJ_TILE = 512
SAMPLE_ROWS_PER_STEP = 8
VMEM_LIMIT = 60 * 1024 * 1024

BF16 = jnp.bfloat16
F32 = jnp.float32


def _dot(a, b):
    return jnp.dot(a, b, preferred_element_type=F32)


def _dot_nt(a, b):
    return lax.dot_general(a, b, (((1,), (1,)), ((), ())), preferred_element_type=F32)


def _layer_norm(y, g, b):
    mu = jnp.mean(y, axis=-1, keepdims=True)
    d = y - mu
    var = jnp.mean(d * d, axis=-1, keepdims=True)
    return d * lax.rsqrt(var + LN_EPS) * g + b


def _rms_norm(y, g):
    return y * lax.rsqrt(jnp.mean(y * y, axis=-1, keepdims=True) + LN_EPS) * g


def _gelu_tanh(x):
    c = np.float32(np.sqrt(2.0 / np.pi))
    return 0.5 * x * (1.0 + jnp.tanh(c * (x + 0.044715 * (x * x * x))))


def _rope_lanes(t, cos, sin_lo, sin_hi):
    pieces = []
    for c in range(t.shape[-1] // LANES):
        x = t[:, c * LANES:(c + 1) * LANES]
        pieces.append(x * cos + pltpu.roll(x, 8, 1) * sin_hi + pltpu.roll(x, LANES - 8, 1) * sin_lo)
    return pieces[0] if len(pieces) == 1 else jnp.concatenate(pieces, axis=-1)


def _ada_kernel(c_ref, w_ref, b_ref, o_ref):
    c = c_ref[...]
    h = (c * jax.nn.sigmoid(c)).astype(BF16)
    o_ref[0] = _dot(h, w_ref[...].astype(BF16)) + b_ref[...]


def _ada(c, w_ada, b_ada):
    rows = c.shape[0]
    n = w_ada.shape[1]
    tn = 1024
    per_plane = D_MODEL // tn
    return pl.pallas_call(
        _ada_kernel,
        grid=(n // tn,),
        in_specs=[
            pl.BlockSpec((rows, D_MODEL), lambda j: (0, 0)),
            pl.BlockSpec((D_MODEL, tn), lambda j: (0, j)),
            pl.BlockSpec((1, tn), lambda j: (0, j)),
        ],
        out_specs=pl.BlockSpec((1, rows, tn), lambda j: (j // per_plane, 0, j % per_plane)),
        out_shape=jax.ShapeDtypeStruct((n // D_MODEL, rows, D_MODEL), F32),
        compiler_params=pltpu.CompilerParams(dimension_semantics=("arbitrary",), vmem_limit_bytes=VMEM_LIMIT),
        name="ada_mod",
    )(c, w_ada, b_ada.reshape(1, n))


def _ff_offset(step):
    return jnp.minimum(step * FF_TILE, FF_LAST_OFF)


def _ffn_kernel(n_side, n_extra, tiles_per_mod, extra_row0, n_tiles, *refs):
    x_ref, mod_ref, wv_ref, wg_ref, wd_ref, gb_ref = refs[:6]
    n_in = 6
    if n_extra:
        xe_ref = refs[n_in]
        n_in += 1
        erows = slice(extra_row0, extra_row0 + n_extra)
    side_in = refs[n_in:n_in + n_side]
    n_in += n_side
    o_ref = refs[n_in]
    n_out = 1
    if n_extra:
        oe_ref = refs[n_in + 1]
        n_out = 2
    side_out = refs[n_in + n_out:n_in + n_out + n_side]
    xm_ref, a_ref, y_ref = refs[n_in + n_out + n_side:]
    i = pl.program_id(0)
    s = pl.program_id(1)
    tm = x_ref.shape[0]
    last = pl.num_programs(1) - 1

    def cast_side():
        for src, dst in zip(side_in, side_out):
            dst[...] = src[...].astype(BF16)

    def norm_piece():
        r0 = pl.multiple_of(jnp.minimum(s, tm // LN_PIECE_ROWS - 1) * LN_PIECE_ROWS, LN_PIECE_ROWS)
        piece = pl.ds(r0, LN_PIECE_ROWS)
        y = jnp.concatenate([y_ref[n, piece, :] for n in range(N_OUT_STEPS)], axis=-1)
        o_ref[piece, :] = _layer_norm(y, gb_ref[0:1], gb_ref[1:2])

    def up_chunk(rows, norm_previous):
        cast_side()
        if norm_previous:
            norm_piece()
        xm = xm_ref[0:rows]
        for c in range(FF_TILE // MXU_COLS):
            sl = slice(c * MXU_COLS, (c + 1) * MXU_COLS)
            hv = _dot(xm, wv_ref[:, sl])
            hg = _dot(xm, wg_ref[:, sl])
            a = hg * jax.nn.sigmoid(hg) * hv
            if c * MXU_COLS < FF_OVERLAP:
                col = lax.broadcasted_iota(jnp.int32, a.shape, 1) + c * MXU_COLS
                a = jnp.where(jnp.logical_and(s == N_FF_STEPS - 1, col < FF_OVERLAP), 0.0, a)
            a_ref[s, 0:rows, sl] = a.astype(BF16)

    def down_slab(rows):
        cast_side()
        acc = None
        for f in range(N_FF_STEPS):
            r0 = min(f * FF_TILE, FF_LAST_OFF)
            part = _dot(a_ref[f, 0:rows], wd_ref[r0:r0 + FF_TILE, :])
            acc = part if acc is None else acc + part
        y_ref[s - N_FF_STEPS, 0:rows] = acc

    first = i == 0
    later = i > 0
    up_phase = s < N_FF_STEPS
    down_phase = s >= N_FF_STEPS
    seq = pl.ds(i // tiles_per_mod, 1)

    @pl.when(s == 0)
    def _():
        xm_ref[0:tm] = (x_ref[...] * (1.0 + mod_ref[1, seq, :]) + mod_ref[0, seq, :]).astype(BF16)

    if n_extra:
        @pl.when(jnp.logical_and(s == 0, first))
        def _():
            xm_ref[tm:tm + n_extra] = (xe_ref[...] * (1.0 + mod_ref[1, erows, :]) + mod_ref[0, erows, :]).astype(BF16)

    pl.when(jnp.logical_and(up_phase, first))(functools.partial(up_chunk, tm + n_extra, False))
    pl.when(jnp.logical_and(up_phase, later))(functools.partial(up_chunk, tm, True))
    pl.when(jnp.logical_and(down_phase, first))(functools.partial(down_slab, tm + n_extra))
    pl.when(jnp.logical_and(down_phase, later))(functools.partial(down_slab, tm))

    @pl.when(s == last)
    def _():
        for n in range(N_OUT_STEPS):
            sl = slice(n * OUT_TILE, (n + 1) * OUT_TILE)
            y_ref[n, 0:tm] = ALPHA * x_ref[:, sl] + 0.5 * mod_ref[2, seq, sl] * y_ref[n, 0:tm]

    @pl.when(jnp.logical_and(s == last, i == n_tiles - 1))
    def _():
        y = jnp.concatenate([y_ref[n, 0:tm] for n in range(N_OUT_STEPS)], axis=-1)
        o_ref[...] = _layer_norm(y, gb_ref[0:1], gb_ref[1:2])

    if n_extra:
        @pl.when(jnp.logical_and(s == last, first))
        def _():
            mixed = jnp.concatenate([y_ref[n, tm:tm + n_extra] for n in range(N_OUT_STEPS)], axis=-1)
            oe_ref[...] = _layer_norm(ALPHA * xe_ref[...] + 0.5 * mod_ref[2, erows, :] * mixed, gb_ref[0:1], gb_ref[1:2])


def _ffn(x, mod, sub, w_up, w_down, ln_gb, *, tm, rows_per_mod, extra=None, extra_row0=0, side=()):
    m = x.shape[0]
    n_steps = N_FF_STEPS + N_OUT_STEPS
    tiles_per_mod = rows_per_mod // tm
    n_extra = 0 if extra is None else extra.shape[0]
    const2 = lambda i, s: (0, 0)
    extra_spec = pl.BlockSpec((n_extra, D_MODEL), const2)
    side_in_specs, side_out_specs, side_shapes = [], [], []
    for arr, lead, rows in side:
        n_rows, n_cols = arr.shape[-2:]
        n_blocks = n_rows // rows
        assert n_blocks * rows == n_rows and n_blocks <= (m // tm) * n_steps
        blk = lambda i, s, n_blocks=n_blocks: jnp.minimum(i * n_steps + s, n_blocks - 1)
        side_in_specs.append(pl.BlockSpec((None,) * len(lead) + (rows, n_cols),
                                          lambda i, s, lead=lead, blk=blk: (*lead, blk(i, s), 0)))
        side_out_specs.append(pl.BlockSpec((rows, n_cols), lambda i, s, blk=blk: (blk(i, s), 0)))
        side_shapes.append(jax.ShapeDtypeStruct((n_rows, n_cols), BF16))
    n_tiles = m // tm
    assert tm % LN_PIECE_ROWS == 0 and tm // LN_PIECE_ROWS <= N_FF_STEPS
    wd_block = lambda i, s: (0, jnp.where(s < N_FF_STEPS, N_OUT_STEPS - 1, s - N_FF_STEPS))
    out_block = lambda i, s: (jnp.where(s < N_FF_STEPS, jnp.maximum(i - 1, 0), i), 0)
    outs = pl.pallas_call(
        functools.partial(_ffn_kernel, len(side), n_extra, tiles_per_mod, extra_row0, n_tiles),
        grid=(n_tiles, n_steps),
        in_specs=[
            pl.BlockSpec((tm, D_MODEL), lambda i, s: (i, 0)),
            pl.BlockSpec((3, mod.shape[1], D_MODEL), lambda i, s: (sub, 0, 0)),
            pl.BlockSpec((pl.Element(D_MODEL), pl.Element(FF_TILE)),
                         lambda i, s: (0, pl.multiple_of(_ff_offset(s), LANES))),
            pl.BlockSpec((pl.Element(D_MODEL), pl.Element(FF_TILE)),
                         lambda i, s: (0, pl.multiple_of(D_FF + _ff_offset(s), LANES))),
            pl.BlockSpec((D_FF, OUT_TILE), wd_block),
            pl.BlockSpec((2, D_MODEL), const2),
            *([extra_spec] if n_extra else []),
            *side_in_specs,
        ],
        out_specs=[pl.BlockSpec((tm, D_MODEL), out_block), *([extra_spec] if n_extra else []),
                   *side_out_specs],
        out_shape=[jax.ShapeDtypeStruct((m, D_MODEL), F32),
                   *([jax.ShapeDtypeStruct((n_extra, D_MODEL), F32)] if n_extra else []), *side_shapes],
        scratch_shapes=[
            pltpu.VMEM((tm + n_extra, D_MODEL), BF16),
            pltpu.VMEM((N_FF_STEPS, tm + n_extra, FF_TILE), BF16),
            pltpu.VMEM((N_OUT_STEPS, tm + n_extra, OUT_TILE), F32),
        ],
        compiler_params=pltpu.CompilerParams(dimension_semantics=("arbitrary", "arbitrary"), vmem_limit_bytes=VMEM_LIMIT),
        name="swiglu_ln",
    )(x, mod, w_up, w_up, w_down, ln_gb, *([extra] if n_extra else []), *[arr for arr, _, _ in side])
    return outs


def _inproj_kernel(row0, x_ref, mod_ref, w_ref, o_ref, xm_ref):
    rows = slice(row0, row0 + x_ref.shape[0])

    @pl.when(pl.program_id(0) == 0)
    def _():
        xm_ref[...] = (x_ref[...] * (1.0 + mod_ref[1, rows, :]) + mod_ref[0, rows, :]).astype(BF16)

    o_ref[...] = _dot(xm_ref[...], w_ref[...])


def _inproj(x, mod, sub, row0, w_in):
    m = x.shape[0]
    tn = 512
    return pl.pallas_call(
        functools.partial(_inproj_kernel, row0),
        grid=(IN_WIDTH // tn,),
        in_specs=[
            pl.BlockSpec((m, D_MODEL), lambda j: (0, 0)),
            pl.BlockSpec((3, mod.shape[1], D_MODEL), lambda j: (sub, 0, 0)),
            pl.BlockSpec((D_MODEL, tn), lambda j: (0, j)),
        ],
        out_specs=pl.BlockSpec((m, tn), lambda j: (0, j)),
        out_shape=jax.ShapeDtypeStruct((m, IN_WIDTH), F32),
        scratch_shapes=[pltpu.VMEM((m, D_MODEL), BF16)],
        compiler_params=pltpu.CompilerParams(dimension_semantics=("arbitrary",), vmem_limit_bytes=VMEM_LIMIT),
        name="mixer_inproj",
    )(x, mod, w_in)


def _outproj_kernel(row0, m_ref, x_ref, mod_ref, w_ref, gb_ref, o_ref):
    rows = slice(row0, row0 + x_ref.shape[0])
    mixed = _dot(m_ref[...], w_ref[...])
    y = ALPHA * x_ref[...] + mod_ref[2, rows, :] * mixed
    o_ref[...] = _layer_norm(y, gb_ref[0:1], gb_ref[1:2])


def _outproj(merged, x, mod, sub, row0, w_o, ln_gb):
    m = x.shape[0]
    return pl.pallas_call(
        functools.partial(_outproj_kernel, row0),
        grid=(1,),
        in_specs=[
            pl.BlockSpec((m, D_MODEL), lambda i: (0, 0)),
            pl.BlockSpec((m, D_MODEL), lambda i: (0, 0)),
            pl.BlockSpec((3, mod.shape[1], D_MODEL), lambda i: (sub, 0, 0)),
            pl.BlockSpec((D_MODEL, D_MODEL), lambda i: (0, 0)),
            pl.BlockSpec((2, D_MODEL), lambda i: (0, 0)),
        ],
        out_specs=pl.BlockSpec((m, D_MODEL), lambda i: (0, 0)),
        out_shape=jax.ShapeDtypeStruct((m, D_MODEL), F32),
        compiler_params=pltpu.CompilerParams(dimension_semantics=("arbitrary",), vmem_limit_bytes=VMEM_LIMIT),
        name="mixer_outproj_ln",
    )(merged, x, mod, w_o, ln_gb)


def _gmlp_norm(pv, gng, gnb):
    v = _gelu_tanh(pv)
    outs = []
    for g in range(N_GMLP_GROUPS):
        sl = slice(g * LANES, (g + 1) * LANES)
        outs.append(_layer_norm(v[:, sl], gng[:, sl], gnb[:, sl]))
    return outs


def _half_lane_tiles(t):
    lo_lane = lax.broadcasted_iota(jnp.int32, (t.shape[0], LANES), 1) < HEAD_DIM
    lo, hi = [], []
    for c in range(KV_WIDTH // LANES):
        col = t[:, c * LANES:(c + 1) * LANES]
        swp = pltpu.roll(col, HEAD_DIM, 1)
        lo += [jnp.where(lo_lane, col, 0.0), jnp.where(lo_lane, swp, 0.0)]
        hi += [jnp.where(lo_lane, 0.0, swp), jnp.where(lo_lane, 0.0, col)]
    return jnp.concatenate(lo + hi, axis=-1).astype(BF16)


def _band_attention(q_s, kx_s, vx_s, attn_s, sinks_ref, nblk, first_tile):
    rows = 2 * WINDOW
    qi = lax.broadcasted_iota(jnp.int32, (rows, WINDOW), 0) & (WINDOW - 1)
    kj = lax.broadcasted_iota(jnp.int32, (rows, WINDOW), 1)
    own = kj <= qi
    first_mask = kj <= jnp.where(first_tile, qi, WINDOW)
    ones = jnp.ones((2 * WINDOW, LANES), BF16)
    units = [(wb, g, hf) for wb in range(nblk) for g in range(N_KV_HEADS) for hf in range(2)]

    def kv_cols(g, hf):
        return slice(hf * 4 * LANES + g * LANES, hf * 4 * LANES + (g + 1) * LANES)

    scores = []
    for wb, g, hf in units:
        r0 = wb * WINDOW
        q2 = jnp.concatenate([q_s[r0:r0 + WINDOW, (2 * g) * LANES:(2 * g + 1) * LANES],
                              q_s[r0:r0 + WINDOW, (2 * g + 1) * LANES:(2 * g + 2) * LANES]], axis=0)
        s = _dot_nt(q2, kx_s[r0:r0 + 2 * WINDOW, kv_cols(g, hf)])
        c = jnp.where(own, s[:, WINDOW:2 * WINDOW], s[:, 0:WINDOW])
        if wb == 0:
            c = jnp.where(first_mask, c, -jnp.inf)
        scores.append(c)

    probs = []
    for (wb, g, hf), c in zip(units, scores):
        sink = jnp.concatenate([jnp.full((WINDOW, LANES), sinks_ref[GQA_GROUP * g + hf], F32),
                                jnp.full((WINDOW, LANES), sinks_ref[GQA_GROUP * g + 2 + hf], F32)], axis=0)
        mx = jnp.maximum(jnp.broadcast_to(jnp.max(c, axis=-1, keepdims=True), c.shape), sink)
        p = jnp.exp(c - mx)
        pcat = jnp.concatenate([jnp.where(own, 0.0, p), jnp.where(own, p, 0.0)], axis=-1).astype(BF16)
        probs.append((pcat, jnp.exp(sink - mx)))

    outs = {}
    for (wb, g, hf), (pcat, esink) in zip(units, probs):
        r0 = wb * WINDOW
        v2 = jnp.concatenate([vx_s[r0:r0 + 2 * WINDOW, kv_cols(g, hf)], ones], axis=-1)
        o = _dot(pcat, v2)
        outs[wb, g, hf] = o[:, 0:LANES] / (o[:, LANES:2 * LANES] + esink)

    for wb in range(nblk):
        r0 = wb * WINDOW
        for g in range(N_KV_HEADS):
            both = outs[wb, g, 0] + outs[wb, g, 1]
            attn_s[r0:r0 + WINDOW, (2 * g) * LANES:(2 * g + 1) * LANES] = both[0:WINDOW]
            attn_s[r0:r0 + WINDOW, (2 * g + 1) * LANES:(2 * g + 2) * LANES] = both[WINDOW:2 * WINDOW]


def _pmix_kernel(tiles_per_seq, n_tiles, sinks_ref, x_ref, mod_ref, win_ref, wo_ref, cos_ref, slo_ref, shi_ref,
                 gng_ref, gnb_ref, ws_ref, bs_ref, og_ref, lngb_ref, side_ref,
                 o_ref, k_ref, v_ref, side_out_ref, p_s, q_s, kk_s, vv_s, kp_s, vp_s, attn_s, m_s, wm_s, y_s):
    n = pl.program_id(0)
    t = n % tiles_per_seq
    tm = x_ref.shape[0]
    nblk = tm // WINDOW

    @pl.when(n == 0)
    def _():
        y_s[...] = jnp.zeros(y_s.shape, F32)
        row = lax.broadcasted_iota(jnp.int32, (CHUNK, CHUNK), 0)
        col = lax.broadcasted_iota(jnp.int32, (CHUNK, CHUNK), 1)
        for g in range(N_GMLP_GROUPS):
            wm_s[g] = jnp.where(col <= row, ws_ref[g], 0.0).astype(BF16)

    @pl.when(t == 0)
    def _():
        kp_s[...] = jnp.zeros(kp_s.shape, BF16)
        vp_s[...] = jnp.zeros(vp_s.shape, BF16)

    @pl.when(n == n_tiles)
    def _():
        side_out_ref[...] = side_ref[...].astype(BF16)
        o_ref[...] = _layer_norm(y_s[...], lngb_ref[0:1], lngb_ref[1:2])

    pl.when(n < n_tiles)(functools.partial(
        _pmix_tile, t, pl.ds(n // tiles_per_seq, 1), sinks_ref, x_ref, mod_ref, win_ref, wo_ref, cos_ref, slo_ref,
        shi_ref, gng_ref, gnb_ref, bs_ref, og_ref, lngb_ref, side_ref, o_ref, k_ref, v_ref, side_out_ref, p_s, q_s,
        kk_s, vv_s, kp_s, vp_s, attn_s, m_s, wm_s, y_s))


def _pmix_tile(t, seq, sinks_ref, x_ref, mod_ref, win_ref, wo_ref, cos_ref, slo_ref, shi_ref, gng_ref, gnb_ref,
               bs_ref, og_ref, lngb_ref, side_ref, o_ref, k_ref, v_ref, side_out_ref, p_s, q_s, kk_s, vv_s, kp_s, vp_s,
               attn_s, m_s, wm_s, y_s):
    tm = x_ref.shape[0]
    nblk = tm // WINDOW

    side_out_ref[...] = side_ref[...].astype(BF16)

    o_ref[...] = _layer_norm(y_s[...], lngb_ref[0:1], lngb_ref[1:2])

    m_s[...] = (x_ref[...] * (1.0 + mod_ref[1, seq, :]) + mod_ref[0, seq, :]).astype(BF16)
    cos, slo, shi = cos_ref[...], slo_ref[...], shi_ref[...]
    for c in range(IN_WIDTH // PROJ_TILE):
        lo_col = c * PROJ_TILE
        h = _dot(m_s[...], win_ref[:, lo_col:lo_col + PROJ_TILE])
        if lo_col + PROJ_TILE <= Q_END:
            q_s[:, lo_col:lo_col + PROJ_TILE] = (_rope_lanes(h, cos, slo, shi) * (HEAD_DIM ** -0.5)).astype(BF16)
        elif lo_col == Q_END and PROJ_TILE == 2 * KV_WIDTH:
            k = _rope_lanes(h[:, 0:KV_WIDTH], cos, slo, shi)
            v = h[:, KV_WIDTH:2 * KV_WIDTH]
            k_ref[0] = k[tm - WINDOW:tm]
            v_ref[0] = v[tm - WINDOW:tm]
            kb, vb = _half_lane_tiles(k), _half_lane_tiles(v)
            kk_s[0:WINDOW, :] = kp_s[...]
            vv_s[0:WINDOW, :] = vp_s[...]
            kk_s[WINDOW:WINDOW + tm, :] = kb
            vv_s[WINDOW:WINDOW + tm, :] = vb
            kp_s[...] = kb[tm - WINDOW:tm]
            vp_s[...] = vb[tm - WINDOW:tm]
        else:
            p_s[:, lo_col - V_END:lo_col - V_END + PROJ_TILE] = h

    _band_attention(q_s, kk_s, vv_s, attn_s, sinks_ref, nblk, t == 0)

    vn = _gmlp_norm(p_s[:, GMLP_WIDTH:2 * GMLP_WIDTH], gng_ref[...], gnb_ref[...])
    gated = []
    for g in range(N_GMLP_GROUPS):
        u = _gelu_tanh(p_s[:, g * LANES:(g + 1) * LANES])
        vg = vn[g].astype(BF16)
        cols = jnp.concatenate([vg[wb * CHUNK:(wb + 1) * CHUNK] for wb in range(nblk)], axis=-1)
        mix = _dot(wm_s[g], cols)
        mix = jnp.concatenate([mix[:, wb * LANES:(wb + 1) * LANES] for wb in range(nblk)], axis=0)
        bias = jnp.concatenate([bs_ref[:, g:g + 1]] * nblk, axis=0)
        gated.append(u * (mix + bias))
    gm = jnp.concatenate(gated, axis=-1)
    og = og_ref[...]
    m_s[:, 0:ATTN_WIDTH] = _rms_norm(attn_s[...], og[:, 0:ATTN_WIDTH]).astype(BF16)
    m_s[:, ATTN_WIDTH:] = _rms_norm(gm, og[:, ATTN_WIDTH:]).astype(BF16)

    for c in range(D_MODEL // PROJ_TILE):
        sl = slice(c * PROJ_TILE, (c + 1) * PROJ_TILE)
        y_s[:, sl] = ALPHA * x_ref[:, sl] + mod_ref[2, seq, sl] * _dot(m_s[...], wo_ref[:, sl])


def _rope_tables(pos):
    half = ROPE_DIM // 2
    inv = ROPE_THETA ** (-(np.arange(half, dtype=np.float64) * 2.0) / ROPE_DIM)
    ang = np.asarray(pos, np.float64)[:, None] * inv[None, :]
    cos, sin = np.cos(ang), np.sin(ang)
    n = ang.shape[0]
    one = np.ones((n, HEAD_DIM - ROPE_DIM))
    zero = np.zeros((n, HEAD_DIM - ROPE_DIM))
    z8 = np.zeros((n, half))
    cos_t = np.concatenate([cos, cos, one], axis=-1)
    sin_lo = np.concatenate([-sin, z8, zero], axis=-1)
    sin_hi = np.concatenate([z8, sin, zero], axis=-1)
    rep = LANES // HEAD_DIM
    return tuple(jnp.asarray(np.tile(t, (1, rep)), F32) for t in (cos_t, sin_lo, sin_hi))


def _pmix(x, mod, sub, w_in, w_o, sinks, tables, gn_g, gn_b, w_s, b_s, out_g, ln_gb, side, *, batch, tm):
    nt = SEQ // tm
    n_tiles = batch * nt
    tile = lambda n: jnp.minimum(n, n_tiles - 1)
    cos_t, sin_lo, sin_hi = tables
    tab_spec = pl.BlockSpec((tm, LANES), lambda n: (tile(n) % nt, 0))
    full2 = lambda n: (0, 0)
    mod_spec = pl.BlockSpec((3, mod.shape[1], D_MODEL), lambda n: (sub, 0, 0))
    kv_spec = pl.BlockSpec((1, WINDOW, KV_WIDTH), lambda n: (tile(n) // nt, 0, 0))
    side_arr, side_lead = side
    side_rows, side_cols = side_arr.shape[-2:]
    assert side_rows % n_tiles == 0
    side_blk = side_rows // n_tiles
    resident = dict(pipeline_mode=pl.Buffered(1))
    return pl.pallas_call(
        functools.partial(_pmix_kernel, nt, n_tiles),
        grid=(n_tiles + 1,),
        in_specs=[
            pl.BlockSpec(memory_space=pltpu.SMEM),
            pl.BlockSpec((tm, D_MODEL), lambda n: (tile(n), 0)),
            mod_spec,
            pl.BlockSpec((D_MODEL, IN_WIDTH), full2, **resident),
            pl.BlockSpec((D_MODEL, D_MODEL), full2, **resident),
            tab_spec, tab_spec, tab_spec,
            pl.BlockSpec((1, GMLP_WIDTH), full2),
            pl.BlockSpec((1, GMLP_WIDTH), full2),
            pl.BlockSpec((N_GMLP_GROUPS, CHUNK, CHUNK), lambda n: (0, 0, 0)),
            pl.BlockSpec((CHUNK, N_GMLP_GROUPS), full2),
            pl.BlockSpec((1, D_MODEL), full2),
            pl.BlockSpec((2, D_MODEL), full2),
            pl.BlockSpec((None,) * len(side_lead) + (side_blk, side_cols), lambda n: (*side_lead, tile(n), 0)),
        ],
        out_specs=[
            pl.BlockSpec((tm, D_MODEL), lambda n: (jnp.maximum(n - 1, 0), 0)),
            kv_spec, kv_spec,
            pl.BlockSpec((side_blk, side_cols), lambda n: (tile(n), 0)),
        ],
        out_shape=[
            jax.ShapeDtypeStruct((batch * SEQ, D_MODEL), F32),
            jax.ShapeDtypeStruct((batch, WINDOW, KV_WIDTH), F32),
            jax.ShapeDtypeStruct((batch, WINDOW, KV_WIDTH), F32),
            jax.ShapeDtypeStruct((side_rows, side_cols), BF16),
        ],
        scratch_shapes=[
            pltpu.VMEM((tm, 2 * GMLP_WIDTH), F32),
            pltpu.VMEM((tm, ATTN_WIDTH), BF16),
            pltpu.VMEM((WINDOW + tm, 4 * KV_WIDTH), BF16),
            pltpu.VMEM((WINDOW + tm, 4 * KV_WIDTH), BF16),
            pltpu.VMEM((WINDOW, 4 * KV_WIDTH), BF16),
            pltpu.VMEM((WINDOW, 4 * KV_WIDTH), BF16),
            pltpu.VMEM((tm, ATTN_WIDTH), F32),
            pltpu.VMEM((tm, D_MODEL), BF16),
            pltpu.VMEM((N_GMLP_GROUPS, CHUNK, CHUNK), BF16),
            pltpu.VMEM((tm, D_MODEL), F32),
        ],
        compiler_params=pltpu.CompilerParams(dimension_semantics=("arbitrary",), vmem_limit_bytes=VMEM_LIMIT),
        name="prompt_mixer",
    )(sinks, x, mod, w_in, w_o, cos_t, sin_lo, sin_hi, gn_g.reshape(1, GMLP_WIDTH),
      gn_b.reshape(1, GMLP_WIDTH), w_s, b_s.T, out_g.reshape(1, D_MODEL), ln_gb, side_arr)


def _sattn_kernel(qe_ref, p_ref, ck_ref, cv_ref, sink_ref, cos_ref, slo_ref, shi_ref, o_ref, ko_ref, vo_ref):
    cos, slo, shi = cos_ref[...], slo_ref[...], shi_ref[...]
    sink = sink_ref[...]
    for r in range(qe_ref.shape[0]):
        qe = _rope_lanes(qe_ref[r], cos, slo, shi) * (HEAD_DIM ** -0.5)
        kn = _rope_lanes(p_ref[r:r + 1, Q_END:K_END], cos, slo, shi)
        vn = p_ref[r:r + 1, K_END:V_END]
        ck = ck_ref[r]
        cv = cv_ref[r]
        s_c = _dot_nt(qe.astype(BF16), ck.astype(BF16))
        key = lax.broadcasted_iota(jnp.int32, s_c.shape, 1)
        s_c = jnp.where(key >= 1, s_c, -jnp.inf)
        s_n = jnp.sum(qe * kn, axis=-1, keepdims=True)
        mx = jnp.maximum(jnp.maximum(jnp.max(s_c, axis=-1, keepdims=True), s_n), sink)
        p_c = jnp.exp(s_c - mx)
        p_n = jnp.exp(s_n - mx)
        denom = jnp.sum(p_c, axis=-1, keepdims=True) + p_n + jnp.exp(sink - mx)
        o = (_dot(p_c.astype(BF16), cv.astype(BF16)) + p_n * vn) / denom
        head = lax.broadcasted_iota(jnp.int32, o.shape, 0)
        lane = lax.broadcasted_iota(jnp.int32, o.shape, 1)
        o_ref[r] = jnp.where(head // GQA_GROUP == lane // HEAD_DIM, o, 0.0)
        row = lax.broadcasted_iota(jnp.int32, ck.shape, 0)
        newest = row == WINDOW - 1
        ko_ref[r] = jnp.where(newest, kn, pltpu.roll(ck, WINDOW - 1, 0))
        vo_ref[r] = jnp.where(newest, vn, pltpu.roll(cv, WINDOW - 1, 0))


def _sattn(qe, p3, ck, cv, sinks, tables):
    b = qe.shape[0]
    rows = SAMPLE_ROWS_PER_STEP
    cos_t, sin_lo, sin_hi = tables
    tab_spec = pl.BlockSpec((1, LANES), lambda i: (0, 0))
    blk3 = lambda i: (i, 0, 0)
    return pl.pallas_call(
        _sattn_kernel,
        grid=(b // rows,),
        in_specs=[
            pl.BlockSpec((rows, N_HEADS, KV_WIDTH), blk3),
            pl.BlockSpec((rows, IN_WIDTH), lambda i: (i, 0)),
            pl.BlockSpec((rows, WINDOW, KV_WIDTH), blk3),
            pl.BlockSpec((rows, WINDOW, KV_WIDTH), blk3),
            pl.BlockSpec((N_HEADS, 1), lambda i: (0, 0)),
            tab_spec, tab_spec, tab_spec,
        ],
        out_specs=[
            pl.BlockSpec((rows, N_HEADS, KV_WIDTH), blk3),
            pl.BlockSpec((rows, WINDOW, KV_WIDTH), blk3),
            pl.BlockSpec((rows, WINDOW, KV_WIDTH), blk3),
        ],
        out_shape=[
            jax.ShapeDtypeStruct((b, N_HEADS, KV_WIDTH), F32),
            jax.ShapeDtypeStruct((b, WINDOW, KV_WIDTH), F32),
            jax.ShapeDtypeStruct((b, WINDOW, KV_WIDTH), F32),
        ],
        compiler_params=pltpu.CompilerParams(dimension_semantics=("arbitrary",)),
        name="sample_attn",
    )(qe, p3, ck, cv, sinks.reshape(N_HEADS, 1), cos_t, sin_lo, sin_hi)


def _smerge_kernel(attn_ref, p_ref, gng_ref, gnb_ref, w0_ref, b0_ref, og_ref, m_ref, vn_ref):
    og = og_ref[...]
    vn = jnp.concatenate(_gmlp_norm(p_ref[:, U_END:IN_WIDTH], gng_ref[...], gnb_ref[...]), axis=-1)
    vn_ref[...] = vn
    u = _gelu_tanh(p_ref[:, V_END:U_END])
    gm = u * (w0_ref[...] * vn + b0_ref[...])
    m_ref[:, 0:ATTN_WIDTH] = _rms_norm(attn_ref[...], og[:, 0:ATTN_WIDTH]).astype(BF16)
    m_ref[:, ATTN_WIDTH:] = _rms_norm(gm, og[:, ATTN_WIDTH:]).astype(BF16)


def _smerge(attn, p, gn_g, gn_b, w_s, b_s, out_g):
    b = attn.shape[0]
    w0 = jnp.repeat(w_s[:, 0, 0], LANES).reshape(1, GMLP_WIDTH)
    b0 = jnp.repeat(b_s[:, 0], LANES).reshape(1, GMLP_WIDTH)
    return pl.pallas_call(
        _smerge_kernel,
        out_shape=[
            jax.ShapeDtypeStruct((b, D_MODEL), BF16),
            jax.ShapeDtypeStruct((b, GMLP_WIDTH), F32),
        ],
        name="sample_gmlp_merge",
    )(attn, p, gn_g.reshape(1, GMLP_WIDTH), gn_b.reshape(1, GMLP_WIDTH), w0, b0, out_g.reshape(1, D_MODEL))


def kernel(x_prompt, x_sample, cache_k_win, cache_v_win, c_prompt, c_sample, w_ada, b_ada, ln_g, ln_b,
           w_ffn_up, w_ffn_down, w_in, attn_sinks, gmlp_norm_g, gmlp_norm_b, w_spatial, b_spatial,
           out_norm_g, w_o):
    batch, seq, _ = x_prompt.shape
    dec_batch = x_sample.shape[0]
    buf = cache_k_win.shape[2]
    assert seq == SEQ and buf == WINDOW and x_sample.shape[1] == 1 and w_ada.shape[0] == DEPTH == 1

    xp = x_prompt.reshape(batch * seq, D_MODEL)
    xs = x_sample.reshape(dec_batch, D_MODEL)

    n_c = batch + dec_batch
    n_c_pad = ((n_c + 7) // 8) * 8
    c_all = jnp.concatenate([c_prompt, c_sample, jnp.zeros((n_c_pad - n_c, D_MODEL), F32)], axis=0)
    mod = _ada(c_all, w_ada[0], b_ada[0])

    ln_gb = jnp.stack([ln_g[0], ln_b[0]], axis=1)
    ffn = functools.partial(_ffn, tm=FFN_ROWS, rows_per_mod=seq, extra_row0=batch)

    w_up0 = w_ffn_up[0, 0].astype(BF16)
    w_down0 = w_ffn_down[0, 0].astype(BF16)
    later = ((w_ffn_down, (0, 1), 32), (w_in, (0,), 16), (w_o, (0,), 16))
    xp, xs, w_down1, w_in_b, w_o_b = ffn(xp, mod, 0, w_up0, w_down0, ln_gb[0], extra=xs, side=later)

    xp, k_p, v_p, w_up1 = _pmix(xp, mod, 1, w_in_b, w_o_b, attn_sinks[0], _rope_tables(np.arange(seq)),
                                gmlp_norm_g[0], gmlp_norm_b[0], w_spatial[0], b_spatial[0], out_norm_g[0], ln_gb[1],
                                (w_ffn_up, (0, 1)), batch=batch, tm=MIX_TILE)

    ps = _inproj(xs, mod, 1, batch, w_in_b)
    eye = jnp.eye(N_KV_HEADS, dtype=F32)
    q5 = ps[:, :Q_END].reshape(dec_batch, N_KV_HEADS, GQA_GROUP, 1, HEAD_DIM)
    qe = (q5 * eye[None, :, None, :, None]).reshape(dec_batch, N_HEADS, KV_WIDTH)
    ck = cache_k_win[0].reshape(dec_batch, buf, KV_WIDTH)
    cv = cache_v_win[0].reshape(dec_batch, buf, KV_WIDTH)
    oe, k_s, v_s = _sattn(qe, ps, ck, cv, attn_sinks[0],
                          _rope_tables(np.full((1,), PAST_LEN)))
    attn_s = oe.reshape(dec_batch, N_KV_HEADS, GQA_GROUP, N_KV_HEADS, HEAD_DIM).sum(axis=3).reshape(dec_batch, ATTN_WIDTH)
    merged_s, vn_s = _smerge(attn_s, ps, gmlp_norm_g[0], gmlp_norm_b[0], w_spatial[0], b_spatial[0], out_norm_g[0])
    xs = _outproj(merged_s, xs, mod, 1, batch, w_o_b, ln_gb[1])

    xp, xs = ffn(xp, mod, 2, w_up1, w_down1, ln_gb[2], extra=xs)

    return (
        xp.reshape(batch, seq, D_MODEL),
        xs.reshape(dec_batch, 1, D_MODEL),
        k_p.reshape(1, batch, WINDOW, N_KV_HEADS, HEAD_DIM),
        v_p.reshape(1, batch, WINDOW, N_KV_HEADS, HEAD_DIM),
        k_s.reshape(1, dec_batch, buf, N_KV_HEADS, HEAD_DIM),
        v_s.reshape(1, dec_batch, buf, N_KV_HEADS, HEAD_DIM),
        vn_s.reshape(1, dec_batch, 1, GMLP_WIDTH),
    )
```

```python
import functools

import jax
import jax.numpy as jnp
import numpy as np
from jax import lax
from jax.experimental import pallas as pl
from jax.experimental.pallas import tpu as pltpu

D_MODEL = 2048
SEQ = 2048
PAST_LEN = 16384
ATTN_WIDTH = 1024
GMLP_WIDTH = 1024
HEAD_DIM = 64
N_HEADS = 16
N_KV_HEADS = 4
GQA_GROUP = 4
KV_WIDTH = 256
WINDOW = 128
ROPE_THETA = 500000.0
ROPE_DIM = 16
CHUNK = 128
N_GMLP_GROUPS = 8
D_FF = 5504
N_SUB = 3
DEPTH = 1
ALPHA = (2.0 * DEPTH) ** 0.25
LN_EPS = 1e-5
Q_END = ATTN_WIDTH
K_END = Q_END + KV_WIDTH
V_END = K_END + KV_WIDTH
U_END = V_END + GMLP_WIDTH
IN_WIDTH = U_END + GMLP_WIDTH

LANES = 128
MXU_COLS = 256
FF_TILE = 512
N_FF_STEPS = -(-D_FF // FF_TILE)
FF_LAST_OFF = D_FF - FF_TILE
FF_OVERLAP = N_FF_STEPS * FF_TILE - D_FF
OUT_TILE = 512
N_OUT_STEPS = D_MODEL // OUT_TILE
FFN_ROWS = 512
LN_PIECE_ROWS = 64
MIX_TILE = 256
PROJ_TILE = 512
SAMPLE_ROWS_PER_STEP = 8
VMEM_LIMIT = 60 * 1024 * 1024

BF16 = jnp.bfloat16
F32 = jnp.float32


def _dot(a, b):
    return jnp.dot(a, b, preferred_element_type=F32)


def _dot_nt(a, b):
    return lax.dot_general(a, b, (((1,), (1,)), ((), ())), preferred_element_type=F32)


def _layer_norm(y, g, b):
    mu = jnp.mean(y, axis=-1, keepdims=True)
    d = y - mu
    var = jnp.mean(d * d, axis=-1, keepdims=True)
    return d * lax.rsqrt(var + LN_EPS) * g + b


def _rms_norm(y, g):
    return y * lax.rsqrt(jnp.mean(y * y, axis=-1, keepdims=True) + LN_EPS) * g


def _gelu_tanh(x):
    c = np.float32(np.sqrt(2.0 / np.pi))
    return 0.5 * x * (1.0 + jnp.tanh(c * (x + 0.044715 * (x * x * x))))


def _rope_lanes(t, cos, sin_lo, sin_hi):
    pieces = []
    for c in range(t.shape[-1] // LANES):
        x = t[:, c * LANES:(c + 1) * LANES]
        pieces.append(x * cos + pltpu.roll(x, 8, 1) * sin_hi + pltpu.roll(x, LANES - 8, 1) * sin_lo)
    return pieces[0] if len(pieces) == 1 else jnp.concatenate(pieces, axis=-1)


def _ada_kernel(c_ref, w_ref, b_ref, o_ref):
    c = c_ref[...]
    h = (c * jax.nn.sigmoid(c)).astype(BF16)
    o_ref[0] = _dot(h, w_ref[...].astype(BF16)) + b_ref[...]


def _ada(c, w_ada, b_ada):
    rows = c.shape[0]
    n = w_ada.shape[1]
    tn = 1024
    per_plane = D_MODEL // tn
    return pl.pallas_call(
        _ada_kernel,
        grid=(n // tn,),
        in_specs=[
            pl.BlockSpec((rows, D_MODEL), lambda j: (0, 0)),
            pl.BlockSpec((D_MODEL, tn), lambda j: (0, j)),
            pl.BlockSpec((1, tn), lambda j: (0, j)),
        ],
        out_specs=pl.BlockSpec((1, rows, tn), lambda j: (j // per_plane, 0, j % per_plane)),
        out_shape=jax.ShapeDtypeStruct((n // D_MODEL, rows, D_MODEL), F32),
        compiler_params=pltpu.CompilerParams(dimension_semantics=("arbitrary",), vmem_limit_bytes=VMEM_LIMIT),
        name="ada_mod",
    )(c, w_ada, b_ada.reshape(1, n))


def _ff_offset(step):
    return jnp.minimum(step * FF_TILE, FF_LAST_OFF)


def _ffn_kernel(n_side, n_extra, tiles_per_mod, extra_row0, n_tiles, *refs):
    x_ref, mod_ref, wv_ref, wg_ref, wd_ref, gb_ref = refs[:6]
    n_in = 6
    if n_extra:
        xe_ref = refs[n_in]
        n_in += 1
        erows = slice(extra_row0, extra_row0 + n_extra)
    side_in = refs[n_in:n_in + n_side]
    n_in += n_side
    o_ref = refs[n_in]
    n_out = 1
    if n_extra:
        oe_ref = refs[n_in + 1]
        n_out = 2
    side_out = refs[n_in + n_out:n_in + n_out + n_side]
    xm_ref, a_ref, y_ref = refs[n_in + n_out + n_side:]
    i = pl.program_id(0)
    s = pl.program_id(1)
    tm = x_ref.shape[0]
    last = pl.num_programs(1) - 1

    def cast_side():
        for src, dst in zip(side_in, side_out):
            dst[...] = src[...].astype(BF16)

    def norm_piece():
        r0 = pl.multiple_of(jnp.minimum(s, tm // LN_PIECE_ROWS - 1) * LN_PIECE_ROWS, LN_PIECE_ROWS)
        piece = pl.ds(r0, LN_PIECE_ROWS)
        y = jnp.concatenate([y_ref[n, piece, :] for n in range(N_OUT_STEPS)], axis=-1)
        o_ref[piece, :] = _layer_norm(y, gb_ref[0:1], gb_ref[1:2])

    def up_chunk(rows, norm_previous):
        cast_side()
        if norm_previous:
            norm_piece()
        xm = xm_ref[0:rows]
        for c in range(FF_TILE // MXU_COLS):
            sl = slice(c * MXU_COLS, (c + 1) * MXU_COLS)
            hv = _dot(xm, wv_ref[:, sl])
            hg = _dot(xm, wg_ref[:, sl])
            a = hg * jax.nn.sigmoid(hg) * hv
            if c * MXU_COLS < FF_OVERLAP:
                col = lax.broadcasted_iota(jnp.int32, a.shape, 1) + c * MXU_COLS
                a = jnp.where(jnp.logical_and(s == N_FF_STEPS - 1, col < FF_OVERLAP), 0.0, a)
            a_ref[s, 0:rows, sl] = a.astype(BF16)

    def down_slab(rows):
        cast_side()
        acc = None
        for f in range(N_FF_STEPS):
            r0 = min(f * FF_TILE, FF_LAST_OFF)
            part = _dot(a_ref[f, 0:rows], wd_ref[r0:r0 + FF_TILE, :])
            acc = part if acc is None else acc + part
        y_ref[s - N_FF_STEPS, 0:rows] = acc

    first = i == 0
    later = i > 0
    up_phase = s < N_FF_STEPS
    down_phase = s >= N_FF_STEPS
    seq = pl.ds(i // tiles_per_mod, 1)

    @pl.when(s == 0)
    def _():
        xm_ref[0:tm] = (x_ref[...] * (1.0 + mod_ref[1, seq, :]) + mod_ref[0, seq, :]).astype(BF16)

    if n_extra:
        @pl.when(jnp.logical_and(s == 0, first))
        def _():
            xm_ref[tm:tm + n_extra] = (xe_ref[...] * (1.0 + mod_ref[1, erows, :]) + mod_ref[0, erows, :]).astype(BF16)

    pl.when(jnp.logical_and(up_phase, first))(functools.partial(up_chunk, tm + n_extra, False))
    pl.when(jnp.logical_and(up_phase, later))(functools.partial(up_chunk, tm, True))
    pl.when(jnp.logical_and(down_phase, first))(functools.partial(down_slab, tm + n_extra))
    pl.when(jnp.logical_and(down_phase, later))(functools.partial(down_slab, tm))

    @pl.when(s == last)
    def _():
        for n in range(N_OUT_STEPS):
            sl = slice(n * OUT_TILE, (n + 1) * OUT_TILE)
            y_ref[n, 0:tm] = ALPHA * x_ref[:, sl] + 0.5 * mod_ref[2, seq, sl] * y_ref[n, 0:tm]

    @pl.when(jnp.logical_and(s == last, i == n_tiles - 1))
    def _():
        y = jnp.concatenate([y_ref[n, 0:tm] for n in range(N_OUT_STEPS)], axis=-1)
        o_ref[...] = _layer_norm(y, gb_ref[0:1], gb_ref[1:2])

    if n_extra:
        @pl.when(jnp.logical_and(s == last, first))
        def _():
            mixed = jnp.concatenate([y_ref[n, tm:tm + n_extra] for n in range(N_OUT_STEPS)], axis=-1)
            oe_ref[...] = _layer_norm(ALPHA * xe_ref[...] + 0.5 * mod_ref[2, erows, :] * mixed, gb_ref[0:1], gb_ref[1:2])


def _ffn(x, mod, sub, w_up, w_down, ln_gb, *, tm, rows_per_mod, extra=None, extra_row0=0, side=()):
    m = x.shape[0]
    n_steps = N_FF_STEPS + N_OUT_STEPS
    tiles_per_mod = rows_per_mod // tm
    n_extra = 0 if extra is None else extra.shape[0]
    const2 = lambda i, s: (0, 0)
    extra_spec = pl.BlockSpec((n_extra, D_MODEL), const2)
    side_in_specs, side_out_specs, side_shapes = [], [], []
    for arr, lead, rows in side:
        n_rows, n_cols = arr.shape[-2:]
        n_blocks = n_rows // rows
        assert n_blocks * rows == n_rows and n_blocks <= (m // tm) * n_steps
        blk = lambda i, s, n_blocks=n_blocks: jnp.minimum(i * n_steps + s, n_blocks - 1)
        side_in_specs.append(pl.BlockSpec((None,) * len(lead) + (rows, n_cols),
                                          lambda i, s, lead=lead, blk=blk: (*lead, blk(i, s), 0)))
        side_out_specs.append(pl.BlockSpec((rows, n_cols), lambda i, s, blk=blk: (blk(i, s), 0)))
        side_shapes.append(jax.ShapeDtypeStruct((n_rows, n_cols), BF16))
    n_tiles = m // tm
    assert tm % LN_PIECE_ROWS == 0 and tm // LN_PIECE_ROWS <= N_FF_STEPS
    wd_block = lambda i, s: (0, jnp.where(s < N_FF_STEPS, N_OUT_STEPS - 1, s - N_FF_STEPS))
    out_block = lambda i, s: (jnp.where(s < N_FF_STEPS, jnp.maximum(i - 1, 0), i), 0)
    outs = pl.pallas_call(
        functools.partial(_ffn_kernel, len(side), n_extra, tiles_per_mod, extra_row0, n_tiles),
        grid=(n_tiles, n_steps),
        in_specs=[
            pl.BlockSpec((tm, D_MODEL), lambda i, s: (i, 0)),
            pl.BlockSpec((3, mod.shape[1], D_MODEL), lambda i, s: (sub, 0, 0)),
            pl.BlockSpec((pl.Element(D_MODEL), pl.Element(FF_TILE)),
                         lambda i, s: (0, pl.multiple_of(_ff_offset(s), LANES))),
            pl.BlockSpec((pl.Element(D_MODEL), pl.Element(FF_TILE)),
                         lambda i, s: (0, pl.multiple_of(D_FF + _ff_offset(s), LANES))),
            pl.BlockSpec((D_FF, OUT_TILE), wd_block),
            pl.BlockSpec((2, D_MODEL), const2),
            *([extra_spec] if n_extra else []),
            *side_in_specs,
        ],
        out_specs=[pl.BlockSpec((tm, D_MODEL), out_block), *([extra_spec] if n_extra else []),
                   *side_out_specs],
        out_shape=[jax.ShapeDtypeStruct((m, D_MODEL), F32),
                   *([jax.ShapeDtypeStruct((n_extra, D_MODEL), F32)] if n_extra else []), *side_shapes],
        scratch_shapes=[
            pltpu.VMEM((tm + n_extra, D_MODEL), BF16),
            pltpu.VMEM((N_FF_STEPS, tm + n_extra, FF_TILE), BF16),
            pltpu.VMEM((N_OUT_STEPS, tm + n_extra, OUT_TILE), F32),
        ],
        compiler_params=pltpu.CompilerParams(dimension_semantics=("arbitrary", "arbitrary"), vmem_limit_bytes=VMEM_LIMIT),
        name="swiglu_ln",
    )(x, mod, w_up, w_up, w_down, ln_gb, *([extra] if n_extra else []), *[arr for arr, _, _ in side])
    return outs


def _inproj_kernel(row0, x_ref, mod_ref, w_ref, o_ref, xm_ref):
    rows = slice(row0, row0 + x_ref.shape[0])

    @pl.when(pl.program_id(0) == 0)
    def _():
        xm_ref[...] = (x_ref[...] * (1.0 + mod_ref[1, rows, :]) + mod_ref[0, rows, :]).astype(BF16)

    o_ref[...] = _dot(xm_ref[...], w_ref[...])


def _inproj(x, mod, sub, row0, w_in):
    m = x.shape[0]
    tn = 512
    return pl.pallas_call(
        functools.partial(_inproj_kernel, row0),
        grid=(IN_WIDTH // tn,),
        in_specs=[
            pl.BlockSpec((m, D_MODEL), lambda j: (0, 0)),
            pl.BlockSpec((3, mod.shape[1], D_MODEL), lambda j: (sub, 0, 0)),
            pl.BlockSpec((D_MODEL, tn), lambda j: (0, j)),
        ],
        out_specs=pl.BlockSpec((m, tn), lambda j: (0, j)),
        out_shape=jax.ShapeDtypeStruct((m, IN_WIDTH), F32),
        scratch_shapes=[pltpu.VMEM((m, D_MODEL), BF16)],
        compiler_params=pltpu.CompilerParams(dimension_semantics=("arbitrary",), vmem_limit_bytes=VMEM_LIMIT),
        name="mixer_inproj",
    )(x, mod, w_in)


def _outproj_kernel(row0, m_ref, x_ref, mod_ref, w_ref, gb_ref, o_ref):
    rows = slice(row0, row0 + x_ref.shape[0])
    mixed = _dot(m_ref[...], w_ref[...])
    y = ALPHA * x_ref[...] + mod_ref[2, rows, :] * mixed
    o_ref[...] = _layer_norm(y, gb_ref[0:1], gb_ref[1:2])


def _outproj(merged, x, mod, sub, row0, w_o, ln_gb):
    m = x.shape[0]
    return pl.pallas_call(
        functools.partial(_outproj_kernel, row0),
        grid=(1,),
        in_specs=[
            pl.BlockSpec((m, D_MODEL), lambda i: (0, 0)),
            pl.BlockSpec((m, D_MODEL), lambda i: (0, 0)),
            pl.BlockSpec((3, mod.shape[1], D_MODEL), lambda i: (sub, 0, 0)),
            pl.BlockSpec((D_MODEL, D_MODEL), lambda i: (0, 0)),
            pl.BlockSpec((2, D_MODEL), lambda i: (0, 0)),
        ],
        out_specs=pl.BlockSpec((m, D_MODEL), lambda i: (0, 0)),
        out_shape=jax.ShapeDtypeStruct((m, D_MODEL), F32),
        compiler_params=pltpu.CompilerParams(dimension_semantics=("arbitrary",), vmem_limit_bytes=VMEM_LIMIT),
        name="mixer_outproj_ln",
    )(merged, x, mod, w_o, ln_gb)


def _gmlp_norm(pv, gng, gnb):
    v = _gelu_tanh(pv)
    outs = []
    for g in range(N_GMLP_GROUPS):
        sl = slice(g * LANES, (g + 1) * LANES)
        outs.append(_layer_norm(v[:, sl], gng[:, sl], gnb[:, sl]))
    return outs


def _half_lane_tiles(t):
    lo_lane = lax.broadcasted_iota(jnp.int32, (t.shape[0], LANES), 1) < HEAD_DIM
    lo, hi = [], []
    for c in range(KV_WIDTH // LANES):
        col = t[:, c * LANES:(c + 1) * LANES]
        swp = pltpu.roll(col, HEAD_DIM, 1)
        lo += [jnp.where(lo_lane, col, 0.0), jnp.where(lo_lane, swp, 0.0)]
        hi += [jnp.where(lo_lane, 0.0, swp), jnp.where(lo_lane, 0.0, col)]
    return jnp.concatenate(lo + hi, axis=-1).astype(BF16)


def _band_attention(q_s, kx_s, vx_s, attn_s, sinks_ref, nblk, first_tile):
    rows = 2 * WINDOW
    qi = lax.broadcasted_iota(jnp.int32, (rows, WINDOW), 0) & (WINDOW - 1)
    kj = lax.broadcasted_iota(jnp.int32, (rows, WINDOW), 1)
    own = kj <= qi
    first_mask = kj <= jnp.where(first_tile, qi, WINDOW)
    ones = jnp.ones((2 * WINDOW, LANES), BF16)
    units = [(wb, g, hf) for wb in range(nblk) for g in range(N_KV_HEADS) for hf in range(2)]

    def kv_cols(g, hf):
        return slice(hf * 4 * LANES + g * LANES, hf * 4 * LANES + (g + 1) * LANES)

    scores = []
    for wb, g, hf in units:
        r0 = wb * WINDOW
        q2 = jnp.concatenate([q_s[r0:r0 + WINDOW, (2 * g) * LANES:(2 * g + 1) * LANES],
                              q_s[r0:r0 + WINDOW, (2 * g + 1) * LANES:(2 * g + 2) * LANES]], axis=0)
        s = _dot_nt(q2, kx_s[r0:r0 + 2 * WINDOW, kv_cols(g, hf)])
        c = jnp.where(own, s[:, WINDOW:2 * WINDOW], s[:, 0:WINDOW])
        if wb == 0:
            c = jnp.where(first_mask, c, -jnp.inf)
        scores.append(c)

    probs = []
    for (wb, g, hf), c in zip(units, scores):
        sink = jnp.concatenate([jnp.full((WINDOW, LANES), sinks_ref[GQA_GROUP * g + hf], F32),
                                jnp.full((WINDOW, LANES), sinks_ref[GQA_GROUP * g + 2 + hf], F32)], axis=0)
        mx = jnp.maximum(jnp.broadcast_to(jnp.max(c, axis=-1, keepdims=True), c.shape), sink)
        p = jnp.exp(c - mx)
        pcat = jnp.concatenate([jnp.where(own, 0.0, p), jnp.where(own, p, 0.0)], axis=-1).astype(BF16)
        probs.append((pcat, jnp.exp(sink - mx)))

    outs = {}
    for (wb, g, hf), (pcat, esink) in zip(units, probs):
        r0 = wb * WINDOW
        v2 = jnp.concatenate([vx_s[r0:r0 + 2 * WINDOW, kv_cols(g, hf)], ones], axis=-1)
        o = _dot(pcat, v2)
        outs[wb, g, hf] = o[:, 0:LANES] / (o[:, LANES:2 * LANES] + esink)

    for wb in range(nblk):
        r0 = wb * WINDOW
        for g in range(N_KV_HEADS):
            both = outs[wb, g, 0] + outs[wb, g, 1]
            attn_s[r0:r0 + WINDOW, (2 * g) * LANES:(2 * g + 1) * LANES] = both[0:WINDOW]
            attn_s[r0:r0 + WINDOW, (2 * g + 1) * LANES:(2 * g + 2) * LANES] = both[WINDOW:2 * WINDOW]


def _pmix_kernel(tiles_per_seq, n_tiles, sinks_ref, x_ref, mod_ref, win_ref, wo_ref, cos_ref, slo_ref, shi_ref,
                 gng_ref, gnb_ref, ws_ref, bs_ref, og_ref, lngb_ref, side_ref,
                 o_ref, k_ref, v_ref, side_out_ref, p_s, q_s, kk_s, vv_s, kp_s, vp_s, attn_s, m_s, wm_s, y_s):
    n = pl.program_id(0)
    t = n % tiles_per_seq
    tm = x_ref.shape[0]
    nblk = tm // WINDOW

    @pl.when(n == 0)
    def _():
        y_s[...] = jnp.zeros(y_s.shape, F32)
        row = lax.broadcasted_iota(jnp.int32, (CHUNK, CHUNK), 0)
        col = lax.broadcasted_iota(jnp.int32, (CHUNK, CHUNK), 1)
        for g in range(N_GMLP_GROUPS):
            wm_s[g] = jnp.where(col <= row, ws_ref[g], 0.0).astype(BF16)

    @pl.when(t == 0)
    def _():
        kp_s[...] = jnp.zeros(kp_s.shape, BF16)
        vp_s[...] = jnp.zeros(vp_s.shape, BF16)

    @pl.when(n == n_tiles)
    def _():
        side_out_ref[...] = side_ref[...].astype(BF16)
        o_ref[...] = _layer_norm(y_s[...], lngb_ref[0:1], lngb_ref[1:2])

    pl.when(n < n_tiles)(functools.partial(
        _pmix_tile, t, pl.ds(n // tiles_per_seq, 1), sinks_ref, x_ref, mod_ref, win_ref, wo_ref, cos_ref, slo_ref,
        shi_ref, gng_ref, gnb_ref, bs_ref, og_ref, lngb_ref, side_ref, o_ref, k_ref, v_ref, side_out_ref, p_s, q_s,
        kk_s, vv_s, kp_s, vp_s, attn_s, m_s, wm_s, y_s))


def _pmix_tile(t, seq, sinks_ref, x_ref, mod_ref, win_ref, wo_ref, cos_ref, slo_ref, shi_ref, gng_ref, gnb_ref,
               bs_ref, og_ref, lngb_ref, side_ref, o_ref, k_ref, v_ref, side_out_ref, p_s, q_s, kk_s, vv_s, kp_s, vp_s,
               attn_s, m_s, wm_s, y_s):
    tm = x_ref.shape[0]
    nblk = tm // WINDOW

    side_out_ref[...] = side_ref[...].astype(BF16)

    o_ref[...] = _layer_norm(y_s[...], lngb_ref[0:1], lngb_ref[1:2])

    m_s[...] = (x_ref[...] * (1.0 + mod_ref[1, seq, :]) + mod_ref[0, seq, :]).astype(BF16)
    cos, slo, shi = cos_ref[...], slo_ref[...], shi_ref[...]
    for c in range(IN_WIDTH // PROJ_TILE):
        lo_col = c * PROJ_TILE
        h = _dot(m_s[...], win_ref[:, lo_col:lo_col + PROJ_TILE])
        if lo_col + PROJ_TILE <= Q_END:
            q_s[:, lo_col:lo_col + PROJ_TILE] = (_rope_lanes(h, cos, slo, shi) * (HEAD_DIM ** -0.5)).astype(BF16)
        elif lo_col == Q_END and PROJ_TILE == 2 * KV_WIDTH:
            k = _rope_lanes(h[:, 0:KV_WIDTH], cos, slo, shi)
            v = h[:, KV_WIDTH:2 * KV_WIDTH]
            k_ref[0] = k[tm - WINDOW:tm]
            v_ref[0] = v[tm - WINDOW:tm]
            kb, vb = _half_lane_tiles(k), _half_lane_tiles(v)
            kk_s[0:WINDOW, :] = kp_s[...]
            vv_s[0:WINDOW, :] = vp_s[...]
            kk_s[WINDOW:WINDOW + tm, :] = kb
            vv_s[WINDOW:WINDOW + tm, :] = vb
            kp_s[...] = kb[tm - WINDOW:tm]
            vp_s[...] = vb[tm - WINDOW:tm]
        else:
            p_s[:, lo_col - V_END:lo_col - V_END + PROJ_TILE] = h

    _band_attention(q_s, kk_s, vv_s, attn_s, sinks_ref, nblk, t == 0)

    vn = _gmlp_norm(p_s[:, GMLP_WIDTH:2 * GMLP_WIDTH], gng_ref[...], gnb_ref[...])
    gated = []
    for g in range(N_GMLP_GROUPS):
        u = _gelu_tanh(p_s[:, g * LANES:(g + 1) * LANES])
        vg = vn[g].astype(BF16)
        cols = jnp.concatenate([vg[wb * CHUNK:(wb + 1) * CHUNK] for wb in range(nblk)], axis=-1)
        mix = _dot(wm_s[g], cols)
        mix = jnp.concatenate([mix[:, wb * LANES:(wb + 1) * LANES] for wb in range(nblk)], axis=0)
        bias = jnp.concatenate([bs_ref[:, g:g + 1]] * nblk, axis=0)
        gated.append(u * (mix + bias))
    gm = jnp.concatenate(gated, axis=-1)
    og = og_ref[...]
    m_s[:, 0:ATTN_WIDTH] = _rms_norm(attn_s[...], og[:, 0:ATTN_WIDTH]).astype(BF16)
    m_s[:, ATTN_WIDTH:] = _rms_norm(gm, og[:, ATTN_WIDTH:]).astype(BF16)

    for c in range(D_MODEL // PROJ_TILE):
        sl = slice(c * PROJ_TILE, (c + 1) * PROJ_TILE)
        y_s[:, sl] = ALPHA * x_ref[:, sl] + mod_ref[2, seq, sl] * _dot(m_s[...], wo_ref[:, sl])


def _rope_tables(pos):
    half = ROPE_DIM // 2
    inv = ROPE_THETA ** (-(np.arange(half, dtype=np.float64) * 2.0) / ROPE_DIM)
    ang = np.asarray(pos, np.float64)[:, None] * inv[None, :]
    cos, sin = np.cos(ang), np.sin(ang)
    n = ang.shape[0]
    one = np.ones((n, HEAD_DIM - ROPE_DIM))
    zero = np.zeros((n, HEAD_DIM - ROPE_DIM))
    z8 = np.zeros((n, half))
    cos_t = np.concatenate([cos, cos, one], axis=-1)
    sin_lo = np.concatenate([-sin, z8, zero], axis=-1)
    sin_hi = np.concatenate([z8, sin, zero], axis=-1)
    rep = LANES // HEAD_DIM
    return tuple(jnp.asarray(np.tile(t, (1, rep)), F32) for t in (cos_t, sin_lo, sin_hi))


def _pmix(x, mod, sub, w_in, w_o, sinks, tables, gn_g, gn_b, w_s, b_s, out_g, ln_gb, side, *, batch, tm):
    nt = SEQ // tm
    n_tiles = batch * nt
    tile = lambda n: jnp.minimum(n, n_tiles - 1)
    cos_t, sin_lo, sin_hi = tables
    tab_spec = pl.BlockSpec((tm, LANES), lambda n: (tile(n) % nt, 0))
    full2 = lambda n: (0, 0)
    mod_spec = pl.BlockSpec((3, mod.shape[1], D_MODEL), lambda n: (sub, 0, 0))
    kv_spec = pl.BlockSpec((1, WINDOW, KV_WIDTH), lambda n: (tile(n) // nt, 0, 0))
    side_arr, side_lead = side
    side_rows, side_cols = side_arr.shape[-2:]
    assert side_rows % n_tiles == 0
    side_blk = side_rows // n_tiles
    resident = dict(pipeline_mode=pl.Buffered(1))
    return pl.pallas_call(
        functools.partial(_pmix_kernel, nt, n_tiles),
        grid=(n_tiles + 1,),
        in_specs=[
            pl.BlockSpec(memory_space=pltpu.SMEM),
            pl.BlockSpec((tm, D_MODEL), lambda n: (tile(n), 0)),
            mod_spec,
            pl.BlockSpec((D_MODEL, IN_WIDTH), full2, **resident),
            pl.BlockSpec((D_MODEL, D_MODEL), full2, **resident),
            tab_spec, tab_spec, tab_spec,
            pl.BlockSpec((1, GMLP_WIDTH), full2),
            pl.BlockSpec((1, GMLP_WIDTH), full2),
            pl.BlockSpec((N_GMLP_GROUPS, CHUNK, CHUNK), lambda n: (0, 0, 0)),
            pl.BlockSpec((CHUNK, N_GMLP_GROUPS), full2),
            pl.BlockSpec((1, D_MODEL), full2),
            pl.BlockSpec((2, D_MODEL), full2),
            pl.BlockSpec((None,) * len(side_lead) + (side_blk, side_cols), lambda n: (*side_lead, tile(n), 0)),
        ],
        out_specs=[
            pl.BlockSpec((tm, D_MODEL), lambda n: (jnp.maximum(n - 1, 0), 0)),
            kv_spec, kv_spec,
            pl.BlockSpec((side_blk, side_cols), lambda n: (tile(n), 0)),
        ],
        out_shape=[
            jax.ShapeDtypeStruct((batch * SEQ, D_MODEL), F32),
            jax.ShapeDtypeStruct((batch, WINDOW, KV_WIDTH), F32),
            jax.ShapeDtypeStruct((batch, WINDOW, KV_WIDTH), F32),
            jax.ShapeDtypeStruct((side_rows, side_cols), BF16),
        ],
        scratch_shapes=[
            pltpu.VMEM((tm, 2 * GMLP_WIDTH), F32),
            pltpu.VMEM((tm, ATTN_WIDTH), BF16),
            pltpu.VMEM((WINDOW + tm, 4 * KV_WIDTH), BF16),
            pltpu.VMEM((WINDOW + tm, 4 * KV_WIDTH), BF16),
            pltpu.VMEM((WINDOW, 4 * KV_WIDTH), BF16),
            pltpu.VMEM((WINDOW, 4 * KV_WIDTH), BF16),
            pltpu.VMEM((tm, ATTN_WIDTH), F32),
            pltpu.VMEM((tm, D_MODEL), BF16),
            pltpu.VMEM((N_GMLP_GROUPS, CHUNK, CHUNK), BF16),
            pltpu.VMEM((tm, D_MODEL), F32),
        ],
        compiler_params=pltpu.CompilerParams(dimension_semantics=("arbitrary",), vmem_limit_bytes=VMEM_LIMIT),
        name="prompt_mixer",
    )(sinks, x, mod, w_in, w_o, cos_t, sin_lo, sin_hi, gn_g.reshape(1, GMLP_WIDTH),
      gn_b.reshape(1, GMLP_WIDTH), w_s, b_s.T, out_g.reshape(1, D_MODEL), ln_gb, side_arr)


def _sattn_kernel(qe_ref, p_ref, ck_ref, cv_ref, sink_ref, cos_ref, slo_ref, shi_ref, o_ref, ko_ref, vo_ref):
    cos, slo, shi = cos_ref[...], slo_ref[...], shi_ref[...]
    sink = sink_ref[...]
    seqs = range(qe_ref.shape[0])
    key = lax.broadcasted_iota(jnp.int32, (N_HEADS, WINDOW), 1)
    head = lax.broadcasted_iota(jnp.int32, (N_HEADS, KV_WIDTH), 0)
    lane = lax.broadcasted_iota(jnp.int32, (N_HEADS, KV_WIDTH), 1)
    newest = lax.broadcasted_iota(jnp.int32, (WINDOW, KV_WIDTH), 0) == WINDOW - 1

    scores = []
    for r in seqs:
        qe = _rope_lanes(qe_ref[r], cos, slo, shi) * (HEAD_DIM ** -0.5)
        kn = _rope_lanes(p_ref[r:r + 1, Q_END:K_END], cos, slo, shi)
        s_c = _dot_nt(qe.astype(BF16), ck_ref[r].astype(BF16))
        s_c = jnp.where(key >= 1, s_c, -jnp.inf)
        s_n = jnp.sum(qe * kn, axis=-1, keepdims=True)
        scores.append((kn, s_c, s_n))
        ko_ref[r] = jnp.where(newest, kn, pltpu.roll(ck_ref[r], WINDOW - 1, 0))

    probs = []
    for kn, s_c, s_n in scores:
        mx = jnp.maximum(jnp.maximum(jnp.max(s_c, axis=-1, keepdims=True), s_n), sink)
        p_c = jnp.exp(s_c - mx)
        p_n = jnp.exp(s_n - mx)
        denom = jnp.sum(p_c, axis=-1, keepdims=True) + p_n + jnp.exp(sink - mx)
        probs.append((p_c, p_n, denom))

    for r, (p_c, p_n, denom) in zip(seqs, probs):
        vn = p_ref[r:r + 1, K_END:V_END]
        cv = cv_ref[r]
        o = (_dot(p_c.astype(BF16), cv.astype(BF16)) + p_n * vn) / denom
        o_ref[r] = jnp.where(head // GQA_GROUP == lane // HEAD_DIM, o, 0.0)
        vo_ref[r] = jnp.where(newest, vn, pltpu.roll(cv, WINDOW - 1, 0))


def _sattn(qe, p3, ck, cv, sinks, tables):
    b = qe.shape[0]
    rows = SAMPLE_ROWS_PER_STEP
    cos_t, sin_lo, sin_hi = tables
    tab_spec = pl.BlockSpec((1, LANES), lambda i: (0, 0))
    blk3 = lambda i: (i, 0, 0)
    return pl.pallas_call(
        _sattn_kernel,
        grid=(b // rows,),
        in_specs=[
            pl.BlockSpec((rows, N_HEADS, KV_WIDTH), blk3),
            pl.BlockSpec((rows, IN_WIDTH), lambda i: (i, 0)),
            pl.BlockSpec((rows, WINDOW, KV_WIDTH), blk3),
            pl.BlockSpec((rows, WINDOW, KV_WIDTH), blk3),
            pl.BlockSpec((N_HEADS, 1), lambda i: (0, 0)),
            tab_spec, tab_spec, tab_spec,
        ],
        out_specs=[
            pl.BlockSpec((rows, N_HEADS, KV_WIDTH), blk3),
            pl.BlockSpec((rows, WINDOW, KV_WIDTH), blk3),
            pl.BlockSpec((rows, WINDOW, KV_WIDTH), blk3),
        ],
        out_shape=[
            jax.ShapeDtypeStruct((b, N_HEADS, KV_WIDTH), F32),
            jax.ShapeDtypeStruct((b, WINDOW, KV_WIDTH), F32),
            jax.ShapeDtypeStruct((b, WINDOW, KV_WIDTH), F32),
        ],
        compiler_params=pltpu.CompilerParams(dimension_semantics=("arbitrary",)),
        name="sample_attn",
    )(qe, p3, ck, cv, sinks.reshape(N_HEADS, 1), cos_t, sin_lo, sin_hi)


def _smerge_kernel(attn_ref, p_ref, gng_ref, gnb_ref, w0_ref, b0_ref, og_ref, m_ref, vn_ref):
    og = og_ref[...]
    vn = jnp.concatenate(_gmlp_norm(p_ref[:, U_END:IN_WIDTH], gng_ref[...], gnb_ref[...]), axis=-1)
    vn_ref[...] = vn
    u = _gelu_tanh(p_ref[:, V_END:U_END])
    gm = u * (w0_ref[...] * vn + b0_ref[...])
    m_ref[:, 0:ATTN_WIDTH] = _rms_norm(attn_ref[...], og[:, 0:ATTN_WIDTH]).astype(BF16)
    m_ref[:, ATTN_WIDTH:] = _rms_norm(gm, og[:, ATTN_WIDTH:]).astype(BF16)


def _smerge(attn, p, gn_g, gn_b, w_s, b_s, out_g):
    b = attn.shape[0]
    w0 = jnp.repeat(w_s[:, 0, 0], LANES).reshape(1, GMLP_WIDTH)
    b0 = jnp.repeat(b_s[:, 0], LANES).reshape(1, GMLP_WIDTH)
    return pl.pallas_call(
        _smerge_kernel,
        out_shape=[
            jax.ShapeDtypeStruct((b, D_MODEL), BF16),
            jax.ShapeDtypeStruct((b, GMLP_WIDTH), F32),
        ],
        name="sample_gmlp_merge",
    )(attn, p, gn_g.reshape(1, GMLP_WIDTH), gn_b.reshape(1, GMLP_WIDTH), w0, b0, out_g.reshape(1, D_MODEL))


def kernel(x_prompt, x_sample, cache_k_win, cache_v_win, c_prompt, c_sample, w_ada, b_ada, ln_g, ln_b,
           w_ffn_up, w_ffn_down, w_in, attn_sinks, gmlp_norm_g, gmlp_norm_b, w_spatial, b_spatial,
           out_norm_g, w_o):
    batch, seq, _ = x_prompt.shape
    dec_batch = x_sample.shape[0]
    buf = cache_k_win.shape[2]
    assert seq == SEQ and buf == WINDOW and x_sample.shape[1] == 1 and w_ada.shape[0] == DEPTH == 1

    xp = x_prompt.reshape(batch * seq, D_MODEL)
    xs = x_sample.reshape(dec_batch, D_MODEL)

    n_c = batch + dec_batch
    n_c_pad = ((n_c + 7) // 8) * 8
    c_all = jnp.concatenate([c_prompt, c_sample, jnp.zeros((n_c_pad - n_c, D_MODEL), F32)], axis=0)
    mod = _ada(c_all, w_ada[0], b_ada[0])

    ln_gb = jnp.stack([ln_g[0], ln_b[0]], axis=1)
    ffn = functools.partial(_ffn, tm=FFN_ROWS, rows_per_mod=seq, extra_row0=batch)

    w_up0 = w_ffn_up[0, 0].astype(BF16)
    w_down0 = w_ffn_down[0, 0].astype(BF16)
    later = ((w_ffn_down, (0, 1), 32), (w_in, (0,), 16), (w_o, (0,), 16))
    xp, xs, w_down1, w_in_b, w_o_b = ffn(xp, mod, 0, w_up0, w_down0, ln_gb[0], extra=xs, side=later)

    xp, k_p, v_p, w_up1 = _pmix(xp, mod, 1, w_in_b, w_o_b, attn_sinks[0], _rope_tables(np.arange(seq)),
                                gmlp_norm_g[0], gmlp_norm_b[0], w_spatial[0], b_spatial[0], out_norm_g[0], ln_gb[1],
                                (w_ffn_up, (0, 1)), batch=batch, tm=MIX_TILE)

    ps = _inproj(xs, mod, 1, batch, w_in_b)
    eye = jnp.eye(N_KV_HEADS, dtype=F32)
    q5 = ps[:, :Q_END].reshape(dec_batch, N_KV_HEADS, GQA_GROUP, 1, HEAD_DIM)
    qe = (q5 * eye[None, :, None, :, None]).reshape(dec_batch, N_HEADS, KV_WIDTH)
    ck = cache_k_win[0].reshape(dec_batch, buf, KV_WIDTH)
    cv = cache_v_win[0].reshape(dec_batch, buf, KV_WIDTH)
    oe, k_s, v_s = _sattn(qe, ps, ck, cv, attn_sinks[0],
                          _rope_tables(np.full((1,), PAST_LEN)))
    attn_s = oe.reshape(dec_batch, N_KV_HEADS, GQA_GROUP, N_KV_HEADS, HEAD_DIM).sum(axis=3).reshape(dec_batch, ATTN_WIDTH)
    merged_s, vn_s = _smerge(attn_s, ps, gmlp_norm_g[0], gmlp_norm_b[0], w_spatial[0], b_spatial[0], out_norm_g[0])
    xs = _outproj(merged_s, xs, mod, 1, batch, w_o_b, ln_gb[1])

    xp, xs = ffn(xp, mod, 2, w_up1, w_down1, ln_gb[2], extra=xs)

    return (
        xp.reshape(batch, seq, D_MODEL),
        xs.reshape(dec_batch, 1, D_MODEL),
        k_p.reshape(1, batch, WINDOW, N_KV_HEADS, HEAD_DIM),
        v_p.reshape(1, batch, WINDOW, N_KV_HEADS, HEAD_DIM),
        k_s.reshape(1, dec_batch, buf, N_KV_HEADS, HEAD_DIM),
        v_s.reshape(1, dec_batch, buf, N_KV_HEADS, HEAD_DIM),
        vn_s.reshape(1, dec_batch, 1, GMLP_WIDTH),
    )
```

```python
import functools

import jax
import jax.numpy as jnp
import numpy as np
from jax import lax
from jax.experimental import pallas as pl
from jax.experimental.pallas import tpu as pltpu

D_MODEL = 2048
SEQ = 2048
PAST_LEN = 16384
ATTN_WIDTH = 1024
GMLP_WIDTH = 1024
HEAD_DIM = 64
N_HEADS = 16
N_KV_HEADS = 4
GQA_GROUP = 4
KV_WIDTH = 256
WINDOW = 128
ROPE_THETA = 500000.0
ROPE_DIM = 16
CHUNK = 128
N_GMLP_GROUPS = 8
D_FF = 5504
N_SUB = 3
DEPTH = 1
ALPHA = (2.0 * DEPTH) ** 0.25
LN_EPS = 1e-5
Q_END = ATTN_WIDTH
K_END = Q_END + KV_WIDTH
V_END = K_END + KV_WIDTH
U_END = V_END + GMLP_WIDTH
IN_WIDTH = U_END + GMLP_WIDTH

LANES = 128
MXU_COLS = 256
FF_TILE = 512
N_FF_STEPS = -(-D_FF // FF_TILE)
FF_LAST_OFF = D_FF - FF_TILE
FF_OVERLAP = N_FF_STEPS * FF_TILE - D_FF
OUT_TILE = 512
N_OUT_STEPS = D_MODEL // OUT_TILE
FFN_ROWS = 512
LN_PIECE_ROWS = 64
MIX_TILE = 256
PROJ_TILE = 512
SAMPLE_ROWS_PER_STEP = 8
VMEM_LIMIT = 60 * 1024 * 1024

BF16 = jnp.bfloat16
F32 = jnp.float32


def _dot(a, b):
    return jnp.dot(a, b, preferred_element_type=F32)


def _dot_nt(a, b):
    return lax.dot_general(a, b, (((1,), (1,)), ((), ())), preferred_element_type=F32)


def _layer_norm(y, g, b):
    mu = jnp.mean(y, axis=-1, keepdims=True)
    d = y - mu
    var = jnp.mean(d * d, axis=-1, keepdims=True)
    return d * lax.rsqrt(var + LN_EPS) * g + b


def _rms_norm(y, g):
    return y * lax.rsqrt(jnp.mean(y * y, axis=-1, keepdims=True) + LN_EPS) * g


def _gelu_tanh(x):
    c = np.float32(np.sqrt(2.0 / np.pi))
    return 0.5 * x * (1.0 + jnp.tanh(c * (x + 0.044715 * (x * x * x))))


def _rope_lanes(t, cos, sin_lo, sin_hi):
    pieces = []
    for c in range(t.shape[-1] // LANES):
        x = t[:, c * LANES:(c + 1) * LANES]
        pieces.append(x * cos + pltpu.roll(x, 8, 1) * sin_hi + pltpu.roll(x, LANES - 8, 1) * sin_lo)
    return pieces[0] if len(pieces) == 1 else jnp.concatenate(pieces, axis=-1)


def _ada_kernel(c_ref, w_ref, b_ref, o_ref):
    c = c_ref[...]
    h = (c * jax.nn.sigmoid(c)).astype(BF16)
    o_ref[0] = _dot(h, w_ref[...].astype(BF16)) + b_ref[...]


def _ada(c, w_ada, b_ada):
    rows = c.shape[0]
    n = w_ada.shape[1]
    tn = 1024
    per_plane = D_MODEL // tn
    return pl.pallas_call(
        _ada_kernel,
        grid=(n // tn,),
        in_specs=[
            pl.BlockSpec((rows, D_MODEL), lambda j: (0, 0)),
            pl.BlockSpec((D_MODEL, tn), lambda j: (0, j)),
            pl.BlockSpec((1, tn), lambda j: (0, j)),
        ],
        out_specs=pl.BlockSpec((1, rows, tn), lambda j: (j // per_plane, 0, j % per_plane)),
        out_shape=jax.ShapeDtypeStruct((n // D_MODEL, rows, D_MODEL), F32),
        compiler_params=pltpu.CompilerParams(dimension_semantics=("arbitrary",), vmem_limit_bytes=VMEM_LIMIT),
        name="ada_mod",
    )(c, w_ada, b_ada.reshape(1, n))


def _ff_offset(step):
    return jnp.minimum(step * FF_TILE, FF_LAST_OFF)


def _ffn_kernel(n_side, n_extra, tiles_per_mod, extra_row0, n_tiles, *refs):
    x_ref, mod_ref, wv_ref, wg_ref, wd_ref, gb_ref = refs[:6]
    n_in = 6
    if n_extra:
        xe_ref = refs[n_in]
        n_in += 1
        erows = slice(extra_row0, extra_row0 + n_extra)
    side_in = refs[n_in:n_in + n_side]
    n_in += n_side
    o_ref = refs[n_in]
    n_out = 1
    if n_extra:
        oe_ref = refs[n_in + 1]
        n_out = 2
    side_out = refs[n_in + n_out:n_in + n_out + n_side]
    xm_ref, a_ref, y_ref = refs[n_in + n_out + n_side:]
    i = pl.program_id(0)
    s = pl.program_id(1)
    tm = x_ref.shape[0]
    last = pl.num_programs(1) - 1

    def cast_side():
        for src, dst in zip(side_in, side_out):
            dst[...] = src[...].astype(BF16)

    def norm_piece():
        r0 = pl.multiple_of(jnp.minimum(s, tm // LN_PIECE_ROWS - 1) * LN_PIECE_ROWS, LN_PIECE_ROWS)
        piece = pl.ds(r0, LN_PIECE_ROWS)
        y = jnp.concatenate([y_ref[n, piece, :] for n in range(N_OUT_STEPS)], axis=-1)
        o_ref[piece, :] = _layer_norm(y, gb_ref[0:1], gb_ref[1:2])

    def up_chunk(rows, norm_previous):
        cast_side()
        if norm_previous:
            norm_piece()
        xm = xm_ref[0:rows]
        for c in range(FF_TILE // MXU_COLS):
            sl = slice(c * MXU_COLS, (c + 1) * MXU_COLS)
            hv = _dot(xm, wv_ref[:, sl])
            hg = _dot(xm, wg_ref[:, sl])
            a = hg * jax.nn.sigmoid(hg) * hv
            if c * MXU_COLS < FF_OVERLAP:
                col = lax.broadcasted_iota(jnp.int32, a.shape, 1) + c * MXU_COLS
                a = jnp.where(jnp.logical_and(s == N_FF_STEPS - 1, col < FF_OVERLAP), 0.0, a)
            a_ref[s, 0:rows, sl] = a.astype(BF16)

    def down_slab(rows):
        cast_side()
        acc = None
        for f in range(N_FF_STEPS):
            r0 = min(f * FF_TILE, FF_LAST_OFF)
            part = _dot(a_ref[f, 0:rows], wd_ref[r0:r0 + FF_TILE, :])
            acc = part if acc is None else acc + part
        y_ref[s - N_FF_STEPS, 0:rows] = acc

    first = i == 0
    later = i > 0
    up_phase = s < N_FF_STEPS
    down_phase = s >= N_FF_STEPS
    seq = pl.ds(i // tiles_per_mod, 1)

    @pl.when(s == 0)
    def _():
        xm_ref[0:tm] = (x_ref[...] * (1.0 + mod_ref[1, seq, :]) + mod_ref[0, seq, :]).astype(BF16)

    if n_extra:
        @pl.when(jnp.logical_and(s == 0, first))
        def _():
            xm_ref[tm:tm + n_extra] = (xe_ref[...] * (1.0 + mod_ref[1, erows, :]) + mod_ref[0, erows, :]).astype(BF16)

    pl.when(jnp.logical_and(up_phase, first))(functools.partial(up_chunk, tm + n_extra, False))
    pl.when(jnp.logical_and(up_phase, later))(functools.partial(up_chunk, tm, True))
    pl.when(jnp.logical_and(down_phase, first))(functools.partial(down_slab, tm + n_extra))
    pl.when(jnp.logical_and(down_phase, later))(functools.partial(down_slab, tm))

    @pl.when(s == last)
    def _():
        for n in range(N_OUT_STEPS):
            sl = slice(n * OUT_TILE, (n + 1) * OUT_TILE)
            y_ref[n, 0:tm] = ALPHA * x_ref[:, sl] + 0.5 * mod_ref[2, seq, sl] * y_ref[n, 0:tm]

    @pl.when(jnp.logical_and(s == last, i == n_tiles - 1))
    def _():
        y = jnp.concatenate([y_ref[n, 0:tm] for n in range(N_OUT_STEPS)], axis=-1)
        o_ref[...] = _layer_norm(y, gb_ref[0:1], gb_ref[1:2])

    if n_extra:
        @pl.when(jnp.logical_and(s == last, first))
        def _():
            mixed = jnp.concatenate([y_ref[n, tm:tm + n_extra] for n in range(N_OUT_STEPS)], axis=-1)
            oe_ref[...] = _layer_norm(ALPHA * xe_ref[...] + 0.5 * mod_ref[2, erows, :] * mixed, gb_ref[0:1], gb_ref[1:2])


def _ffn(x, mod, sub, w_up, w_down, ln_gb, *, tm, rows_per_mod, extra=None, extra_row0=0, side=()):
    m = x.shape[0]
    n_steps = N_FF_STEPS + N_OUT_STEPS
    tiles_per_mod = rows_per_mod // tm
    n_extra = 0 if extra is None else extra.shape[0]
    const2 = lambda i, s: (0, 0)
    extra_spec = pl.BlockSpec((n_extra, D_MODEL), const2)
    side_in_specs, side_out_specs, side_shapes = [], [], []
    for arr, lead, rows in side:
        n_rows, n_cols = arr.shape[-2:]
        n_blocks = n_rows // rows
        assert n_blocks * rows == n_rows and n_blocks <= (m // tm) * n_steps
        blk = lambda i, s, n_blocks=n_blocks: jnp.minimum(i * n_steps + s, n_blocks - 1)
        side_in_specs.append(pl.BlockSpec((None,) * len(lead) + (rows, n_cols),
                                          lambda i, s, lead=lead, blk=blk: (*lead, blk(i, s), 0)))
        side_out_specs.append(pl.BlockSpec((rows, n_cols), lambda i, s, blk=blk: (blk(i, s), 0)))
        side_shapes.append(jax.ShapeDtypeStruct((n_rows, n_cols), BF16))
    n_tiles = m // tm
    assert tm % LN_PIECE_ROWS == 0 and tm // LN_PIECE_ROWS <= N_FF_STEPS
    wd_block = lambda i, s: (0, jnp.where(s < N_FF_STEPS, N_OUT_STEPS - 1, s - N_FF_STEPS))
    out_block = lambda i, s: (jnp.where(s < N_FF_STEPS, jnp.maximum(i - 1, 0), i), 0)
    outs = pl.pallas_call(
        functools.partial(_ffn_kernel, len(side), n_extra, tiles_per_mod, extra_row0, n_tiles),
        grid=(n_tiles, n_steps),
        in_specs=[
            pl.BlockSpec((tm, D_MODEL), lambda i, s: (i, 0)),
            pl.BlockSpec((3, mod.shape[1], D_MODEL), lambda i, s: (sub, 0, 0)),
            pl.BlockSpec((pl.Element(D_MODEL), pl.Element(FF_TILE)),
                         lambda i, s: (0, pl.multiple_of(_ff_offset(s), LANES))),
            pl.BlockSpec((pl.Element(D_MODEL), pl.Element(FF_TILE)),
                         lambda i, s: (0, pl.multiple_of(D_FF + _ff_offset(s), LANES))),
            pl.BlockSpec((D_FF, OUT_TILE), wd_block),
            pl.BlockSpec((2, D_MODEL), const2),
            *([extra_spec] if n_extra else []),
            *side_in_specs,
        ],
        out_specs=[pl.BlockSpec((tm, D_MODEL), out_block), *([extra_spec] if n_extra else []),
                   *side_out_specs],
        out_shape=[jax.ShapeDtypeStruct((m, D_MODEL), F32),
                   *([jax.ShapeDtypeStruct((n_extra, D_MODEL), F32)] if n_extra else []), *side_shapes],
        scratch_shapes=[
            pltpu.VMEM((tm + n_extra, D_MODEL), BF16),
            pltpu.VMEM((N_FF_STEPS, tm + n_extra, FF_TILE), BF16),
            pltpu.VMEM((N_OUT_STEPS, tm + n_extra, OUT_TILE), F32),
        ],
        compiler_params=pltpu.CompilerParams(dimension_semantics=("arbitrary", "arbitrary"), vmem_limit_bytes=VMEM_LIMIT),
        name="swiglu_ln",
    )(x, mod, w_up, w_up, w_down, ln_gb, *([extra] if n_extra else []), *[arr for arr, _, _ in side])
    return outs


def _inproj_kernel(row0, x_ref, mod_ref, w_ref, o_ref, xm_ref):
    rows = slice(row0, row0 + x_ref.shape[0])

    @pl.when(pl.program_id(0) == 0)
    def _():
        xm_ref[...] = (x_ref[...] * (1.0 + mod_ref[1, rows, :]) + mod_ref[0, rows, :]).astype(BF16)

    o_ref[...] = _dot(xm_ref[...], w_ref[...])


def _inproj(x, mod, sub, row0, w_in):
    m = x.shape[0]
    tn = IN_WIDTH // 2
    return pl.pallas_call(
        functools.partial(_inproj_kernel, row0),
        grid=(IN_WIDTH // tn,),
        in_specs=[
            pl.BlockSpec((m, D_MODEL), lambda j: (0, 0)),
            pl.BlockSpec((3, mod.shape[1], D_MODEL), lambda j: (sub, 0, 0)),
            pl.BlockSpec((D_MODEL, tn), lambda j: (0, j)),
        ],
        out_specs=pl.BlockSpec((m, tn), lambda j: (0, j)),
        out_shape=jax.ShapeDtypeStruct((m, IN_WIDTH), F32),
        scratch_shapes=[pltpu.VMEM((m, D_MODEL), BF16)],
        compiler_params=pltpu.CompilerParams(dimension_semantics=("arbitrary",), vmem_limit_bytes=VMEM_LIMIT),
        name="mixer_inproj",
    )(x, mod, w_in)


def _outproj_kernel(row0, m_ref, x_ref, mod_ref, w_ref, gb_ref, o_ref):
    rows = slice(row0, row0 + x_ref.shape[0])
    mixed = _dot(m_ref[...], w_ref[...])
    y = ALPHA * x_ref[...] + mod_ref[2, rows, :] * mixed
    o_ref[...] = _layer_norm(y, gb_ref[0:1], gb_ref[1:2])


def _outproj(merged, x, mod, sub, row0, w_o, ln_gb):
    m = x.shape[0]
    return pl.pallas_call(
        functools.partial(_outproj_kernel, row0),
        grid=(1,),
        in_specs=[
            pl.BlockSpec((m, D_MODEL), lambda i: (0, 0)),
            pl.BlockSpec((m, D_MODEL), lambda i: (0, 0)),
            pl.BlockSpec((3, mod.shape[1], D_MODEL), lambda i: (sub, 0, 0)),
            pl.BlockSpec((D_MODEL, D_MODEL), lambda i: (0, 0)),
            pl.BlockSpec((2, D_MODEL), lambda i: (0, 0)),
        ],
        out_specs=pl.BlockSpec((m, D_MODEL), lambda i: (0, 0)),
        out_shape=jax.ShapeDtypeStruct((m, D_MODEL), F32),
        compiler_params=pltpu.CompilerParams(dimension_semantics=("arbitrary",), vmem_limit_bytes=VMEM_LIMIT),
        name="mixer_outproj_ln",
    )(merged, x, mod, w_o, ln_gb)


def _gmlp_norm(pv, gng, gnb):
    v = _gelu_tanh(pv)
    outs = []
    for g in range(N_GMLP_GROUPS):
        sl = slice(g * LANES, (g + 1) * LANES)
        outs.append(_layer_norm(v[:, sl], gng[:, sl], gnb[:, sl]))
    return outs


def _half_lane_tiles(t):
    lo_lane = lax.broadcasted_iota(jnp.int32, (t.shape[0], LANES), 1) < HEAD_DIM
    lo, hi = [], []
    for c in range(KV_WIDTH // LANES):
        col = t[:, c * LANES:(c + 1) * LANES]
        swp = pltpu.roll(col, HEAD_DIM, 1)
        lo += [jnp.where(lo_lane, col, 0.0), jnp.where(lo_lane, swp, 0.0)]
        hi += [jnp.where(lo_lane, 0.0, swp), jnp.where(lo_lane, 0.0, col)]
    return jnp.concatenate(lo + hi, axis=-1).astype(BF16)


def _band_attention(q_s, kx_s, vx_s, attn_s, sinks_ref, nblk, first_tile):
    rows = 2 * WINDOW
    qi = lax.broadcasted_iota(jnp.int32, (rows, WINDOW), 0) & (WINDOW - 1)
    kj = lax.broadcasted_iota(jnp.int32, (rows, WINDOW), 1)
    own = kj <= qi
    first_mask = kj <= jnp.where(first_tile, qi, WINDOW)
    ones = jnp.ones((2 * WINDOW, LANES), BF16)
    units = [(wb, g, hf) for wb in range(nblk) for g in range(N_KV_HEADS) for hf in range(2)]

    def kv_cols(g, hf):
        return slice(hf * 4 * LANES + g * LANES, hf * 4 * LANES + (g + 1) * LANES)

    scores = []
    for wb, g, hf in units:
        r0 = wb * WINDOW
        q2 = jnp.concatenate([q_s[r0:r0 + WINDOW, (2 * g) * LANES:(2 * g + 1) * LANES],
                              q_s[r0:r0 + WINDOW, (2 * g + 1) * LANES:(2 * g + 2) * LANES]], axis=0)
        s = _dot_nt(q2, kx_s[r0:r0 + 2 * WINDOW, kv_cols(g, hf)])
        c = jnp.where(own, s[:, WINDOW:2 * WINDOW], s[:, 0:WINDOW])
        if wb == 0:
            c = jnp.where(first_mask, c, -jnp.inf)
        scores.append(c)

    probs = []
    for (wb, g, hf), c in zip(units, scores):
        sink = jnp.concatenate([jnp.full((WINDOW, LANES), sinks_ref[GQA_GROUP * g + hf], F32),
                                jnp.full((WINDOW, LANES), sinks_ref[GQA_GROUP * g + 2 + hf], F32)], axis=0)
        mx = jnp.maximum(jnp.broadcast_to(jnp.max(c, axis=-1, keepdims=True), c.shape), sink)
        p = jnp.exp(c - mx)
        pcat = jnp.concatenate([jnp.where(own, 0.0, p), jnp.where(own, p, 0.0)], axis=-1).astype(BF16)
        probs.append((pcat, jnp.exp(sink - mx)))

    outs = {}
    for (wb, g, hf), (pcat, esink) in zip(units, probs):
        r0 = wb * WINDOW
        v2 = jnp.concatenate([vx_s[r0:r0 + 2 * WINDOW, kv_cols(g, hf)], ones], axis=-1)
        o = _dot(pcat, v2)
        outs[wb, g, hf] = o[:, 0:LANES] / (o[:, LANES:2 * LANES] + esink)

    for wb in range(nblk):
        r0 = wb * WINDOW
        for g in range(N_KV_HEADS):
            both = outs[wb, g, 0] + outs[wb, g, 1]
            attn_s[r0:r0 + WINDOW, (2 * g) * LANES:(2 * g + 1) * LANES] = both[0:WINDOW]
            attn_s[r0:r0 + WINDOW, (2 * g + 1) * LANES:(2 * g + 2) * LANES] = both[WINDOW:2 * WINDOW]


def _pmix_kernel(tiles_per_seq, n_tiles, sinks_ref, x_ref, mod_ref, win_ref, wo_ref, cos_ref, slo_ref, shi_ref,
                 gng_ref, gnb_ref, ws_ref, bs_ref, og_ref, lngb_ref, side_ref,
                 o_ref, k_ref, v_ref, side_out_ref, p_s, q_s, kk_s, vv_s, kp_s, vp_s, attn_s, m_s, wm_s, y_s):
    n = pl.program_id(0)
    t = n % tiles_per_seq
    tm = x_ref.shape[0]
    nblk = tm // WINDOW

    @pl.when(n == 0)
    def _():
        y_s[...] = jnp.zeros(y_s.shape, F32)
        row = lax.broadcasted_iota(jnp.int32, (CHUNK, CHUNK), 0)
        col = lax.broadcasted_iota(jnp.int32, (CHUNK, CHUNK), 1)
        for g in range(N_GMLP_GROUPS):
            wm_s[g] = jnp.where(col <= row, ws_ref[g], 0.0).astype(BF16)

    @pl.when(t == 0)
    def _():
        kp_s[...] = jnp.zeros(kp_s.shape, BF16)
        vp_s[...] = jnp.zeros(vp_s.shape, BF16)

    @pl.when(n == n_tiles)
    def _():
        side_out_ref[...] = side_ref[...].astype(BF16)
        o_ref[...] = _layer_norm(y_s[...], lngb_ref[0:1], lngb_ref[1:2])

    pl.when(n < n_tiles)(functools.partial(
        _pmix_tile, t, pl.ds(n // tiles_per_seq, 1), sinks_ref, x_ref, mod_ref, win_ref, wo_ref, cos_ref, slo_ref,
        shi_ref, gng_ref, gnb_ref, bs_ref, og_ref, lngb_ref, side_ref, o_ref, k_ref, v_ref, side_out_ref, p_s, q_s,
        kk_s, vv_s, kp_s, vp_s, attn_s, m_s, wm_s, y_s))


def _pmix_tile(t, seq, sinks_ref, x_ref, mod_ref, win_ref, wo_ref, cos_ref, slo_ref, shi_ref, gng_ref, gnb_ref,
               bs_ref, og_ref, lngb_ref, side_ref, o_ref, k_ref, v_ref, side_out_ref, p_s, q_s, kk_s, vv_s, kp_s, vp_s,
               attn_s, m_s, wm_s, y_s):
    tm = x_ref.shape[0]
    nblk = tm // WINDOW

    side_out_ref[...] = side_ref[...].astype(BF16)

    o_ref[...] = _layer_norm(y_s[...], lngb_ref[0:1], lngb_ref[1:2])

    m_s[...] = (x_ref[...] * (1.0 + mod_ref[1, seq, :]) + mod_ref[0, seq, :]).astype(BF16)
    cos, slo, shi = cos_ref[...], slo_ref[...], shi_ref[...]
    for c in range(IN_WIDTH // PROJ_TILE):
        lo_col = c * PROJ_TILE
        h = _dot(m_s[...], win_ref[:, lo_col:lo_col + PROJ_TILE])
        if lo_col + PROJ_TILE <= Q_END:
            q_s[:, lo_col:lo_col + PROJ_TILE] = (_rope_lanes(h, cos, slo, shi) * (HEAD_DIM ** -0.5)).astype(BF16)
        elif lo_col == Q_END and PROJ_TILE == 2 * KV_WIDTH:
            k = _rope_lanes(h[:, 0:KV_WIDTH], cos, slo, shi)
            v = h[:, KV_WIDTH:2 * KV_WIDTH]
            k_ref[0] = k[tm - WINDOW:tm]
            v_ref[0] = v[tm - WINDOW:tm]
            kb, vb = _half_lane_tiles(k), _half_lane_tiles(v)
            kk_s[0:WINDOW, :] = kp_s[...]
            vv_s[0:WINDOW, :] = vp_s[...]
            kk_s[WINDOW:WINDOW + tm, :] = kb
            vv_s[WINDOW:WINDOW + tm, :] = vb
            kp_s[...] = kb[tm - WINDOW:tm]
            vp_s[...] = vb[tm - WINDOW:tm]
        else:
            p_s[:, lo_col - V_END:lo_col - V_END + PROJ_TILE] = h

    _band_attention(q_s, kk_s, vv_s, attn_s, sinks_ref, nblk, t == 0)

    vn = _gmlp_norm(p_s[:, GMLP_WIDTH:2 * GMLP_WIDTH], gng_ref[...], gnb_ref[...])
    gated = []
    for g in range(N_GMLP_GROUPS):
        u = _gelu_tanh(p_s[:, g * LANES:(g + 1) * LANES])
        vg = vn[g].astype(BF16)
        cols = jnp.concatenate([vg[wb * CHUNK:(wb + 1) * CHUNK] for wb in range(nblk)], axis=-1)
        mix = _dot(wm_s[g], cols)
        mix = jnp.concatenate([mix[:, wb * LANES:(wb + 1) * LANES] for wb in range(nblk)], axis=0)
        bias = jnp.concatenate([bs_ref[:, g:g + 1]] * nblk, axis=0)
        gated.append(u * (mix + bias))
    gm = jnp.concatenate(gated, axis=-1)
    og = og_ref[...]
    m_s[:, 0:ATTN_WIDTH] = _rms_norm(attn_s[...], og[:, 0:ATTN_WIDTH]).astype(BF16)
    m_s[:, ATTN_WIDTH:] = _rms_norm(gm, og[:, ATTN_WIDTH:]).astype(BF16)

    for c in range(D_MODEL // PROJ_TILE):
        sl = slice(c * PROJ_TILE, (c + 1) * PROJ_TILE)
        y_s[:, sl] = ALPHA * x_ref[:, sl] + mod_ref[2, seq, sl] * _dot(m_s[...], wo_ref[:, sl])


def _rope_tables(pos):
    half = ROPE_DIM // 2
    inv = ROPE_THETA ** (-(np.arange(half, dtype=np.float64) * 2.0) / ROPE_DIM)
    ang = np.asarray(pos, np.float64)[:, None] * inv[None, :]
    cos, sin = np.cos(ang), np.sin(ang)
    n = ang.shape[0]
    one = np.ones((n, HEAD_DIM - ROPE_DIM))
    zero = np.zeros((n, HEAD_DIM - ROPE_DIM))
    z8 = np.zeros((n, half))
    cos_t = np.concatenate([cos, cos, one], axis=-1)
    sin_lo = np.concatenate([-sin, z8, zero], axis=-1)
    sin_hi = np.concatenate([z8, sin, zero], axis=-1)
    rep = LANES // HEAD_DIM
    return tuple(jnp.asarray(np.tile(t, (1, rep)), F32) for t in (cos_t, sin_lo, sin_hi))


def _pmix(x, mod, sub, w_in, w_o, sinks, tables, gn_g, gn_b, w_s, b_s, out_g, ln_gb, side, *, batch, tm):
    nt = SEQ // tm
    n_tiles = batch * nt
    tile = lambda n: jnp.minimum(n, n_tiles - 1)
    cos_t, sin_lo, sin_hi = tables
    tab_spec = pl.BlockSpec((tm, LANES), lambda n: (tile(n) % nt, 0))
    full2 = lambda n: (0, 0)
    mod_spec = pl.BlockSpec((3, mod.shape[1], D_MODEL), lambda n: (sub, 0, 0))
    kv_spec = pl.BlockSpec((1, WINDOW, KV_WIDTH), lambda n: (tile(n) // nt, 0, 0))
    side_arr, side_lead = side
    side_rows, side_cols = side_arr.shape[-2:]
    assert side_rows % n_tiles == 0
    side_blk = side_rows // n_tiles
    resident = dict(pipeline_mode=pl.Buffered(1))
    return pl.pallas_call(
        functools.partial(_pmix_kernel, nt, n_tiles),
        grid=(n_tiles + 1,),
        in_specs=[
            pl.BlockSpec(memory_space=pltpu.SMEM),
            pl.BlockSpec((tm, D_MODEL), lambda n: (tile(n), 0)),
            mod_spec,
            pl.BlockSpec((D_MODEL, IN_WIDTH), full2, **resident),
            pl.BlockSpec((D_MODEL, D_MODEL), full2, **resident),
            tab_spec, tab_spec, tab_spec,
            pl.BlockSpec((1, GMLP_WIDTH), full2),
            pl.BlockSpec((1, GMLP_WIDTH), full2),
            pl.BlockSpec((N_GMLP_GROUPS, CHUNK, CHUNK), lambda n: (0, 0, 0)),
            pl.BlockSpec((CHUNK, N_GMLP_GROUPS), full2),
            pl.BlockSpec((1, D_MODEL), full2),
            pl.BlockSpec((2, D_MODEL), full2),
            pl.BlockSpec((None,) * len(side_lead) + (side_blk, side_cols), lambda n: (*side_lead, tile(n), 0)),
        ],
        out_specs=[
            pl.BlockSpec((tm, D_MODEL), lambda n: (jnp.maximum(n - 1, 0), 0)),
            kv_spec, kv_spec,
            pl.BlockSpec((side_blk, side_cols), lambda n: (tile(n), 0)),
        ],
        out_shape=[
            jax.ShapeDtypeStruct((batch * SEQ, D_MODEL), F32),
            jax.ShapeDtypeStruct((batch, WINDOW, KV_WIDTH), F32),
            jax.ShapeDtypeStruct((batch, WINDOW, KV_WIDTH), F32),
            jax.ShapeDtypeStruct((side_rows, side_cols), BF16),
        ],
        scratch_shapes=[
            pltpu.VMEM((tm, 2 * GMLP_WIDTH), F32),
            pltpu.VMEM((tm, ATTN_WIDTH), BF16),
            pltpu.VMEM((WINDOW + tm, 4 * KV_WIDTH), BF16),
            pltpu.VMEM((WINDOW + tm, 4 * KV_WIDTH), BF16),
            pltpu.VMEM((WINDOW, 4 * KV_WIDTH), BF16),
            pltpu.VMEM((WINDOW, 4 * KV_WIDTH), BF16),
            pltpu.VMEM((tm, ATTN_WIDTH), F32),
            pltpu.VMEM((tm, D_MODEL), BF16),
            pltpu.VMEM((N_GMLP_GROUPS, CHUNK, CHUNK), BF16),
            pltpu.VMEM((tm, D_MODEL), F32),
        ],
        compiler_params=pltpu.CompilerParams(dimension_semantics=("arbitrary",), vmem_limit_bytes=VMEM_LIMIT),
        name="prompt_mixer",
    )(sinks, x, mod, w_in, w_o, cos_t, sin_lo, sin_hi, gn_g.reshape(1, GMLP_WIDTH),
      gn_b.reshape(1, GMLP_WIDTH), w_s, b_s.T, out_g.reshape(1, D_MODEL), ln_gb, side_arr)


def _sattn_kernel(qe_ref, p_ref, ck_ref, cv_ref, sink_ref, cos_ref, slo_ref, shi_ref, o_ref, ko_ref, vo_ref):
    cos, slo, shi = cos_ref[...], slo_ref[...], shi_ref[...]
    sink = sink_ref[...]
    seqs = range(qe_ref.shape[0])
    key = lax.broadcasted_iota(jnp.int32, (N_HEADS, WINDOW), 1)
    head = lax.broadcasted_iota(jnp.int32, (N_HEADS, KV_WIDTH), 0)
    lane = lax.broadcasted_iota(jnp.int32, (N_HEADS, KV_WIDTH), 1)
    newest = lax.broadcasted_iota(jnp.int32, (WINDOW, KV_WIDTH), 0) == WINDOW - 1

    scores = []
    for r in seqs:
        qe = _rope_lanes(qe_ref[r], cos, slo, shi) * (HEAD_DIM ** -0.5)
        kn = _rope_lanes(p_ref[r:r + 1, Q_END:K_END], cos, slo, shi)
        s_c = _dot_nt(qe.astype(BF16), ck_ref[r].astype(BF16))
        s_c = jnp.where(key >= 1, s_c, -jnp.inf)
        s_n = jnp.sum(qe * kn, axis=-1, keepdims=True)
        scores.append((kn, s_c, s_n))
        ko_ref[r] = jnp.where(newest, kn, pltpu.roll(ck_ref[r], WINDOW - 1, 0))

    probs = []
    for kn, s_c, s_n in scores:
        mx = jnp.maximum(jnp.maximum(jnp.max(s_c, axis=-1, keepdims=True), s_n), sink)
        p_c = jnp.exp(s_c - mx)
        p_n = jnp.exp(s_n - mx)
        denom = jnp.sum(p_c, axis=-1, keepdims=True) + p_n + jnp.exp(sink - mx)
        probs.append((p_c, p_n, denom))

    for r, (p_c, p_n, denom) in zip(seqs, probs):
        vn = p_ref[r:r + 1, K_END:V_END]
        cv = cv_ref[r]
        o = (_dot(p_c.astype(BF16), cv.astype(BF16)) + p_n * vn) / denom
        o_ref[r] = jnp.where(head // GQA_GROUP == lane // HEAD_DIM, o, 0.0)
        vo_ref[r] = jnp.where(newest, vn, pltpu.roll(cv, WINDOW - 1, 0))


def _sattn(qe, p3, ck, cv, sinks, tables):
    b = qe.shape[0]
    rows = SAMPLE_ROWS_PER_STEP
    cos_t, sin_lo, sin_hi = tables
    tab_spec = pl.BlockSpec((1, LANES), lambda i: (0, 0))
    blk3 = lambda i: (i, 0, 0)
    return pl.pallas_call(
        _sattn_kernel,
        grid=(b // rows,),
        in_specs=[
            pl.BlockSpec((rows, N_HEADS, KV_WIDTH), blk3),
            pl.BlockSpec((rows, IN_WIDTH), lambda i: (i, 0)),
            pl.BlockSpec((rows, WINDOW, KV_WIDTH), blk3),
            pl.BlockSpec((rows, WINDOW, KV_WIDTH), blk3),
            pl.BlockSpec((N_HEADS, 1), lambda i: (0, 0)),
            tab_spec, tab_spec, tab_spec,
        ],
        out_specs=[
            pl.BlockSpec((rows, N_HEADS, KV_WIDTH), blk3),
            pl.BlockSpec((rows, WINDOW, KV_WIDTH), blk3),
            pl.BlockSpec((rows, WINDOW, KV_WIDTH), blk3),
        ],
        out_shape=[
            jax.ShapeDtypeStruct((b, N_HEADS, KV_WIDTH), F32),
            jax.ShapeDtypeStruct((b, WINDOW, KV_WIDTH), F32),
            jax.ShapeDtypeStruct((b, WINDOW, KV_WIDTH), F32),
        ],
        compiler_params=pltpu.CompilerParams(dimension_semantics=("arbitrary",)),
        name="sample_attn",
    )(qe, p3, ck, cv, sinks.reshape(N_HEADS, 1), cos_t, sin_lo, sin_hi)


def _smerge_kernel(attn_ref, p_ref, gng_ref, gnb_ref, w0_ref, b0_ref, og_ref, m_ref, vn_ref):
    og = og_ref[...]
    vn = jnp.concatenate(_gmlp_norm(p_ref[:, U_END:IN_WIDTH], gng_ref[...], gnb_ref[...]), axis=-1)
    vn_ref[...] = vn
    u = _gelu_tanh(p_ref[:, V_END:U_END])
    gm = u * (w0_ref[...] * vn + b0_ref[...])
    m_ref[:, 0:ATTN_WIDTH] = _rms_norm(attn_ref[...], og[:, 0:ATTN_WIDTH]).astype(BF16)
    m_ref[:, ATTN_WIDTH:] = _rms_norm(gm, og[:, ATTN_WIDTH:]).astype(BF16)


def _smerge(attn, p, gn_g, gn_b, w_s, b_s, out_g):
    b = attn.shape[0]
    w0 = jnp.repeat(w_s[:, 0, 0], LANES).reshape(1, GMLP_WIDTH)
    b0 = jnp.repeat(b_s[:, 0], LANES).reshape(1, GMLP_WIDTH)
    return pl.pallas_call(
        _smerge_kernel,
        out_shape=[
            jax.ShapeDtypeStruct((b, D_MODEL), BF16),
            jax.ShapeDtypeStruct((b, GMLP_WIDTH), F32),
        ],
        name="sample_gmlp_merge",
    )(attn, p, gn_g.reshape(1, GMLP_WIDTH), gn_b.reshape(1, GMLP_WIDTH), w0, b0, out_g.reshape(1, D_MODEL))


def kernel(x_prompt, x_sample, cache_k_win, cache_v_win, c_prompt, c_sample, w_ada, b_ada, ln_g, ln_b,
           w_ffn_up, w_ffn_down, w_in, attn_sinks, gmlp_norm_g, gmlp_norm_b, w_spatial, b_spatial,
           out_norm_g, w_o):
    batch, seq, _ = x_prompt.shape
    dec_batch = x_sample.shape[0]
    buf = cache_k_win.shape[2]
    assert seq == SEQ and buf == WINDOW and x_sample.shape[1] == 1 and w_ada.shape[0] == DEPTH == 1

    xp = x_prompt.reshape(batch * seq, D_MODEL)
    xs = x_sample.reshape(dec_batch, D_MODEL)

    n_c = batch + dec_batch
    n_c_pad = ((n_c + 7) // 8) * 8
    c_all = jnp.concatenate([c_prompt, c_sample, jnp.zeros((n_c_pad - n_c, D_MODEL), F32)], axis=0)
    mod = _ada(c_all, w_ada[0], b_ada[0])

    ln_gb = jnp.stack([ln_g[0], ln_b[0]], axis=1)
    ffn = functools.partial(_ffn, tm=FFN_ROWS, rows_per_mod=seq, extra_row0=batch)

    w_up0 = w_ffn_up[0, 0].astype(BF16)
    w_down0 = w_ffn_down[0, 0].astype(BF16)
    later = ((w_ffn_down, (0, 1), 32), (w_in, (0,), 16), (w_o, (0,), 16))
    xp, xs, w_down1, w_in_b, w_o_b = ffn(xp, mod, 0, w_up0, w_down0, ln_gb[0], extra=xs, side=later)

    xp, k_p, v_p, w_up1 = _pmix(xp, mod, 1, w_in_b, w_o_b, attn_sinks[0], _rope_tables(np.arange(seq)),
                                gmlp_norm_g[0], gmlp_norm_b[0], w_spatial[0], b_spatial[0], out_norm_g[0], ln_gb[1],
                                (w_ffn_up, (0, 1)), batch=batch, tm=MIX_TILE)

    ps = _inproj(xs, mod, 1, batch, w_in_b)
    eye = jnp.eye(N_KV_HEADS, dtype=F32)
    q5 = ps[:, :Q_END].reshape(dec_batch, N_KV_HEADS, GQA_GROUP, 1, HEAD_DIM)
    qe = (q5 * eye[None, :, None, :, None]).reshape(dec_batch, N_HEADS, KV_WIDTH)
    ck = cache_k_win[0].reshape(dec_batch, buf, KV_WIDTH)
    cv = cache_v_win[0].reshape(dec_batch, buf, KV_WIDTH)
    oe, k_s, v_s = _sattn(qe, ps, ck, cv, attn_sinks[0],
                          _rope_tables(np.full((1,), PAST_LEN)))
    attn_s = oe.reshape(dec_batch, N_KV_HEADS, GQA_GROUP, N_KV_HEADS, HEAD_DIM).sum(axis=3).reshape(dec_batch, ATTN_WIDTH)
    merged_s, vn_s = _smerge(attn_s, ps, gmlp_norm_g[0], gmlp_norm_b[0], w_spatial[0], b_spatial[0], out_norm_g[0])
    xs = _outproj(merged_s, xs, mod, 1, batch, w_o_b, ln_gb[1])

    xp, xs = ffn(xp, mod, 2, w_up1, w_down1, ln_gb[2], extra=xs)

    return (
        xp.reshape(batch, seq, D_MODEL),
        xs.reshape(dec_batch, 1, D_MODEL),
        k_p.reshape(1, batch, WINDOW, N_KV_HEADS, HEAD_DIM),
        v_p.reshape(1, batch, WINDOW, N_KV_HEADS, HEAD_DIM),
        k_s.reshape(1, dec_batch, buf, N_KV_HEADS, HEAD_DIM),
        v_s.reshape(1, dec_batch, buf, N_KV_HEADS, HEAD_DIM),
        vn_s.reshape(1, dec_batch, 1, GMLP_WIDTH),
    )
```

```python
import functools

import jax
import jax.numpy as jnp
import numpy as np
from jax import lax
from jax.experimental import pallas as pl
from jax.experimental.pallas import tpu as pltpu

D_MODEL = 2048
SEQ = 2048
PAST_LEN = 16384
ATTN_WIDTH = 1024
GMLP_WIDTH = 1024
HEAD_DIM = 64
N_HEADS = 16
N_KV_HEADS = 4
GQA_GROUP = 4
KV_WIDTH = 256
WINDOW = 128
ROPE_THETA = 500000.0
ROPE_DIM = 16
CHUNK = 128
N_GMLP_GROUPS = 8
D_FF = 5504
N_SUB = 3
DEPTH = 1
ALPHA = (2.0 * DEPTH) ** 0.25
LN_EPS = 1e-5
Q_END = ATTN_WIDTH
K_END = Q_END + KV_WIDTH
V_END = K_END + KV_WIDTH
U_END = V_END + GMLP_WIDTH
IN_WIDTH = U_END + GMLP_WIDTH

LANES = 128
MXU_COLS = 256
FF_TILE = 512
N_FF_STEPS = -(-D_FF // FF_TILE)
FF_LAST_OFF = D_FF - FF_TILE
FF_OVERLAP = N_FF_STEPS * FF_TILE - D_FF
OUT_TILE = 512
N_OUT_STEPS = D_MODEL // OUT_TILE
FFN_ROWS = 512
LN_PIECE_ROWS = 64
MIX_TILE = 256
PROJ_TILE = 512
SAMPLE_ROWS_PER_STEP = 8
VMEM_LIMIT = 60 * 1024 * 1024

BF16 = jnp.bfloat16
F32 = jnp.float32


def _dot(a, b):
    return jnp.dot(a, b, preferred_element_type=F32)


def _dot_nt(a, b):
    return lax.dot_general(a, b, (((1,), (1,)), ((), ())), preferred_element_type=F32)


def _layer_norm(y, g, b):
    mu = jnp.mean(y, axis=-1, keepdims=True)
    d = y - mu
    var = jnp.mean(d * d, axis=-1, keepdims=True)
    return d * lax.rsqrt(var + LN_EPS) * g + b


def _rms_norm(y, g):
    return y * lax.rsqrt(jnp.mean(y * y, axis=-1, keepdims=True) + LN_EPS) * g


def _gelu_tanh(x):
    c = np.float32(np.sqrt(2.0 / np.pi))
    return 0.5 * x * (1.0 + jnp.tanh(c * (x + 0.044715 * (x * x * x))))


def _rope_lanes(t, cos, sin_lo, sin_hi):
    pieces = []
    for c in range(t.shape[-1] // LANES):
        x = t[:, c * LANES:(c + 1) * LANES]
        pieces.append(x * cos + pltpu.roll(x, 8, 1) * sin_hi + pltpu.roll(x, LANES - 8, 1) * sin_lo)
    return pieces[0] if len(pieces) == 1 else jnp.concatenate(pieces, axis=-1)


def _ada_kernel(c_ref, w_ref, b_ref, o_ref):
    c = c_ref[...]
    h = (c * jax.nn.sigmoid(c)).astype(BF16)
    o_ref[0] = _dot(h, w_ref[...].astype(BF16)) + b_ref[...]


def _ada(c, w_ada, b_ada):
    rows = c.shape[0]
    n = w_ada.shape[1]
    tn = 1024
    per_plane = D_MODEL // tn
    return pl.pallas_call(
        _ada_kernel,
        grid=(n // tn,),
        in_specs=[
            pl.BlockSpec((rows, D_MODEL), lambda j: (0, 0)),
            pl.BlockSpec((D_MODEL, tn), lambda j: (0, j)),
            pl.BlockSpec((1, tn), lambda j: (0, j)),
        ],
        out_specs=pl.BlockSpec((1, rows, tn), lambda j: (j // per_plane, 0, j % per_plane)),
        out_shape=jax.ShapeDtypeStruct((n // D_MODEL, rows, D_MODEL), F32),
        compiler_params=pltpu.CompilerParams(dimension_semantics=("arbitrary",), vmem_limit_bytes=VMEM_LIMIT),
        name="ada_mod",
    )(c, w_ada, b_ada.reshape(1, n))


def _ff_offset(step):
    return jnp.minimum(step * FF_TILE, FF_LAST_OFF)


def _ffn_kernel(n_side, n_extra, tiles_per_mod, extra_row0, n_tiles, *refs):
    x_ref, mod_ref, wv_ref, wg_ref, wd_ref, gb_ref = refs[:6]
    n_in = 6
    if n_extra:
        xe_ref = refs[n_in]
        n_in += 1
        erows = slice(extra_row0, extra_row0 + n_extra)
    side_in = refs[n_in:n_in + n_side]
    n_in += n_side
    o_ref = refs[n_in]
    n_out = 1
    if n_extra:
        oe_ref = refs[n_in + 1]
        n_out = 2
    side_out = refs[n_in + n_out:n_in + n_out + n_side]
    xm_ref, a_ref, y_ref = refs[n_in + n_out + n_side:]
    i = pl.program_id(0)
    s = pl.program_id(1)
    tm = x_ref.shape[0]
    last = pl.num_programs(1) - 1

    def cast_side():
        for src, dst in zip(side_in, side_out):
            dst[...] = src[...].astype(BF16)

    def norm_piece():
        r0 = pl.multiple_of(jnp.minimum(s, tm // LN_PIECE_ROWS - 1) * LN_PIECE_ROWS, LN_PIECE_ROWS)
        piece = pl.ds(r0, LN_PIECE_ROWS)
        y = jnp.concatenate([y_ref[n, piece, :] for n in range(N_OUT_STEPS)], axis=-1)
        o_ref[piece, :] = _layer_norm(y, gb_ref[0:1], gb_ref[1:2])

    def up_chunk(rows, norm_previous):
        cast_side()
        if norm_previous:
            norm_piece()
        xm = xm_ref[0:rows]
        for c in range(FF_TILE // MXU_COLS):
            sl = slice(c * MXU_COLS, (c + 1) * MXU_COLS)
            hv = _dot(xm, wv_ref[:, sl])
            hg = _dot(xm, wg_ref[:, sl])
            a = hg * jax.nn.sigmoid(hg) * hv
            if c * MXU_COLS < FF_OVERLAP:
                col = lax.broadcasted_iota(jnp.int32, a.shape, 1) + c * MXU_COLS
                a = jnp.where(jnp.logical_and(s == N_FF_STEPS - 1, col < FF_OVERLAP), 0.0, a)
            a_ref[s, 0:rows, sl] = a.astype(BF16)

    def down_slab(rows):
        cast_side()
        acc = None
        for f in range(N_FF_STEPS):
            r0 = min(f * FF_TILE, FF_LAST_OFF)
            part = _dot(a_ref[f, 0:rows], wd_ref[r0:r0 + FF_TILE, :])
            acc = part if acc is None else acc + part
        y_ref[s - N_FF_STEPS, 0:rows] = acc

    first = i == 0
    later = i > 0
    up_phase = s < N_FF_STEPS
    down_phase = s >= N_FF_STEPS
    seq = pl.ds(i // tiles_per_mod, 1)

    @pl.when(s == 0)
    def _():
        xm_ref[0:tm] = (x_ref[...] * (1.0 + mod_ref[1, seq, :]) + mod_ref[0, seq, :]).astype(BF16)

    if n_extra:
        @pl.when(jnp.logical_and(s == 0, first))
        def _():
            xm_ref[tm:tm + n_extra] = (xe_ref[...] * (1.0 + mod_ref[1, erows, :]) + mod_ref[0, erows, :]).astype(BF16)

    pl.when(jnp.logical_and(up_phase, first))(functools.partial(up_chunk, tm + n_extra, False))
    pl.when(jnp.logical_and(up_phase, later))(functools.partial(up_chunk, tm, True))
    pl.when(jnp.logical_and(down_phase, first))(functools.partial(down_slab, tm + n_extra))
    pl.when(jnp.logical_and(down_phase, later))(functools.partial(down_slab, tm))

    @pl.when(s == last)
    def _():
        for n in range(N_OUT_STEPS):
            sl = slice(n * OUT_TILE, (n + 1) * OUT_TILE)
            y_ref[n, 0:tm] = ALPHA * x_ref[:, sl] + 0.5 * mod_ref[2, seq, sl] * y_ref[n, 0:tm]

    @pl.when(jnp.logical_and(s == last, i == n_tiles - 1))
    def _():
        y = jnp.concatenate([y_ref[n, 0:tm] for n in range(N_OUT_STEPS)], axis=-1)
        o_ref[...] = _layer_norm(y, gb_ref[0:1], gb_ref[1:2])

    if n_extra:
        @pl.when(jnp.logical_and(s == last, first))
        def _():
            mixed = jnp.concatenate([y_ref[n, tm:tm + n_extra] for n in range(N_OUT_STEPS)], axis=-1)
            oe_ref[...] = _layer_norm(ALPHA * xe_ref[...] + 0.5 * mod_ref[2, erows, :] * mixed, gb_ref[0:1], gb_ref[1:2])


def _ffn(x, mod, sub, w_up, w_down, ln_gb, *, tm, rows_per_mod, extra=None, extra_row0=0, side=()):
    m = x.shape[0]
    n_steps = N_FF_STEPS + N_OUT_STEPS
    tiles_per_mod = rows_per_mod // tm
    n_extra = 0 if extra is None else extra.shape[0]
    const2 = lambda i, s: (0, 0)
    extra_spec = pl.BlockSpec((n_extra, D_MODEL), const2)
    side_in_specs, side_out_specs, side_shapes = [], [], []
    for arr, lead, rows in side:
        n_rows, n_cols = arr.shape[-2:]
        n_blocks = n_rows // rows
        assert n_blocks * rows == n_rows and n_blocks <= (m // tm) * n_steps
        blk = lambda i, s, n_blocks=n_blocks: jnp.minimum(i * n_steps + s, n_blocks - 1)
        side_in_specs.append(pl.BlockSpec((None,) * len(lead) + (rows, n_cols),
                                          lambda i, s, lead=lead, blk=blk: (*lead, blk(i, s), 0)))
        side_out_specs.append(pl.BlockSpec((rows, n_cols), lambda i, s, blk=blk: (blk(i, s), 0)))
        side_shapes.append(jax.ShapeDtypeStruct((n_rows, n_cols), BF16))
    n_tiles = m // tm
    assert tm % LN_PIECE_ROWS == 0 and tm // LN_PIECE_ROWS <= N_FF_STEPS
    wd_block = lambda i, s: (0, jnp.where(s < N_FF_STEPS, N_OUT_STEPS - 1, s - N_FF_STEPS))
    out_block = lambda i, s: (jnp.where(s < N_FF_STEPS, jnp.maximum(i - 1, 0), i), 0)
    outs = pl.pallas_call(
        functools.partial(_ffn_kernel, len(side), n_extra, tiles_per_mod, extra_row0, n_tiles),
        grid=(n_tiles, n_steps),
        in_specs=[
            pl.BlockSpec((tm, D_MODEL), lambda i, s: (i, 0)),
            pl.BlockSpec((3, mod.shape[1], D_MODEL), lambda i, s: (sub, 0, 0)),
            pl.BlockSpec((pl.Element(D_MODEL), pl.Element(FF_TILE)),
                         lambda i, s: (0, pl.multiple_of(_ff_offset(s), LANES))),
            pl.BlockSpec((pl.Element(D_MODEL), pl.Element(FF_TILE)),
                         lambda i, s: (0, pl.multiple_of(D_FF + _ff_offset(s), LANES))),
            pl.BlockSpec((D_FF, OUT_TILE), wd_block),
            pl.BlockSpec((2, D_MODEL), const2),
            *([extra_spec] if n_extra else []),
            *side_in_specs,
        ],
        out_specs=[pl.BlockSpec((tm, D_MODEL), out_block), *([extra_spec] if n_extra else []),
                   *side_out_specs],
        out_shape=[jax.ShapeDtypeStruct((m, D_MODEL), F32),
                   *([jax.ShapeDtypeStruct((n_extra, D_MODEL), F32)] if n_extra else []), *side_shapes],
        scratch_shapes=[
            pltpu.VMEM((tm + n_extra, D_MODEL), BF16),
            pltpu.VMEM((N_FF_STEPS, tm + n_extra, FF_TILE), BF16),
            pltpu.VMEM((N_OUT_STEPS, tm + n_extra, OUT_TILE), F32),
        ],
        compiler_params=pltpu.CompilerParams(dimension_semantics=("arbitrary", "arbitrary"), vmem_limit_bytes=VMEM_LIMIT),
        name="swiglu_ln",
    )(x, mod, w_up, w_up, w_down, ln_gb, *([extra] if n_extra else []), *[arr for arr, _, _ in side])
    return outs


def _inproj_kernel(row0, x_ref, mod_ref, w_ref, o_ref, xm_ref):
    rows = slice(row0, row0 + x_ref.shape[0])

    @pl.when(pl.program_id(0) == 0)
    def _():
        xm_ref[...] = (x_ref[...] * (1.0 + mod_ref[1, rows, :]) + mod_ref[0, rows, :]).astype(BF16)

    o_ref[...] = _dot(xm_ref[...], w_ref[...])


def _inproj(x, mod, sub, row0, w_in):
    m = x.shape[0]
    tn = 512
    return pl.pallas_call(
        functools.partial(_inproj_kernel, row0),
        grid=(IN_WIDTH // tn,),
        in_specs=[
            pl.BlockSpec((m, D_MODEL), lambda j: (0, 0)),
            pl.BlockSpec((3, mod.shape[1], D_MODEL), lambda j: (sub, 0, 0)),
            pl.BlockSpec((D_MODEL, tn), lambda j: (0, j)),
        ],
        out_specs=pl.BlockSpec((m, tn), lambda j: (0, j)),
        out_shape=jax.ShapeDtypeStruct((m, IN_WIDTH), F32),
        scratch_shapes=[pltpu.VMEM((m, D_MODEL), BF16)],
        compiler_params=pltpu.CompilerParams(dimension_semantics=("arbitrary",), vmem_limit_bytes=VMEM_LIMIT),
        name="mixer_inproj",
    )(x, mod, w_in)


def _gmlp_norm(pv, gng, gnb):
    v = _gelu_tanh(pv)
    outs = []
    for g in range(N_GMLP_GROUPS):
        sl = slice(g * LANES, (g + 1) * LANES)
        outs.append(_layer_norm(v[:, sl], gng[:, sl], gnb[:, sl]))
    return outs


def _half_lane_tiles(t):
    lo_lane = lax.broadcasted_iota(jnp.int32, (t.shape[0], LANES), 1) < HEAD_DIM
    lo, hi = [], []
    for c in range(KV_WIDTH // LANES):
        col = t[:, c * LANES:(c + 1) * LANES]
        swp = pltpu.roll(col, HEAD_DIM, 1)
        lo += [jnp.where(lo_lane, col, 0.0), jnp.where(lo_lane, swp, 0.0)]
        hi += [jnp.where(lo_lane, 0.0, swp), jnp.where(lo_lane, 0.0, col)]
    return jnp.concatenate(lo + hi, axis=-1).astype(BF16)


def _band_attention(q_s, kx_s, vx_s, attn_s, sinks_ref, nblk, first_tile):
    rows = 2 * WINDOW
    qi = lax.broadcasted_iota(jnp.int32, (rows, WINDOW), 0) & (WINDOW - 1)
    kj = lax.broadcasted_iota(jnp.int32, (rows, WINDOW), 1)
    own = kj <= qi
    first_mask = kj <= jnp.where(first_tile, qi, WINDOW)
    ones = jnp.ones((2 * WINDOW, LANES), BF16)
    units = [(wb, g, hf) for wb in range(nblk) for g in range(N_KV_HEADS) for hf in range(2)]

    def kv_cols(g, hf):
        return slice(hf * 4 * LANES + g * LANES, hf * 4 * LANES + (g + 1) * LANES)

    scores = []
    for wb, g, hf in units:
        r0 = wb * WINDOW
        q2 = jnp.concatenate([q_s[r0:r0 + WINDOW, (2 * g) * LANES:(2 * g + 1) * LANES],
                              q_s[r0:r0 + WINDOW, (2 * g + 1) * LANES:(2 * g + 2) * LANES]], axis=0)
        s = _dot_nt(q2, kx_s[r0:r0 + 2 * WINDOW, kv_cols(g, hf)])
        c = jnp.where(own, s[:, WINDOW:2 * WINDOW], s[:, 0:WINDOW])
        if wb == 0:
            c = jnp.where(first_mask, c, -jnp.inf)
        scores.append(c)

    probs = []
    for (wb, g, hf), c in zip(units, scores):
        sink = jnp.concatenate([jnp.full((WINDOW, LANES), sinks_ref[GQA_GROUP * g + hf], F32),
                                jnp.full((WINDOW, LANES), sinks_ref[GQA_GROUP * g + 2 + hf], F32)], axis=0)
        mx = jnp.maximum(jnp.broadcast_to(jnp.max(c, axis=-1, keepdims=True), c.shape), sink)
        p = jnp.exp(c - mx)
        pcat = jnp.concatenate([jnp.where(own, 0.0, p), jnp.where(own, p, 0.0)], axis=-1).astype(BF16)
        probs.append((pcat, jnp.exp(sink - mx)))

    outs = {}
    for (wb, g, hf), (pcat, esink) in zip(units, probs):
        r0 = wb * WINDOW
        v2 = jnp.concatenate([vx_s[r0:r0 + 2 * WINDOW, kv_cols(g, hf)], ones], axis=-1)
        o = _dot(pcat, v2)
        outs[wb, g, hf] = o[:, 0:LANES] / (o[:, LANES:2 * LANES] + esink)

    for wb in range(nblk):
        r0 = wb * WINDOW
        for g in range(N_KV_HEADS):
            both = outs[wb, g, 0] + outs[wb, g, 1]
            attn_s[r0:r0 + WINDOW, (2 * g) * LANES:(2 * g + 1) * LANES] = both[0:WINDOW]
            attn_s[r0:r0 + WINDOW, (2 * g + 1) * LANES:(2 * g + 2) * LANES] = both[WINDOW:2 * WINDOW]


def _pmix_kernel(tiles_per_seq, n_tiles, sinks_ref, x_ref, mod_ref, win_ref, wo_ref, cos_ref, slo_ref, shi_ref,
                 gng_ref, gnb_ref, ws_ref, bs_ref, og_ref, lngb_ref, side_ref,
                 o_ref, k_ref, v_ref, side_out_ref, p_s, q_s, kk_s, vv_s, kp_s, vp_s, attn_s, m_s, wm_s, y_s):
    n = pl.program_id(0)
    t = n % tiles_per_seq
    tm = x_ref.shape[0]
    nblk = tm // WINDOW

    @pl.when(n == 0)
    def _():
        y_s[...] = jnp.zeros(y_s.shape, F32)
        row = lax.broadcasted_iota(jnp.int32, (CHUNK, CHUNK), 0)
        col = lax.broadcasted_iota(jnp.int32, (CHUNK, CHUNK), 1)
        for g in range(N_GMLP_GROUPS):
            wm_s[g] = jnp.where(col <= row, ws_ref[g], 0.0).astype(BF16)

    @pl.when(t == 0)
    def _():
        kp_s[...] = jnp.zeros(kp_s.shape, BF16)
        vp_s[...] = jnp.zeros(vp_s.shape, BF16)

    @pl.when(n == n_tiles)
    def _():
        side_out_ref[...] = side_ref[...].astype(BF16)
        o_ref[...] = _layer_norm(y_s[...], lngb_ref[0:1], lngb_ref[1:2])

    pl.when(n < n_tiles)(functools.partial(
        _pmix_tile, t, pl.ds(n // tiles_per_seq, 1), sinks_ref, x_ref, mod_ref, win_ref, wo_ref, cos_ref, slo_ref,
        shi_ref, gng_ref, gnb_ref, bs_ref, og_ref, lngb_ref, side_ref, o_ref, k_ref, v_ref, side_out_ref, p_s, q_s,
        kk_s, vv_s, kp_s, vp_s, attn_s, m_s, wm_s, y_s))


def _pmix_tile(t, seq, sinks_ref, x_ref, mod_ref, win_ref, wo_ref, cos_ref, slo_ref, shi_ref, gng_ref, gnb_ref,
               bs_ref, og_ref, lngb_ref, side_ref, o_ref, k_ref, v_ref, side_out_ref, p_s, q_s, kk_s, vv_s, kp_s, vp_s,
               attn_s, m_s, wm_s, y_s):
    tm = x_ref.shape[0]
    nblk = tm // WINDOW

    side_out_ref[...] = side_ref[...].astype(BF16)

    o_ref[...] = _layer_norm(y_s[...], lngb_ref[0:1], lngb_ref[1:2])

    m_s[...] = (x_ref[...] * (1.0 + mod_ref[1, seq, :]) + mod_ref[0, seq, :]).astype(BF16)
    cos, slo, shi = cos_ref[...], slo_ref[...], shi_ref[...]
    for c in range(IN_WIDTH // PROJ_TILE):
        lo_col = c * PROJ_TILE
        h = _dot(m_s[...], win_ref[:, lo_col:lo_col + PROJ_TILE])
        if lo_col + PROJ_TILE <= Q_END:
            q_s[:, lo_col:lo_col + PROJ_TILE] = (_rope_lanes(h, cos, slo, shi) * (HEAD_DIM ** -0.5)).astype(BF16)
        elif lo_col == Q_END and PROJ_TILE == 2 * KV_WIDTH:
            k = _rope_lanes(h[:, 0:KV_WIDTH], cos, slo, shi)
            v = h[:, KV_WIDTH:2 * KV_WIDTH]
            k_ref[0] = k[tm - WINDOW:tm]
            v_ref[0] = v[tm - WINDOW:tm]
            kb, vb = _half_lane_tiles(k), _half_lane_tiles(v)
            kk_s[0:WINDOW, :] = kp_s[...]
            vv_s[0:WINDOW, :] = vp_s[...]
            kk_s[WINDOW:WINDOW + tm, :] = kb
            vv_s[WINDOW:WINDOW + tm, :] = vb
            kp_s[...] = kb[tm - WINDOW:tm]
            vp_s[...] = vb[tm - WINDOW:tm]
        else:
            p_s[:, lo_col - V_END:lo_col - V_END + PROJ_TILE] = h

    _band_attention(q_s, kk_s, vv_s, attn_s, sinks_ref, nblk, t == 0)

    vn = _gmlp_norm(p_s[:, GMLP_WIDTH:2 * GMLP_WIDTH], gng_ref[...], gnb_ref[...])
    gated = []
    for g in range(N_GMLP_GROUPS):
        u = _gelu_tanh(p_s[:, g * LANES:(g + 1) * LANES])
        vg = vn[g].astype(BF16)
        cols = jnp.concatenate([vg[wb * CHUNK:(wb + 1) * CHUNK] for wb in range(nblk)], axis=-1)
        mix = _dot(wm_s[g], cols)
        mix = jnp.concatenate([mix[:, wb * LANES:(wb + 1) * LANES] for wb in range(nblk)], axis=0)
        bias = jnp.concatenate([bs_ref[:, g:g + 1]] * nblk, axis=0)
        gated.append(u * (mix + bias))
    gm = jnp.concatenate(gated, axis=-1)
    og = og_ref[...]
    m_s[:, 0:ATTN_WIDTH] = _rms_norm(attn_s[...], og[:, 0:ATTN_WIDTH]).astype(BF16)
    m_s[:, ATTN_WIDTH:] = _rms_norm(gm, og[:, ATTN_WIDTH:]).astype(BF16)

    for c in range(D_MODEL // PROJ_TILE):
        sl = slice(c * PROJ_TILE, (c + 1) * PROJ_TILE)
        y_s[:, sl] = ALPHA * x_ref[:, sl] + mod_ref[2, seq, sl] * _dot(m_s[...], wo_ref[:, sl])


def _rope_tables(pos):
    half = ROPE_DIM // 2
    inv = ROPE_THETA ** (-(np.arange(half, dtype=np.float64) * 2.0) / ROPE_DIM)
    ang = np.asarray(pos, np.float64)[:, None] * inv[None, :]
    cos, sin = np.cos(ang), np.sin(ang)
    n = ang.shape[0]
    one = np.ones((n, HEAD_DIM - ROPE_DIM))
    zero = np.zeros((n, HEAD_DIM - ROPE_DIM))
    z8 = np.zeros((n, half))
    cos_t = np.concatenate([cos, cos, one], axis=-1)
    sin_lo = np.concatenate([-sin, z8, zero], axis=-1)
    sin_hi = np.concatenate([z8, sin, zero], axis=-1)
    rep = LANES // HEAD_DIM
    return tuple(jnp.asarray(np.tile(t, (1, rep)), F32) for t in (cos_t, sin_lo, sin_hi))


def _pmix(x, mod, sub, w_in, w_o, sinks, tables, gn_g, gn_b, w_s, b_s, out_g, ln_gb, side, *, batch, tm):
    nt = SEQ // tm
    n_tiles = batch * nt
    tile = lambda n: jnp.minimum(n, n_tiles - 1)
    cos_t, sin_lo, sin_hi = tables
    tab_spec = pl.BlockSpec((tm, LANES), lambda n: (tile(n) % nt, 0))
    full2 = lambda n: (0, 0)
    mod_spec = pl.BlockSpec((3, mod.shape[1], D_MODEL), lambda n: (sub, 0, 0))
    kv_spec = pl.BlockSpec((1, WINDOW, KV_WIDTH), lambda n: (tile(n) // nt, 0, 0))
    side_arr, side_lead = side
    side_rows, side_cols = side_arr.shape[-2:]
    assert side_rows % n_tiles == 0
    side_blk = side_rows // n_tiles
    resident = dict(pipeline_mode=pl.Buffered(1))
    return pl.pallas_call(
        functools.partial(_pmix_kernel, nt, n_tiles),
        grid=(n_tiles + 1,),
        in_specs=[
            pl.BlockSpec(memory_space=pltpu.SMEM),
            pl.BlockSpec((tm, D_MODEL), lambda n: (tile(n), 0)),
            mod_spec,
            pl.BlockSpec((D_MODEL, IN_WIDTH), full2, **resident),
            pl.BlockSpec((D_MODEL, D_MODEL), full2, **resident),
            tab_spec, tab_spec, tab_spec,
            pl.BlockSpec((1, GMLP_WIDTH), full2),
            pl.BlockSpec((1, GMLP_WIDTH), full2),
            pl.BlockSpec((N_GMLP_GROUPS, CHUNK, CHUNK), lambda n: (0, 0, 0)),
            pl.BlockSpec((CHUNK, N_GMLP_GROUPS), full2),
            pl.BlockSpec((1, D_MODEL), full2),
            pl.BlockSpec((2, D_MODEL), full2),
            pl.BlockSpec((None,) * len(side_lead) + (side_blk, side_cols), lambda n: (*side_lead, tile(n), 0)),
        ],
        out_specs=[
            pl.BlockSpec((tm, D_MODEL), lambda n: (jnp.maximum(n - 1, 0), 0)),
            kv_spec, kv_spec,
            pl.BlockSpec((side_blk, side_cols), lambda n: (tile(n), 0)),
        ],
        out_shape=[
            jax.ShapeDtypeStruct((batch * SEQ, D_MODEL), F32),
            jax.ShapeDtypeStruct((batch, WINDOW, KV_WIDTH), F32),
            jax.ShapeDtypeStruct((batch, WINDOW, KV_WIDTH), F32),
            jax.ShapeDtypeStruct((side_rows, side_cols), BF16),
        ],
        scratch_shapes=[
            pltpu.VMEM((tm, 2 * GMLP_WIDTH), F32),
            pltpu.VMEM((tm, ATTN_WIDTH), BF16),
            pltpu.VMEM((WINDOW + tm, 4 * KV_WIDTH), BF16),
            pltpu.VMEM((WINDOW + tm, 4 * KV_WIDTH), BF16),
            pltpu.VMEM((WINDOW, 4 * KV_WIDTH), BF16),
            pltpu.VMEM((WINDOW, 4 * KV_WIDTH), BF16),
            pltpu.VMEM((tm, ATTN_WIDTH), F32),
            pltpu.VMEM((tm, D_MODEL), BF16),
            pltpu.VMEM((N_GMLP_GROUPS, CHUNK, CHUNK), BF16),
            pltpu.VMEM((tm, D_MODEL), F32),
        ],
        compiler_params=pltpu.CompilerParams(dimension_semantics=("arbitrary",), vmem_limit_bytes=VMEM_LIMIT),
        name="prompt_mixer",
    )(sinks, x, mod, w_in, w_o, cos_t, sin_lo, sin_hi, gn_g.reshape(1, GMLP_WIDTH),
      gn_b.reshape(1, GMLP_WIDTH), w_s, b_s.T, out_g.reshape(1, D_MODEL), ln_gb, side_arr)


def _sattn_kernel(qe_ref, p_ref, ck_ref, cv_ref, sink_ref, cos_ref, slo_ref, shi_ref, o_ref, ko_ref, vo_ref):
    cos, slo, shi = cos_ref[...], slo_ref[...], shi_ref[...]
    sink = sink_ref[...]
    seqs = range(qe_ref.shape[0])
    key = lax.broadcasted_iota(jnp.int32, (N_HEADS, WINDOW), 1)
    head = lax.broadcasted_iota(jnp.int32, (N_HEADS, KV_WIDTH), 0)
    lane = lax.broadcasted_iota(jnp.int32, (N_HEADS, KV_WIDTH), 1)
    newest = lax.broadcasted_iota(jnp.int32, (WINDOW, KV_WIDTH), 0) == WINDOW - 1

    scores = []
    for r in seqs:
        qe = _rope_lanes(qe_ref[r], cos, slo, shi) * (HEAD_DIM ** -0.5)
        kn = _rope_lanes(p_ref[r:r + 1, Q_END:K_END], cos, slo, shi)
        s_c = _dot_nt(qe.astype(BF16), ck_ref[r].astype(BF16))
        s_c = jnp.where(key >= 1, s_c, -jnp.inf)
        s_n = jnp.sum(qe * kn, axis=-1, keepdims=True)
        scores.append((kn, s_c, s_n))
        ko_ref[r] = jnp.where(newest, kn, pltpu.roll(ck_ref[r], WINDOW - 1, 0))

    probs = []
    for kn, s_c, s_n in scores:
        mx = jnp.maximum(jnp.maximum(jnp.max(s_c, axis=-1, keepdims=True), s_n), sink)
        p_c = jnp.exp(s_c - mx)
        p_n = jnp.exp(s_n - mx)
        denom = jnp.sum(p_c, axis=-1, keepdims=True) + p_n + jnp.exp(sink - mx)
        probs.append((p_c, p_n, denom))

    for r, (p_c, p_n, denom) in zip(seqs, probs):
        vn = p_ref[r:r + 1, K_END:V_END]
        cv = cv_ref[r]
        o = (_dot(p_c.astype(BF16), cv.astype(BF16)) + p_n * vn) / denom
        o_ref[r] = jnp.where(head // GQA_GROUP == lane // HEAD_DIM, o, 0.0)
        vo_ref[r] = jnp.where(newest, vn, pltpu.roll(cv, WINDOW - 1, 0))


def _sattn(qe, p3, ck, cv, sinks, tables):
    b = qe.shape[0]
    rows = SAMPLE_ROWS_PER_STEP
    cos_t, sin_lo, sin_hi = tables
    tab_spec = pl.BlockSpec((1, LANES), lambda i: (0, 0))
    blk3 = lambda i: (i, 0, 0)
    return pl.pallas_call(
        _sattn_kernel,
        grid=(b // rows,),
        in_specs=[
            pl.BlockSpec((rows, N_HEADS, KV_WIDTH), blk3),
            pl.BlockSpec((rows, IN_WIDTH), lambda i: (i, 0)),
            pl.BlockSpec((rows, WINDOW, KV_WIDTH), blk3),
            pl.BlockSpec((rows, WINDOW, KV_WIDTH), blk3),
            pl.BlockSpec((N_HEADS, 1), lambda i: (0, 0)),
            tab_spec, tab_spec, tab_spec,
        ],
        out_specs=[
            pl.BlockSpec((rows, N_HEADS, KV_WIDTH), blk3),
            pl.BlockSpec((rows, WINDOW, KV_WIDTH), blk3),
            pl.BlockSpec((rows, WINDOW, KV_WIDTH), blk3),
        ],
        out_shape=[
            jax.ShapeDtypeStruct((b, N_HEADS, KV_WIDTH), F32),
            jax.ShapeDtypeStruct((b, WINDOW, KV_WIDTH), F32),
            jax.ShapeDtypeStruct((b, WINDOW, KV_WIDTH), F32),
        ],
        compiler_params=pltpu.CompilerParams(dimension_semantics=("arbitrary",)),
        name="sample_attn",
    )(qe, p3, ck, cv, sinks.reshape(N_HEADS, 1), cos_t, sin_lo, sin_hi)


def _smerge_kernel(row0, attn_ref, p_ref, gng_ref, gnb_ref, w0_ref, b0_ref, og_ref, x_ref, mod_ref, w_ref, gb_ref,
                   o_ref, vn_ref):
    og = og_ref[...]
    vn = jnp.concatenate(_gmlp_norm(p_ref[:, U_END:IN_WIDTH], gng_ref[...], gnb_ref[...]), axis=-1)
    vn_ref[...] = vn
    u = _gelu_tanh(p_ref[:, V_END:U_END])
    gm = u * (w0_ref[...] * vn + b0_ref[...])
    merged = jnp.concatenate([_rms_norm(attn_ref[...], og[:, 0:ATTN_WIDTH]).astype(BF16),
                              _rms_norm(gm, og[:, ATTN_WIDTH:]).astype(BF16)], axis=-1)
    rows = slice(row0, row0 + x_ref.shape[0])
    y = ALPHA * x_ref[...] + mod_ref[2, rows, :] * _dot(merged, w_ref[...])
    o_ref[...] = _layer_norm(y, gb_ref[0:1], gb_ref[1:2])


def _smerge(attn, p, gn_g, gn_b, w_s, b_s, out_g, x, mod, sub, row0, w_o, ln_gb):
    b = attn.shape[0]
    w0 = jnp.repeat(w_s[:, 0, 0], LANES).reshape(1, GMLP_WIDTH)
    b0 = jnp.repeat(b_s[:, 0], LANES).reshape(1, GMLP_WIDTH)
    full = lambda shape: pl.BlockSpec(shape, lambda i: (0,) * len(shape))
    return pl.pallas_call(
        functools.partial(_smerge_kernel, row0),
        grid=(1,),
        in_specs=[
            full((b, ATTN_WIDTH)), full((b, IN_WIDTH)), full((1, GMLP_WIDTH)), full((1, GMLP_WIDTH)),
            full((1, GMLP_WIDTH)), full((1, GMLP_WIDTH)), full((1, D_MODEL)), full((b, D_MODEL)),
            pl.BlockSpec((3, mod.shape[1], D_MODEL), lambda i: (sub, 0, 0)),
            full((D_MODEL, D_MODEL)), full((2, D_MODEL)),
        ],
        out_specs=[full((b, D_MODEL)), full((b, GMLP_WIDTH))],
        out_shape=[
            jax.ShapeDtypeStruct((b, D_MODEL), F32),
            jax.ShapeDtypeStruct((b, GMLP_WIDTH), F32),
        ],
        compiler_params=pltpu.CompilerParams(dimension_semantics=("arbitrary",), vmem_limit_bytes=VMEM_LIMIT),
        name="sample_merge_outproj_ln",
    )(attn, p, gn_g.reshape(1, GMLP_WIDTH), gn_b.reshape(1, GMLP_WIDTH), w0, b0, out_g.reshape(1, D_MODEL),
      x, mod, w_o, ln_gb)


def kernel(x_prompt, x_sample, cache_k_win, cache_v_win, c_prompt, c_sample, w_ada, b_ada, ln_g, ln_b,
           w_ffn_up, w_ffn_down, w_in, attn_sinks, gmlp_norm_g, gmlp_norm_b, w_spatial, b_spatial,
           out_norm_g, w_o):
    batch, seq, _ = x_prompt.shape
    dec_batch = x_sample.shape[0]
    buf = cache_k_win.shape[2]
    assert seq == SEQ and buf == WINDOW and x_sample.shape[1] == 1 and w_ada.shape[0] == DEPTH == 1

    xp = x_prompt.reshape(batch * seq, D_MODEL)
    xs = x_sample.reshape(dec_batch, D_MODEL)

    n_c = batch + dec_batch
    n_c_pad = ((n_c + 7) // 8) * 8
    c_all = jnp.concatenate([c_prompt, c_sample, jnp.zeros((n_c_pad - n_c, D_MODEL), F32)], axis=0)
    mod = _ada(c_all, w_ada[0], b_ada[0])

    ln_gb = jnp.stack([ln_g[0], ln_b[0]], axis=1)
    ffn = functools.partial(_ffn, tm=FFN_ROWS, rows_per_mod=seq, extra_row0=batch)

    w_up0 = w_ffn_up[0, 0].astype(BF16)
    w_down0 = w_ffn_down[0, 0].astype(BF16)
    later = ((w_ffn_down, (0, 1), 32), (w_in, (0,), 16), (w_o, (0,), 16))
    xp, xs, w_down1, w_in_b, w_o_b = ffn(xp, mod, 0, w_up0, w_down0, ln_gb[0], extra=xs, side=later)

    xp, k_p, v_p, w_up1 = _pmix(xp, mod, 1, w_in_b, w_o_b, attn_sinks[0], _rope_tables(np.arange(seq)),
                                gmlp_norm_g[0], gmlp_norm_b[0], w_spatial[0], b_spatial[0], out_norm_g[0], ln_gb[1],
                                (w_ffn_up, (0, 1)), batch=batch, tm=MIX_TILE)

    ps = _inproj(xs, mod, 1, batch, w_in_b)
    eye = jnp.eye(N_KV_HEADS, dtype=F32)
    q5 = ps[:, :Q_END].reshape(dec_batch, N_KV_HEADS, GQA_GROUP, 1, HEAD_DIM)
    qe = (q5 * eye[None, :, None, :, None]).reshape(dec_batch, N_HEADS, KV_WIDTH)
    ck = cache_k_win[0].reshape(dec_batch, buf, KV_WIDTH)
    cv = cache_v_win[0].reshape(dec_batch, buf, KV_WIDTH)
    oe, k_s, v_s = _sattn(qe, ps, ck, cv, attn_sinks[0],
                          _rope_tables(np.full((1,), PAST_LEN)))
    attn_s = oe.reshape(dec_batch, N_KV_HEADS, GQA_GROUP, N_KV_HEADS, HEAD_DIM).sum(axis=3).reshape(dec_batch, ATTN_WIDTH)
    xs, vn_s = _smerge(attn_s, ps, gmlp_norm_g[0], gmlp_norm_b[0], w_spatial[0], b_spatial[0], out_norm_g[0],
                       xs, mod, 1, batch, w_o_b, ln_gb[1])

    xp, xs = ffn(xp, mod, 2, w_up1, w_down1, ln_gb[2], extra=xs)

    return (
        xp.reshape(batch, seq, D_MODEL),
        xs.reshape(dec_batch, 1, D_MODEL),
        k_p.reshape(1, batch, WINDOW, N_KV_HEADS, HEAD_DIM),
        v_p.reshape(1, batch, WINDOW, N_KV_HEADS, HEAD_DIM),
        k_s.reshape(1, dec_batch, buf, N_KV_HEADS, HEAD_DIM),
        v_s.reshape(1, dec_batch, buf, N_KV_HEADS, HEAD_DIM),
        vn_s.reshape(1, dec_batch, 1, GMLP_WIDTH),
    )
```
